```python
import functools
import jax, jax.numpy as jnp
from jax import lax
import numpy as np

D_MODEL = 1024
BATCH = 2
SEQ = 8192
DEPTH = 2
DEC_BATCH = 128
DEC_SEQ = 4
PAST_LEN = 8192
PAGE_SIZE = 128

HEAD_DIM = 64
A_HEADS = D_MODEL // (2 * HEAD_DIM)
A_BRANCHES = ((128, 1), (512, 4), (2048, 16))
A_WINDOW_MAX = 2048
BAND = 128
B_HEADS = D_MODEL // (2 * HEAD_DIM)
B_KV_HEADS = 2
B_GROUP = B_HEADS // B_KV_HEADS
B_WINDOW = 128
ROPE_THETA = 10000.0
NORM_EPS = 1e-6
CHUNK = 128
C_DIM = D_MODEL
C_GROUPS = 8
C_GROUP_DIM = C_DIM // C_GROUPS
FF_DENSE = 128 * ((8 * D_MODEL // 3 + 127) // 128)
N_EXPERTS = 8
TOP_K = 2
FF_EXPERT = 7 * D_MODEL // 2
MOE_BLOCK = 128
N_EVEN = (DEPTH + 1) // 2
N_ODD = DEPTH // 2
NEG_INF = -1e30

kernel_name = 'hybrid_dilated_swa_gmlp_moe_decode_step'


def rms_norm(x, g):
    xf = x.astype(jnp.float32)
    y = xf * lax.rsqrt(jnp.mean(xf * xf, axis=-1, keepdims=True) + NORM_EPS)
    return (y * g.astype(jnp.float32)).astype(x.dtype)


def layer_norm(x, g, b):
    xf = x.astype(jnp.float32)
    mu = jnp.mean(xf, axis=-1, keepdims=True)
    xc = xf - mu
    y = xc * lax.rsqrt(jnp.mean(xc * xc, axis=-1, keepdims=True) + NORM_EPS)
    return (y * g.astype(jnp.float32) + b.astype(jnp.float32)).astype(x.dtype)


def rope(x, pos):
    half = HEAD_DIM // 2
    inv_freq = ROPE_THETA ** (-jnp.arange(half, dtype=jnp.float32) / half)
    ang = pos.astype(jnp.float32)[:, None] * inv_freq[None, :]
    cos, sin = jnp.cos(ang)[:, None, :], jnp.sin(ang)[:, None, :]
    xf = x.astype(jnp.float32)
    x1, x2 = xf[..., :half], xf[..., half:]
    return jnp.concatenate([x1 * cos - x2 * sin, x2 * cos + x1 * sin], axis=-1).astype(x.dtype)


def masked_stats(s, mask):
    s = jnp.where(mask, s, NEG_INF)
    m = jnp.max(s, axis=-1)
    p = jnp.exp(s - m[..., None])
    return m, jnp.sum(p, axis=-1), p


def combine_by_denominators(stats):
    big_m = functools.reduce(jnp.maximum, [m for m, _, _ in stats])
    num = sum(jnp.exp(m - big_m)[..., None] * acc for m, _, acc in stats)
    den = sum(jnp.exp(m - big_m) * l for m, l, _ in stats)
    return num / den[..., None]


def apply_sinks(m, l, acc, sink):
    sink = sink.astype(jnp.float32)
    m2 = jnp.maximum(m, sink)
    a = jnp.exp(m - m2)
    den = l * a + jnp.exp(sink - m2)
    return acc * (a / den)[..., None]


def band_attn_stats(q, k, v):
    n, L, hk, g, hd = q.shape
    nb = -(-L // BAND)
    pad = nb * BAND - L
    q = jnp.pad(q, ((0, 0), (0, pad), (0, 0), (0, 0), (0, 0)))
    k = jnp.pad(k, ((0, 0), (0, pad), (0, 0), (0, 0)))
    v = jnp.pad(v, ((0, 0), (0, pad), (0, 0), (0, 0)))
    qb = q.reshape(n, nb, BAND, hk, g, hd)

    def with_prev(t):
        tb = t.reshape(n, nb, BAND, hk, hd)
        prev = jnp.pad(tb, ((0, 0), (1, 0), (0, 0), (0, 0), (0, 0)))[:, :nb]
        return jnp.concatenate([prev, tb], axis=2)

    kk, vv = with_prev(k), with_prev(v)
    s = jnp.einsum('nbqhgd,nbkhd->nbhgqk', qb, kk, preferred_element_type=jnp.float32) * HEAD_DIM ** -0.5
    qi = jnp.arange(BAND)[:, None] + BAND
    ki = jnp.arange(2 * BAND)[None, :]
    dist = qi - ki
    in_band = (dist >= 0) & (dist <= BAND)
    real_prev = (jnp.arange(nb)[:, None, None] > 0) | (ki[None] >= BAND)
    mask = (in_band[None] & real_prev)[None, :, None, None]
    m, l, p = masked_stats(s, mask)
    acc = jnp.einsum('nbhgqk,nbkhd->nbqhgd', p, vv.astype(jnp.float32))
    m = m.transpose(0, 1, 4, 2, 3).reshape(n, nb * BAND, hk, g)[:, :L]
    l = l.transpose(0, 1, 4, 2, 3).reshape(n, nb * BAND, hk, g)[:, :L]
    acc = acc.reshape(n, nb * BAND, hk, g, hd)[:, :L]
    return m, l, acc


def dilated_attn_prompt(q, k, v):
    n, S, h, hd = q.shape
    stats = []
    for _, r in A_BRANCHES:
        L = S // r

        def to_sub(t):
            return t.reshape(n, L, r, h, hd).transpose(0, 2, 1, 3, 4).reshape(n * r, L, h, hd)

        def from_sub(t):
            return t.reshape((n, r, L) + t.shape[2:]).swapaxes(1, 2).reshape((n, S) + t.shape[2:])

        m, l, acc = band_attn_stats(to_sub(q)[:, :, :, None], to_sub(k), to_sub(v))
        stats.append((from_sub(m[..., 0]), from_sub(l[..., 0]), from_sub(acc[..., 0, :])))
    return combine_by_denominators(stats).reshape(n, S, h * hd)


def dilated_attn_sample(q, k_all, v_all):
    n, T, h, hd = q.shape
    la = k_all.shape[1] - T
    stats = []
    for w, r in A_BRANCHES:
        j = jnp.arange(w // r + 1)
        idx = la + jnp.arange(T)[:, None] - r * j[None, :]
        valid = idx >= 0
        idx = jnp.maximum(idx, 0)
        kg, vg = k_all[:, idx], v_all[:, idx]
        s = jnp.einsum('nthd,ntjhd->nthj', q, kg, preferred_element_type=jnp.float32) * HEAD_DIM ** -0.5
        m, l, p = masked_stats(s, valid[None, :, None, :])
        acc = jnp.einsum('nthj,ntjhd->nthd', p, vg.astype(jnp.float32))
        stats.append((m, l, acc))
    return combine_by_denominators(stats).reshape(n, T, h * hd)


def sink_swa_prompt(q, k, v, sinks):
    n, L = q.shape[:2]
    qg = q.reshape(n, L, B_KV_HEADS, B_GROUP, HEAD_DIM)
    m, l, acc = band_attn_stats(qg, k, v)
    o = apply_sinks(m, l, acc, sinks.reshape(B_KV_HEADS, B_GROUP))
    return o.reshape(n, L, B_HEADS * HEAD_DIM)


def sink_swa_sample(q, k_all, v_all, sinks):
    n, T = q.shape[:2]
    lb = k_all.shape[1] - T
    qg = q.reshape(n, T, B_KV_HEADS, B_GROUP, HEAD_DIM)
    s = jnp.einsum('nthgd,nshd->nthgs', qg, k_all, preferred_element_type=jnp.float32) * HEAD_DIM ** -0.5
    dist = (lb + jnp.arange(T))[:, None] - jnp.arange(lb + T)[None, :]
    mask = (dist >= 0) & (dist <= B_WINDOW)
    m, l, p = masked_stats(s, mask[None, :, None, None, :])
    acc = jnp.einsum('nthgs,nshd->nthgd', p, v_all.astype(jnp.float32))
    o = apply_sinks(m, l, acc, sinks.reshape(B_KV_HEADS, B_GROUP))
    return o.reshape(n, T, B_HEADS * HEAD_DIM)


def ab_projections(h, w_in, qk_gain, pos):
    aw, bw, kvw = A_HEADS * HEAD_DIM, B_HEADS * HEAD_DIM, B_KV_HEADS * HEAD_DIM
    proj = h @ w_in
    qa, ka, va, qb, kb, vb = jnp.split(proj, [aw, 2 * aw, 3 * aw, 3 * aw + bw, 3 * aw + bw + kvw], axis=-1)

    def heads(t, nh):
        return t.reshape(t.shape[:-1] + (nh, HEAD_DIM))

    qa = rope(rms_norm(heads(qa, A_HEADS), qk_gain[0]), pos)
    ka = rope(rms_norm(heads(ka, A_HEADS), qk_gain[1]), pos)
    qb = rope(rms_norm(heads(qb, B_HEADS), qk_gain[2]), pos)
    kb = rope(rms_norm(heads(kb, B_KV_HEADS), qk_gain[3]), pos)
    return qa, ka, heads(va, A_HEADS), qb, kb, heads(vb, B_KV_HEADS)


def swiglu(h, w1, w3, w2):
    return (jax.nn.silu(h @ w1) * (h @ w3)) @ w2


def chunk_mlp(h, w_in, ln_g, ln_b, w_s, b_s, w_out, chunk_len):
    n, L, _ = h.shape
    z = jax.nn.gelu(h @ w_in, approximate=False)
    u, v = z[..., :C_DIM], z[..., C_DIM:]
    v = layer_norm(v, ln_g, ln_b)
    vc = v.reshape(n, L // chunk_len, chunk_len, C_GROUPS, C_GROUP_DIM)
    w = (w_s * jnp.tril(jnp.ones((CHUNK, CHUNK), w_s.dtype)))[:, :chunk_len, :chunk_len]
    f = jnp.einsum('gij,ncjgd->ncigd', w, vc) + b_s[:, :chunk_len].T[None, None, :, :, None]
    out = (u * f.reshape(n, L, C_DIM)) @ w_out
    return out, v


def moe_swiglu(h, w_router, w1, w3, w2):
    shp = h.shape
    x = h.reshape(-1, shp[-1])
    t = x.shape[0]
    logits = jnp.einsum('td,de->te', x, w_router, preferred_element_type=jnp.float32)
    top_val, top_idx = lax.top_k(logits, TOP_K)
    gates = jax.nn.softmax(top_val, axis=-1)
    n_assign = t * TOP_K
    expert = top_idx.reshape(-1)
    token = jnp.arange(n_assign, dtype=jnp.int32) // TOP_K
    order = jnp.argsort(expert)
    e_sorted = expert[order]
    counts = jnp.bincount(expert, length=N_EXPERTS)
    starts = jnp.cumsum(counts) - counts
    padded = (counts + MOE_BLOCK - 1) // MOE_BLOCK * MOE_BLOCK
    pad_ends = jnp.cumsum(padded)
    pad_starts = pad_ends - padded
    dest = pad_starts[e_sorted] + (jnp.arange(n_assign) - starts[e_sorted])
    n_blocks = -(-n_assign // MOE_BLOCK) + N_EXPERTS
    row_token = jnp.full((n_blocks * MOE_BLOCK,), t, jnp.int32).at[dest].set(token[order])
    x_pad = jnp.concatenate([x, jnp.zeros((1, x.shape[1]), x.dtype)], axis=0)
    xb = x_pad[row_token].reshape(n_blocks, MOE_BLOCK, x.shape[1])
    block_expert = jnp.minimum(jnp.searchsorted(pad_ends, jnp.arange(n_blocks) * MOE_BLOCK, side='right'), N_EXPERTS - 1)

    def expert_block(args):
        xe, e = args
        return (jax.nn.silu(xe @ w1[e]) * (xe @ w3[e])) @ w2[e]

    yb = lax.map(expert_block, (xb, block_expert)).reshape(-1, x.shape[1])
    y_assign = yb[dest] * gates.reshape(-1)[order][:, None]
    y = jax.ops.segment_sum(y_assign, token[order], num_segments=t)
    return y.astype(h.dtype).reshape(shp)


def setup_inputs(seed: int = 0) -> dict:
    key = jax.random.key(seed)
    ks = jax.random.split(key, 26)
    f32 = jnp.float32

    def nrm(k, shape, scale=1.0):
        return scale * jax.random.normal(k, shape, f32)

    la = min(A_WINDOW_MAX, PAST_LEN)
    lb = min(B_WINDOW, PAST_LEN)
    proj_w = 3 * A_HEADS * HEAD_DIM + B_HEADS * HEAD_DIM + 2 * B_KV_HEADS * HEAD_DIM
    mix_w = (A_HEADS + B_HEADS) * HEAD_DIM
    return {
        'x_prompt': nrm(ks[0], (BATCH, SEQ, D_MODEL)),
        'x_sample': nrm(ks[1], (DEC_BATCH, DEC_SEQ, D_MODEL)),
        'cache_a_k': nrm(ks[2], (N_EVEN, DEC_BATCH, la, A_HEADS, HEAD_DIM)),
        'cache_a_v': nrm(ks[3], (N_EVEN, DEC_BATCH, la, A_HEADS, HEAD_DIM)),
        'cache_b_k': nrm(ks[4], (N_EVEN, DEC_BATCH, lb, B_KV_HEADS, HEAD_DIM)),
        'cache_b_v': nrm(ks[5], (N_EVEN, DEC_BATCH, lb, B_KV_HEADS, HEAD_DIM)),
        'norm_mix': 1.0 + nrm(ks[6], (DEPTH, D_MODEL), 0.1),
        'norm_ffn': 1.0 + nrm(ks[7], (DEPTH, D_MODEL), 0.1),
        'ab_w_in': nrm(ks[8], (N_EVEN, D_MODEL, proj_w), D_MODEL ** -0.5),
        'ab_qk_gain': 1.0 + nrm(ks[9], (N_EVEN, 4, HEAD_DIM), 0.1),
        'ab_sinks': nrm(ks[10], (N_EVEN, B_HEADS), 0.5),
        'ab_w_out': nrm(ks[11], (N_EVEN, mix_w, D_MODEL), mix_w ** -0.5),
        'ffn_w1': nrm(ks[12], (N_EVEN, D_MODEL, FF_DENSE), D_MODEL ** -0.5),
        'ffn_w3': nrm(ks[13], (N_EVEN, D_MODEL, FF_DENSE), D_MODEL ** -0.5),
        'ffn_w2': nrm(ks[14], (N_EVEN, FF_DENSE, D_MODEL), FF_DENSE ** -0.5),
        'c_w_in': nrm(ks[15], (N_ODD, D_MODEL, 2 * C_DIM), D_MODEL ** -0.5),
        'c_ln_g': 1.0 + nrm(ks[16], (N_ODD, C_DIM), 0.1),
        'c_ln_b': nrm(ks[17], (N_ODD, C_DIM), 0.1),
        'c_w_s': nrm(ks[18], (N_ODD, C_GROUPS, CHUNK, CHUNK), 0.5 * CHUNK ** -0.5),
        'c_b_s': 1.0 + nrm(ks[19], (N_ODD, C_GROUPS, CHUNK), 0.1),
        'c_w_out': nrm(ks[20], (N_ODD, C_DIM, D_MODEL), C_DIM ** -0.5),
        'moe_router': nrm(ks[21], (N_ODD, D_MODEL, N_EXPERTS), D_MODEL ** -0.5),
        'moe_w1': nrm(ks[22], (N_ODD, N_EXPERTS, D_MODEL, FF_EXPERT), D_MODEL ** -0.5),
        'moe_w3': nrm(ks[23], (N_ODD, N_EXPERTS, D_MODEL, FF_EXPERT), D_MODEL ** -0.5),
        'moe_w2': nrm(ks[24], (N_ODD, N_EXPERTS, FF_EXPERT, D_MODEL), FF_EXPERT ** -0.5),
    }


def reference(x_prompt, x_sample, cache_a_k, cache_a_v, cache_b_k, cache_b_v,
              norm_mix, norm_ffn, ab_w_in, ab_qk_gain, ab_sinks, ab_w_out,
              ffn_w1, ffn_w3, ffn_w2, c_w_in, c_ln_g, c_ln_b, c_w_s, c_b_s, c_w_out,
              moe_router, moe_w1, moe_w3, moe_w2):
    xp, xs = x_prompt, x_sample
    S, T = xp.shape[1], xs.shape[1]
    pos_p = jnp.arange(S)
    pos_s = PAST_LEN + jnp.arange(T)
    la_p, lb_p = min(A_WINDOW_MAX, S), min(B_WINDOW, S)
    akp, avp, bkp, bvp, aks, avs, bks, bvs, cvs = [], [], [], [], [], [], [], [], []
    for layer in range(DEPTH):
        i = layer // 2
        if layer % 2 == 0:
            qa, ka, va, qb, kb, vb = ab_projections(rms_norm(xp, norm_mix[layer]), ab_w_in[i], ab_qk_gain[i], pos_p)
            o = jnp.concatenate([dilated_attn_prompt(qa, ka, va), sink_swa_prompt(qb, kb, vb, ab_sinks[i])], axis=-1)
            xp = xp + o.astype(xp.dtype) @ ab_w_out[i]
            xp = xp + swiglu(rms_norm(xp, norm_ffn[layer]), ffn_w1[i], ffn_w3[i], ffn_w2[i])
            akp.append(ka[:, S - la_p:])
            avp.append(va[:, S - la_p:])
            bkp.append(kb[:, S - lb_p:])
            bvp.append(vb[:, S - lb_p:])
            qa, ka, va, qb, kb, vb = ab_projections(rms_norm(xs, norm_mix[layer]), ab_w_in[i], ab_qk_gain[i], pos_s)
            ka_all = jnp.concatenate([cache_a_k[i].astype(ka.dtype), ka], axis=1)
            va_all = jnp.concatenate([cache_a_v[i].astype(va.dtype), va], axis=1)
            kb_all = jnp.concatenate([cache_b_k[i].astype(kb.dtype), kb], axis=1)
            vb_all = jnp.concatenate([cache_b_v[i].astype(vb.dtype), vb], axis=1)
            o = jnp.concatenate([dilated_attn_sample(qa, ka_all, va_all), sink_swa_sample(qb, kb_all, vb_all, ab_sinks[i])], axis=-1)
            xs = xs + o.astype(xs.dtype) @ ab_w_out[i]
            xs = xs + swiglu(rms_norm(xs, norm_ffn[layer]), ffn_w1[i], ffn_w3[i], ffn_w2[i])
            aks.append(ka)
            avs.append(va)
            bks.append(kb)
            bvs.append(vb)
        else:
            out, _ = chunk_mlp(rms_norm(xp, norm_mix[layer]), c_w_in[i], c_ln_g[i], c_ln_b[i], c_w_s[i], c_b_s[i], c_w_out[i], CHUNK)
            xp = xp + out
            xp = xp + moe_swiglu(rms_norm(xp, norm_ffn[layer]), moe_router[i], moe_w1[i], moe_w3[i], moe_w2[i])
            out, v_new = chunk_mlp(rms_norm(xs, norm_mix[layer]), c_w_in[i], c_ln_g[i], c_ln_b[i], c_w_s[i], c_b_s[i], c_w_out[i], T)
            xs = xs + out
            xs = xs + moe_swiglu(rms_norm(xs, norm_ffn[layer]), moe_router[i], moe_w1[i], moe_w3[i], moe_w2[i])
            cvs.append(v_new)
    return (xp, xs,
            jnp.stack(akp), jnp.stack(avp), jnp.stack(bkp), jnp.stack(bvp),
            jnp.stack(aks), jnp.stack(avs), jnp.stack(bks), jnp.stack(bvs),
            jnp.stack(cvs))
```

```python
import functools

import jax
import jax.numpy as jnp
from jax import lax
from jax.experimental import pallas as pl
from jax.experimental.pallas import tpu as pltpu

F32 = jnp.float32
BF16 = jnp.bfloat16

HEAD_DIM = 64
HALF_DIM = HEAD_DIM // 2
N_HEADS = 8
B_KV_HEADS = 2
A_DILATIONS = (1, 4, 16)
BAND = 128
ROPE_THETA = 10000.0
NORM_EPS = 1e-6
NEG_INF = -1e30
CHUNK = 128
C_GROUPS = 8
N_EXPERTS = 8
TOP_K = 2

LANES = 128
SUBLANES = 8
VMEM_LIMIT_BYTES = 56 * 1024 * 1024

AW = N_HEADS * HEAD_DIM
KVW = B_KV_HEADS * HEAD_DIM
HEADS_PER_VREG = LANES // HEAD_DIM
N_HEAD_BLOCKS = AW // LANES


def _cparams(*sem):
    return pltpu.CompilerParams(dimension_semantics=sem, vmem_limit_bytes=VMEM_LIMIT_BYTES)


def _rms_norm(x, g):
    return x * lax.rsqrt(jnp.mean(x * x, axis=-1, keepdims=True) + NORM_EPS) * g


def _dot(a, b):
    return jnp.dot(a, b, preferred_element_type=F32)


def _dot_nt(a, b):
    return lax.dot_general(a, b, (((1,), (1,)), ((), ())), preferred_element_type=F32)


_QA_BLOCKS = range(0, 4)
_KA_BLOCKS = range(4, 8)
_VA_BLOCKS = range(8, 12)
_QB_BLOCKS = range(12, 16)
_KB_BLOCK = 16
_VB_BLOCK = 17


def _proj_kernel(x_ref, g_ref, w_ref, gain_ref, cos_ref, sin_ref,
                 qa_ref, ka_ref, va_ref, qb_ref, kb_ref, vb_ref):
    h = _rms_norm(x_ref[...], g_ref[...]).astype(BF16)
    p = _dot(h, w_ref[...])
    tm = p.shape[0]
    cos = cos_ref[...]
    sin = sin_ref[...]
    lane = lax.broadcasted_iota(jnp.int32, (tm, LANES), 1)
    low_head = lane < HEAD_DIM
    first_half = (lane & HALF_DIM) == 0

    def norm_rope(blk):
        pc = p[:, blk * LANES:(blk + 1) * LANES]
        sq = pc * pc
        ms_lo = jnp.sum(jnp.where(low_head, sq, 0.0), axis=-1, keepdims=True) * (1.0 / HEAD_DIM)
        ms_hi = jnp.sum(jnp.where(low_head, 0.0, sq), axis=-1, keepdims=True) * (1.0 / HEAD_DIM)
        inv = jnp.where(low_head, lax.rsqrt(ms_lo + NORM_EPS), lax.rsqrt(ms_hi + NORM_EPS))
        y = pc * inv * gain_ref[:, blk * LANES:(blk + 1) * LANES]
        partner = jnp.where(first_half, pltpu.roll(y, LANES - HALF_DIM, 1), pltpu.roll(y, HALF_DIM, 1))
        return y * cos + partner * sin

    scale = HEAD_DIM ** -0.5
    for j, blk in enumerate(_QA_BLOCKS):
        qa_ref[:, j * LANES:(j + 1) * LANES] = (norm_rope(blk) * scale).astype(qa_ref.dtype)
    for j, blk in enumerate(_KA_BLOCKS):
        ka_ref[:, j * LANES:(j + 1) * LANES] = norm_rope(blk)
    for j, blk in enumerate(_VA_BLOCKS):
        va_ref[:, j * LANES:(j + 1) * LANES] = p[:, blk * LANES:(blk + 1) * LANES]
    for j, blk in enumerate(_QB_BLOCKS):
        qb_ref[:, j * LANES:(j + 1) * LANES] = (norm_rope(blk) * scale).astype(qb_ref.dtype)
    kb_ref[...] = norm_rope(_KB_BLOCK)
    vb_ref[...] = p[:, _VB_BLOCK * LANES:(_VB_BLOCK + 1) * LANES]


def _qkv_proj(x, g, w, gain_row, cos, sin, *, tm, q_dtype):
    t, d = x.shape
    n_out = w.shape[1]
    row = lambda i: (i, 0)
    fixed = lambda i: (0, 0)
    sds = jax.ShapeDtypeStruct
    return pl.pallas_call(
        _proj_kernel,
        grid=(t // tm,),
        in_specs=[pl.BlockSpec((tm, d), row), pl.BlockSpec((1, d), fixed), pl.BlockSpec((d, n_out), fixed),
                  pl.BlockSpec((1, n_out), fixed), pl.BlockSpec((tm, LANES), row), pl.BlockSpec((tm, LANES), row)],
        out_specs=[pl.BlockSpec((tm, AW), row), pl.BlockSpec((tm, AW), row), pl.BlockSpec((tm, AW), row),
                   pl.BlockSpec((tm, AW), row), pl.BlockSpec((tm, KVW), row), pl.BlockSpec((tm, KVW), row)],
        out_shape=[sds((t, AW), q_dtype), sds((t, AW), F32), sds((t, AW), F32),
                   sds((t, AW), q_dtype), sds((t, KVW), F32), sds((t, KVW), F32)],
        compiler_params=_cparams("arbitrary"),
        name="qkv_proj",
    )(x, g, w, gain_row, cos, sin)


def _fill_kv(kbuf, vbuf, kp_ref, kc_ref, vp_ref, vc_ref):
    kbuf[0:BAND, :] = kp_ref[...].astype(BF16)
    kbuf[BAND:, :] = kc_ref[...].astype(BF16)
    vbuf[0:BAND, :] = vp_ref[...].astype(BF16)
    vbuf[BAND:, :] = vc_ref[...].astype(BF16)


def _band_mask(first_block):
    row = lax.broadcasted_iota(jnp.int32, (BAND, 2 * BAND), 0)
    col = lax.broadcasted_iota(jnp.int32, (BAND, 2 * BAND), 1)
    dist = BAND + row - col
    first_key = jnp.where(first_block, BAND, 0)
    return (dist >= 0) & (dist <= BAND) & (col >= first_key)


def _masked_softmax_pv(qm, kc, vc, mask):
    s = jnp.where(mask, _dot_nt(qm, kc), NEG_INF)
    m = jnp.max(s, axis=-1, keepdims=True)
    p = jnp.exp(s - m)
    l = jnp.sum(p, axis=-1, keepdims=True)
    return m, l, _dot(p.astype(BF16), vc)


def _attn_a_kernel(q_ref, kc_ref, kp_ref, vc_ref, vp_ref, *rest, tq, has_prev, write_lse):
    rest = list(rest)
    if has_prev:
        op_ref, lp_ref = rest.pop(0), rest.pop(0)
    o_ref = rest.pop(0)
    if write_lse:
        lse_ref = rest.pop(0)
    kbuf, vbuf = rest
    blk = pl.program_id(2)
    _fill_kv(kbuf, vbuf, kp_ref, kc_ref, vp_ref, vc_ref)
    lane = lax.broadcasted_iota(jnp.int32, (BAND, LANES), 1)
    low_head = lane < HEAD_DIM

    def sub_block(sb, carry):
        r0 = pl.multiple_of(sb * BAND, BAND)
        mask = _band_mask((blk + sb) == 0)
        for hb in range(N_HEAD_BLOCKS):
            cs = slice(hb * LANES, (hb + 1) * LANES)
            kc = kbuf[pl.ds(r0, 2 * BAND), cs]
            vc = vbuf[pl.ds(r0, 2 * BAND), cs]
            q2 = q_ref[pl.ds(r0, BAND), cs]
            outs, lses = [], []
            for hh in range(HEADS_PER_VREG):
                sel = low_head if hh == 0 else jnp.logical_not(low_head)
                qm = jnp.where(sel, q2, jnp.zeros_like(q2))
                m, l, pv = _masked_softmax_pv(qm, kc, vc, mask)
                outs.append(pv / l)
                lses.append(m + jnp.log(l))
            oc = jnp.where(low_head, outs[0], outs[1])
            lc = jnp.where(low_head, lses[0], lses[1])
            if has_prev:
                lp = lp_ref[pl.ds(r0, BAND), cs]
                op = op_ref[pl.ds(r0, BAND), cs]
                mx = jnp.maximum(lp, lc)
                wp = jnp.exp(lp - mx)
                wc = jnp.exp(lc - mx)
                den = wp + wc
                oc = (op * wp + oc * wc) / den
                lc = mx + jnp.log(den)
            o_ref[pl.ds(r0, BAND), cs] = oc.astype(o_ref.dtype)
            if write_lse:
                lse_ref[pl.ds(r0, BAND), cs] = lc
        return carry

    lax.fori_loop(0, tq // BAND, sub_block, 0)


def _attn_a_branch(q, k, v, prev, *, dilation, last):
    n, s, w = q.shape
    r = dilation
    sub_len = s // r
    tq = min(512, sub_len)
    view = lambda a: a.reshape(n, sub_len, r * w)
    cur = lambda b, c, i: (b, i, c)
    prv = lambda b, c, i: (b, jnp.maximum(i * (tq // BAND) - 1, 0), c)
    blk_cur = pl.BlockSpec((None, tq, w), cur)
    blk_prv = pl.BlockSpec((None, BAND, w), prv)
    in_specs = [blk_cur, blk_cur, blk_prv, blk_cur, blk_prv]
    args = [view(q), view(k), view(k), view(v), view(v)]
    if prev is not None:
        in_specs += [blk_cur, blk_cur]
        args += [view(prev[0]), view(prev[1])]
    sds = jax.ShapeDtypeStruct
    out_specs = [blk_cur]
    out_shape = [sds((n, sub_len, r * w), BF16 if last else F32)]
    if not last:
        out_specs.append(blk_cur)
        out_shape.append(sds((n, sub_len, r * w), F32))
    outs = pl.pallas_call(
        functools.partial(_attn_a_kernel, tq=tq, has_prev=prev is not None, write_lse=not last),
        grid=(n, r, sub_len // tq),
        in_specs=in_specs, out_specs=out_specs, out_shape=out_shape,
        scratch_shapes=[pltpu.VMEM((tq + BAND, w), BF16), pltpu.VMEM((tq + BAND, w), BF16)],
        compiler_params=_cparams("arbitrary", "arbitrary", "arbitrary"),
        name=f"attn_a_dil{r}",
    )(*args)
    outs = [o.reshape(n, s, w) for o in outs]
    return outs[0], (None if last else outs[1])


def _attn_b_kernel(q_ref, kc_ref, kp_ref, vc_ref, vp_ref, sink_ref, o_ref, kbuf, vbuf, *, tq):
    blk = pl.program_id(1)
    _fill_kv(kbuf, vbuf, kp_ref, kc_ref, vp_ref, vc_ref)
    lane = lax.broadcasted_iota(jnp.int32, (BAND, LANES), 1)
    low_head = lane < HEAD_DIM
    group = N_HEADS // B_KV_HEADS

    def sub_block(sb, carry):
        r0 = pl.multiple_of(sb * BAND, BAND)
        mask = _band_mask((blk + sb) == 0)
        kc = kbuf[pl.ds(r0, 2 * BAND), :]
        vc = vbuf[pl.ds(r0, 2 * BAND), :]
        for hb in range(N_HEAD_BLOCKS):
            cs = slice(hb * LANES, (hb + 1) * LANES)
            q2 = q_ref[pl.ds(r0, BAND), cs].astype(F32)
            q2_swapped = pltpu.roll(q2, HEAD_DIM, 1)
            placed = []
            for hh in range(HEADS_PER_VREG):
                head = hb * HEADS_PER_VREG + hh
                kv = head // group
                kv_sel = low_head if kv == 0 else jnp.logical_not(low_head)
                q_at_kv = q2 if hh == kv else q2_swapped
                qm = jnp.where(kv_sel, q_at_kv, 0.0).astype(BF16)
                m, l, pv = _masked_softmax_pv(qm, kc, vc, mask)
                sink = sink_ref[head:head + 1, 0:1]
                m2 = jnp.maximum(m, sink)
                a = jnp.exp(m - m2)
                o = pv * (a / (l * a + jnp.exp(sink - m2)))
                placed.append(o if hh == kv else pltpu.roll(o, HEAD_DIM, 1))
            o_ref[pl.ds(r0, BAND), cs] = jnp.where(low_head, placed[0], placed[1]).astype(o_ref.dtype)
        return carry

    lax.fori_loop(0, tq // BAND, sub_block, 0)


def _attn_b(q, k, v, sinks_rows):
    n, s, w = q.shape
    tq = min(512, s)
    cur = lambda b, i: (b, i, 0)
    prv = lambda b, i: (b, jnp.maximum(i * (tq // BAND) - 1, 0), 0)
    kv_cur = pl.BlockSpec((None, tq, KVW), cur)
    kv_prv = pl.BlockSpec((None, BAND, KVW), prv)
    return pl.pallas_call(
        functools.partial(_attn_b_kernel, tq=tq),
        grid=(n, s // tq),
        in_specs=[pl.BlockSpec((None, tq, w), cur), kv_cur, kv_prv, kv_cur, kv_prv,
                  pl.BlockSpec((N_HEADS, LANES), lambda b, i: (0, 0))],
        out_specs=pl.BlockSpec((None, tq, w), cur),
        out_shape=jax.ShapeDtypeStruct((n, s, w), BF16),
        scratch_shapes=[pltpu.VMEM((tq + BAND, KVW), BF16), pltpu.VMEM((tq + BAND, KVW), BF16)],
        compiler_params=_cparams("arbitrary", "arbitrary"),
        name="attn_b",
    )(q, k, k, v, v, sinks_rows)


NEW_ROWS = 8


def _sample_attn_kernel(qa_ref, kna_ref, vna_ref, qb_ref, knb_ref, vnb_ref,
                        k16_ref, k4_ref, k1_ref, v16_ref, v4_ref, v1_ref, cbk_ref, cbv_ref, sink_ref,
                        oa_ref, ob_ref, *, nb, t_new):
    lane_w = lax.broadcasted_iota(jnp.int32, (AW, LANES), 0) // HEAD_DIM
    head_w = lax.broadcasted_iota(jnp.int32, (AW, LANES), 1)
    head_sum = (lane_w == head_w).astype(BF16)
    head_e = lax.broadcasted_iota(jnp.int32, (LANES, AW), 0)
    lane_e = lax.broadcasted_iota(jnp.int32, (LANES, AW), 1) // HEAD_DIM
    head_expand = (head_e == lane_e).astype(BF16)
    row_c = lax.broadcasted_iota(jnp.int32, (BAND, 1), 0)
    row_n = lax.broadcasted_iota(jnp.int32, (NEW_ROWS, 1), 0)
    lane = lax.broadcasted_iota(jnp.int32, (BAND, LANES), 1)
    low_head = lane < HEAD_DIM
    low_head_n = low_head[:NEW_ROWS]
    sink_row = sink_ref[...]

    def scores(k, q_row):
        return _dot((k * q_row).astype(BF16), head_sum)

    def attend(q_row, blocks, sink=None):
        ss = []
        m = None
        for k, _, valid, _ in blocks:
            s = scores(k, q_row)
            if valid is not None:
                s = jnp.where(valid, s, NEG_INF)
            ss.append(s)
            bm = jnp.max(s, axis=0, keepdims=True)
            m = bm if m is None else jnp.maximum(m, bm)
        if sink is not None:
            m = jnp.maximum(m, sink)
        ps = []
        l = jnp.exp(sink - m) if sink is not None else 0.0
        for s, (_, _, valid, mult) in zip(ss, blocks):
            p = jnp.exp(s - m)
            if mult is not None:
                p = p * mult
            ps.append(p)
            l = l + jnp.sum(p, axis=0, keepdims=True)
        inv_l = 1.0 / l
        acc = None
        for p, (_, v, _, _) in zip(ps, blocks):
            pe = _dot((p * inv_l).astype(BF16), head_expand)
            part = jnp.sum(pe * v, axis=0, keepdims=True)
            acc = part if acc is None else acc + part
        return acc

    def expand_kv(x, low):
        sw = pltpu.roll(x, HEAD_DIM, 1)
        kv0 = jnp.where(low, x, sw)
        kv1 = jnp.where(low, sw, x)
        return jnp.concatenate([kv0, kv0, kv1, kv1], axis=1)

    def one_sequence(n, carry):
        new0 = pl.multiple_of(n * NEW_ROWS, NEW_ROWS)
        kna = kna_ref[pl.ds(new0, NEW_ROWS), :]
        vna = vna_ref[pl.ds(new0, NEW_ROWS), :]
        k1 = k1_ref[n]
        v1 = v1_ref[n]
        kb_c = expand_kv(cbk_ref[n], low_head)
        vb_c = expand_kv(cbv_ref[n], low_head)
        kb_n = expand_kv(knb_ref[pl.ds(new0, NEW_ROWS), :], low_head_n)
        vb_n = expand_kv(vnb_ref[pl.ds(new0, NEW_ROWS), :], low_head_n)
        qa = qa_ref[pl.ds(new0, NEW_ROWS), :]
        qb = qb_ref[pl.ds(new0, NEW_ROWS), :]
        out_a = jnp.zeros((NEW_ROWS, AW), F32)
        out_b = jnp.zeros((NEW_ROWS, AW), F32)
        for t in range(t_new):
            cs = slice(t * AW, (t + 1) * AW)
            mult_a = (row_n <= t).astype(F32) + 2.0 * (row_n == t).astype(F32)
            acc_a = attend(qa[t:t + 1, :], [
                (k16_ref[n, :, cs], v16_ref[n, :, cs], None, None),
                (k4_ref[n, :, cs], v4_ref[n, :, cs], None, None),
                (k1, v1, row_c >= t, None),
                (kna, vna, row_n <= t, mult_a),
            ])
            acc_b = attend(qb[t:t + 1, :], [
                (kb_c, vb_c, row_c >= t, None),
                (kb_n, vb_n, row_n <= t, None),
            ], sink=sink_row)
            out_a = jnp.where(row_n == t, acc_a, out_a)
            out_b = jnp.where(row_n == t, acc_b, out_b)
        oa_ref[pl.ds(new0, NEW_ROWS), :] = out_a.astype(oa_ref.dtype)
        ob_ref[pl.ds(new0, NEW_ROWS), :] = out_b.astype(ob_ref.dtype)
        return carry

    lax.fori_loop(0, nb, one_sequence, 0)


def _pad_new_rows(a, n_seq, t_new):
    w = a.shape[-1]
    a = a.reshape(n_seq, t_new, w)
    a = jnp.pad(a, ((0, 0), (0, NEW_ROWS - t_new), (0, 0)))
    return a.reshape(n_seq * NEW_ROWS, w)


def _sample_attn(qa, ka, va, qb, kb, vb, cache_a_k, cache_a_v, cache_b_k, cache_b_v, sink_row, *, t_new):
    n_seq, la, w = cache_a_k.shape
    assert la == BAND * A_DILATIONS[-1] and t_new <= min(A_DILATIONS[1], NEW_ROWS)
    nb = 4 if n_seq % 4 == 0 else 1
    padded = [_pad_new_rows(a, n_seq, t_new) for a in (qa, ka, va, qb, kb, vb)]
    r4, r16 = A_DILATIONS[1], A_DILATIONS[2]
    view16 = lambda c: c.reshape(n_seq, la // r16, r16 * w)
    view4 = lambda c: c.reshape(n_seq, la // r4, r4 * w)
    new_a = pl.BlockSpec((nb * NEW_ROWS, w), lambda i: (i, 0))
    new_b = pl.BlockSpec((nb * NEW_ROWS, KVW), lambda i: (i, 0))
    spec16 = pl.BlockSpec((nb, BAND, r4 * w), lambda i: (i, 0, 0))
    spec4 = pl.BlockSpec((nb, BAND, r4 * w), lambda i: (i, la // r4 // BAND - 1, 0))
    spec1 = pl.BlockSpec((nb, BAND, w), lambda i: (i, la // BAND - 1, 0))
    spec_cb = pl.BlockSpec((nb, BAND, KVW), lambda i: (i, 0, 0))
    sds = jax.ShapeDtypeStruct
    oa, ob = pl.pallas_call(
        functools.partial(_sample_attn_kernel, nb=nb, t_new=t_new),
        grid=(n_seq // nb,),
        in_specs=[new_a, new_a, new_a, new_a, new_b, new_b,
                  spec16, spec4, spec1, spec16, spec4, spec1, spec_cb, spec_cb,
                  pl.BlockSpec((1, LANES), lambda i: (0, 0))],
        out_specs=[new_a, new_a],
        out_shape=[sds((n_seq * NEW_ROWS, w), BF16), sds((n_seq * NEW_ROWS, w), BF16)],
        compiler_params=_cparams("arbitrary"),
        name="sample_attn",
    )(*padded, view16(cache_a_k), view4(cache_a_k), cache_a_k, view16(cache_a_v), view4(cache_a_v), cache_a_v,
      cache_b_k, cache_b_v, sink_row)
    unpad = lambda o: o.reshape(n_seq, NEW_ROWS, w)[:, :t_new].reshape(n_seq * t_new, w)
    return unpad(oa), unpad(ob)


def _silu(a):
    return a * (1.0 / (1.0 + jnp.exp(-a)))


def _mix_out_ffn_kernel(x_ref, oa_ref, ob_ref, wo_ref, g_ref, w1_ref, w3_ref, w2_ref, out_ref,
                        x1_ref, h_ref, acc_ref):
    j = pl.program_id(1)

    @pl.when(j == 0)
    def _():
        x1 = x_ref[...] + _dot(oa_ref[...], wo_ref[0:AW, :]) + _dot(ob_ref[...], wo_ref[AW:, :])
        x1_ref[...] = x1
        h_ref[...] = _rms_norm(x1, g_ref[...]).astype(BF16)
        acc_ref[...] = jnp.zeros_like(acc_ref)

    h = h_ref[...]
    gate = (_silu(_dot(h, w1_ref[...])) * _dot(h, w3_ref[...])).astype(BF16)
    acc_ref[...] += _dot(gate, w2_ref[...])

    @pl.when(j == pl.num_programs(1) - 1)
    def _():
        out_ref[...] = x1_ref[...] + acc_ref[...]


def _ff_tile(ff, target):
    best = LANES
    for k in range(1, ff // LANES + 1):
        if ff % (k * LANES) == 0 and k * LANES <= target:
            best = k * LANES
    return best


def _mix_out_ffn(x, oa, ob, wo, g, w1, w3, w2, *, tm):
    t, d = x.shape
    ff = w1.shape[1]
    tf = _ff_tile(ff, 1408)
    row = lambda i, j: (i, 0)
    fixed = lambda i, j: (0, 0)
    return pl.pallas_call(
        _mix_out_ffn_kernel,
        grid=(t // tm, ff // tf),
        in_specs=[pl.BlockSpec((tm, d), row), pl.BlockSpec((tm, AW), row), pl.BlockSpec((tm, AW), row),
                  pl.BlockSpec((2 * AW, d), fixed), pl.BlockSpec((1, d), fixed),
                  pl.BlockSpec((d, tf), lambda i, j: (0, j)), pl.BlockSpec((d, tf), lambda i, j: (0, j)),
                  pl.BlockSpec((tf, d), lambda i, j: (j, 0))],
        out_specs=pl.BlockSpec((tm, d), row),
        out_shape=jax.ShapeDtypeStruct((t, d), F32),
        scratch_shapes=[pltpu.VMEM((tm, d), F32), pltpu.VMEM((tm, d), BF16), pltpu.VMEM((tm, d), F32)],
        compiler_params=_cparams("arbitrary", "arbitrary"),
        name="mix_out_ffn",
    )(x, oa, ob, wo, g, w1, w3, w2)


def _gelu_exact(z):
    return 0.5 * z * (1.0 + lax.erf(z * (2.0 ** -0.5)))


def _gmlp_kernel(x_ref, g_ref, win_ref, lng_ref, lnb_ref, mix_ref, bias_ref, wout_ref, gffn_ref, wr_ref,
                 *out_refs, write_v):
    if write_v:
        xo_ref, h2_ref, logit_ref, v_ref = out_refs[:4]
    else:
        xo_ref, h2_ref, logit_ref = out_refs[:3]
    gate_ref = out_refs[-1]
    x = x_ref[...]
    tm, cd = x.shape
    h = _rms_norm(x, g_ref[...]).astype(BF16)
    z = _gelu_exact(_dot(h, win_ref[...]))
    u = z[:, :cd]
    v = z[:, cd:]
    mu = jnp.mean(v, axis=-1, keepdims=True)
    vc = v - mu
    v = vc * lax.rsqrt(jnp.mean(vc * vc, axis=-1, keepdims=True) + NORM_EPS) * lng_ref[...] + lnb_ref[...]
    if write_v:
        v_ref[...] = v
    vb = v.astype(BF16)
    bias = bias_ref[...]
    gw = cd // C_GROUPS
    for c in range(tm // CHUNK):
        rs = slice(c * CHUNK, (c + 1) * CHUNK)
        for g in range(C_GROUPS):
            cs = slice(g * gw, (g + 1) * gw)
            f = _dot(mix_ref[g], vb[rs, cs]) + bias[:, cs]
            gate_ref[rs, cs] = (u[rs, cs] * f).astype(BF16)
    xo = x + _dot(gate_ref[...], wout_ref[...])
    xo_ref[...] = xo
    h2 = _rms_norm(xo, gffn_ref[...])
    h2_ref[...] = h2
    h_hi = h2.astype(BF16)
    h_lo = (h2 - h_hi.astype(F32)).astype(BF16)
    wr = wr_ref[...]
    both = _dot(h_hi, wr)
    logit_ref[...] = both[:, :LANES] + both[:, LANES:] + _dot(h_lo, wr[:, :LANES])


def _gmlp_block(x, g, w_in, ln_g, ln_b, mix, bias_full, w_out, g_ffn, wr_hi_lo, *, tm, write_v):
    t, d = x.shape
    cd = w_out.shape[0]
    row = lambda i: (i, 0)
    fixed = lambda i: (0, 0)
    sds = jax.ShapeDtypeStruct
    out_specs = [pl.BlockSpec((tm, d), row), pl.BlockSpec((tm, d), row), pl.BlockSpec((tm, LANES), row)]
    out_shape = [sds((t, d), F32), sds((t, d), F32), sds((t, LANES), F32)]
    if write_v:
        out_specs.append(pl.BlockSpec((tm, cd), row))
        out_shape.append(sds((t, cd), F32))
    return pl.pallas_call(
        functools.partial(_gmlp_kernel, write_v=write_v),
        grid=(t // tm,),
        in_specs=[pl.BlockSpec((tm, d), row), pl.BlockSpec((1, d), fixed), pl.BlockSpec((d, 2 * cd), fixed),
                  pl.BlockSpec((1, cd), fixed), pl.BlockSpec((1, cd), fixed),
                  pl.BlockSpec((C_GROUPS, CHUNK, CHUNK), lambda i: (0, 0, 0)),
                  pl.BlockSpec((CHUNK, cd), fixed), pl.BlockSpec((cd, d), fixed), pl.BlockSpec((1, d), fixed),
                  pl.BlockSpec((d, 2 * LANES), fixed)],
        out_specs=out_specs, out_shape=out_shape,
        scratch_shapes=[pltpu.VMEM((tm, cd), BF16)],
        compiler_params=_cparams("arbitrary"),
        name="gmlp_block",
    )(x, g, w_in, ln_g, ln_b, mix, bias_full, w_out, g_ffn, wr_hi_lo)


MOE_BLOCK = 1024
DISPATCH_ROWS = 512


def _dispatch_kernel(dest_ref, h_ref, xs_in_ref, xs_ref, sem):
    del xs_in_ref
    rows = h_ref.shape[0]

    def issue(r, carry):
        for k in range(TOP_K):
            d = dest_ref[0, 0, r * TOP_K + k]
            pltpu.make_async_copy(h_ref.at[pl.ds(r, 1), :], xs_ref.at[pl.ds(d, 1), :], sem).start()
        return carry

    lax.fori_loop(0, rows, issue, 0)
    for _ in range(TOP_K):
        pltpu.make_async_copy(h_ref, xs_ref.at[pl.ds(0, rows), :], sem).wait()


def _dispatch(h, dest, n_rows):
    t, d = h.shape
    tg = DISPATCH_ROWS
    xs0 = jnp.zeros((n_rows, d), F32)
    return pl.pallas_call(
        _dispatch_kernel,
        grid=(t // tg,),
        in_specs=[pl.BlockSpec((1, 1, tg * TOP_K), lambda i: (i, 0, 0), memory_space=pltpu.SMEM),
                  pl.BlockSpec((tg, d), lambda i: (i, 0)),
                  pl.BlockSpec(memory_space=pl.ANY)],
        out_specs=pl.BlockSpec(memory_space=pl.ANY),
        out_shape=jax.ShapeDtypeStruct((n_rows, d), F32),
        scratch_shapes=[pltpu.SemaphoreType.DMA],
        input_output_aliases={2: 0},
        compiler_params=_cparams("arbitrary"),
        name="moe_dispatch",
    )(dest.reshape(t // tg, 1, tg * TOP_K), h, xs0)


def _expert_kernel(be_ref, na_ref, xs_ref, w1_ref, w3_ref, w2_ref, y_ref, h_ref, acc_ref):
    del be_ref
    b = pl.program_id(0)
    j = pl.program_id(1)
    active = b < na_ref[0]

    @pl.when(active & (j == 0))
    def _():
        h_ref[...] = xs_ref[...].astype(BF16)
        acc_ref[...] = jnp.zeros_like(acc_ref)

    @pl.when(active)
    def _():
        h = h_ref[...]
        w1 = w1_ref[...].astype(BF16)
        w3 = w3_ref[...].astype(BF16)
        gate = (_silu(_dot(h, w1)) * _dot(h, w3)).astype(BF16)
        acc_ref[...] += _dot(gate, w2_ref[...].astype(BF16))

    @pl.when(j == pl.num_programs(1) - 1)
    def _():
        y_ref[...] = jnp.where(active, acc_ref[...], 0.0)


def _experts(xs, block_expert, n_active, w1, w3, w2):
    n_rows, d = xs.shape
    ff = w1.shape[2]
    tf = _ff_tile(ff, 512)
    n_blocks = n_rows // MOE_BLOCK
    nj = ff // tf
    def wcol(b, j, be, na):
        live = b < na[0]
        return (be[b], 0, jnp.where(live, j, nj - 1))

    def wrow(b, j, be, na):
        live = b < na[0]
        return (be[b], jnp.where(live, j, nj - 1), 0)

    def xrow(b, j, be, na):
        return (jnp.minimum(b, na[0] - 1), 0)

    grid_spec = pltpu.PrefetchScalarGridSpec(
        num_scalar_prefetch=2,
        grid=(n_blocks, nj),
        in_specs=[pl.BlockSpec((MOE_BLOCK, d), xrow),
                  pl.BlockSpec((None, d, tf), wcol), pl.BlockSpec((None, d, tf), wcol),
                  pl.BlockSpec((None, tf, d), wrow)],
        out_specs=pl.BlockSpec((MOE_BLOCK, d), lambda b, j, be, na: (b, 0)),
        scratch_shapes=[pltpu.VMEM((MOE_BLOCK, d), BF16), pltpu.VMEM((MOE_BLOCK, d), F32)],
    )
    return pl.pallas_call(
        _expert_kernel,
        grid_spec=grid_spec,
        out_shape=jax.ShapeDtypeStruct((n_rows, d), F32),
        compiler_params=_cparams("arbitrary", "arbitrary"),
        name="moe_experts",
    )(block_expert, n_active, xs, w1, w3, w2)


def _combine_kernel(dest_ref, x_ref, gate_ref, y_ref, out_ref, buf, sem):
    rows = x_ref.shape[0]

    def issue(r, carry):
        for k in range(TOP_K):
            d = dest_ref[0, 0, r * TOP_K + k]
            pltpu.make_async_copy(y_ref.at[pl.ds(d, 1), :], buf.at[k, pl.ds(r, 1), :], sem).start()
        return carry

    lax.fori_loop(0, rows, issue, 0)
    for k in range(TOP_K):
        pltpu.make_async_copy(y_ref.at[pl.ds(0, rows), :], buf.at[k], sem).wait()
    gates = gate_ref[...]
    y = buf[0] * gates[:, 0:1]
    for k in range(1, TOP_K):
        y = y + buf[k] * gates[:, k:k + 1]
    out_ref[...] = x_ref[...] + y


def _combine(x, gates, dest, y):
    t, d = x.shape
    tg = DISPATCH_ROWS
    return pl.pallas_call(
        _combine_kernel,
        grid=(t // tg,),
        in_specs=[pl.BlockSpec((1, 1, tg * TOP_K), lambda i: (i, 0, 0), memory_space=pltpu.SMEM),
                  pl.BlockSpec((tg, d), lambda i: (i, 0)),
                  pl.BlockSpec((tg, TOP_K), lambda i: (i, 0)),
                  pl.BlockSpec(memory_space=pl.ANY)],
        out_specs=pl.BlockSpec((tg, d), lambda i: (i, 0)),
        out_shape=jax.ShapeDtypeStruct((t, d), F32),
        scratch_shapes=[pltpu.VMEM((TOP_K, tg, d), F32), pltpu.SemaphoreType.DMA],
        compiler_params=_cparams("arbitrary"),
        name="moe_combine",
    )(dest.reshape(t // tg, 1, tg * TOP_K), x, gates, y)


def _route(logits):
    t = logits.shape[0]
    top_val, top_idx = lax.top_k(logits[:, :N_EXPERTS], TOP_K)
    gates = jax.nn.softmax(top_val, axis=-1)
    expert = top_idx.reshape(-1)
    one_hot = (expert[:, None] == jnp.arange(N_EXPERTS, dtype=expert.dtype)[None, :]).astype(jnp.int32)
    running = jnp.cumsum(one_hot, axis=0)
    rank = jnp.sum(running * one_hot, axis=1) - 1
    counts = running[-1]
    padded = (counts + MOE_BLOCK - 1) // MOE_BLOCK * MOE_BLOCK
    pad_ends = jnp.cumsum(padded)
    pad_starts = pad_ends - padded
    dest = (jnp.sum(pad_starts[None, :] * one_hot, axis=1) + rank).astype(jnp.int32)
    n_blocks = -(-(t * TOP_K) // MOE_BLOCK) + N_EXPERTS
    block_start = jnp.arange(n_blocks, dtype=jnp.int32) * MOE_BLOCK
    block_expert = jnp.sum((block_start[:, None] >= pad_ends[None, :]).astype(jnp.int32), axis=1)
    block_expert = jnp.minimum(block_expert, N_EXPERTS - 1).astype(jnp.int32)
    n_active = (pad_ends[-1:] // MOE_BLOCK).astype(jnp.int32)
    return gates, dest, block_expert, n_active, n_blocks * MOE_BLOCK


def _moe(x, h2, logits, w1, w3, w2):
    gates, dest, block_expert, n_active, n_rows = _route(logits)
    xs = _dispatch(h2, dest, n_rows)
    y = _experts(xs, block_expert, n_active, w1, w3, w2)
    return _combine(x, gates, dest, y)


def _rope_tables(pos):
    inv_freq = ROPE_THETA ** (-jnp.arange(HALF_DIM, dtype=F32) / HALF_DIM)
    ang = pos.astype(F32)[:, None] * inv_freq[None, :]
    cos, sin = jnp.cos(ang), jnp.sin(ang)
    reps = LANES // HEAD_DIM
    return jnp.tile(cos, (1, 2 * reps)), jnp.tile(jnp.concatenate([-sin, sin], axis=1), (1, reps))


def _qk_gain_row(qk_gain):
    ones_a = jnp.ones((AW,), F32)
    ones_b = jnp.ones((KVW,), F32)
    return jnp.concatenate([jnp.tile(qk_gain[0], N_HEADS), jnp.tile(qk_gain[1], N_HEADS), ones_a,
                            jnp.tile(qk_gain[2], N_HEADS), jnp.tile(qk_gain[3], B_KV_HEADS), ones_b])[None, :]


def _row_block(t):
    return 512 if t % 512 == 0 else t


def kernel(x_prompt, x_sample, cache_a_k, cache_a_v, cache_b_k, cache_b_v, norm_mix, norm_ffn, ab_w_in,
           ab_qk_gain, ab_sinks, ab_w_out, ffn_w1, ffn_w3, ffn_w2, c_w_in, c_ln_g, c_ln_b, c_w_s, c_b_s,
           c_w_out, moe_router, moe_w1, moe_w3, moe_w2):
    n_p, s_p, d = x_prompt.shape
    n_s, t_new, _ = x_sample.shape
    depth = norm_mix.shape[0]
    past_len = 8192
    xp = x_prompt.reshape(n_p * s_p, d)
    xs = x_sample.reshape(n_s * t_new, d)
    pos_p = jnp.tile(jnp.arange(s_p), n_p)
    pos_s = jnp.tile(past_len + jnp.arange(t_new), n_s)
    cos_p, sin_p = _rope_tables(pos_p)
    cos_s, sin_s = _rope_tables(pos_s)
    row = lambda a: a[None, :]
    akp, avp, bkp, bvp, aks, avs, bks, bvs, cvs = [], [], [], [], [], [], [], [], []
    la_p, lb_p = min(BAND * A_DILATIONS[-1], s_p), min(BAND, s_p)
    for layer in range(depth):
        i = layer // 2
        if layer % 2 == 0:
            w_in = ab_w_in[i].astype(BF16)
            gain_row = _qk_gain_row(ab_qk_gain[i])
            w_out = ab_w_out[i].astype(BF16)
            w1, w3, w2 = ffn_w1[i].astype(BF16), ffn_w3[i].astype(BF16), ffn_w2[i].astype(BF16)
            sinks = ab_sinks[i].astype(F32)
            sink_rows = jnp.broadcast_to(sinks[:, None], (N_HEADS, LANES))
            sink_lane_row = jnp.pad(sinks, (0, LANES - N_HEADS))[None, :]
            qa, ka, va, qb, kb, vb = _qkv_proj(xp, row(norm_mix[layer]), w_in, gain_row, cos_p, sin_p,
                                               tm=_row_block(xp.shape[0]), q_dtype=BF16)
            seq = lambda a: a.reshape(n_p, s_p, a.shape[-1])
            state = None
            for r in A_DILATIONS:
                state = _attn_a_branch(seq(qa), seq(ka), seq(va), state, dilation=r, last=r == A_DILATIONS[-1])
            oa = state[0].reshape(n_p * s_p, AW)
            ob = _attn_b(seq(qb), seq(kb), seq(vb), sink_rows).reshape(n_p * s_p, AW)
            xp = _mix_out_ffn(xp, oa, ob, w_out, row(norm_ffn[layer]), w1, w3, w2, tm=_row_block(xp.shape[0]))
            heads = lambda a, nh: a.reshape(n_p, s_p, nh, HEAD_DIM)
            akp.append(heads(ka, N_HEADS)[:, s_p - la_p:])
            avp.append(heads(va, N_HEADS)[:, s_p - la_p:])
            bkp.append(heads(kb, B_KV_HEADS)[:, s_p - lb_p:])
            bvp.append(heads(vb, B_KV_HEADS)[:, s_p - lb_p:])
            qa, ka, va, qb, kb, vb = _qkv_proj(xs, row(norm_mix[layer]), w_in, gain_row, cos_s, sin_s,
                                               tm=_row_block(xs.shape[0]), q_dtype=F32)
            flat = lambda c: c.reshape(c.shape[0], c.shape[1], c.shape[2] * c.shape[3])
            oa, ob = _sample_attn(qa, ka, va, qb, kb, vb, flat(cache_a_k[i]), flat(cache_a_v[i]),
                                  flat(cache_b_k[i]), flat(cache_b_v[i]), sink_lane_row, t_new=t_new)
            xs = _mix_out_ffn(xs, oa, ob, w_out, row(norm_ffn[layer]), w1, w3, w2, tm=_row_block(xs.shape[0]))
            heads_s = lambda a, nh: a.reshape(n_s, t_new, nh, HEAD_DIM)
            aks.append(heads_s(ka, N_HEADS))
            avs.append(heads_s(va, N_HEADS))
            bks.append(heads_s(kb, B_KV_HEADS))
            bvs.append(heads_s(vb, B_KV_HEADS))
        else:
            w_in = c_w_in[i].astype(BF16)
            w_out = c_w_out[i].astype(BF16)
            tril = jnp.tril(jnp.ones((CHUNK, CHUNK), F32))
            mix_p = (c_w_s[i] * tril).astype(BF16)
            gw = w_out.shape[0] // C_GROUPS
            bias_p = jnp.repeat(c_b_s[i].T, gw, axis=1)
            per_tile = CHUNK // t_new
            mix_s = jnp.einsum("ab,gij->gaibj", jnp.eye(per_tile, dtype=F32), (c_w_s[i] * tril)[:, :t_new, :t_new])
            mix_s = mix_s.reshape(C_GROUPS, CHUNK, CHUNK).astype(BF16)
            bias_s = jnp.tile(bias_p[:t_new], (per_tile, 1))
            wr = moe_router[i]
            wr_hi = wr.astype(BF16)
            wr_lo = (wr - wr_hi.astype(F32)).astype(BF16)
            pad_e = lambda a: jnp.pad(a, ((0, 0), (0, LANES - N_EXPERTS)))
            wr_hi_lo = jnp.concatenate([pad_e(wr_hi), pad_e(wr_lo)], axis=1)
            common = (row(c_ln_g[i]), row(c_ln_b[i]))
            xp1, hp, lp = _gmlp_block(xp, row(norm_mix[layer]), w_in, *common, mix_p, bias_p, w_out,
                                      row(norm_ffn[layer]), wr_hi_lo, tm=_row_block(xp.shape[0]), write_v=False)
            xs1, hs, ls, v_new = _gmlp_block(xs, row(norm_mix[layer]), w_in, *common, mix_s, bias_s, w_out,
                                             row(norm_ffn[layer]), wr_hi_lo, tm=_row_block(xs.shape[0]),
                                             write_v=True)
            cvs.append(v_new.reshape(n_s, t_new, v_new.shape[-1]))
            x_all = jnp.concatenate([xp1, xs1], axis=0)
            h_all = jnp.concatenate([hp, hs], axis=0)
            l_all = jnp.concatenate([lp, ls], axis=0)
            y_all = _moe(x_all, h_all, l_all, moe_w1[i], moe_w3[i], moe_w2[i])
            xp, xs = y_all[:xp.shape[0]], y_all[xp.shape[0]:]
    return (xp.reshape(n_p, s_p, d), xs.reshape(n_s, t_new, d),
            jnp.stack(akp), jnp.stack(avp), jnp.stack(bkp), jnp.stack(bvp),
            jnp.stack(aks), jnp.stack(avs), jnp.stack(bks), jnp.stack(bvs),
            jnp.stack(cvs))
```

```python
import functools

import jax
import jax.numpy as jnp
from jax import lax
from jax.experimental import pallas as pl
from jax.experimental.pallas import tpu as pltpu

F32 = jnp.float32
BF16 = jnp.bfloat16

HEAD_DIM = 64
HALF_DIM = HEAD_DIM // 2
N_HEADS = 8
B_KV_HEADS = 2
A_DILATIONS = (1, 4, 16)
BAND = 128
ROPE_THETA = 10000.0
NORM_EPS = 1e-6
NEG_INF = -1e30
CHUNK = 128
C_GROUPS = 8
N_EXPERTS = 8
TOP_K = 2

LANES = 128
SUBLANES = 8
VMEM_LIMIT_BYTES = 56 * 1024 * 1024

AW = N_HEADS * HEAD_DIM
KVW = B_KV_HEADS * HEAD_DIM
HEADS_PER_VREG = LANES // HEAD_DIM
N_HEAD_BLOCKS = AW // LANES


def _cparams(*sem):
    return pltpu.CompilerParams(dimension_semantics=sem, vmem_limit_bytes=VMEM_LIMIT_BYTES)


def _rms_norm(x, g):
    return x * lax.rsqrt(jnp.mean(x * x, axis=-1, keepdims=True) + NORM_EPS) * g


def _dot(a, b):
    return jnp.dot(a, b, preferred_element_type=F32)


def _dot_nt(a, b):
    return lax.dot_general(a, b, (((1,), (1,)), ((), ())), preferred_element_type=F32)


_QA_BLOCKS = range(0, 4)
_KA_BLOCKS = range(4, 8)
_VA_BLOCKS = range(8, 12)
_QB_BLOCKS = range(12, 16)
_KB_BLOCK = 16
_VB_BLOCK = 17


def _proj_kernel(x_ref, g_ref, w_ref, gain_ref, cos_ref, sin_ref,
                 qa_ref, ka_ref, va_ref, qb_ref, kb_ref, vb_ref):
    h = _rms_norm(x_ref[...], g_ref[...]).astype(BF16)
    p = _dot(h, w_ref[...])
    tm = p.shape[0]
    cos = cos_ref[...]
    sin = sin_ref[...]
    lane = lax.broadcasted_iota(jnp.int32, (tm, LANES), 1)
    low_head = lane < HEAD_DIM
    first_half = (lane & HALF_DIM) == 0

    def norm_rope(blk):
        pc = p[:, blk * LANES:(blk + 1) * LANES]
        sq = pc * pc
        ms_lo = jnp.sum(jnp.where(low_head, sq, 0.0), axis=-1, keepdims=True) * (1.0 / HEAD_DIM)
        ms_hi = jnp.sum(jnp.where(low_head, 0.0, sq), axis=-1, keepdims=True) * (1.0 / HEAD_DIM)
        inv = jnp.where(low_head, lax.rsqrt(ms_lo + NORM_EPS), lax.rsqrt(ms_hi + NORM_EPS))
        y = pc * inv * gain_ref[:, blk * LANES:(blk + 1) * LANES]
        partner = jnp.where(first_half, pltpu.roll(y, LANES - HALF_DIM, 1), pltpu.roll(y, HALF_DIM, 1))
        return y * cos + partner * sin

    scale = HEAD_DIM ** -0.5
    for j, blk in enumerate(_QA_BLOCKS):
        qa_ref[:, j * LANES:(j + 1) * LANES] = (norm_rope(blk) * scale).astype(qa_ref.dtype)
    for j, blk in enumerate(_KA_BLOCKS):
        ka_ref[:, j * LANES:(j + 1) * LANES] = norm_rope(blk)
    for j, blk in enumerate(_VA_BLOCKS):
        va_ref[:, j * LANES:(j + 1) * LANES] = p[:, blk * LANES:(blk + 1) * LANES]
    for j, blk in enumerate(_QB_BLOCKS):
        qb_ref[:, j * LANES:(j + 1) * LANES] = (norm_rope(blk) * scale).astype(qb_ref.dtype)
    kb_ref[...] = norm_rope(_KB_BLOCK)
    vb_ref[...] = p[:, _VB_BLOCK * LANES:(_VB_BLOCK + 1) * LANES]


def _qkv_proj(x, g, w, gain_row, cos, sin, *, tm, q_dtype):
    t, d = x.shape
    n_out = w.shape[1]
    row = lambda i: (i, 0)
    fixed = lambda i: (0, 0)
    sds = jax.ShapeDtypeStruct
    return pl.pallas_call(
        _proj_kernel,
        grid=(t // tm,),
        in_specs=[pl.BlockSpec((tm, d), row), pl.BlockSpec((1, d), fixed), pl.BlockSpec((d, n_out), fixed),
                  pl.BlockSpec((1, n_out), fixed), pl.BlockSpec((tm, LANES), row), pl.BlockSpec((tm, LANES), row)],
        out_specs=[pl.BlockSpec((tm, AW), row), pl.BlockSpec((tm, AW), row), pl.BlockSpec((tm, AW), row),
                   pl.BlockSpec((tm, AW), row), pl.BlockSpec((tm, KVW), row), pl.BlockSpec((tm, KVW), row)],
        out_shape=[sds((t, AW), q_dtype), sds((t, AW), F32), sds((t, AW), F32),
                   sds((t, AW), q_dtype), sds((t, KVW), F32), sds((t, KVW), F32)],
        compiler_params=_cparams("arbitrary"),
        name="qkv_proj",
    )(x, g, w, gain_row, cos, sin)


def _fill_kv(kbuf, vbuf, kp_ref, kc_ref, vp_ref, vc_ref):
    kbuf[0:BAND, :] = kp_ref[...].astype(BF16)
    kbuf[BAND:, :] = kc_ref[...].astype(BF16)
    vbuf[0:BAND, :] = vp_ref[...].astype(BF16)
    vbuf[BAND:, :] = vc_ref[...].astype(BF16)


def _band_mask(first_block):
    row = lax.broadcasted_iota(jnp.int32, (BAND, 2 * BAND), 0)
    col = lax.broadcasted_iota(jnp.int32, (BAND, 2 * BAND), 1)
    dist = BAND + row - col
    first_key = jnp.where(first_block, BAND, 0)
    return (dist >= 0) & (dist <= BAND) & (col >= first_key)


def _masked_softmax_pv(qm, kc, vc, mask):
    s = jnp.where(mask, _dot_nt(qm, kc), NEG_INF)
    m = jnp.max(s, axis=-1, keepdims=True)
    p = jnp.exp(s - m)
    l = jnp.sum(p, axis=-1, keepdims=True)
    return m, l, _dot(p.astype(BF16), vc)


def _attn_a_kernel(q_ref, kc_ref, kp_ref, vc_ref, vp_ref, *rest, tq, has_prev, write_lse):
    rest = list(rest)
    if has_prev:
        op_ref, lp_ref = rest.pop(0), rest.pop(0)
    o_ref = rest.pop(0)
    if write_lse:
        lse_ref = rest.pop(0)
    kbuf, vbuf = rest
    blk = pl.program_id(2)
    _fill_kv(kbuf, vbuf, kp_ref, kc_ref, vp_ref, vc_ref)
    lane = lax.broadcasted_iota(jnp.int32, (BAND, LANES), 1)
    low_head = lane < HEAD_DIM

    def sub_block(sb, carry):
        r0 = pl.multiple_of(sb * BAND, BAND)
        mask = _band_mask((blk + sb) == 0)
        for hb in range(N_HEAD_BLOCKS):
            cs = slice(hb * LANES, (hb + 1) * LANES)
            kc = kbuf[pl.ds(r0, 2 * BAND), cs]
            vc = vbuf[pl.ds(r0, 2 * BAND), cs]
            q2 = q_ref[pl.ds(r0, BAND), cs]
            outs, lses = [], []
            for hh in range(HEADS_PER_VREG):
                sel = low_head if hh == 0 else jnp.logical_not(low_head)
                qm = jnp.where(sel, q2, jnp.zeros_like(q2))
                m, l, pv = _masked_softmax_pv(qm, kc, vc, mask)
                outs.append(pv / l)
                lses.append(m + jnp.log(l))
            oc = jnp.where(low_head, outs[0], outs[1])
            lc = jnp.where(low_head, lses[0], lses[1])
            if has_prev:
                lp = lp_ref[pl.ds(r0, BAND), cs]
                op = op_ref[pl.ds(r0, BAND), cs]
                mx = jnp.maximum(lp, lc)
                wp = jnp.exp(lp - mx)
                wc = jnp.exp(lc - mx)
                den = wp + wc
                oc = (op * wp + oc * wc) / den
                lc = mx + jnp.log(den)
            o_ref[pl.ds(r0, BAND), cs] = oc.astype(o_ref.dtype)
            if write_lse:
                lse_ref[pl.ds(r0, BAND), cs] = lc
        return carry

    lax.fori_loop(0, tq // BAND, sub_block, 0)


def _attn_a_branch(q, k, v, prev, *, dilation, last):
    n, s, w = q.shape
    r = dilation
    sub_len = s // r
    tq = min(512, sub_len)
    view = lambda a: a.reshape(n, sub_len, r * w)
    cur = lambda b, c, i: (b, i, c)
    prv = lambda b, c, i: (b, jnp.maximum(i * (tq // BAND) - 1, 0), c)
    blk_cur = pl.BlockSpec((None, tq, w), cur)
    blk_prv = pl.BlockSpec((None, BAND, w), prv)
    in_specs = [blk_cur, blk_cur, blk_prv, blk_cur, blk_prv]
    args = [view(q), view(k), view(k), view(v), view(v)]
    if prev is not None:
        in_specs += [blk_cur, blk_cur]
        args += [view(prev[0]), view(prev[1])]
    sds = jax.ShapeDtypeStruct
    out_specs = [blk_cur]
    out_shape = [sds((n, sub_len, r * w), BF16 if last else F32)]
    if not last:
        out_specs.append(blk_cur)
        out_shape.append(sds((n, sub_len, r * w), F32))
    outs = pl.pallas_call(
        functools.partial(_attn_a_kernel, tq=tq, has_prev=prev is not None, write_lse=not last),
        grid=(n, r, sub_len // tq),
        in_specs=in_specs, out_specs=out_specs, out_shape=out_shape,
        scratch_shapes=[pltpu.VMEM((tq + BAND, w), BF16), pltpu.VMEM((tq + BAND, w), BF16)],
        compiler_params=_cparams("arbitrary", "arbitrary", "arbitrary"),
        name=f"attn_a_dil{r}",
    )(*args)
    outs = [o.reshape(n, s, w) for o in outs]
    return outs[0], (None if last else outs[1])


def _attn_b_kernel(q_ref, kc_ref, kp_ref, vc_ref, vp_ref, sink_ref, o_ref, kbuf, vbuf, *, tq):
    blk = pl.program_id(1)
    _fill_kv(kbuf, vbuf, kp_ref, kc_ref, vp_ref, vc_ref)
    lane = lax.broadcasted_iota(jnp.int32, (BAND, LANES), 1)
    low_head = lane < HEAD_DIM
    group = N_HEADS // B_KV_HEADS

    def sub_block(sb, carry):
        r0 = pl.multiple_of(sb * BAND, BAND)
        mask = _band_mask((blk + sb) == 0)
        kc = kbuf[pl.ds(r0, 2 * BAND), :]
        vc = vbuf[pl.ds(r0, 2 * BAND), :]
        for hb in range(N_HEAD_BLOCKS):
            cs = slice(hb * LANES, (hb + 1) * LANES)
            q2 = q_ref[pl.ds(r0, BAND), cs].astype(F32)
            q2_swapped = pltpu.roll(q2, HEAD_DIM, 1)
            placed = []
            for hh in range(HEADS_PER_VREG):
                head = hb * HEADS_PER_VREG + hh
                kv = head // group
                kv_sel = low_head if kv == 0 else jnp.logical_not(low_head)
                q_at_kv = q2 if hh == kv else q2_swapped
                qm = jnp.where(kv_sel, q_at_kv, 0.0).astype(BF16)
                m, l, pv = _masked_softmax_pv(qm, kc, vc, mask)
                sink = sink_ref[head:head + 1, 0:1]
                m2 = jnp.maximum(m, sink)
                a = jnp.exp(m - m2)
                o = pv * (a / (l * a + jnp.exp(sink - m2)))
                placed.append(o if hh == kv else pltpu.roll(o, HEAD_DIM, 1))
            o_ref[pl.ds(r0, BAND), cs] = jnp.where(low_head, placed[0], placed[1]).astype(o_ref.dtype)
        return carry

    lax.fori_loop(0, tq // BAND, sub_block, 0)


def _attn_b(q, k, v, sinks_rows):
    n, s, w = q.shape
    tq = min(512, s)
    cur = lambda b, i: (b, i, 0)
    prv = lambda b, i: (b, jnp.maximum(i * (tq // BAND) - 1, 0), 0)
    kv_cur = pl.BlockSpec((None, tq, KVW), cur)
    kv_prv = pl.BlockSpec((None, BAND, KVW), prv)
    return pl.pallas_call(
        functools.partial(_attn_b_kernel, tq=tq),
        grid=(n, s // tq),
        in_specs=[pl.BlockSpec((None, tq, w), cur), kv_cur, kv_prv, kv_cur, kv_prv,
                  pl.BlockSpec((N_HEADS, LANES), lambda b, i: (0, 0))],
        out_specs=pl.BlockSpec((None, tq, w), cur),
        out_shape=jax.ShapeDtypeStruct((n, s, w), BF16),
        scratch_shapes=[pltpu.VMEM((tq + BAND, KVW), BF16), pltpu.VMEM((tq + BAND, KVW), BF16)],
        compiler_params=_cparams("arbitrary", "arbitrary"),
        name="attn_b",
    )(q, k, k, v, v, sinks_rows)


NEW_ROWS = 8


def _sample_attn_kernel(qa_ref, kna_ref, vna_ref, qb_ref, knb_ref, vnb_ref,
                        k16_ref, k4_ref, k1_ref, v16_ref, v4_ref, v1_ref, cbk_ref, cbv_ref, sink_ref,
                        oa_ref, ob_ref, *, nb, t_new):
    ones_dd = jnp.ones((HEAD_DIM, HEAD_DIM), BF16)
    pos_c = lax.broadcasted_iota(jnp.int32, (BAND, N_HEADS, HEAD_DIM), 0)
    pos_n = lax.broadcasted_iota(jnp.int32, (NEW_ROWS, N_HEADS, HEAD_DIM), 0)

    def scores_a(k, q):
        rows = k.shape[0]
        prod = (k * q[None]).reshape(rows * N_HEADS, HEAD_DIM).astype(BF16)
        return _dot(prod, ones_dd).reshape(rows, N_HEADS, HEAD_DIM)

    def attend_a(q, blocks):
        ss = []
        m = None
        for k, _, valid, _ in blocks:
            s = scores_a(k, q)
            if valid is not None:
                s = jnp.where(valid, s, NEG_INF)
            ss.append(s)
            bm = jnp.max(s, axis=0)
            m = bm if m is None else jnp.maximum(m, bm)
        l = 0.0
        acc = 0.0
        for s, (_, v, _, mult) in zip(ss, blocks):
            p = jnp.exp(s - m[None])
            if mult is not None:
                p = p * mult
            l = l + jnp.sum(p, axis=0)
            acc = acc + jnp.sum(p * v, axis=0)
        return acc / l

    lane_w = lax.broadcasted_iota(jnp.int32, (AW, LANES), 0) // HEAD_DIM
    head_w = lax.broadcasted_iota(jnp.int32, (AW, LANES), 1)
    head_sum = (lane_w == head_w).astype(BF16)
    head_e = lax.broadcasted_iota(jnp.int32, (LANES, AW), 0)
    lane_e = lax.broadcasted_iota(jnp.int32, (LANES, AW), 1) // HEAD_DIM
    head_expand = (head_e == lane_e).astype(BF16)
    row_c = lax.broadcasted_iota(jnp.int32, (BAND, 1), 0)
    row_n = lax.broadcasted_iota(jnp.int32, (NEW_ROWS, 1), 0)
    lane = lax.broadcasted_iota(jnp.int32, (BAND, LANES), 1)
    low_head = lane < HEAD_DIM
    low_head_n = low_head[:NEW_ROWS]
    sink_row = sink_ref[...]

    def scores(k, q_row):
        return _dot((k * q_row).astype(BF16), head_sum)

    def attend(q_row, blocks, sink=None):
        ss = []
        m = None
        for k, _, valid, _ in blocks:
            s = scores(k, q_row)
            if valid is not None:
                s = jnp.where(valid, s, NEG_INF)
            ss.append(s)
            bm = jnp.max(s, axis=0, keepdims=True)
            m = bm if m is None else jnp.maximum(m, bm)
        if sink is not None:
            m = jnp.maximum(m, sink)
        ps = []
        l = jnp.exp(sink - m) if sink is not None else 0.0
        for s, (_, _, valid, mult) in zip(ss, blocks):
            p = jnp.exp(s - m)
            if mult is not None:
                p = p * mult
            ps.append(p)
            l = l + jnp.sum(p, axis=0, keepdims=True)
        inv_l = 1.0 / l
        acc = None
        for p, (_, v, _, _) in zip(ps, blocks):
            pe = _dot((p * inv_l).astype(BF16), head_expand)
            part = jnp.sum(pe * v, axis=0, keepdims=True)
            acc = part if acc is None else acc + part
        return acc

    def expand_kv(x, low):
        sw = pltpu.roll(x, HEAD_DIM, 1)
        kv0 = jnp.where(low, x, sw)
        kv1 = jnp.where(low, sw, x)
        return jnp.concatenate([kv0, kv0, kv1, kv1], axis=1)

    def one_sequence(n, carry):
        new0 = pl.multiple_of(n * NEW_ROWS, NEW_ROWS)
        kna = kna_ref[n]
        vna = vna_ref[n]
        k1 = k1_ref[n]
        v1 = v1_ref[n]
        kb_c = expand_kv(cbk_ref[n], low_head)
        vb_c = expand_kv(cbv_ref[n], low_head)
        kb_n = expand_kv(knb_ref[pl.ds(new0, NEW_ROWS), :], low_head_n)
        vb_n = expand_kv(vnb_ref[pl.ds(new0, NEW_ROWS), :], low_head_n)
        qb = qb_ref[pl.ds(new0, NEW_ROWS), :]
        oa_ref[n] = jnp.zeros((NEW_ROWS, N_HEADS, HEAD_DIM), F32)
        out_b = jnp.zeros((NEW_ROWS, AW), F32)
        for t in range(t_new):
            mult_a = (pos_n <= t).astype(F32) + 2.0 * (pos_n == t).astype(F32)
            oa_ref[n, t] = attend_a(qa_ref[n, t], [
                (k16_ref[n, :, t], v16_ref[n, :, t], None, None),
                (k4_ref[n, :, t], v4_ref[n, :, t], None, None),
                (k1, v1, pos_c >= t, None),
                (kna, vna, pos_n <= t, mult_a),
            ])
            acc_b = attend(qb[t:t + 1, :], [
                (kb_c, vb_c, row_c >= t, None),
                (kb_n, vb_n, row_n <= t, None),
            ], sink=sink_row)
            out_b = jnp.where(row_n == t, acc_b, out_b)
        ob_ref[pl.ds(new0, NEW_ROWS), :] = out_b.astype(ob_ref.dtype)
        return carry

    lax.fori_loop(0, nb, one_sequence, 0)


def _pad_new_rows(a, n_seq, t_new):
    w = a.shape[-1]
    a = a.reshape(n_seq, t_new, w)
    a = jnp.pad(a, ((0, 0), (0, NEW_ROWS - t_new), (0, 0)))
    return a.reshape(n_seq * NEW_ROWS, w)


def _sample_attn(qa, ka, va, qb, kb, vb, cache_a_k, cache_a_v, cache_b_k, cache_b_v, sink_row, *, t_new):
    n_seq, la, nh, hd = cache_a_k.shape
    w = nh * hd
    assert la == BAND * A_DILATIONS[-1] and t_new <= min(A_DILATIONS[1], NEW_ROWS)
    nb = 2 if n_seq % 2 == 0 else 1
    heads = lambda a: _pad_new_rows(a, n_seq, t_new).reshape(n_seq, NEW_ROWS, nh, hd)
    qb_p, kb_p, vb_p = [_pad_new_rows(a, n_seq, t_new) for a in (qb, kb, vb)]
    r4, r16 = A_DILATIONS[1], A_DILATIONS[2]
    view16 = lambda c: c.reshape(n_seq, la // r16, r16, nh, hd)
    view4 = lambda c: c.reshape(n_seq, la // r4, r4, nh, hd)
    new_a = pl.BlockSpec((nb, NEW_ROWS, nh, hd), lambda i: (i, 0, 0, 0))
    new_b = pl.BlockSpec((nb * NEW_ROWS, KVW), lambda i: (i, 0))
    out_b = pl.BlockSpec((nb * NEW_ROWS, w), lambda i: (i, 0))
    spec16 = pl.BlockSpec((nb, BAND, r4, nh, hd), lambda i: (i, 0, 0, 0, 0))
    spec4 = pl.BlockSpec((nb, BAND, r4, nh, hd), lambda i: (i, la // r4 // BAND - 1, 0, 0, 0))
    spec1 = pl.BlockSpec((nb, BAND, nh, hd), lambda i: (i, la // BAND - 1, 0, 0))
    spec_cb = pl.BlockSpec((nb, BAND, KVW), lambda i: (i, 0, 0))
    sds = jax.ShapeDtypeStruct
    oa, ob = pl.pallas_call(
        functools.partial(_sample_attn_kernel, nb=nb, t_new=t_new),
        grid=(n_seq // nb,),
        in_specs=[new_a, new_a, new_a, out_b, new_b, new_b,
                  spec16, spec4, spec1, spec16, spec4, spec1, spec_cb, spec_cb,
                  pl.BlockSpec((1, LANES), lambda i: (0, 0))],
        out_specs=[new_a, out_b],
        out_shape=[sds((n_seq, NEW_ROWS, nh, hd), F32), sds((n_seq * NEW_ROWS, w), BF16)],
        compiler_params=_cparams("arbitrary"),
        name="sample_attn",
    )(heads(qa), heads(ka), heads(va), qb_p, kb_p, vb_p,
      view16(cache_a_k), view4(cache_a_k), cache_a_k, view16(cache_a_v), view4(cache_a_v), cache_a_v,
      cache_b_k, cache_b_v, sink_row)
    oa = oa[:, :t_new].reshape(n_seq * t_new, w).astype(BF16)
    ob = ob.reshape(n_seq, NEW_ROWS, w)[:, :t_new].reshape(n_seq * t_new, w)
    return oa, ob


def _silu(a):
    return a * (1.0 / (1.0 + jnp.exp(-a)))


def _mix_out_ffn_kernel(x_ref, oa_ref, ob_ref, wo_ref, g_ref, w1_ref, w3_ref, w2_ref, out_ref,
                        x1_ref, h_ref, acc_ref):
    j = pl.program_id(1)

    @pl.when(j == 0)
    def _():
        x1 = x_ref[...] + _dot(oa_ref[...], wo_ref[0:AW, :]) + _dot(ob_ref[...], wo_ref[AW:, :])
        x1_ref[...] = x1
        h_ref[...] = _rms_norm(x1, g_ref[...]).astype(BF16)
        acc_ref[...] = jnp.zeros_like(acc_ref)

    h = h_ref[...]
    gate = (_silu(_dot(h, w1_ref[...])) * _dot(h, w3_ref[...])).astype(BF16)
    acc_ref[...] += _dot(gate, w2_ref[...])

    @pl.when(j == pl.num_programs(1) - 1)
    def _():
        out_ref[...] = x1_ref[...] + acc_ref[...]


def _ff_tile(ff, target):
    best = LANES
    for k in range(1, ff // LANES + 1):
        if ff % (k * LANES) == 0 and k * LANES <= target:
            best = k * LANES
    return best


def _mix_out_ffn(x, oa, ob, wo, g, w1, w3, w2, *, tm):
    t, d = x.shape
    ff = w1.shape[1]
    tf = _ff_tile(ff, 1408)
    row = lambda i, j: (i, 0)
    fixed = lambda i, j: (0, 0)
    return pl.pallas_call(
        _mix_out_ffn_kernel,
        grid=(t // tm, ff // tf),
        in_specs=[pl.BlockSpec((tm, d), row), pl.BlockSpec((tm, AW), row), pl.BlockSpec((tm, AW), row),
                  pl.BlockSpec((2 * AW, d), fixed), pl.BlockSpec((1, d), fixed),
                  pl.BlockSpec((d, tf), lambda i, j: (0, j)), pl.BlockSpec((d, tf), lambda i, j: (0, j)),
                  pl.BlockSpec((tf, d), lambda i, j: (j, 0))],
        out_specs=pl.BlockSpec((tm, d), row),
        out_shape=jax.ShapeDtypeStruct((t, d), F32),
        scratch_shapes=[pltpu.VMEM((tm, d), F32), pltpu.VMEM((tm, d), BF16), pltpu.VMEM((tm, d), F32)],
        compiler_params=_cparams("arbitrary", "arbitrary"),
        name="mix_out_ffn",
    )(x, oa, ob, wo, g, w1, w3, w2)


def _gelu_exact(z):
    return 0.5 * z * (1.0 + lax.erf(z * (2.0 ** -0.5)))


def _gmlp_kernel(x_ref, g_ref, win_ref, lng_ref, lnb_ref, mix_ref, bias_ref, wout_ref, gffn_ref, wr_ref,
                 *out_refs, write_v):
    if write_v:
        xo_ref, h2_ref, logit_ref, v_ref = out_refs[:4]
    else:
        xo_ref, h2_ref, logit_ref = out_refs[:3]
    gate_ref = out_refs[-1]
    x = x_ref[...]
    tm, cd = x.shape
    h = _rms_norm(x, g_ref[...]).astype(BF16)
    z = _gelu_exact(_dot(h, win_ref[...]))
    u = z[:, :cd]
    v = z[:, cd:]
    mu = jnp.mean(v, axis=-1, keepdims=True)
    vc = v - mu
    v = vc * lax.rsqrt(jnp.mean(vc * vc, axis=-1, keepdims=True) + NORM_EPS) * lng_ref[...] + lnb_ref[...]
    if write_v:
        v_ref[...] = v
    vb = v.astype(BF16)
    bias = bias_ref[...]
    gw = cd // C_GROUPS
    for c in range(tm // CHUNK):
        rs = slice(c * CHUNK, (c + 1) * CHUNK)
        for g in range(C_GROUPS):
            cs = slice(g * gw, (g + 1) * gw)
            f = _dot(mix_ref[g], vb[rs, cs]) + bias[:, cs]
            gate_ref[rs, cs] = (u[rs, cs] * f).astype(BF16)
    xo = x + _dot(gate_ref[...], wout_ref[...])
    xo_ref[...] = xo
    h2 = _rms_norm(xo, gffn_ref[...])
    h2_ref[...] = h2
    h_hi = h2.astype(BF16)
    h_lo = (h2 - h_hi.astype(F32)).astype(BF16)
    wr = wr_ref[...]
    both = _dot(h_hi, wr)
    logit_ref[...] = both[:, :LANES] + both[:, LANES:] + _dot(h_lo, wr[:, :LANES])


def _gmlp_block(x, g, w_in, ln_g, ln_b, mix, bias_full, w_out, g_ffn, wr_hi_lo, *, tm, write_v):
    t, d = x.shape
    cd = w_out.shape[0]
    row = lambda i: (i, 0)
    fixed = lambda i: (0, 0)
    sds = jax.ShapeDtypeStruct
    out_specs = [pl.BlockSpec((tm, d), row), pl.BlockSpec((tm, d), row), pl.BlockSpec((tm, LANES), row)]
    out_shape = [sds((t, d), F32), sds((t, d), F32), sds((t, LANES), F32)]
    if write_v:
        out_specs.append(pl.BlockSpec((tm, cd), row))
        out_shape.append(sds((t, cd), F32))
    return pl.pallas_call(
        functools.partial(_gmlp_kernel, write_v=write_v),
        grid=(t // tm,),
        in_specs=[pl.BlockSpec((tm, d), row), pl.BlockSpec((1, d), fixed), pl.BlockSpec((d, 2 * cd), fixed),
                  pl.BlockSpec((1, cd), fixed), pl.BlockSpec((1, cd), fixed),
                  pl.BlockSpec((C_GROUPS, CHUNK, CHUNK), lambda i: (0, 0, 0)),
                  pl.BlockSpec((CHUNK, cd), fixed), pl.BlockSpec((cd, d), fixed), pl.BlockSpec((1, d), fixed),
                  pl.BlockSpec((d, 2 * LANES), fixed)],
        out_specs=out_specs, out_shape=out_shape,
        scratch_shapes=[pltpu.VMEM((tm, cd), BF16)],
        compiler_params=_cparams("arbitrary"),
        name="gmlp_block",
    )(x, g, w_in, ln_g, ln_b, mix, bias_full, w_out, g_ffn, wr_hi_lo)


MOE_BLOCK = 1024
DISPATCH_ROWS = 512


def _dispatch_kernel(dest_ref, h_ref, xs_in_ref, xs_ref, sem):
    del xs_in_ref
    rows = h_ref.shape[0]

    def issue(r, carry):
        for k in range(TOP_K):
            d = dest_ref[0, 0, r * TOP_K + k]
            pltpu.make_async_copy(h_ref.at[pl.ds(r, 1), :], xs_ref.at[pl.ds(d, 1), :], sem).start()
        return carry

    lax.fori_loop(0, rows, issue, 0)
    for _ in range(TOP_K):
        pltpu.make_async_copy(h_ref, xs_ref.at[pl.ds(0, rows), :], sem).wait()


def _dispatch(h, dest, n_rows):
    t, d = h.shape
    tg = DISPATCH_ROWS
    xs0 = jnp.zeros((n_rows, d), F32)
    return pl.pallas_call(
        _dispatch_kernel,
        grid=(t // tg,),
        in_specs=[pl.BlockSpec((1, 1, tg * TOP_K), lambda i: (i, 0, 0), memory_space=pltpu.SMEM),
                  pl.BlockSpec((tg, d), lambda i: (i, 0)),
                  pl.BlockSpec(memory_space=pl.ANY)],
        out_specs=pl.BlockSpec(memory_space=pl.ANY),
        out_shape=jax.ShapeDtypeStruct((n_rows, d), F32),
        scratch_shapes=[pltpu.SemaphoreType.DMA],
        input_output_aliases={2: 0},
        compiler_params=_cparams("arbitrary"),
        name="moe_dispatch",
    )(dest.reshape(t // tg, 1, tg * TOP_K), h, xs0)


def _expert_kernel(be_ref, na_ref, xs_ref, w1_ref, w3_ref, w2_ref, y_ref, h_ref, acc_ref):
    del be_ref
    b = pl.program_id(0)
    j = pl.program_id(1)
    active = b < na_ref[0]

    @pl.when(active & (j == 0))
    def _():
        h_ref[...] = xs_ref[...].astype(BF16)
        acc_ref[...] = jnp.zeros_like(acc_ref)

    @pl.when(active)
    def _():
        h = h_ref[...]
        w1 = w1_ref[...].astype(BF16)
        w3 = w3_ref[...].astype(BF16)
        gate = (_silu(_dot(h, w1)) * _dot(h, w3)).astype(BF16)
        acc_ref[...] += _dot(gate, w2_ref[...].astype(BF16))

    @pl.when(j == pl.num_programs(1) - 1)
    def _():
        y_ref[...] = jnp.where(active, acc_ref[...], 0.0)


def _experts(xs, block_expert, n_active, w1, w3, w2):
    n_rows, d = xs.shape
    ff = w1.shape[2]
    tf = _ff_tile(ff, 512)
    n_blocks = n_rows // MOE_BLOCK
    nj = ff // tf
    def wcol(b, j, be, na):
        live = b < na[0]
        return (be[b], 0, jnp.where(live, j, nj - 1))

    def wrow(b, j, be, na):
        live = b < na[0]
        return (be[b], jnp.where(live, j, nj - 1), 0)

    def xrow(b, j, be, na):
        return (jnp.minimum(b, na[0] - 1), 0)

    grid_spec = pltpu.PrefetchScalarGridSpec(
        num_scalar_prefetch=2,
        grid=(n_blocks, nj),
        in_specs=[pl.BlockSpec((MOE_BLOCK, d), xrow),
                  pl.BlockSpec((None, d, tf), wcol), pl.BlockSpec((None, d, tf), wcol),
                  pl.BlockSpec((None, tf, d), wrow)],
        out_specs=pl.BlockSpec((MOE_BLOCK, d), lambda b, j, be, na: (b, 0)),
        scratch_shapes=[pltpu.VMEM((MOE_BLOCK, d), BF16), pltpu.VMEM((MOE_BLOCK, d), F32)],
    )
    return pl.pallas_call(
        _expert_kernel,
        grid_spec=grid_spec,
        out_shape=jax.ShapeDtypeStruct((n_rows, d), F32),
        compiler_params=_cparams("arbitrary", "arbitrary"),
        name="moe_experts",
    )(block_expert, n_active, xs, w1, w3, w2)


def _combine_kernel(dest_ref, x_ref, gate_ref, y_ref, out_ref, buf, sem):
    rows = x_ref.shape[0]

    def issue(r, carry):
        for k in range(TOP_K):
            d = dest_ref[0, 0, r * TOP_K + k]
            pltpu.make_async_copy(y_ref.at[pl.ds(d, 1), :], buf.at[k, pl.ds(r, 1), :], sem).start()
        return carry

    lax.fori_loop(0, rows, issue, 0)
    for k in range(TOP_K):
        pltpu.make_async_copy(y_ref.at[pl.ds(0, rows), :], buf.at[k], sem).wait()
    gates = gate_ref[...]
    y = buf[0] * gates[:, 0:1]
    for k in range(1, TOP_K):
        y = y + buf[k] * gates[:, k:k + 1]
    out_ref[...] = x_ref[...] + y


def _combine(x, gates, dest, y):
    t, d = x.shape
    tg = DISPATCH_ROWS
    return pl.pallas_call(
        _combine_kernel,
        grid=(t // tg,),
        in_specs=[pl.BlockSpec((1, 1, tg * TOP_K), lambda i: (i, 0, 0), memory_space=pltpu.SMEM),
                  pl.BlockSpec((tg, d), lambda i: (i, 0)),
                  pl.BlockSpec((tg, TOP_K), lambda i: (i, 0)),
                  pl.BlockSpec(memory_space=pl.ANY)],
        out_specs=pl.BlockSpec((tg, d), lambda i: (i, 0)),
        out_shape=jax.ShapeDtypeStruct((t, d), F32),
        scratch_shapes=[pltpu.VMEM((TOP_K, tg, d), F32), pltpu.SemaphoreType.DMA],
        compiler_params=_cparams("arbitrary"),
        name="moe_combine",
    )(dest.reshape(t // tg, 1, tg * TOP_K), x, gates, y)


def _route(logits):
    t = logits.shape[0]
    top_val, top_idx = lax.top_k(logits[:, :N_EXPERTS], TOP_K)
    gates = jax.nn.softmax(top_val, axis=-1)
    expert = top_idx.reshape(-1)
    one_hot = (expert[:, None] == jnp.arange(N_EXPERTS, dtype=expert.dtype)[None, :]).astype(jnp.int32)
    running = jnp.cumsum(one_hot, axis=0)
    rank = jnp.sum(running * one_hot, axis=1) - 1
    counts = running[-1]
    padded = (counts + MOE_BLOCK - 1) // MOE_BLOCK * MOE_BLOCK
    pad_ends = jnp.cumsum(padded)
    pad_starts = pad_ends - padded
    dest = (jnp.sum(pad_starts[None, :] * one_hot, axis=1) + rank).astype(jnp.int32)
    n_blocks = -(-(t * TOP_K) // MOE_BLOCK) + N_EXPERTS
    block_start = jnp.arange(n_blocks, dtype=jnp.int32) * MOE_BLOCK
    block_expert = jnp.sum((block_start[:, None] >= pad_ends[None, :]).astype(jnp.int32), axis=1)
    block_expert = jnp.minimum(block_expert, N_EXPERTS - 1).astype(jnp.int32)
    n_active = (pad_ends[-1:] // MOE_BLOCK).astype(jnp.int32)
    return gates, dest, block_expert, n_active, n_blocks * MOE_BLOCK


def _moe(x, h2, logits, w1, w3, w2):
    gates, dest, block_expert, n_active, n_rows = _route(logits)
    xs = _dispatch(h2, dest, n_rows)
    y = _experts(xs, block_expert, n_active, w1, w3, w2)
    return _combine(x, gates, dest, y)


def _rope_tables(pos):
    inv_freq = ROPE_THETA ** (-jnp.arange(HALF_DIM, dtype=F32) / HALF_DIM)
    ang = pos.astype(F32)[:, None] * inv_freq[None, :]
    cos, sin = jnp.cos(ang), jnp.sin(ang)
    reps = LANES // HEAD_DIM
    return jnp.tile(cos, (1, 2 * reps)), jnp.tile(jnp.concatenate([-sin, sin], axis=1), (1, reps))


def _qk_gain_row(qk_gain):
    ones_a = jnp.ones((AW,), F32)
    ones_b = jnp.ones((KVW,), F32)
    return jnp.concatenate([jnp.tile(qk_gain[0], N_HEADS), jnp.tile(qk_gain[1], N_HEADS), ones_a,
                            jnp.tile(qk_gain[2], N_HEADS), jnp.tile(qk_gain[3], B_KV_HEADS), ones_b])[None, :]


def _row_block(t):
    return 512 if t % 512 == 0 else t


def kernel(x_prompt, x_sample, cache_a_k, cache_a_v, cache_b_k, cache_b_v, norm_mix, norm_ffn, ab_w_in,
           ab_qk_gain, ab_sinks, ab_w_out, ffn_w1, ffn_w3, ffn_w2, c_w_in, c_ln_g, c_ln_b, c_w_s, c_b_s,
           c_w_out, moe_router, moe_w1, moe_w3, moe_w2):
    n_p, s_p, d = x_prompt.shape
    n_s, t_new, _ = x_sample.shape
    depth = norm_mix.shape[0]
    past_len = 8192
    xp = x_prompt.reshape(n_p * s_p, d)
    xs = x_sample.reshape(n_s * t_new, d)
    pos_p = jnp.tile(jnp.arange(s_p), n_p)
    pos_s = jnp.tile(past_len + jnp.arange(t_new), n_s)
    cos_p, sin_p = _rope_tables(pos_p)
    cos_s, sin_s = _rope_tables(pos_s)
    row = lambda a: a[None, :]
    akp, avp, bkp, bvp, aks, avs, bks, bvs, cvs = [], [], [], [], [], [], [], [], []
    la_p, lb_p = min(BAND * A_DILATIONS[-1], s_p), min(BAND, s_p)
    for layer in range(depth):
        i = layer // 2
        if layer % 2 == 0:
            w_in = ab_w_in[i].astype(BF16)
            gain_row = _qk_gain_row(ab_qk_gain[i])
            w_out = ab_w_out[i].astype(BF16)
            w1, w3, w2 = ffn_w1[i].astype(BF16), ffn_w3[i].astype(BF16), ffn_w2[i].astype(BF16)
            sinks = ab_sinks[i].astype(F32)
            sink_rows = jnp.broadcast_to(sinks[:, None], (N_HEADS, LANES))
            sink_lane_row = jnp.pad(sinks, (0, LANES - N_HEADS))[None, :]
            qa, ka, va, qb, kb, vb = _qkv_proj(xp, row(norm_mix[layer]), w_in, gain_row, cos_p, sin_p,
                                               tm=_row_block(xp.shape[0]), q_dtype=BF16)
            seq = lambda a: a.reshape(n_p, s_p, a.shape[-1])
            state = None
            for r in A_DILATIONS:
                state = _attn_a_branch(seq(qa), seq(ka), seq(va), state, dilation=r, last=r == A_DILATIONS[-1])
            oa = state[0].reshape(n_p * s_p, AW)
            ob = _attn_b(seq(qb), seq(kb), seq(vb), sink_rows).reshape(n_p * s_p, AW)
            xp = _mix_out_ffn(xp, oa, ob, w_out, row(norm_ffn[layer]), w1, w3, w2, tm=_row_block(xp.shape[0]))
            heads = lambda a, nh: a.reshape(n_p, s_p, nh, HEAD_DIM)
            akp.append(heads(ka, N_HEADS)[:, s_p - la_p:])
            avp.append(heads(va, N_HEADS)[:, s_p - la_p:])
            bkp.append(heads(kb, B_KV_HEADS)[:, s_p - lb_p:])
            bvp.append(heads(vb, B_KV_HEADS)[:, s_p - lb_p:])
            qa, ka, va, qb, kb, vb = _qkv_proj(xs, row(norm_mix[layer]), w_in, gain_row, cos_s, sin_s,
                                               tm=_row_block(xs.shape[0]), q_dtype=F32)
            flat = lambda c: c.reshape(c.shape[0], c.shape[1], c.shape[2] * c.shape[3])
            oa, ob = _sample_attn(qa, ka, va, qb, kb, vb, cache_a_k[i], cache_a_v[i],
                                  flat(cache_b_k[i]), flat(cache_b_v[i]), sink_lane_row, t_new=t_new)
            xs = _mix_out_ffn(xs, oa, ob, w_out, row(norm_ffn[layer]), w1, w3, w2, tm=_row_block(xs.shape[0]))
            heads_s = lambda a, nh: a.reshape(n_s, t_new, nh, HEAD_DIM)
            aks.append(heads_s(ka, N_HEADS))
            avs.append(heads_s(va, N_HEADS))
            bks.append(heads_s(kb, B_KV_HEADS))
            bvs.append(heads_s(vb, B_KV_HEADS))
        else:
            w_in = c_w_in[i].astype(BF16)
            w_out = c_w_out[i].astype(BF16)
            tril = jnp.tril(jnp.ones((CHUNK, CHUNK), F32))
            mix_p = (c_w_s[i] * tril).astype(BF16)
            gw = w_out.shape[0] // C_GROUPS
            bias_p = jnp.repeat(c_b_s[i].T, gw, axis=1)
            per_tile = CHUNK // t_new
            mix_s = jnp.einsum("ab,gij->gaibj", jnp.eye(per_tile, dtype=F32), (c_w_s[i] * tril)[:, :t_new, :t_new])
            mix_s = mix_s.reshape(C_GROUPS, CHUNK, CHUNK).astype(BF16)
            bias_s = jnp.tile(bias_p[:t_new], (per_tile, 1))
            wr = moe_router[i]
            wr_hi = wr.astype(BF16)
            wr_lo = (wr - wr_hi.astype(F32)).astype(BF16)
            pad_e = lambda a: jnp.pad(a, ((0, 0), (0, LANES - N_EXPERTS)))
            wr_hi_lo = jnp.concatenate([pad_e(wr_hi), pad_e(wr_lo)], axis=1)
            common = (row(c_ln_g[i]), row(c_ln_b[i]))
            xp1, hp, lp = _gmlp_block(xp, row(norm_mix[layer]), w_in, *common, mix_p, bias_p, w_out,
                                      row(norm_ffn[layer]), wr_hi_lo, tm=_row_block(xp.shape[0]), write_v=False)
            xs1, hs, ls, v_new = _gmlp_block(xs, row(norm_mix[layer]), w_in, *common, mix_s, bias_s, w_out,
                                             row(norm_ffn[layer]), wr_hi_lo, tm=_row_block(xs.shape[0]),
                                             write_v=True)
            cvs.append(v_new.reshape(n_s, t_new, v_new.shape[-1]))
            x_all = jnp.concatenate([xp1, xs1], axis=0)
            h_all = jnp.concatenate([hp, hs], axis=0)
            l_all = jnp.concatenate([lp, ls], axis=0)
            y_all = _moe(x_all, h_all, l_all, moe_w1[i], moe_w3[i], moe_w2[i])
            xp, xs = y_all[:xp.shape[0]], y_all[xp.shape[0]:]
    return (xp.reshape(n_p, s_p, d), xs.reshape(n_s, t_new, d),
            jnp.stack(akp), jnp.stack(avp), jnp.stack(bkp), jnp.stack(bvp),
            jnp.stack(aks), jnp.stack(avs), jnp.stack(bks), jnp.stack(bvs),
            jnp.stack(cvs))
```

```python
import functools

import jax
import jax.numpy as jnp
from jax import lax
from jax.experimental import pallas as pl
from jax.experimental.pallas import tpu as pltpu

F32 = jnp.float32
BF16 = jnp.bfloat16

HEAD_DIM = 64
HALF_DIM = HEAD_DIM // 2
N_HEADS = 8
B_KV_HEADS = 2
A_DILATIONS = (1, 4, 16)
BAND = 128
ROPE_THETA = 10000.0
NORM_EPS = 1e-6
NEG_INF = -1e30
CHUNK = 128
C_GROUPS = 8
N_EXPERTS = 8
TOP_K = 2

LANES = 128
SUBLANES = 8
VMEM_LIMIT_BYTES = 56 * 1024 * 1024

AW = N_HEADS * HEAD_DIM
KVW = B_KV_HEADS * HEAD_DIM
HEADS_PER_VREG = LANES // HEAD_DIM
N_HEAD_BLOCKS = AW // LANES


def _cparams(*sem):
    return pltpu.CompilerParams(dimension_semantics=sem, vmem_limit_bytes=VMEM_LIMIT_BYTES)


def _rms_norm(x, g):
    return x * lax.rsqrt(jnp.mean(x * x, axis=-1, keepdims=True) + NORM_EPS) * g


def _dot(a, b):
    return jnp.dot(a, b, preferred_element_type=F32)


def _dot_nt(a, b):
    return lax.dot_general(a, b, (((1,), (1,)), ((), ())), preferred_element_type=F32)


_QA_BLOCKS = range(0, 4)
_KA_BLOCKS = range(4, 8)
_VA_BLOCKS = range(8, 12)
_QB_BLOCKS = range(12, 16)
_KB_BLOCK = 16
_VB_BLOCK = 17


def _proj_kernel(x_ref, g_ref, w_ref, gain_ref, cos_ref, sin_ref,
                 qa_ref, ka_ref, va_ref, qb_ref, kb_ref, vb_ref):
    h = _rms_norm(x_ref[...], g_ref[...]).astype(BF16)
    p = _dot(h, w_ref[...])
    tm = p.shape[0]
    cos = cos_ref[...]
    sin = sin_ref[...]
    lane = lax.broadcasted_iota(jnp.int32, (tm, LANES), 1)
    low_head = lane < HEAD_DIM
    first_half = (lane & HALF_DIM) == 0

    def norm_rope(blk):
        pc = p[:, blk * LANES:(blk + 1) * LANES]
        sq = pc * pc
        ms_lo = jnp.sum(jnp.where(low_head, sq, 0.0), axis=-1, keepdims=True) * (1.0 / HEAD_DIM)
        ms_hi = jnp.sum(jnp.where(low_head, 0.0, sq), axis=-1, keepdims=True) * (1.0 / HEAD_DIM)
        inv = jnp.where(low_head, lax.rsqrt(ms_lo + NORM_EPS), lax.rsqrt(ms_hi + NORM_EPS))
        y = pc * inv * gain_ref[:, blk * LANES:(blk + 1) * LANES]
        partner = jnp.where(first_half, pltpu.roll(y, LANES - HALF_DIM, 1), pltpu.roll(y, HALF_DIM, 1))
        return y * cos + partner * sin

    scale = HEAD_DIM ** -0.5
    for j, blk in enumerate(_QA_BLOCKS):
        qa_ref[:, j * LANES:(j + 1) * LANES] = (norm_rope(blk) * scale).astype(qa_ref.dtype)
    for j, blk in enumerate(_KA_BLOCKS):
        ka_ref[:, j * LANES:(j + 1) * LANES] = norm_rope(blk)
    for j, blk in enumerate(_VA_BLOCKS):
        va_ref[:, j * LANES:(j + 1) * LANES] = p[:, blk * LANES:(blk + 1) * LANES]
    for j, blk in enumerate(_QB_BLOCKS):
        qb_ref[:, j * LANES:(j + 1) * LANES] = (norm_rope(blk) * scale).astype(qb_ref.dtype)
    kb_ref[...] = norm_rope(_KB_BLOCK)
    vb_ref[...] = p[:, _VB_BLOCK * LANES:(_VB_BLOCK + 1) * LANES]


def _qkv_proj(x, g, w, gain_row, cos, sin, *, tm, q_dtype):
    t, d = x.shape
    n_out = w.shape[1]
    row = lambda i: (i, 0)
    fixed = lambda i: (0, 0)
    sds = jax.ShapeDtypeStruct
    return pl.pallas_call(
        _proj_kernel,
        grid=(t // tm,),
        in_specs=[pl.BlockSpec((tm, d), row), pl.BlockSpec((1, d), fixed), pl.BlockSpec((d, n_out), fixed),
                  pl.BlockSpec((1, n_out), fixed), pl.BlockSpec((tm, LANES), row), pl.BlockSpec((tm, LANES), row)],
        out_specs=[pl.BlockSpec((tm, AW), row), pl.BlockSpec((tm, AW), row), pl.BlockSpec((tm, AW), row),
                   pl.BlockSpec((tm, AW), row), pl.BlockSpec((tm, KVW), row), pl.BlockSpec((tm, KVW), row)],
        out_shape=[sds((t, AW), q_dtype), sds((t, AW), F32), sds((t, AW), F32),
                   sds((t, AW), q_dtype), sds((t, KVW), F32), sds((t, KVW), F32)],
        compiler_params=_cparams("arbitrary"),
        name="qkv_proj",
    )(x, g, w, gain_row, cos, sin)


def _fill_kv(kbuf, vbuf, kp_ref, kc_ref, vp_ref, vc_ref):
    kbuf[0:BAND, :] = kp_ref[...].astype(BF16)
    kbuf[BAND:, :] = kc_ref[...].astype(BF16)
    vbuf[0:BAND, :] = vp_ref[...].astype(BF16)
    vbuf[BAND:, :] = vc_ref[...].astype(BF16)


def _band_mask(first_block):
    row = lax.broadcasted_iota(jnp.int32, (BAND, 2 * BAND), 0)
    col = lax.broadcasted_iota(jnp.int32, (BAND, 2 * BAND), 1)
    dist = BAND + row - col
    first_key = jnp.where(first_block, BAND, 0)
    return (dist >= 0) & (dist <= BAND) & (col >= first_key)


def _masked_softmax_pv(qm, kc, vc, mask):
    s = jnp.where(mask, _dot_nt(qm, kc), NEG_INF)
    m = jnp.max(s, axis=-1, keepdims=True)
    p = jnp.exp(s - m)
    l = jnp.sum(p, axis=-1, keepdims=True)
    return m, l, _dot(p.astype(BF16), vc)


def _attn_a_kernel(q_ref, kc_ref, kp_ref, vc_ref, vp_ref, *rest, tq, has_prev, write_lse):
    rest = list(rest)
    if has_prev:
        op_ref, lp_ref = rest.pop(0), rest.pop(0)
    o_ref = rest.pop(0)
    if write_lse:
        lse_ref = rest.pop(0)
    kbuf, vbuf = rest
    blk = pl.program_id(2)
    _fill_kv(kbuf, vbuf, kp_ref, kc_ref, vp_ref, vc_ref)
    lane = lax.broadcasted_iota(jnp.int32, (BAND, LANES), 1)
    low_head = lane < HEAD_DIM

    def sub_block(sb, carry):
        r0 = pl.multiple_of(sb * BAND, BAND)
        mask = _band_mask((blk + sb) == 0)
        for hb in range(N_HEAD_BLOCKS):
            cs = slice(hb * LANES, (hb + 1) * LANES)
            kc = kbuf[pl.ds(r0, 2 * BAND), cs]
            vc = vbuf[pl.ds(r0, 2 * BAND), cs]
            q2 = q_ref[pl.ds(r0, BAND), cs]
            outs, lses = [], []
            for hh in range(HEADS_PER_VREG):
                sel = low_head if hh == 0 else jnp.logical_not(low_head)
                qm = jnp.where(sel, q2, jnp.zeros_like(q2))
                m, l, pv = _masked_softmax_pv(qm, kc, vc, mask)
                outs.append(pv / l)
                lses.append(m + jnp.log(l))
            oc = jnp.where(low_head, outs[0], outs[1])
            lc = jnp.where(low_head, lses[0], lses[1])
            if has_prev:
                lp = lp_ref[pl.ds(r0, BAND), cs]
                op = op_ref[pl.ds(r0, BAND), cs]
                mx = jnp.maximum(lp, lc)
                wp = jnp.exp(lp - mx)
                wc = jnp.exp(lc - mx)
                den = wp + wc
                oc = (op * wp + oc * wc) / den
                lc = mx + jnp.log(den)
            o_ref[pl.ds(r0, BAND), cs] = oc.astype(o_ref.dtype)
            if write_lse:
                lse_ref[pl.ds(r0, BAND), cs] = lc
        return carry

    lax.fori_loop(0, tq // BAND, sub_block, 0)


def _attn_a_branch(q, k, v, prev, *, dilation, last):
    n, s, w = q.shape
    r = dilation
    sub_len = s // r
    tq = min(512, sub_len)
    view = lambda a: a.reshape(n, sub_len, r * w)
    cur = lambda b, c, i: (b, i, c)
    prv = lambda b, c, i: (b, jnp.maximum(i * (tq // BAND) - 1, 0), c)
    blk_cur = pl.BlockSpec((None, tq, w), cur)
    blk_prv = pl.BlockSpec((None, BAND, w), prv)
    in_specs = [blk_cur, blk_cur, blk_prv, blk_cur, blk_prv]
    args = [view(q), view(k), view(k), view(v), view(v)]
    if prev is not None:
        in_specs += [blk_cur, blk_cur]
        args += [view(prev[0]), view(prev[1])]
    sds = jax.ShapeDtypeStruct
    out_specs = [blk_cur]
    out_shape = [sds((n, sub_len, r * w), BF16 if last else F32)]
    if not last:
        out_specs.append(blk_cur)
        out_shape.append(sds((n, sub_len, r * w), F32))
    outs = pl.pallas_call(
        functools.partial(_attn_a_kernel, tq=tq, has_prev=prev is not None, write_lse=not last),
        grid=(n, r, sub_len // tq),
        in_specs=in_specs, out_specs=out_specs, out_shape=out_shape,
        scratch_shapes=[pltpu.VMEM((tq + BAND, w), BF16), pltpu.VMEM((tq + BAND, w), BF16)],
        compiler_params=_cparams("arbitrary", "arbitrary", "arbitrary"),
        name=f"attn_a_dil{r}",
    )(*args)
    outs = [o.reshape(n, s, w) for o in outs]
    return outs[0], (None if last else outs[1])


def _attn_b_kernel(q_ref, kc_ref, kp_ref, vc_ref, vp_ref, sink_ref, o_ref, kbuf, vbuf, *, tq):
    blk = pl.program_id(1)
    _fill_kv(kbuf, vbuf, kp_ref, kc_ref, vp_ref, vc_ref)
    lane = lax.broadcasted_iota(jnp.int32, (BAND, LANES), 1)
    low_head = lane < HEAD_DIM
    group = N_HEADS // B_KV_HEADS

    def sub_block(sb, carry):
        r0 = pl.multiple_of(sb * BAND, BAND)
        mask = _band_mask((blk + sb) == 0)
        kc = kbuf[pl.ds(r0, 2 * BAND), :]
        vc = vbuf[pl.ds(r0, 2 * BAND), :]
        for hb in range(N_HEAD_BLOCKS):
            cs = slice(hb * LANES, (hb + 1) * LANES)
            q2 = q_ref[pl.ds(r0, BAND), cs].astype(F32)
            q2_swapped = pltpu.roll(q2, HEAD_DIM, 1)
            placed = []
            for hh in range(HEADS_PER_VREG):
                head = hb * HEADS_PER_VREG + hh
                kv = head // group
                kv_sel = low_head if kv == 0 else jnp.logical_not(low_head)
                q_at_kv = q2 if hh == kv else q2_swapped
                qm = jnp.where(kv_sel, q_at_kv, 0.0).astype(BF16)
                m, l, pv = _masked_softmax_pv(qm, kc, vc, mask)
                sink = sink_ref[head:head + 1, 0:1]
                m2 = jnp.maximum(m, sink)
                a = jnp.exp(m - m2)
                o = pv * (a / (l * a + jnp.exp(sink - m2)))
                placed.append(o if hh == kv else pltpu.roll(o, HEAD_DIM, 1))
            o_ref[pl.ds(r0, BAND), cs] = jnp.where(low_head, placed[0], placed[1]).astype(o_ref.dtype)
        return carry

    lax.fori_loop(0, tq // BAND, sub_block, 0)


def _attn_b(q, k, v, sinks_rows):
    n, s, w = q.shape
    tq = min(512, s)
    cur = lambda b, i: (b, i, 0)
    prv = lambda b, i: (b, jnp.maximum(i * (tq // BAND) - 1, 0), 0)
    kv_cur = pl.BlockSpec((None, tq, KVW), cur)
    kv_prv = pl.BlockSpec((None, BAND, KVW), prv)
    return pl.pallas_call(
        functools.partial(_attn_b_kernel, tq=tq),
        grid=(n, s // tq),
        in_specs=[pl.BlockSpec((None, tq, w), cur), kv_cur, kv_prv, kv_cur, kv_prv,
                  pl.BlockSpec((N_HEADS, LANES), lambda b, i: (0, 0))],
        out_specs=pl.BlockSpec((None, tq, w), cur),
        out_shape=jax.ShapeDtypeStruct((n, s, w), BF16),
        scratch_shapes=[pltpu.VMEM((tq + BAND, KVW), BF16), pltpu.VMEM((tq + BAND, KVW), BF16)],
        compiler_params=_cparams("arbitrary", "arbitrary"),
        name="attn_b",
    )(q, k, k, v, v, sinks_rows)


NEW_ROWS = 8


def _branch_multiplicity(t_query, key_pos, la):
    back = la + t_query - key_pos
    mult = jnp.zeros(back.shape, F32)
    for r in A_DILATIONS:
        mult = mult + ((back >= 0) & (back <= BAND * r) & (back % r == 0)).astype(F32)
    return mult


def _sample_attn_kernel(qa_ref, kna_ref, vna_ref, qb_ref, knb_ref, vnb_ref,
                        kt_ref, vt_ref, cbk_ref, cbv_ref, sink_ref,
                        oa_ref, ob_ref, *, nb, t_new):
    la = kt_ref.shape[-1]
    t_c = lax.broadcasted_iota(jnp.int32, (NEW_ROWS, la), 0)
    t_c = jnp.where(t_c < t_new, t_c, 0)
    mult_c = _branch_multiplicity(t_c, lax.broadcasted_iota(jnp.int32, (NEW_ROWS, la), 1), la)
    t_n = lax.broadcasted_iota(jnp.int32, (NEW_ROWS, NEW_ROWS), 0)
    t_n = jnp.where(t_n < t_new, t_n, 0)
    key_n = lax.broadcasted_iota(jnp.int32, (NEW_ROWS, NEW_ROWS), 1)
    mult_n = jnp.where(key_n < t_new, _branch_multiplicity(t_n, la + key_n, la), 0.0)

    def head_attn_a(q_h, kt, vt, kn_h, vn_h):
        s_c = jnp.where(mult_c > 0.0, _dot(q_h, kt), NEG_INF)
        s_n = jnp.where(mult_n > 0.0, _dot_nt(q_h, kn_h), NEG_INF)
        m = jnp.maximum(jnp.max(s_c, axis=-1, keepdims=True), jnp.max(s_n, axis=-1, keepdims=True))
        p_c = jnp.exp(s_c - m) * mult_c
        p_n = jnp.exp(s_n - m) * mult_n
        l = jnp.sum(p_c, axis=-1, keepdims=True) + jnp.sum(p_n, axis=-1, keepdims=True)
        return (_dot_nt(p_c.astype(BF16), vt) + _dot(p_n.astype(BF16), vn_h)) / l

    lane_w = lax.broadcasted_iota(jnp.int32, (AW, LANES), 0) // HEAD_DIM
    head_w = lax.broadcasted_iota(jnp.int32, (AW, LANES), 1)
    head_sum = (lane_w == head_w).astype(BF16)
    head_e = lax.broadcasted_iota(jnp.int32, (LANES, AW), 0)
    lane_e = lax.broadcasted_iota(jnp.int32, (LANES, AW), 1) // HEAD_DIM
    head_expand = (head_e == lane_e).astype(BF16)
    row_c = lax.broadcasted_iota(jnp.int32, (BAND, 1), 0)
    row_n = lax.broadcasted_iota(jnp.int32, (NEW_ROWS, 1), 0)
    lane = lax.broadcasted_iota(jnp.int32, (BAND, LANES), 1)
    low_head = lane < HEAD_DIM
    low_head_n = low_head[:NEW_ROWS]
    sink_row = sink_ref[...]

    def scores(k, q_row):
        return _dot((k * q_row).astype(BF16), head_sum)

    def attend(q_row, blocks, sink=None):
        ss = []
        m = None
        for k, _, valid, _ in blocks:
            s = scores(k, q_row)
            if valid is not None:
                s = jnp.where(valid, s, NEG_INF)
            ss.append(s)
            bm = jnp.max(s, axis=0, keepdims=True)
            m = bm if m is None else jnp.maximum(m, bm)
        if sink is not None:
            m = jnp.maximum(m, sink)
        ps = []
        l = jnp.exp(sink - m) if sink is not None else 0.0
        for s, (_, _, valid, mult) in zip(ss, blocks):
            p = jnp.exp(s - m)
            if mult is not None:
                p = p * mult
            ps.append(p)
            l = l + jnp.sum(p, axis=0, keepdims=True)
        inv_l = 1.0 / l
        acc = None
        for p, (_, v, _, _) in zip(ps, blocks):
            pe = _dot((p * inv_l).astype(BF16), head_expand)
            part = jnp.sum(pe * v, axis=0, keepdims=True)
            acc = part if acc is None else acc + part
        return acc

    def expand_kv(x, low):
        sw = pltpu.roll(x, HEAD_DIM, 1)
        kv0 = jnp.where(low, x, sw)
        kv1 = jnp.where(low, sw, x)
        return jnp.concatenate([kv0, kv0, kv1, kv1], axis=1)

    def one_sequence(n, carry):
        new0 = pl.multiple_of(n * NEW_ROWS, NEW_ROWS)
        qa = qa_ref[pl.ds(new0, NEW_ROWS), :].astype(BF16)
        kna = kna_ref[pl.ds(new0, NEW_ROWS), :].astype(BF16)
        vna = vna_ref[pl.ds(new0, NEW_ROWS), :].astype(BF16)
        heads_out = []
        for h in range(N_HEADS):
            hs = slice(h * HEAD_DIM, (h + 1) * HEAD_DIM)
            heads_out.append(head_attn_a(qa[:, hs], kt_ref[n, h].astype(BF16), vt_ref[n, h].astype(BF16),
                                         kna[:, hs], vna[:, hs]))
        oa_ref[pl.ds(new0, NEW_ROWS), :] = jnp.concatenate(heads_out, axis=1).astype(oa_ref.dtype)
        kb_c = expand_kv(cbk_ref[n], low_head)
        vb_c = expand_kv(cbv_ref[n], low_head)
        kb_n = expand_kv(knb_ref[pl.ds(new0, NEW_ROWS), :], low_head_n)
        vb_n = expand_kv(vnb_ref[pl.ds(new0, NEW_ROWS), :], low_head_n)
        qb = qb_ref[pl.ds(new0, NEW_ROWS), :]
        out_b = jnp.zeros((NEW_ROWS, AW), F32)
        for t in range(t_new):
            acc_b = attend(qb[t:t + 1, :], [
                (kb_c, vb_c, row_c >= t, None),
                (kb_n, vb_n, row_n <= t, None),
            ], sink=sink_row)
            out_b = jnp.where(row_n == t, acc_b, out_b)
        ob_ref[pl.ds(new0, NEW_ROWS), :] = out_b.astype(ob_ref.dtype)
        return carry

    lax.fori_loop(0, nb, one_sequence, 0)


def _pad_new_rows(a, n_seq, t_new):
    w = a.shape[-1]
    a = a.reshape(n_seq, t_new, w)
    a = jnp.pad(a, ((0, 0), (0, NEW_ROWS - t_new), (0, 0)))
    return a.reshape(n_seq * NEW_ROWS, w)


def _sample_attn(qa, ka, va, qb, kb, vb, cache_a_k, cache_a_v, cache_b_k, cache_b_v, sink_row, *, t_new):
    n_seq, la, nh, hd = cache_a_k.shape
    w = nh * hd
    assert t_new <= NEW_ROWS
    nb = 2 if n_seq % 2 == 0 else 1
    padded = [_pad_new_rows(a, n_seq, t_new) for a in (qa, ka, va, qb, kb, vb)]
    dim_major = lambda c: jnp.transpose(c, (0, 2, 3, 1))
    new_a = pl.BlockSpec((nb * NEW_ROWS, w), lambda i: (i, 0))
    new_b = pl.BlockSpec((nb * NEW_ROWS, KVW), lambda i: (i, 0))
    spec_ca = pl.BlockSpec((nb, nh, hd, la), lambda i: (i, 0, 0, 0))
    spec_cb = pl.BlockSpec((nb, BAND, KVW), lambda i: (i, 0, 0))
    sds = jax.ShapeDtypeStruct
    oa, ob = pl.pallas_call(
        functools.partial(_sample_attn_kernel, nb=nb, t_new=t_new),
        grid=(n_seq // nb,),
        in_specs=[new_a, new_a, new_a, new_a, new_b, new_b, spec_ca, spec_ca, spec_cb, spec_cb,
                  pl.BlockSpec((1, LANES), lambda i: (0, 0))],
        out_specs=[new_a, new_a],
        out_shape=[sds((n_seq * NEW_ROWS, w), BF16), sds((n_seq * NEW_ROWS, w), BF16)],
        compiler_params=_cparams("arbitrary"),
        name="sample_attn",
    )(*padded, dim_major(cache_a_k), dim_major(cache_a_v), cache_b_k, cache_b_v, sink_row)
    unpad = lambda o: o.reshape(n_seq, NEW_ROWS, w)[:, :t_new].reshape(n_seq * t_new, w)
    return unpad(oa), unpad(ob)


def _silu(a):
    return a * (1.0 / (1.0 + jnp.exp(-a)))


def _mix_out_ffn_kernel(x_ref, oa_ref, ob_ref, wo_ref, g_ref, w1_ref, w3_ref, w2_ref, out_ref,
                        x1_ref, h_ref, acc_ref):
    j = pl.program_id(1)

    @pl.when(j == 0)
    def _():
        x1 = x_ref[...] + _dot(oa_ref[...], wo_ref[0:AW, :]) + _dot(ob_ref[...], wo_ref[AW:, :])
        x1_ref[...] = x1
        h_ref[...] = _rms_norm(x1, g_ref[...]).astype(BF16)
        acc_ref[...] = jnp.zeros_like(acc_ref)

    h = h_ref[...]
    gate = (_silu(_dot(h, w1_ref[...])) * _dot(h, w3_ref[...])).astype(BF16)
    acc_ref[...] += _dot(gate, w2_ref[...])

    @pl.when(j == pl.num_programs(1) - 1)
    def _():
        out_ref[...] = x1_ref[...] + acc_ref[...]


def _ff_tile(ff, target):
    best = LANES
    for k in range(1, ff // LANES + 1):
        if ff % (k * LANES) == 0 and k * LANES <= target:
            best = k * LANES
    return best


def _mix_out_ffn(x, oa, ob, wo, g, w1, w3, w2, *, tm):
    t, d = x.shape
    ff = w1.shape[1]
    tf = _ff_tile(ff, 1408)
    row = lambda i, j: (i, 0)
    fixed = lambda i, j: (0, 0)
    return pl.pallas_call(
        _mix_out_ffn_kernel,
        grid=(t // tm, ff // tf),
        in_specs=[pl.BlockSpec((tm, d), row), pl.BlockSpec((tm, AW), row), pl.BlockSpec((tm, AW), row),
                  pl.BlockSpec((2 * AW, d), fixed), pl.BlockSpec((1, d), fixed),
                  pl.BlockSpec((d, tf), lambda i, j: (0, j)), pl.BlockSpec((d, tf), lambda i, j: (0, j)),
                  pl.BlockSpec((tf, d), lambda i, j: (j, 0))],
        out_specs=pl.BlockSpec((tm, d), row),
        out_shape=jax.ShapeDtypeStruct((t, d), F32),
        scratch_shapes=[pltpu.VMEM((tm, d), F32), pltpu.VMEM((tm, d), BF16), pltpu.VMEM((tm, d), F32)],
        compiler_params=_cparams("arbitrary", "arbitrary"),
        name="mix_out_ffn",
    )(x, oa, ob, wo, g, w1, w3, w2)


def _gelu_exact(z):
    return 0.5 * z * (1.0 + lax.erf(z * (2.0 ** -0.5)))


def _gmlp_kernel(x_ref, g_ref, win_ref, lng_ref, lnb_ref, mix_ref, bias_ref, wout_ref, gffn_ref, wr_ref,
                 *out_refs, write_v):
    if write_v:
        xo_ref, h2_ref, logit_ref, v_ref = out_refs[:4]
    else:
        xo_ref, h2_ref, logit_ref = out_refs[:3]
    gate_ref = out_refs[-1]
    x = x_ref[...]
    tm, cd = x.shape
    h = _rms_norm(x, g_ref[...]).astype(BF16)
    z = _gelu_exact(_dot(h, win_ref[...]))
    u = z[:, :cd]
    v = z[:, cd:]
    mu = jnp.mean(v, axis=-1, keepdims=True)
    vc = v - mu
    v = vc * lax.rsqrt(jnp.mean(vc * vc, axis=-1, keepdims=True) + NORM_EPS) * lng_ref[...] + lnb_ref[...]
    if write_v:
        v_ref[...] = v
    vb = v.astype(BF16)
    bias = bias_ref[...]
    gw = cd // C_GROUPS
    for c in range(tm // CHUNK):
        rs = slice(c * CHUNK, (c + 1) * CHUNK)
        for g in range(C_GROUPS):
            cs = slice(g * gw, (g + 1) * gw)
            f = _dot(mix_ref[g], vb[rs, cs]) + bias[:, cs]
            gate_ref[rs, cs] = (u[rs, cs] * f).astype(BF16)
    xo = x + _dot(gate_ref[...], wout_ref[...])
    xo_ref[...] = xo
    h2 = _rms_norm(xo, gffn_ref[...])
    h2_ref[...] = h2
    h_hi = h2.astype(BF16)
    h_lo = (h2 - h_hi.astype(F32)).astype(BF16)
    wr = wr_ref[...]
    both = _dot(h_hi, wr)
    logit_ref[...] = both[:, :LANES] + both[:, LANES:] + _dot(h_lo, wr[:, :LANES])


def _gmlp_block(x, g, w_in, ln_g, ln_b, mix, bias_full, w_out, g_ffn, wr_hi_lo, *, tm, write_v):
    t, d = x.shape
    cd = w_out.shape[0]
    row = lambda i: (i, 0)
    fixed = lambda i: (0, 0)
    sds = jax.ShapeDtypeStruct
    out_specs = [pl.BlockSpec((tm, d), row), pl.BlockSpec((tm, d), row), pl.BlockSpec((tm, LANES), row)]
    out_shape = [sds((t, d), F32), sds((t, d), F32), sds((t, LANES), F32)]
    if write_v:
        out_specs.append(pl.BlockSpec((tm, cd), row))
        out_shape.append(sds((t, cd), F32))
    return pl.pallas_call(
        functools.partial(_gmlp_kernel, write_v=write_v),
        grid=(t // tm,),
        in_specs=[pl.BlockSpec((tm, d), row), pl.BlockSpec((1, d), fixed), pl.BlockSpec((d, 2 * cd), fixed),
                  pl.BlockSpec((1, cd), fixed), pl.BlockSpec((1, cd), fixed),
                  pl.BlockSpec((C_GROUPS, CHUNK, CHUNK), lambda i: (0, 0, 0)),
                  pl.BlockSpec((CHUNK, cd), fixed), pl.BlockSpec((cd, d), fixed), pl.BlockSpec((1, d), fixed),
                  pl.BlockSpec((d, 2 * LANES), fixed)],
        out_specs=out_specs, out_shape=out_shape,
        scratch_shapes=[pltpu.VMEM((tm, cd), BF16)],
        compiler_params=_cparams("arbitrary"),
        name="gmlp_block",
    )(x, g, w_in, ln_g, ln_b, mix, bias_full, w_out, g_ffn, wr_hi_lo)


MOE_BLOCK = 1024
DISPATCH_ROWS = 512


def _dispatch_kernel(dest_ref, h_ref, xs_in_ref, xs_ref, sem):
    del xs_in_ref
    rows = h_ref.shape[0]

    def issue(r, carry):
        for k in range(TOP_K):
            d = dest_ref[0, 0, r * TOP_K + k]
            pltpu.make_async_copy(h_ref.at[pl.ds(r, 1), :], xs_ref.at[pl.ds(d, 1), :], sem).start()
        return carry

    lax.fori_loop(0, rows, issue, 0)
    for _ in range(TOP_K):
        pltpu.make_async_copy(h_ref, xs_ref.at[pl.ds(0, rows), :], sem).wait()


def _dispatch(h, dest, n_rows):
    t, d = h.shape
    tg = DISPATCH_ROWS
    xs0 = jnp.zeros((n_rows, d), F32)
    return pl.pallas_call(
        _dispatch_kernel,
        grid=(t // tg,),
        in_specs=[pl.BlockSpec((1, 1, tg * TOP_K), lambda i: (i, 0, 0), memory_space=pltpu.SMEM),
                  pl.BlockSpec((tg, d), lambda i: (i, 0)),
                  pl.BlockSpec(memory_space=pl.ANY)],
        out_specs=pl.BlockSpec(memory_space=pl.ANY),
        out_shape=jax.ShapeDtypeStruct((n_rows, d), F32),
        scratch_shapes=[pltpu.SemaphoreType.DMA],
        input_output_aliases={2: 0},
        compiler_params=_cparams("arbitrary"),
        name="moe_dispatch",
    )(dest.reshape(t // tg, 1, tg * TOP_K), h, xs0)


def _expert_kernel(be_ref, na_ref, xs_ref, w1_ref, w3_ref, w2_ref, y_ref, h_ref, acc_ref):
    del be_ref
    b = pl.program_id(0)
    j = pl.program_id(1)
    active = b < na_ref[0]

    @pl.when(active & (j == 0))
    def _():
        h_ref[...] = xs_ref[...].astype(BF16)
        acc_ref[...] = jnp.zeros_like(acc_ref)

    @pl.when(active)
    def _():
        h = h_ref[...]
        w1 = w1_ref[...].astype(BF16)
        w3 = w3_ref[...].astype(BF16)
        gate = (_silu(_dot(h, w1)) * _dot(h, w3)).astype(BF16)
        acc_ref[...] += _dot(gate, w2_ref[...].astype(BF16))

    @pl.when(j == pl.num_programs(1) - 1)
    def _():
        y_ref[...] = jnp.where(active, acc_ref[...], 0.0)


def _experts(xs, block_expert, n_active, w1, w3, w2):
    n_rows, d = xs.shape
    ff = w1.shape[2]
    tf = _ff_tile(ff, 512)
    n_blocks = n_rows // MOE_BLOCK
    nj = ff // tf
    def wcol(b, j, be, na):
        live = b < na[0]
        return (be[b], 0, jnp.where(live, j, nj - 1))

    def wrow(b, j, be, na):
        live = b < na[0]
        return (be[b], jnp.where(live, j, nj - 1), 0)

    def xrow(b, j, be, na):
        return (jnp.minimum(b, na[0] - 1), 0)

    grid_spec = pltpu.PrefetchScalarGridSpec(
        num_scalar_prefetch=2,
        grid=(n_blocks, nj),
        in_specs=[pl.BlockSpec((MOE_BLOCK, d), xrow),
                  pl.BlockSpec((None, d, tf), wcol), pl.BlockSpec((None, d, tf), wcol),
                  pl.BlockSpec((None, tf, d), wrow)],
        out_specs=pl.BlockSpec((MOE_BLOCK, d), lambda b, j, be, na: (b, 0)),
        scratch_shapes=[pltpu.VMEM((MOE_BLOCK, d), BF16), pltpu.VMEM((MOE_BLOCK, d), F32)],
    )
    return pl.pallas_call(
        _expert_kernel,
        grid_spec=grid_spec,
        out_shape=jax.ShapeDtypeStruct((n_rows, d), F32),
        compiler_params=_cparams("arbitrary", "arbitrary"),
        name="moe_experts",
    )(block_expert, n_active, xs, w1, w3, w2)


def _combine_kernel(dest_ref, x_ref, gate_ref, y_ref, out_ref, buf, sem):
    rows = x_ref.shape[0]

    def issue(r, carry):
        for k in range(TOP_K):
            d = dest_ref[0, 0, r * TOP_K + k]
            pltpu.make_async_copy(y_ref.at[pl.ds(d, 1), :], buf.at[k, pl.ds(r, 1), :], sem).start()
        return carry

    lax.fori_loop(0, rows, issue, 0)
    for k in range(TOP_K):
        pltpu.make_async_copy(y_ref.at[pl.ds(0, rows), :], buf.at[k], sem).wait()
    gates = gate_ref[...]
    y = buf[0] * gates[:, 0:1]
    for k in range(1, TOP_K):
        y = y + buf[k] * gates[:, k:k + 1]
    out_ref[...] = x_ref[...] + y


def _combine(x, gates, dest, y):
    t, d = x.shape
    tg = DISPATCH_ROWS
    return pl.pallas_call(
        _combine_kernel,
        grid=(t // tg,),
        in_specs=[pl.BlockSpec((1, 1, tg * TOP_K), lambda i: (i, 0, 0), memory_space=pltpu.SMEM),
                  pl.BlockSpec((tg, d), lambda i: (i, 0)),
                  pl.BlockSpec((tg, TOP_K), lambda i: (i, 0)),
                  pl.BlockSpec(memory_space=pl.ANY)],
        out_specs=pl.BlockSpec((tg, d), lambda i: (i, 0)),
        out_shape=jax.ShapeDtypeStruct((t, d), F32),
        scratch_shapes=[pltpu.VMEM((TOP_K, tg, d), F32), pltpu.SemaphoreType.DMA],
        compiler_params=_cparams("arbitrary"),
        name="moe_combine",
    )(dest.reshape(t // tg, 1, tg * TOP_K), x, gates, y)


def _route(logits):
    t = logits.shape[0]
    top_val, top_idx = lax.top_k(logits[:, :N_EXPERTS], TOP_K)
    gates = jax.nn.softmax(top_val, axis=-1)
    expert = top_idx.reshape(-1)
    one_hot = (expert[:, None] == jnp.arange(N_EXPERTS, dtype=expert.dtype)[None, :]).astype(jnp.int32)
    running = jnp.cumsum(one_hot, axis=0)
    rank = jnp.sum(running * one_hot, axis=1) - 1
    counts = running[-1]
    padded = (counts + MOE_BLOCK - 1) // MOE_BLOCK * MOE_BLOCK
    pad_ends = jnp.cumsum(padded)
    pad_starts = pad_ends - padded
    dest = (jnp.sum(pad_starts[None, :] * one_hot, axis=1) + rank).astype(jnp.int32)
    n_blocks = -(-(t * TOP_K) // MOE_BLOCK) + N_EXPERTS
    block_start = jnp.arange(n_blocks, dtype=jnp.int32) * MOE_BLOCK
    block_expert = jnp.sum((block_start[:, None] >= pad_ends[None, :]).astype(jnp.int32), axis=1)
    block_expert = jnp.minimum(block_expert, N_EXPERTS - 1).astype(jnp.int32)
    n_active = (pad_ends[-1:] // MOE_BLOCK).astype(jnp.int32)
    return gates, dest, block_expert, n_active, n_blocks * MOE_BLOCK


def _moe(x, h2, logits, w1, w3, w2):
    gates, dest, block_expert, n_active, n_rows = _route(logits)
    xs = _dispatch(h2, dest, n_rows)
    y = _experts(xs, block_expert, n_active, w1, w3, w2)
    return _combine(x, gates, dest, y)


def _rope_tables(pos):
    inv_freq = ROPE_THETA ** (-jnp.arange(HALF_DIM, dtype=F32) / HALF_DIM)
    ang = pos.astype(F32)[:, None] * inv_freq[None, :]
    cos, sin = jnp.cos(ang), jnp.sin(ang)
    reps = LANES // HEAD_DIM
    return jnp.tile(cos, (1, 2 * reps)), jnp.tile(jnp.concatenate([-sin, sin], axis=1), (1, reps))


def _qk_gain_row(qk_gain):
    ones_a = jnp.ones((AW,), F32)
    ones_b = jnp.ones((KVW,), F32)
    return jnp.concatenate([jnp.tile(qk_gain[0], N_HEADS), jnp.tile(qk_gain[1], N_HEADS), ones_a,
                            jnp.tile(qk_gain[2], N_HEADS), jnp.tile(qk_gain[3], B_KV_HEADS), ones_b])[None, :]


def _row_block(t):
    return 512 if t % 512 == 0 else t


def kernel(x_prompt, x_sample, cache_a_k, cache_a_v, cache_b_k, cache_b_v, norm_mix, norm_ffn, ab_w_in,
           ab_qk_gain, ab_sinks, ab_w_out, ffn_w1, ffn_w3, ffn_w2, c_w_in, c_ln_g, c_ln_b, c_w_s, c_b_s,
           c_w_out, moe_router, moe_w1, moe_w3, moe_w2):
    n_p, s_p, d = x_prompt.shape
    n_s, t_new, _ = x_sample.shape
    depth = norm_mix.shape[0]
    past_len = 8192
    xp = x_prompt.reshape(n_p * s_p, d)
    xs = x_sample.reshape(n_s * t_new, d)
    pos_p = jnp.tile(jnp.arange(s_p), n_p)
    pos_s = jnp.tile(past_len + jnp.arange(t_new), n_s)
    cos_p, sin_p = _rope_tables(pos_p)
    cos_s, sin_s = _rope_tables(pos_s)
    row = lambda a: a[None, :]
    akp, avp, bkp, bvp, aks, avs, bks, bvs, cvs = [], [], [], [], [], [], [], [], []
    la_p, lb_p = min(BAND * A_DILATIONS[-1], s_p), min(BAND, s_p)
    for layer in range(depth):
        i = layer // 2
        if layer % 2 == 0:
            w_in = ab_w_in[i].astype(BF16)
            gain_row = _qk_gain_row(ab_qk_gain[i])
            w_out = ab_w_out[i].astype(BF16)
            w1, w3, w2 = ffn_w1[i].astype(BF16), ffn_w3[i].astype(BF16), ffn_w2[i].astype(BF16)
            sinks = ab_sinks[i].astype(F32)
            sink_rows = jnp.broadcast_to(sinks[:, None], (N_HEADS, LANES))
            sink_lane_row = jnp.pad(sinks, (0, LANES - N_HEADS))[None, :]
            qa, ka, va, qb, kb, vb = _qkv_proj(xp, row(norm_mix[layer]), w_in, gain_row, cos_p, sin_p,
                                               tm=_row_block(xp.shape[0]), q_dtype=BF16)
            seq = lambda a: a.reshape(n_p, s_p, a.shape[-1])
            state = None
            for r in A_DILATIONS:
                state = _attn_a_branch(seq(qa), seq(ka), seq(va), state, dilation=r, last=r == A_DILATIONS[-1])
            oa = state[0].reshape(n_p * s_p, AW)
            ob = _attn_b(seq(qb), seq(kb), seq(vb), sink_rows).reshape(n_p * s_p, AW)
            xp = _mix_out_ffn(xp, oa, ob, w_out, row(norm_ffn[layer]), w1, w3, w2, tm=_row_block(xp.shape[0]))
            heads = lambda a, nh: a.reshape(n_p, s_p, nh, HEAD_DIM)
            akp.append(heads(ka, N_HEADS)[:, s_p - la_p:])
            avp.append(heads(va, N_HEADS)[:, s_p - la_p:])
            bkp.append(heads(kb, B_KV_HEADS)[:, s_p - lb_p:])
            bvp.append(heads(vb, B_KV_HEADS)[:, s_p - lb_p:])
            qa, ka, va, qb, kb, vb = _qkv_proj(xs, row(norm_mix[layer]), w_in, gain_row, cos_s, sin_s,
                                               tm=_row_block(xs.shape[0]), q_dtype=F32)
            flat = lambda c: c.reshape(c.shape[0], c.shape[1], c.shape[2] * c.shape[3])
            oa, ob = _sample_attn(qa, ka, va, qb, kb, vb, cache_a_k[i], cache_a_v[i],
                                  flat(cache_b_k[i]), flat(cache_b_v[i]), sink_lane_row, t_new=t_new)
            xs = _mix_out_ffn(xs, oa, ob, w_out, row(norm_ffn[layer]), w1, w3, w2, tm=_row_block(xs.shape[0]))
            heads_s = lambda a, nh: a.reshape(n_s, t_new, nh, HEAD_DIM)
            aks.append(heads_s(ka, N_HEADS))
            avs.append(heads_s(va, N_HEADS))
            bks.append(heads_s(kb, B_KV_HEADS))
            bvs.append(heads_s(vb, B_KV_HEADS))
        else:
            w_in = c_w_in[i].astype(BF16)
            w_out = c_w_out[i].astype(BF16)
            tril = jnp.tril(jnp.ones((CHUNK, CHUNK), F32))
            mix_p = (c_w_s[i] * tril).astype(BF16)
            gw = w_out.shape[0] // C_GROUPS
            bias_p = jnp.repeat(c_b_s[i].T, gw, axis=1)
            per_tile = CHUNK // t_new
            mix_s = jnp.einsum("ab,gij->gaibj", jnp.eye(per_tile, dtype=F32), (c_w_s[i] * tril)[:, :t_new, :t_new])
            mix_s = mix_s.reshape(C_GROUPS, CHUNK, CHUNK).astype(BF16)
            bias_s = jnp.tile(bias_p[:t_new], (per_tile, 1))
            wr = moe_router[i]
            wr_hi = wr.astype(BF16)
            wr_lo = (wr - wr_hi.astype(F32)).astype(BF16)
            pad_e = lambda a: jnp.pad(a, ((0, 0), (0, LANES - N_EXPERTS)))
            wr_hi_lo = jnp.concatenate([pad_e(wr_hi), pad_e(wr_lo)], axis=1)
            common = (row(c_ln_g[i]), row(c_ln_b[i]))
            xp1, hp, lp = _gmlp_block(xp, row(norm_mix[layer]), w_in, *common, mix_p, bias_p, w_out,
                                      row(norm_ffn[layer]), wr_hi_lo, tm=_row_block(xp.shape[0]), write_v=False)
            xs1, hs, ls, v_new = _gmlp_block(xs, row(norm_mix[layer]), w_in, *common, mix_s, bias_s, w_out,
                                             row(norm_ffn[layer]), wr_hi_lo, tm=_row_block(xs.shape[0]),
                                             write_v=True)
            cvs.append(v_new.reshape(n_s, t_new, v_new.shape[-1]))
            x_all = jnp.concatenate([xp1, xs1], axis=0)
            h_all = jnp.concatenate([hp, hs], axis=0)
            l_all = jnp.concatenate([lp, ls], axis=0)
            y_all = _moe(x_all, h_all, l_all, moe_w1[i], moe_w3[i], moe_w2[i])
            xp, xs = y_all[:xp.shape[0]], y_all[xp.shape[0]:]
    return (xp.reshape(n_p, s_p, d), xs.reshape(n_s, t_new, d),
            jnp.stack(akp), jnp.stack(avp), jnp.stack(bkp), jnp.stack(bvp),
            jnp.stack(aks), jnp.stack(avs), jnp.stack(bks), jnp.stack(bvs),
            jnp.stack(cvs))
```

```python
import functools

import jax
import jax.numpy as jnp
from jax import lax
from jax.experimental import pallas as pl
from jax.experimental.pallas import tpu as pltpu

F32 = jnp.float32
BF16 = jnp.bfloat16

HEAD_DIM = 64
HALF_DIM = HEAD_DIM // 2
N_HEADS = 8
B_KV_HEADS = 2
A_DILATIONS = (1, 4, 16)
BAND = 128
ROPE_THETA = 10000.0
NORM_EPS = 1e-6
NEG_INF = -1e30
CHUNK = 128
C_GROUPS = 8
N_EXPERTS = 8
TOP_K = 2

LANES = 128
SUBLANES = 8
VMEM_LIMIT_BYTES = 56 * 1024 * 1024

AW = N_HEADS * HEAD_DIM
KVW = B_KV_HEADS * HEAD_DIM
HEADS_PER_VREG = LANES // HEAD_DIM
N_HEAD_BLOCKS = AW // LANES


def _cparams(*sem):
    return pltpu.CompilerParams(dimension_semantics=sem, vmem_limit_bytes=VMEM_LIMIT_BYTES)


def _rms_norm(x, g):
    return x * lax.rsqrt(jnp.mean(x * x, axis=-1, keepdims=True) + NORM_EPS) * g


def _dot(a, b):
    return jnp.dot(a, b, preferred_element_type=F32)


def _dot_nt(a, b):
    return lax.dot_general(a, b, (((1,), (1,)), ((), ())), preferred_element_type=F32)


_QA_BLOCKS = range(0, 4)
_KA_BLOCKS = range(4, 8)
_VA_BLOCKS = range(8, 12)
_QB_BLOCKS = range(12, 16)
_KB_BLOCK = 16
_VB_BLOCK = 17


def _proj_kernel(x_ref, g_ref, w_ref, gain_ref, cos_ref, sin_ref,
                 qa_ref, ka_ref, va_ref, qb_ref, kb_ref, vb_ref, *bf16_kv_refs):
    h = _rms_norm(x_ref[...], g_ref[...]).astype(BF16)
    p = _dot(h, w_ref[...])
    tm = p.shape[0]
    cos = cos_ref[...]
    sin = sin_ref[...]
    lane = lax.broadcasted_iota(jnp.int32, (tm, LANES), 1)
    low_head = lane < HEAD_DIM
    first_half = (lane & HALF_DIM) == 0

    def norm_rope(blk):
        pc = p[:, blk * LANES:(blk + 1) * LANES]
        sq = pc * pc
        ms_lo = jnp.sum(jnp.where(low_head, sq, 0.0), axis=-1, keepdims=True) * (1.0 / HEAD_DIM)
        ms_hi = jnp.sum(jnp.where(low_head, 0.0, sq), axis=-1, keepdims=True) * (1.0 / HEAD_DIM)
        inv = jnp.where(low_head, lax.rsqrt(ms_lo + NORM_EPS), lax.rsqrt(ms_hi + NORM_EPS))
        y = pc * inv * gain_ref[:, blk * LANES:(blk + 1) * LANES]
        partner = jnp.where(first_half, pltpu.roll(y, LANES - HALF_DIM, 1), pltpu.roll(y, HALF_DIM, 1))
        return y * cos + partner * sin

    scale = HEAD_DIM ** -0.5
    for j, blk in enumerate(_QA_BLOCKS):
        qa_ref[:, j * LANES:(j + 1) * LANES] = (norm_rope(blk) * scale).astype(qa_ref.dtype)
    for j, blk in enumerate(_KA_BLOCKS):
        ka_ref[:, j * LANES:(j + 1) * LANES] = norm_rope(blk)
    for j, blk in enumerate(_VA_BLOCKS):
        va_ref[:, j * LANES:(j + 1) * LANES] = p[:, blk * LANES:(blk + 1) * LANES]
    for j, blk in enumerate(_QB_BLOCKS):
        qb_ref[:, j * LANES:(j + 1) * LANES] = (norm_rope(blk) * scale).astype(qb_ref.dtype)
    kb_ref[...] = norm_rope(_KB_BLOCK)
    vb_ref[...] = p[:, _VB_BLOCK * LANES:(_VB_BLOCK + 1) * LANES]
    if bf16_kv_refs:
        for dst, src in zip(bf16_kv_refs, (ka_ref, va_ref, kb_ref, vb_ref)):
            dst[...] = src[...].astype(BF16)


def _qkv_proj(x, g, w, gain_row, cos, sin, *, tm, q_dtype, bf16_kv):
    t, d = x.shape
    n_out = w.shape[1]
    row = lambda i: (i, 0)
    fixed = lambda i: (0, 0)
    sds = jax.ShapeDtypeStruct
    widths = (AW, AW, AW, AW, KVW, KVW) + ((AW, AW, KVW, KVW) if bf16_kv else ())
    dtypes = (q_dtype, F32, F32, q_dtype, F32, F32) + ((BF16,) * 4 if bf16_kv else ())
    return pl.pallas_call(
        _proj_kernel,
        grid=(t // tm,),
        in_specs=[pl.BlockSpec((tm, d), row), pl.BlockSpec((1, d), fixed), pl.BlockSpec((d, n_out), fixed),
                  pl.BlockSpec((1, n_out), fixed), pl.BlockSpec((tm, LANES), row), pl.BlockSpec((tm, LANES), row)],
        out_specs=[pl.BlockSpec((tm, wd), row) for wd in widths],
        out_shape=[sds((t, wd), dt) for wd, dt in zip(widths, dtypes)],
        compiler_params=_cparams("arbitrary"),
        name="qkv_proj",
    )(x, g, w, gain_row, cos, sin)


def _fill_kv(kbuf, vbuf, kp_ref, kc_ref, vp_ref, vc_ref):
    kbuf[0:BAND, :] = kp_ref[...].astype(BF16)
    kbuf[BAND:, :] = kc_ref[...].astype(BF16)
    vbuf[0:BAND, :] = vp_ref[...].astype(BF16)
    vbuf[BAND:, :] = vc_ref[...].astype(BF16)


def _band_mask(first_block):
    row = lax.broadcasted_iota(jnp.int32, (BAND, 2 * BAND), 0)
    col = lax.broadcasted_iota(jnp.int32, (BAND, 2 * BAND), 1)
    dist = BAND + row - col
    first_key = jnp.where(first_block, BAND, 0)
    return (dist >= 0) & (dist <= BAND) & (col >= first_key)


def _masked_softmax_pv(qm, kc, vc, mask):
    s = jnp.where(mask, _dot_nt(qm, kc), NEG_INF)
    m = jnp.max(s, axis=-1, keepdims=True)
    p = jnp.exp(s - m)
    l = jnp.sum(p, axis=-1, keepdims=True)
    return m, l, _dot(p.astype(BF16), vc)


def _two_head_attention(q2, kc, vc, mask2, low_head):
    zero = jnp.zeros_like(q2)
    qm = jnp.concatenate([jnp.where(low_head, q2, zero), jnp.where(low_head, zero, q2)], axis=0)
    m, l, pv = _masked_softmax_pv(qm, kc, vc, mask2)
    o = pv * (1.0 / l)
    return m, l, jnp.where(low_head, o[:BAND], o[BAND:])


def _attn_a_kernel(q_ref, kc_ref, kp_ref, vc_ref, vp_ref, o_ref, lse_ref, kbuf, vbuf, *, tq):
    blk = pl.program_id(2)
    _fill_kv(kbuf, vbuf, kp_ref, kc_ref, vp_ref, vc_ref)
    lane = lax.broadcasted_iota(jnp.int32, (BAND, LANES), 1)
    low_head = lane < HEAD_DIM

    def sub_block(sb, carry):
        r0 = pl.multiple_of(sb * BAND, BAND)
        mask = _band_mask((blk + sb) == 0)
        mask2 = jnp.concatenate([mask, mask], axis=0)
        lse_tile = jnp.zeros((BAND, LANES), F32)
        for hb in range(N_HEAD_BLOCKS):
            cs = slice(hb * LANES, (hb + 1) * LANES)
            m, l, o = _two_head_attention(q_ref[pl.ds(r0, BAND), cs], kbuf[pl.ds(r0, 2 * BAND), cs],
                                          vbuf[pl.ds(r0, 2 * BAND), cs], mask2, low_head)
            o_ref[pl.ds(r0, BAND), cs] = o.astype(o_ref.dtype)
            lse = m + jnp.log(l)
            for hh in range(HEADS_PER_VREG):
                lse_tile = jnp.where(lane == hb * HEADS_PER_VREG + hh, lse[hh * BAND:(hh + 1) * BAND], lse_tile)
        lse_ref[pl.ds(r0, BAND), :] = lse_tile
        return carry

    lax.fori_loop(0, tq // BAND, sub_block, 0)


def _attn_a_branch(q, k, v, *, dilation):
    n, s, w = q.shape
    r = dilation
    sub_len = s // r
    tq = min(512, sub_len)
    view = lambda a: a.reshape(n, sub_len, r * a.shape[-1])
    cur = lambda b, c, i: (b, i, c)
    prv = lambda b, c, i: (b, jnp.maximum(i * (tq // BAND) - 1, 0), c)
    blk_cur = pl.BlockSpec((None, tq, w), cur)
    blk_prv = pl.BlockSpec((None, BAND, w), prv)
    sds = jax.ShapeDtypeStruct
    o, lse = pl.pallas_call(
        functools.partial(_attn_a_kernel, tq=tq),
        grid=(n, r, sub_len // tq),
        in_specs=[blk_cur, blk_cur, blk_prv, blk_cur, blk_prv],
        out_specs=[blk_cur, pl.BlockSpec((None, tq, LANES), cur)],
        out_shape=[sds((n, sub_len, r * w), BF16), sds((n, sub_len, r * LANES), F32)],
        scratch_shapes=[pltpu.VMEM((tq + BAND, w), BF16), pltpu.VMEM((tq + BAND, w), BF16)],
        compiler_params=_cparams("arbitrary", "arbitrary", "arbitrary"),
        name=f"attn_a_dil{r}",
    )(view(q), view(k), view(k), view(v), view(v))
    return o.reshape(n, s, w), lse.reshape(n, s, LANES)


def _attn_b_kernel(q_ref, kc_ref, kp_ref, vc_ref, vp_ref, sink_ref, o_ref, kbuf, vbuf, *, tq):
    blk = pl.program_id(1)
    _fill_kv(kbuf, vbuf, kp_ref, kc_ref, vp_ref, vc_ref)
    lane = lax.broadcasted_iota(jnp.int32, (BAND, LANES), 1)
    low_head = lane < HEAD_DIM
    group = N_HEADS // B_KV_HEADS

    def sub_block(sb, carry):
        r0 = pl.multiple_of(sb * BAND, BAND)
        mask = _band_mask((blk + sb) == 0)
        kc = kbuf[pl.ds(r0, 2 * BAND), :]
        vc = vbuf[pl.ds(r0, 2 * BAND), :]
        for hb in range(N_HEAD_BLOCKS):
            cs = slice(hb * LANES, (hb + 1) * LANES)
            q2 = q_ref[pl.ds(r0, BAND), cs].astype(F32)
            q2_swapped = pltpu.roll(q2, HEAD_DIM, 1)
            placed = []
            for hh in range(HEADS_PER_VREG):
                head = hb * HEADS_PER_VREG + hh
                kv = head // group
                kv_sel = low_head if kv == 0 else jnp.logical_not(low_head)
                q_at_kv = q2 if hh == kv else q2_swapped
                qm = jnp.where(kv_sel, q_at_kv, 0.0).astype(BF16)
                m, l, pv = _masked_softmax_pv(qm, kc, vc, mask)
                sink = sink_ref[head:head + 1, 0:1]
                m2 = jnp.maximum(m, sink)
                a = jnp.exp(m - m2)
                o = pv * (a / (l * a + jnp.exp(sink - m2)))
                placed.append(o if hh == kv else pltpu.roll(o, HEAD_DIM, 1))
            o_ref[pl.ds(r0, BAND), cs] = jnp.where(low_head, placed[0], placed[1]).astype(o_ref.dtype)
        return carry

    lax.fori_loop(0, tq // BAND, sub_block, 0)


def _attn_b(q, k, v, sinks_rows):
    n, s, w = q.shape
    tq = min(512, s)
    cur = lambda b, i: (b, i, 0)
    prv = lambda b, i: (b, jnp.maximum(i * (tq // BAND) - 1, 0), 0)
    kv_cur = pl.BlockSpec((None, tq, KVW), cur)
    kv_prv = pl.BlockSpec((None, BAND, KVW), prv)
    return pl.pallas_call(
        functools.partial(_attn_b_kernel, tq=tq),
        grid=(n, s // tq),
        in_specs=[pl.BlockSpec((None, tq, w), cur), kv_cur, kv_prv, kv_cur, kv_prv,
                  pl.BlockSpec((N_HEADS, LANES), lambda b, i: (0, 0))],
        out_specs=pl.BlockSpec((None, tq, w), cur),
        out_shape=jax.ShapeDtypeStruct((n, s, w), BF16),
        scratch_shapes=[pltpu.VMEM((tq + BAND, KVW), BF16), pltpu.VMEM((tq + BAND, KVW), BF16)],
        compiler_params=_cparams("arbitrary", "arbitrary"),
        name="attn_b",
    )(q, k, k, v, v, sinks_rows)


NEW_ROWS = 8


def _branch_multiplicity(t_query, key_pos, la):
    back = la + t_query - key_pos
    mult = jnp.zeros(back.shape, F32)
    for r in A_DILATIONS:
        mult = mult + ((back >= 0) & (back <= BAND * r) & (back % r == 0)).astype(F32)
    return mult


def _sample_attn_kernel(qa_ref, kna_ref, vna_ref, qb_ref, knb_ref, vnb_ref,
                        kt_ref, vt_ref, cbk_ref, cbv_ref, sink_ref,
                        oa_ref, ob_ref, *, nb, t_new):
    la = kt_ref.shape[-1]
    t_c = lax.broadcasted_iota(jnp.int32, (NEW_ROWS, la), 0)
    t_c = jnp.where(t_c < t_new, t_c, 0)
    mult_c = _branch_multiplicity(t_c, lax.broadcasted_iota(jnp.int32, (NEW_ROWS, la), 1), la)
    t_n = lax.broadcasted_iota(jnp.int32, (NEW_ROWS, NEW_ROWS), 0)
    t_n = jnp.where(t_n < t_new, t_n, 0)
    key_n = lax.broadcasted_iota(jnp.int32, (NEW_ROWS, NEW_ROWS), 1)
    mult_n = jnp.where(key_n < t_new, _branch_multiplicity(t_n, la + key_n, la), 0.0)

    def head_attn_a(q_h, kt, vt, kn_h, vn_h):
        s_c = jnp.where(mult_c > 0.0, _dot(q_h, kt), NEG_INF)
        s_n = jnp.where(mult_n > 0.0, _dot_nt(q_h, kn_h), NEG_INF)
        m = jnp.maximum(jnp.max(s_c, axis=-1, keepdims=True), jnp.max(s_n, axis=-1, keepdims=True))
        p_c = jnp.exp(s_c - m) * mult_c
        p_n = jnp.exp(s_n - m) * mult_n
        l = jnp.sum(p_c, axis=-1, keepdims=True) + jnp.sum(p_n, axis=-1, keepdims=True)
        return (_dot_nt(p_c.astype(BF16), vt) + _dot(p_n.astype(BF16), vn_h)) / l

    lane_w = lax.broadcasted_iota(jnp.int32, (AW, LANES), 0) // HEAD_DIM
    head_w = lax.broadcasted_iota(jnp.int32, (AW, LANES), 1)
    head_sum = (lane_w == head_w).astype(BF16)
    head_e = lax.broadcasted_iota(jnp.int32, (LANES, AW), 0)
    lane_e = lax.broadcasted_iota(jnp.int32, (LANES, AW), 1) // HEAD_DIM
    head_expand = (head_e == lane_e).astype(BF16)
    row_c = lax.broadcasted_iota(jnp.int32, (BAND, 1), 0)
    row_n = lax.broadcasted_iota(jnp.int32, (NEW_ROWS, 1), 0)
    lane = lax.broadcasted_iota(jnp.int32, (BAND, LANES), 1)
    low_head = lane < HEAD_DIM
    low_head_n = low_head[:NEW_ROWS]
    sink_row = sink_ref[...]

    def scores(k, q_row):
        return _dot((k * q_row).astype(BF16), head_sum)

    def attend(q_row, blocks, sink=None):
        ss = []
        m = None
        for k, _, valid, _ in blocks:
            s = scores(k, q_row)
            if valid is not None:
                s = jnp.where(valid, s, NEG_INF)
            ss.append(s)
            bm = jnp.max(s, axis=0, keepdims=True)
            m = bm if m is None else jnp.maximum(m, bm)
        if sink is not None:
            m = jnp.maximum(m, sink)
        ps = []
        l = jnp.exp(sink - m) if sink is not None else 0.0
        for s, (_, _, valid, mult) in zip(ss, blocks):
            p = jnp.exp(s - m)
            if mult is not None:
                p = p * mult
            ps.append(p)
            l = l + jnp.sum(p, axis=0, keepdims=True)
        inv_l = 1.0 / l
        acc = None
        for p, (_, v, _, _) in zip(ps, blocks):
            pe = _dot((p * inv_l).astype(BF16), head_expand)
            part = jnp.sum(pe * v, axis=0, keepdims=True)
            acc = part if acc is None else acc + part
        return acc

    def expand_kv(x, low):
        sw = pltpu.roll(x, HEAD_DIM, 1)
        kv0 = jnp.where(low, x, sw)
        kv1 = jnp.where(low, sw, x)
        return jnp.concatenate([kv0, kv0, kv1, kv1], axis=1)

    def one_sequence(n, carry):
        new0 = pl.multiple_of(n * NEW_ROWS, NEW_ROWS)
        qa = qa_ref[pl.ds(new0, NEW_ROWS), :].astype(BF16)
        kna = kna_ref[pl.ds(new0, NEW_ROWS), :].astype(BF16)
        vna = vna_ref[pl.ds(new0, NEW_ROWS), :].astype(BF16)
        heads_out = []
        for h in range(N_HEADS):
            hs = slice(h * HEAD_DIM, (h + 1) * HEAD_DIM)
            heads_out.append(head_attn_a(qa[:, hs], kt_ref[n, h].astype(BF16), vt_ref[n, h].astype(BF16),
                                         kna[:, hs], vna[:, hs]))
        oa_ref[pl.ds(new0, NEW_ROWS), :] = jnp.concatenate(heads_out, axis=1).astype(oa_ref.dtype)
        kb_c = expand_kv(cbk_ref[n], low_head)
        vb_c = expand_kv(cbv_ref[n], low_head)
        kb_n = expand_kv(knb_ref[pl.ds(new0, NEW_ROWS), :], low_head_n)
        vb_n = expand_kv(vnb_ref[pl.ds(new0, NEW_ROWS), :], low_head_n)
        qb = qb_ref[pl.ds(new0, NEW_ROWS), :]
        out_b = jnp.zeros((NEW_ROWS, AW), F32)
        for t in range(t_new):
            acc_b = attend(qb[t:t + 1, :], [
                (kb_c, vb_c, row_c >= t, None),
                (kb_n, vb_n, row_n <= t, None),
            ], sink=sink_row)
            out_b = jnp.where(row_n == t, acc_b, out_b)
        ob_ref[pl.ds(new0, NEW_ROWS), :] = out_b.astype(ob_ref.dtype)
        return carry

    lax.fori_loop(0, nb, one_sequence, 0)


def _pad_new_rows(a, n_seq, t_new):
    w = a.shape[-1]
    a = a.reshape(n_seq, t_new, w)
    a = jnp.pad(a, ((0, 0), (0, NEW_ROWS - t_new), (0, 0)))
    return a.reshape(n_seq * NEW_ROWS, w)


def _sample_attn(qa, ka, va, qb, kb, vb, cache_a_k, cache_a_v, cache_b_k, cache_b_v, sink_row, *, t_new):
    n_seq, la, nh, hd = cache_a_k.shape
    w = nh * hd
    assert t_new <= NEW_ROWS
    nb = 2 if n_seq % 2 == 0 else 1
    padded = [_pad_new_rows(a, n_seq, t_new) for a in (qa, ka, va, qb, kb, vb)]
    dim_major = lambda c: jnp.transpose(c, (0, 2, 3, 1))
    new_a = pl.BlockSpec((nb * NEW_ROWS, w), lambda i: (i, 0))
    new_b = pl.BlockSpec((nb * NEW_ROWS, KVW), lambda i: (i, 0))
    spec_ca = pl.BlockSpec((nb, nh, hd, la), lambda i: (i, 0, 0, 0))
    spec_cb = pl.BlockSpec((nb, BAND, KVW), lambda i: (i, 0, 0))
    sds = jax.ShapeDtypeStruct
    oa, ob = pl.pallas_call(
        functools.partial(_sample_attn_kernel, nb=nb, t_new=t_new),
        grid=(n_seq // nb,),
        in_specs=[new_a, new_a, new_a, new_a, new_b, new_b, spec_ca, spec_ca, spec_cb, spec_cb,
                  pl.BlockSpec((1, LANES), lambda i: (0, 0))],
        out_specs=[new_a, new_a],
        out_shape=[sds((n_seq * NEW_ROWS, w), BF16), sds((n_seq * NEW_ROWS, w), BF16)],
        compiler_params=_cparams("arbitrary"),
        name="sample_attn",
    )(*padded, dim_major(cache_a_k), dim_major(cache_a_v), cache_b_k, cache_b_v, sink_row)
    unpad = lambda o: o.reshape(n_seq, NEW_ROWS, w)[:, :t_new].reshape(n_seq * t_new, w)
    return unpad(oa), unpad(ob)


def _silu(a):
    return a * (1.0 / (1.0 + jnp.exp(-a)))


def _merge_branches(o_refs, lse_refs):
    lses = [r[...] for r in lse_refs]
    top = functools.reduce(jnp.maximum, lses)
    ws = [jnp.exp(l - top) for l in lses]
    inv_den = 1.0 / functools.reduce(lambda a, b: a + b, ws)
    head = lax.broadcasted_iota(jnp.int32, (LANES, AW), 0)
    lane_head = lax.broadcasted_iota(jnp.int32, (LANES, AW), 1) // HEAD_DIM
    expand = (head == lane_head).astype(BF16)
    oa = None
    for w, o_ref in zip(ws, o_refs):
        w = w * inv_den
        w_hi = w.astype(BF16)
        w_lo = (w - w_hi.astype(F32)).astype(BF16)
        part = (_dot(w_hi, expand) + _dot(w_lo, expand)) * o_ref[...].astype(F32)
        oa = part if oa is None else oa + part
    return oa.astype(BF16)


def _mix_out_ffn_kernel(x_ref, *refs, n_branches):
    o_refs, refs = refs[:n_branches], refs[n_branches:]
    if n_branches > 1:
        lse_refs, refs = refs[:n_branches], refs[n_branches:]
    ob_ref, wo_ref, g_ref, w1_ref, w3_ref, w2_ref, out_ref, x1_ref, h_ref, acc_ref = refs
    j = pl.program_id(1)

    @pl.when(j == 0)
    def _():
        oa = _merge_branches(o_refs, lse_refs) if n_branches > 1 else o_refs[0][...]
        x1 = x_ref[...] + _dot(oa, wo_ref[0:AW, :]) + _dot(ob_ref[...], wo_ref[AW:, :])
        x1_ref[...] = x1
        h_ref[...] = _rms_norm(x1, g_ref[...]).astype(BF16)
        acc_ref[...] = jnp.zeros_like(acc_ref)

    h = h_ref[...]
    gate = (_silu(_dot(h, w1_ref[...])) * _dot(h, w3_ref[...])).astype(BF16)
    acc_ref[...] += _dot(gate, w2_ref[...])

    @pl.when(j == pl.num_programs(1) - 1)
    def _():
        out_ref[...] = x1_ref[...] + acc_ref[...]


def _ff_tile(ff, target):
    best = LANES
    for k in range(1, ff // LANES + 1):
        if ff % (k * LANES) == 0 and k * LANES <= target:
            best = k * LANES
    return best


def _mix_out_ffn(x, oa_branches, lse_branches, ob, wo, g, w1, w3, w2, *, tm):
    t, d = x.shape
    ff = w1.shape[1]
    tf = _ff_tile(ff, 1408)
    row = lambda i, j: (i, 0)
    fixed = lambda i, j: (0, 0)
    n_branches = len(oa_branches)
    assert len(lse_branches) == (n_branches if n_branches > 1 else 0)
    return pl.pallas_call(
        functools.partial(_mix_out_ffn_kernel, n_branches=n_branches),
        grid=(t // tm, ff // tf),
        in_specs=[pl.BlockSpec((tm, d), row)] + [pl.BlockSpec((tm, AW), row)] * n_branches
                 + [pl.BlockSpec((tm, LANES), row)] * len(lse_branches)
                 + [pl.BlockSpec((tm, AW), row),
                  pl.BlockSpec((2 * AW, d), fixed), pl.BlockSpec((1, d), fixed),
                  pl.BlockSpec((d, tf), lambda i, j: (0, j)), pl.BlockSpec((d, tf), lambda i, j: (0, j)),
                  pl.BlockSpec((tf, d), lambda i, j: (j, 0))],
        out_specs=pl.BlockSpec((tm, d), row),
        out_shape=jax.ShapeDtypeStruct((t, d), F32),
        scratch_shapes=[pltpu.VMEM((tm, d), F32), pltpu.VMEM((tm, d), BF16), pltpu.VMEM((tm, d), F32)],
        compiler_params=_cparams("arbitrary", "arbitrary"),
        name="mix_out_ffn",
    )(x, *oa_branches, *lse_branches, ob, wo, g, w1, w3, w2)


def _gelu_exact(z):
    return 0.5 * z * (1.0 + lax.erf(z * (2.0 ** -0.5)))


def _gmlp_kernel(x_ref, g_ref, win_ref, lng_ref, lnb_ref, mix_ref, bias_ref, wout_ref, gffn_ref, wr_ref,
                 *out_refs, write_v):
    if write_v:
        xo_ref, h2_ref, logit_ref, v_ref = out_refs[:4]
    else:
        xo_ref, h2_ref, logit_ref = out_refs[:3]
    gate_ref = out_refs[-1]
    x = x_ref[...]
    tm, cd = x.shape
    h = _rms_norm(x, g_ref[...]).astype(BF16)
    z = _gelu_exact(_dot(h, win_ref[...]))
    u = z[:, :cd]
    v = z[:, cd:]
    mu = jnp.mean(v, axis=-1, keepdims=True)
    vc = v - mu
    v = vc * lax.rsqrt(jnp.mean(vc * vc, axis=-1, keepdims=True) + NORM_EPS) * lng_ref[...] + lnb_ref[...]
    if write_v:
        v_ref[...] = v
    vb = v.astype(BF16)
    bias = bias_ref[...]
    gw = cd // C_GROUPS
    for c in range(tm // CHUNK):
        rs = slice(c * CHUNK, (c + 1) * CHUNK)
        for g in range(C_GROUPS):
            cs = slice(g * gw, (g + 1) * gw)
            f = _dot(mix_ref[g], vb[rs, cs]) + bias[:, cs]
            gate_ref[rs, cs] = (u[rs, cs] * f).astype(BF16)
    xo = x + _dot(gate_ref[...], wout_ref[...])
    xo_ref[...] = xo
    h2 = _rms_norm(xo, gffn_ref[...])
    h2_ref[...] = h2
    h_hi = h2.astype(BF16)
    h_lo = (h2 - h_hi.astype(F32)).astype(BF16)
    wr = wr_ref[...]
    both = _dot(h_hi, wr)
    logit_ref[...] = both[:, :LANES] + both[:, LANES:] + _dot(h_lo, wr[:, :LANES])


def _gmlp_block(x, g, w_in, ln_g, ln_b, mix, bias_full, w_out, g_ffn, wr_hi_lo, *, tm, write_v):
    t, d = x.shape
    cd = w_out.shape[0]
    row = lambda i: (i, 0)
    fixed = lambda i: (0, 0)
    sds = jax.ShapeDtypeStruct
    out_specs = [pl.BlockSpec((tm, d), row), pl.BlockSpec((tm, d), row), pl.BlockSpec((tm, LANES), row)]
    out_shape = [sds((t, d), F32), sds((t, d), F32), sds((t, LANES), F32)]
    if write_v:
        out_specs.append(pl.BlockSpec((tm, cd), row))
        out_shape.append(sds((t, cd), F32))
    return pl.pallas_call(
        functools.partial(_gmlp_kernel, write_v=write_v),
        grid=(t // tm,),
        in_specs=[pl.BlockSpec((tm, d), row), pl.BlockSpec((1, d), fixed), pl.BlockSpec((d, 2 * cd), fixed),
                  pl.BlockSpec((1, cd), fixed), pl.BlockSpec((1, cd), fixed),
                  pl.BlockSpec((C_GROUPS, CHUNK, CHUNK), lambda i: (0, 0, 0)),
                  pl.BlockSpec((CHUNK, cd), fixed), pl.BlockSpec((cd, d), fixed), pl.BlockSpec((1, d), fixed),
                  pl.BlockSpec((d, 2 * LANES), fixed)],
        out_specs=out_specs, out_shape=out_shape,
        scratch_shapes=[pltpu.VMEM((tm, cd), BF16)],
        compiler_params=_cparams("arbitrary"),
        name="gmlp_block",
    )(x, g, w_in, ln_g, ln_b, mix, bias_full, w_out, g_ffn, wr_hi_lo)


MOE_BLOCK = 1024
MOE_SUB = 256
DISPATCH_ROWS = 512


def _dispatch_kernel(dest_ref, h_ref, xs_in_ref, xs_ref, sem):
    del xs_in_ref
    rows = h_ref.shape[0]

    def issue(r, carry):
        for k in range(TOP_K):
            d = dest_ref[0, 0, r * TOP_K + k]
            pltpu.make_async_copy(h_ref.at[pl.ds(r, 1), :], xs_ref.at[pl.ds(d, 1), :], sem).start(priority=k)
        return carry

    lax.fori_loop(0, rows, issue, 0)
    for _ in range(TOP_K):
        pltpu.make_async_copy(h_ref, xs_ref.at[pl.ds(0, rows), :], sem).wait()


def _dispatch(h, dest, xs):
    t, d = h.shape
    tg = DISPATCH_ROWS
    return pl.pallas_call(
        _dispatch_kernel,
        grid=(t // tg,),
        in_specs=[pl.BlockSpec((1, 1, tg * TOP_K), lambda i: (i, 0, 0), memory_space=pltpu.SMEM),
                  pl.BlockSpec((tg, d), lambda i: (i, 0)),
                  pl.BlockSpec(memory_space=pl.ANY)],
        out_specs=pl.BlockSpec(memory_space=pl.ANY),
        out_shape=jax.ShapeDtypeStruct(xs.shape, xs.dtype),
        scratch_shapes=[pltpu.SemaphoreType.DMA],
        input_output_aliases={2: 0},
        compiler_params=_cparams("arbitrary"),
        name="moe_dispatch",
    )(dest.reshape(t // tg, 1, tg * TOP_K), h, xs)


def _expert_kernel(be_ref, na_ref, ns_ref, xs_ref, w1_ref, w3_ref, w2_ref, y_ref, h_ref, acc_ref):
    del be_ref, na_ref
    b = pl.program_id(0)
    j = pl.program_id(1)
    n_sub = ns_ref[b]

    @pl.when((n_sub > 0) & (j == 0))
    def _():
        h_ref[...] = xs_ref[...].astype(BF16)
        acc_ref[...] = jnp.zeros_like(acc_ref)

    for k in range(1, MOE_BLOCK // MOE_SUB + 1):
        @pl.when(n_sub == k)
        def _(rows=k * MOE_SUB):
            h = h_ref[0:rows, :]
            w1 = w1_ref[...].astype(BF16)
            w3 = w3_ref[...].astype(BF16)
            gate = (_silu(_dot(h, w1)) * _dot(h, w3)).astype(BF16)
            acc_ref[0:rows, :] += _dot(gate, w2_ref[...].astype(BF16))

    @pl.when(j == pl.num_programs(1) - 1)
    def _():
        y_ref[...] = jnp.where(n_sub > 0, acc_ref[...], 0.0)


def _experts(xs, block_expert, n_active, n_sub, w1, w3, w2):
    n_rows, d = xs.shape
    ff = w1.shape[2]
    tf = _ff_tile(ff, 512)
    n_blocks = n_rows // MOE_BLOCK
    nj = ff // tf
    def wcol(b, j, be, na, ns):
        live = b < na[0]
        return (be[b], 0, jnp.where(live, j, nj - 1))

    def wrow(b, j, be, na, ns):
        live = b < na[0]
        return (be[b], jnp.where(live, j, nj - 1), 0)

    def xrow(b, j, be, na, ns):
        return (jnp.minimum(b, na[0] - 1), 0)

    grid_spec = pltpu.PrefetchScalarGridSpec(
        num_scalar_prefetch=3,
        grid=(n_blocks, nj),
        in_specs=[pl.BlockSpec((MOE_BLOCK, d), xrow),
                  pl.BlockSpec((None, d, tf), wcol), pl.BlockSpec((None, d, tf), wcol),
                  pl.BlockSpec((None, tf, d), wrow)],
        out_specs=pl.BlockSpec((MOE_BLOCK, d), lambda b, j, be, na, ns: (b, 0)),
        scratch_shapes=[pltpu.VMEM((MOE_BLOCK, d), BF16), pltpu.VMEM((MOE_BLOCK, d), F32)],
    )
    return pl.pallas_call(
        _expert_kernel,
        grid_spec=grid_spec,
        out_shape=jax.ShapeDtypeStruct((n_rows, d), F32),
        compiler_params=_cparams("arbitrary", "arbitrary"),
        name="moe_experts",
    )(block_expert, n_active, n_sub, xs, w1, w3, w2)


def _combine_kernel(dest_ref, x_ref, gate_ref, y_ref, out_ref, buf, sem):
    rows = x_ref.shape[0]

    def issue(r, carry):
        for k in range(TOP_K):
            d = dest_ref[0, 0, r * TOP_K + k]
            pltpu.make_async_copy(y_ref.at[pl.ds(d, 1), :], buf.at[k, pl.ds(r, 1), :], sem).start(priority=k)
        return carry

    lax.fori_loop(0, rows, issue, 0)
    for k in range(TOP_K):
        pltpu.make_async_copy(y_ref.at[pl.ds(0, rows), :], buf.at[k], sem).wait()
    gates = gate_ref[...]
    y = buf[0] * gates[:, 0:1]
    for k in range(1, TOP_K):
        y = y + buf[k] * gates[:, k:k + 1]
    out_ref[...] = x_ref[...] + y


def _combine(x, gates, dest, y):
    t, d = x.shape
    tg = DISPATCH_ROWS
    return pl.pallas_call(
        _combine_kernel,
        grid=(t // tg,),
        in_specs=[pl.BlockSpec((1, 1, tg * TOP_K), lambda i: (i, 0, 0), memory_space=pltpu.SMEM),
                  pl.BlockSpec((tg, d), lambda i: (i, 0)),
                  pl.BlockSpec((tg, TOP_K), lambda i: (i, 0)),
                  pl.BlockSpec(memory_space=pl.ANY)],
        out_specs=pl.BlockSpec((tg, d), lambda i: (i, 0)),
        out_shape=jax.ShapeDtypeStruct((t, d), F32),
        scratch_shapes=[pltpu.VMEM((TOP_K, tg, d), F32), pltpu.SemaphoreType.DMA],
        compiler_params=_cparams("arbitrary"),
        name="moe_combine",
    )(dest.reshape(t // tg, 1, tg * TOP_K), x, gates, y)


def _route(logits):
    t = logits.shape[0]
    top_val, top_idx = lax.top_k(logits, TOP_K)
    gates = jax.nn.softmax(top_val, axis=-1)
    expert = top_idx.reshape(-1)
    one_hot = (expert[:, None] == jnp.arange(N_EXPERTS, dtype=expert.dtype)[None, :]).astype(jnp.int32)
    running = jnp.cumsum(one_hot, axis=0)
    rank = jnp.sum(running * one_hot, axis=1) - 1
    counts = running[-1]
    padded = (counts + MOE_BLOCK - 1) // MOE_BLOCK * MOE_BLOCK
    pad_ends = jnp.cumsum(padded)
    pad_starts = pad_ends - padded
    dest = (jnp.sum(pad_starts[None, :] * one_hot, axis=1) + rank).astype(jnp.int32)
    n_blocks = -(-(t * TOP_K) // MOE_BLOCK) + N_EXPERTS
    block_start = jnp.arange(n_blocks, dtype=jnp.int32) * MOE_BLOCK
    block_expert = jnp.sum((block_start[:, None] >= pad_ends[None, :]).astype(jnp.int32), axis=1)
    block_expert = jnp.minimum(block_expert, N_EXPERTS - 1).astype(jnp.int32)
    used_end = (pad_starts + counts)[block_expert]
    used_rows = jnp.clip(used_end - block_start, 0, MOE_BLOCK)
    used_rows = jnp.where(block_start < pad_ends[-1], used_rows, 0)
    n_sub = ((used_rows + MOE_SUB - 1) // MOE_SUB).astype(jnp.int32)
    n_active = (pad_ends[-1:] // MOE_BLOCK).astype(jnp.int32)
    return gates, dest.reshape(t, TOP_K), block_expert, n_sub, n_active, n_blocks * MOE_BLOCK


def _moe(groups, w1, w3, w2):
    sizes = [x.shape[0] for x, _, _ in groups]
    gates, dest, block_expert, n_sub, n_active, n_rows = _route(jnp.concatenate([l for _, _, l in groups]))
    xs = jnp.zeros((n_rows, w1.shape[1]), F32)
    start = 0
    for (_, h2, _), size in zip(groups, sizes):
        xs = _dispatch(h2, dest[start:start + size].reshape(-1), xs)
        start += size
    y = _experts(xs, block_expert, n_active, n_sub, w1, w3, w2)
    outs, start = [], 0
    for (x, _, _), size in zip(groups, sizes):
        outs.append(_combine(x, gates[start:start + size], dest[start:start + size].reshape(-1), y))
        start += size
    return outs


def _rope_tables(pos):
    inv_freq = ROPE_THETA ** (-jnp.arange(HALF_DIM, dtype=F32) / HALF_DIM)
    ang = pos.astype(F32)[:, None] * inv_freq[None, :]
    cos, sin = jnp.cos(ang), jnp.sin(ang)
    reps = LANES // HEAD_DIM
    return jnp.tile(cos, (1, 2 * reps)), jnp.tile(jnp.concatenate([-sin, sin], axis=1), (1, reps))


def _qk_gain_row(qk_gain):
    ones_a = jnp.ones((AW,), F32)
    ones_b = jnp.ones((KVW,), F32)
    return jnp.concatenate([jnp.tile(qk_gain[0], N_HEADS), jnp.tile(qk_gain[1], N_HEADS), ones_a,
                            jnp.tile(qk_gain[2], N_HEADS), jnp.tile(qk_gain[3], B_KV_HEADS), ones_b])[None, :]


def _row_block(t):
    return 512 if t % 512 == 0 else t


def kernel(x_prompt, x_sample, cache_a_k, cache_a_v, cache_b_k, cache_b_v, norm_mix, norm_ffn, ab_w_in,
           ab_qk_gain, ab_sinks, ab_w_out, ffn_w1, ffn_w3, ffn_w2, c_w_in, c_ln_g, c_ln_b, c_w_s, c_b_s,
           c_w_out, moe_router, moe_w1, moe_w3, moe_w2):
    n_p, s_p, d = x_prompt.shape
    n_s, t_new, _ = x_sample.shape
    depth = norm_mix.shape[0]
    past_len = 8192
    xp = x_prompt.reshape(n_p * s_p, d)
    xs = x_sample.reshape(n_s * t_new, d)
    pos_p = jnp.tile(jnp.arange(s_p), n_p)
    pos_s = jnp.tile(past_len + jnp.arange(t_new), n_s)
    cos_p, sin_p = _rope_tables(pos_p)
    cos_s, sin_s = _rope_tables(pos_s)
    row = lambda a: a[None, :]
    akp, avp, bkp, bvp, aks, avs, bks, bvs, cvs = [], [], [], [], [], [], [], [], []
    la_p, lb_p = min(BAND * A_DILATIONS[-1], s_p), min(BAND, s_p)
    for layer in range(depth):
        i = layer // 2
        if layer % 2 == 0:
            w_in = ab_w_in[i].astype(BF16)
            gain_row = _qk_gain_row(ab_qk_gain[i])
            w_out = ab_w_out[i].astype(BF16)
            w1, w3, w2 = ffn_w1[i].astype(BF16), ffn_w3[i].astype(BF16), ffn_w2[i].astype(BF16)
            sinks = ab_sinks[i].astype(F32)
            sink_rows = jnp.broadcast_to(sinks[:, None], (N_HEADS, LANES))
            sink_lane_row = jnp.pad(sinks, (0, LANES - N_HEADS))[None, :]
            qa, ka, va, qb, kb, vb, ka16, va16, kb16, vb16 = _qkv_proj(
                xp, row(norm_mix[layer]), w_in, gain_row, cos_p, sin_p,
                tm=_row_block(xp.shape[0]), q_dtype=BF16, bf16_kv=True)
            seq = lambda a: a.reshape(n_p, s_p, a.shape[-1])
            branches = [_attn_a_branch(seq(qa), seq(ka16), seq(va16), dilation=r) for r in A_DILATIONS]
            oas = [o.reshape(n_p * s_p, AW) for o, _ in branches]
            lses = [l.reshape(n_p * s_p, LANES) for _, l in branches]
            ob = _attn_b(seq(qb), seq(kb16), seq(vb16), sink_rows).reshape(n_p * s_p, AW)
            xp = _mix_out_ffn(xp, oas, lses, ob, w_out, row(norm_ffn[layer]), w1, w3, w2,
                              tm=_row_block(xp.shape[0]))
            heads = lambda a, nh: a.reshape(n_p, s_p, nh, HEAD_DIM)
            akp.append(heads(ka, N_HEADS)[:, s_p - la_p:])
            avp.append(heads(va, N_HEADS)[:, s_p - la_p:])
            bkp.append(heads(kb, B_KV_HEADS)[:, s_p - lb_p:])
            bvp.append(heads(vb, B_KV_HEADS)[:, s_p - lb_p:])
            qa, ka, va, qb, kb, vb = _qkv_proj(xs, row(norm_mix[layer]), w_in, gain_row, cos_s, sin_s,
                                               tm=_row_block(xs.shape[0]), q_dtype=F32, bf16_kv=False)
            flat = lambda c: c.reshape(c.shape[0], c.shape[1], c.shape[2] * c.shape[3])
            oa, ob = _sample_attn(qa, ka, va, qb, kb, vb, cache_a_k[i], cache_a_v[i],
                                  flat(cache_b_k[i]), flat(cache_b_v[i]), sink_lane_row, t_new=t_new)
            xs = _mix_out_ffn(xs, [oa], [], ob, w_out, row(norm_ffn[layer]), w1, w3, w2,
                              tm=_row_block(xs.shape[0]))
            heads_s = lambda a, nh: a.reshape(n_s, t_new, nh, HEAD_DIM)
            aks.append(heads_s(ka, N_HEADS))
            avs.append(heads_s(va, N_HEADS))
            bks.append(heads_s(kb, B_KV_HEADS))
            bvs.append(heads_s(vb, B_KV_HEADS))
        else:
            w_in = c_w_in[i].astype(BF16)
            w_out = c_w_out[i].astype(BF16)
            tril = jnp.tril(jnp.ones((CHUNK, CHUNK), F32))
            mix_p = (c_w_s[i] * tril).astype(BF16)
            gw = w_out.shape[0] // C_GROUPS
            bias_p = jnp.repeat(c_b_s[i].T, gw, axis=1)
            per_tile = CHUNK // t_new
            mix_s = jnp.einsum("ab,gij->gaibj", jnp.eye(per_tile, dtype=F32), (c_w_s[i] * tril)[:, :t_new, :t_new])
            mix_s = mix_s.reshape(C_GROUPS, CHUNK, CHUNK).astype(BF16)
            bias_s = jnp.tile(bias_p[:t_new], (per_tile, 1))
            wr = moe_router[i]
            wr_hi = wr.astype(BF16)
            wr_lo = (wr - wr_hi.astype(F32)).astype(BF16)
            pad_e = lambda a: jnp.pad(a, ((0, 0), (0, LANES - N_EXPERTS)))
            wr_hi_lo = jnp.concatenate([pad_e(wr_hi), pad_e(wr_lo)], axis=1)
            common = (row(c_ln_g[i]), row(c_ln_b[i]))
            xp1, hp, lp = _gmlp_block(xp, row(norm_mix[layer]), w_in, *common, mix_p, bias_p, w_out,
                                      row(norm_ffn[layer]), wr_hi_lo, tm=_row_block(xp.shape[0]), write_v=False)
            xs1, hs, ls, v_new = _gmlp_block(xs, row(norm_mix[layer]), w_in, *common, mix_s, bias_s, w_out,
                                             row(norm_ffn[layer]), wr_hi_lo, tm=_row_block(xs.shape[0]),
                                             write_v=True)
            cvs.append(v_new.reshape(n_s, t_new, v_new.shape[-1]))
            xp, xs = _moe([(xp1, hp, lp[:, :N_EXPERTS]), (xs1, hs, ls[:, :N_EXPERTS])],
                          moe_w1[i], moe_w3[i], moe_w2[i])
    return (xp.reshape(n_p, s_p, d), xs.reshape(n_s, t_new, d),
            jnp.stack(akp), jnp.stack(avp), jnp.stack(bkp), jnp.stack(bvp),
            jnp.stack(aks), jnp.stack(avs), jnp.stack(bks), jnp.stack(bvs),
            jnp.stack(cvs))
```

```python
import functools

import jax
import jax.numpy as jnp
from jax import lax
from jax.experimental import pallas as pl
from jax.experimental.pallas import tpu as pltpu

F32 = jnp.float32
BF16 = jnp.bfloat16

HEAD_DIM = 64
HALF_DIM = HEAD_DIM // 2
N_HEADS = 8
B_KV_HEADS = 2
A_DILATIONS = (1, 4, 16)
BAND = 128
ROPE_THETA = 10000.0
NORM_EPS = 1e-6
NEG_INF = -1e30
CHUNK = 128
C_GROUPS = 8
N_EXPERTS = 8
TOP_K = 2

LANES = 128
SUBLANES = 8
VMEM_LIMIT_BYTES = 56 * 1024 * 1024

AW = N_HEADS * HEAD_DIM
KVW = B_KV_HEADS * HEAD_DIM
HEADS_PER_VREG = LANES // HEAD_DIM
N_HEAD_BLOCKS = AW // LANES


def _cparams(*sem):
    return pltpu.CompilerParams(dimension_semantics=sem, vmem_limit_bytes=VMEM_LIMIT_BYTES)


def _rms_norm(x, g):
    return x * lax.rsqrt(jnp.mean(x * x, axis=-1, keepdims=True) + NORM_EPS) * g


def _dot(a, b):
    return jnp.dot(a, b, preferred_element_type=F32)


def _dot_nt(a, b):
    return lax.dot_general(a, b, (((1,), (1,)), ((), ())), preferred_element_type=F32)


_QA_BLOCKS = range(0, 4)
_KA_BLOCKS = range(4, 8)
_VA_BLOCKS = range(8, 12)
_QB_BLOCKS = range(12, 16)
_KB_BLOCK = 16
_VB_BLOCK = 17


def _proj_kernel(x_ref, g_ref, w_ref, gain_ref, cos_ref, sin_ref,
                 qa_ref, ka_ref, va_ref, qb_ref, kb_ref, vb_ref, *bf16_kv_refs):
    h = _rms_norm(x_ref[...], g_ref[...]).astype(BF16)
    p = _dot(h, w_ref[...])
    tm = p.shape[0]
    cos = cos_ref[...]
    sin = sin_ref[...]
    lane = lax.broadcasted_iota(jnp.int32, (tm, LANES), 1)
    low_head = lane < HEAD_DIM
    first_half = (lane & HALF_DIM) == 0

    def norm_rope(blk):
        pc = p[:, blk * LANES:(blk + 1) * LANES]
        sq = pc * pc
        ms_lo = jnp.sum(jnp.where(low_head, sq, 0.0), axis=-1, keepdims=True) * (1.0 / HEAD_DIM)
        ms_hi = jnp.sum(jnp.where(low_head, 0.0, sq), axis=-1, keepdims=True) * (1.0 / HEAD_DIM)
        inv = jnp.where(low_head, lax.rsqrt(ms_lo + NORM_EPS), lax.rsqrt(ms_hi + NORM_EPS))
        y = pc * inv * gain_ref[:, blk * LANES:(blk + 1) * LANES]
        partner = jnp.where(first_half, pltpu.roll(y, LANES - HALF_DIM, 1), pltpu.roll(y, HALF_DIM, 1))
        return y * cos + partner * sin

    scale = HEAD_DIM ** -0.5
    for j, blk in enumerate(_QA_BLOCKS):
        qa_ref[:, j * LANES:(j + 1) * LANES] = (norm_rope(blk) * scale).astype(qa_ref.dtype)
    for j, blk in enumerate(_KA_BLOCKS):
        ka_ref[:, j * LANES:(j + 1) * LANES] = norm_rope(blk)
    for j, blk in enumerate(_VA_BLOCKS):
        va_ref[:, j * LANES:(j + 1) * LANES] = p[:, blk * LANES:(blk + 1) * LANES]
    for j, blk in enumerate(_QB_BLOCKS):
        qb_ref[:, j * LANES:(j + 1) * LANES] = (norm_rope(blk) * scale).astype(qb_ref.dtype)
    kb_ref[...] = norm_rope(_KB_BLOCK)
    vb_ref[...] = p[:, _VB_BLOCK * LANES:(_VB_BLOCK + 1) * LANES]
    if bf16_kv_refs:
        ka16_ref, va16_ref, kvb16_ref = bf16_kv_refs
        ka16_ref[...] = ka_ref[...].astype(BF16)
        va16_ref[...] = va_ref[...].astype(BF16)
        kb = kb_ref[...]
        vb = vb_ref[...]
        parts = (kb, pltpu.roll(kb, HEAD_DIM, 1), vb, pltpu.roll(vb, HEAD_DIM, 1))
        for j, part in enumerate(parts):
            kvb16_ref[:, j * KVW:(j + 1) * KVW] = part.astype(BF16)


def _qkv_proj(x, g, w, gain_row, cos, sin, *, tm, q_dtype, bf16_kv):
    t, d = x.shape
    n_out = w.shape[1]
    row = lambda i: (i, 0)
    fixed = lambda i: (0, 0)
    sds = jax.ShapeDtypeStruct
    widths = (AW, AW, AW, AW, KVW, KVW) + ((AW, AW, 4 * KVW) if bf16_kv else ())
    dtypes = (q_dtype, F32, F32, q_dtype, F32, F32) + ((BF16,) * 3 if bf16_kv else ())
    return pl.pallas_call(
        _proj_kernel,
        grid=(t // tm,),
        in_specs=[pl.BlockSpec((tm, d), row), pl.BlockSpec((1, d), fixed), pl.BlockSpec((d, n_out), fixed),
                  pl.BlockSpec((1, n_out), fixed), pl.BlockSpec((tm, LANES), row), pl.BlockSpec((tm, LANES), row)],
        out_specs=[pl.BlockSpec((tm, wd), row) for wd in widths],
        out_shape=[sds((t, wd), dt) for wd, dt in zip(widths, dtypes)],
        compiler_params=_cparams("arbitrary"),
        name="qkv_proj",
    )(x, g, w, gain_row, cos, sin)


def _fill_kv(kbuf, vbuf, kp_ref, kc_ref, vp_ref, vc_ref):
    kbuf[0:BAND, :] = kp_ref[...].astype(BF16)
    kbuf[BAND:, :] = kc_ref[...].astype(BF16)
    vbuf[0:BAND, :] = vp_ref[...].astype(BF16)
    vbuf[BAND:, :] = vc_ref[...].astype(BF16)


def _band_mask(first_block):
    row = lax.broadcasted_iota(jnp.int32, (BAND, 2 * BAND), 0)
    col = lax.broadcasted_iota(jnp.int32, (BAND, 2 * BAND), 1)
    dist = BAND + row - col
    first_key = jnp.where(first_block, BAND, 0)
    return (dist >= 0) & (dist <= BAND) & (col >= first_key)


def _masked_softmax_pv(qm, kc, vc, mask, floor_row=None):
    if floor_row is None:
        s = jnp.where(mask, _dot_nt(qm, kc), NEG_INF)
        m = jnp.max(s, axis=-1, keepdims=True)
        p = jnp.exp(s - m)
    else:
        s = jnp.where(mask, _dot_nt(qm, kc), floor_row)
        m = jnp.max(s, axis=-1, keepdims=True)
        p = jnp.where(mask, jnp.exp(s - m), 0.0)
    l = jnp.sum(p, axis=-1, keepdims=True)
    return m, l, _dot(p.astype(BF16), vc)


def _two_head_attention(q2, kc, vc, mask2, low_head):
    zero = jnp.zeros_like(q2)
    qm = jnp.concatenate([jnp.where(low_head, q2, zero), jnp.where(low_head, zero, q2)], axis=0)
    m, l, pv = _masked_softmax_pv(qm, kc, vc, mask2)
    o = pv * (1.0 / l)
    return m, l, jnp.where(low_head, o[:BAND], o[BAND:])


def _attn_a_kernel(q_ref, kc_ref, kp_ref, vc_ref, vp_ref, o_ref, lse_ref, kbuf, vbuf, *, tq):
    blk = pl.program_id(2)
    _fill_kv(kbuf, vbuf, kp_ref, kc_ref, vp_ref, vc_ref)
    lane = lax.broadcasted_iota(jnp.int32, (BAND, LANES), 1)
    low_head = lane < HEAD_DIM

    def sub_block(sb, carry):
        r0 = pl.multiple_of(sb * BAND, BAND)
        mask = _band_mask((blk + sb) == 0)
        mask2 = jnp.concatenate([mask, mask], axis=0)
        lse_tile = jnp.zeros((BAND, LANES), F32)
        for hb in range(N_HEAD_BLOCKS):
            cs = slice(hb * LANES, (hb + 1) * LANES)
            m, l, o = _two_head_attention(q_ref[pl.ds(r0, BAND), cs], kbuf[pl.ds(r0, 2 * BAND), cs],
                                          vbuf[pl.ds(r0, 2 * BAND), cs], mask2, low_head)
            o_ref[pl.ds(r0, BAND), cs] = o.astype(o_ref.dtype)
            lse = m + jnp.log(l)
            for hh in range(HEADS_PER_VREG):
                lse_tile = jnp.where(lane == hb * HEADS_PER_VREG + hh, lse[hh * BAND:(hh + 1) * BAND], lse_tile)
        lse_ref[pl.ds(r0, BAND), :] = lse_tile
        return carry

    lax.fori_loop(0, tq // BAND, sub_block, 0)


def _attn_a_branch(q, k, v, *, dilation):
    n, s, w = q.shape
    r = dilation
    sub_len = s // r
    tq = min(512, sub_len)
    view = lambda a: a.reshape(n, sub_len, r * a.shape[-1])
    cur = lambda b, c, i: (b, i, c)
    prv = lambda b, c, i: (b, jnp.maximum(i * (tq // BAND) - 1, 0), c)
    blk_cur = pl.BlockSpec((None, tq, w), cur)
    blk_prv = pl.BlockSpec((None, BAND, w), prv)
    sds = jax.ShapeDtypeStruct
    o, lse = pl.pallas_call(
        functools.partial(_attn_a_kernel, tq=tq),
        grid=(n, r, sub_len // tq),
        in_specs=[blk_cur, blk_cur, blk_prv, blk_cur, blk_prv],
        out_specs=[blk_cur, pl.BlockSpec((None, tq, LANES), cur)],
        out_shape=[sds((n, sub_len, r * w), BF16), sds((n, sub_len, r * LANES), F32)],
        scratch_shapes=[pltpu.VMEM((tq + BAND, w), BF16), pltpu.VMEM((tq + BAND, w), BF16)],
        compiler_params=_cparams("arbitrary", "arbitrary", "arbitrary"),
        name=f"attn_a_dil{r}",
    )(view(q), view(k), view(k), view(v), view(v))
    return o.reshape(n, s, w), lse.reshape(n, s, LANES)


_KVB_K, _KVB_K_SWAPPED, _KVB_V, _KVB_V_SWAPPED = range(4)
_KVB_WIDTH = 4 * KVW


def _attn_b_kernel(q_ref, kvc_ref, kvp_ref, sink_ref, o_ref, kvbuf, *, tq):
    blk = pl.program_id(1)
    kvbuf[0:BAND, :] = kvp_ref[...]
    kvbuf[BAND:, :] = kvc_ref[...]
    lane = lax.broadcasted_iota(jnp.int32, (BAND, LANES), 1)
    low_head = lane < HEAD_DIM
    group = N_HEADS // B_KV_HEADS
    plain = [h for h in range(N_HEADS) if h % HEADS_PER_VREG == h // group]
    swapped = [h for h in range(N_HEADS) if h % HEADS_PER_VREG != h // group]

    def sub_block(sb, carry):
        r0 = pl.multiple_of(sb * BAND, BAND)
        mask = _band_mask((blk + sb) == 0)
        blocks = [q_ref[pl.ds(r0, BAND), hb * LANES:(hb + 1) * LANES] for hb in range(N_HEAD_BLOCKS)]
        outs = [None] * N_HEADS
        for heads, k_blk, v_blk in ((plain, _KVB_K, _KVB_V), (swapped, _KVB_K_SWAPPED, _KVB_V_SWAPPED)):
            kc = kvbuf[pl.ds(r0, 2 * BAND), k_blk * KVW:(k_blk + 1) * KVW]
            vc = kvbuf[pl.ds(r0, 2 * BAND), v_blk * KVW:(v_blk + 1) * KVW]
            for h in heads:
                q2 = blocks[h // HEADS_PER_VREG]
                own_half = low_head if h % HEADS_PER_VREG == 0 else jnp.logical_not(low_head)
                qm = jnp.where(own_half, q2, jnp.zeros_like(q2))
                sink = sink_ref[h:h + 1, :]
                m, l, pv = _masked_softmax_pv(qm, kc, vc, mask, floor_row=jnp.concatenate([sink, sink], axis=1))
                outs[h] = pv * (1.0 / (l + jnp.exp(sink - m)))
        for hb in range(N_HEAD_BLOCKS):
            o_ref[pl.ds(r0, BAND), hb * LANES:(hb + 1) * LANES] = jnp.where(
                low_head, outs[hb * HEADS_PER_VREG], outs[hb * HEADS_PER_VREG + 1]).astype(o_ref.dtype)
        return carry

    lax.fori_loop(0, tq // BAND, sub_block, 0)


def _attn_b(q, kv, sinks_rows):
    n, s, w = q.shape
    tq = min(512, s)
    cur = lambda b, i: (b, i, 0)
    prv = lambda b, i: (b, jnp.maximum(i * (tq // BAND) - 1, 0), 0)
    return pl.pallas_call(
        functools.partial(_attn_b_kernel, tq=tq),
        grid=(n, s // tq),
        in_specs=[pl.BlockSpec((None, tq, w), cur), pl.BlockSpec((None, tq, _KVB_WIDTH), cur),
                  pl.BlockSpec((None, BAND, _KVB_WIDTH), prv),
                  pl.BlockSpec((N_HEADS, LANES), lambda b, i: (0, 0))],
        out_specs=pl.BlockSpec((None, tq, w), cur),
        out_shape=jax.ShapeDtypeStruct((n, s, w), BF16),
        scratch_shapes=[pltpu.VMEM((tq + BAND, _KVB_WIDTH), BF16)],
        compiler_params=_cparams("arbitrary", "arbitrary"),
        name="attn_b",
    )(q, kv, kv, sinks_rows)


NEW_ROWS = 8


def _branch_multiplicity(t_query, key_pos, la):
    back = la + t_query - key_pos
    mult = jnp.zeros(back.shape, F32)
    for r in A_DILATIONS:
        mult = mult + ((back >= 0) & (back <= BAND * r) & (back % r == 0)).astype(F32)
    return mult


def _sample_attn_kernel(qa_ref, kna_ref, vna_ref, qb_ref, knb_ref, vnb_ref,
                        kt_ref, vt_ref, cbk_ref, cbv_ref, sink_ref,
                        oa_ref, ob_ref, *, nb, t_new):
    la = kt_ref.shape[-1]
    t_c = lax.broadcasted_iota(jnp.int32, (NEW_ROWS, la), 0)
    t_c = jnp.where(t_c < t_new, t_c, 0)
    mult_c = _branch_multiplicity(t_c, lax.broadcasted_iota(jnp.int32, (NEW_ROWS, la), 1), la)
    t_n = lax.broadcasted_iota(jnp.int32, (NEW_ROWS, NEW_ROWS), 0)
    t_n = jnp.where(t_n < t_new, t_n, 0)
    key_n = lax.broadcasted_iota(jnp.int32, (NEW_ROWS, NEW_ROWS), 1)
    mult_n = jnp.where(key_n < t_new, _branch_multiplicity(t_n, la + key_n, la), 0.0)

    def head_attn_a(q_h, kt, vt, kn_h, vn_h):
        s_c = jnp.where(mult_c > 0.0, _dot(q_h, kt), NEG_INF)
        s_n = jnp.where(mult_n > 0.0, _dot_nt(q_h, kn_h), NEG_INF)
        m = jnp.maximum(jnp.max(s_c, axis=-1, keepdims=True), jnp.max(s_n, axis=-1, keepdims=True))
        p_c = jnp.exp(s_c - m) * mult_c
        p_n = jnp.exp(s_n - m) * mult_n
        l = jnp.sum(p_c, axis=-1, keepdims=True) + jnp.sum(p_n, axis=-1, keepdims=True)
        return (_dot_nt(p_c.astype(BF16), vt) + _dot(p_n.astype(BF16), vn_h)) / l

    lane_w = lax.broadcasted_iota(jnp.int32, (AW, LANES), 0) // HEAD_DIM
    head_w = lax.broadcasted_iota(jnp.int32, (AW, LANES), 1)
    head_sum = (lane_w == head_w).astype(BF16)
    head_e = lax.broadcasted_iota(jnp.int32, (LANES, AW), 0)
    lane_e = lax.broadcasted_iota(jnp.int32, (LANES, AW), 1) // HEAD_DIM
    head_expand = (head_e == lane_e).astype(BF16)
    row_c = lax.broadcasted_iota(jnp.int32, (BAND, 1), 0)
    row_n = lax.broadcasted_iota(jnp.int32, (NEW_ROWS, 1), 0)
    lane = lax.broadcasted_iota(jnp.int32, (BAND, LANES), 1)
    low_head = lane < HEAD_DIM
    low_head_n = low_head[:NEW_ROWS]
    sink_row = sink_ref[...]

    def scores(k, q_row):
        return _dot((k * q_row).astype(BF16), head_sum)

    def attend(q_row, blocks, sink=None):
        ss = []
        m = None
        for k, _, valid, _ in blocks:
            s = scores(k, q_row)
            if valid is not None:
                s = jnp.where(valid, s, NEG_INF)
            ss.append(s)
            bm = jnp.max(s, axis=0, keepdims=True)
            m = bm if m is None else jnp.maximum(m, bm)
        if sink is not None:
            m = jnp.maximum(m, sink)
        ps = []
        l = jnp.exp(sink - m) if sink is not None else 0.0
        for s, (_, _, valid, mult) in zip(ss, blocks):
            p = jnp.exp(s - m)
            if mult is not None:
                p = p * mult
            ps.append(p)
            l = l + jnp.sum(p, axis=0, keepdims=True)
        inv_l = 1.0 / l
        acc = None
        for p, (_, v, _, _) in zip(ps, blocks):
            pe = _dot((p * inv_l).astype(BF16), head_expand)
            part = jnp.sum(pe * v, axis=0, keepdims=True)
            acc = part if acc is None else acc + part
        return acc

    def expand_kv(x, low):
        sw = pltpu.roll(x, HEAD_DIM, 1)
        kv0 = jnp.where(low, x, sw)
        kv1 = jnp.where(low, sw, x)
        return jnp.concatenate([kv0, kv0, kv1, kv1], axis=1)

    def one_sequence(n, carry):
        new0 = pl.multiple_of(n * NEW_ROWS, NEW_ROWS)
        qa = qa_ref[pl.ds(new0, NEW_ROWS), :].astype(BF16)
        kna = kna_ref[pl.ds(new0, NEW_ROWS), :].astype(BF16)
        vna = vna_ref[pl.ds(new0, NEW_ROWS), :].astype(BF16)
        heads_out = []
        for h in range(N_HEADS):
            hs = slice(h * HEAD_DIM, (h + 1) * HEAD_DIM)
            heads_out.append(head_attn_a(qa[:, hs], kt_ref[n, h].astype(BF16), vt_ref[n, h].astype(BF16),
                                         kna[:, hs], vna[:, hs]))
        oa_ref[pl.ds(new0, NEW_ROWS), :] = jnp.concatenate(heads_out, axis=1).astype(oa_ref.dtype)
        kb_c = expand_kv(cbk_ref[n], low_head)
        vb_c = expand_kv(cbv_ref[n], low_head)
        kb_n = expand_kv(knb_ref[pl.ds(new0, NEW_ROWS), :], low_head_n)
        vb_n = expand_kv(vnb_ref[pl.ds(new0, NEW_ROWS), :], low_head_n)
        qb = qb_ref[pl.ds(new0, NEW_ROWS), :]
        out_b = jnp.zeros((NEW_ROWS, AW), F32)
        for t in range(t_new):
            acc_b = attend(qb[t:t + 1, :], [
                (kb_c, vb_c, row_c >= t, None),
                (kb_n, vb_n, row_n <= t, None),
            ], sink=sink_row)
            out_b = jnp.where(row_n == t, acc_b, out_b)
        ob_ref[pl.ds(new0, NEW_ROWS), :] = out_b.astype(ob_ref.dtype)
        return carry

    lax.fori_loop(0, nb, one_sequence, 0)


def _pad_new_rows(a, n_seq, t_new):
    w = a.shape[-1]
    a = a.reshape(n_seq, t_new, w)
    a = jnp.pad(a, ((0, 0), (0, NEW_ROWS - t_new), (0, 0)))
    return a.reshape(n_seq * NEW_ROWS, w)


def _sample_attn(qa, ka, va, qb, kb, vb, cache_a_k, cache_a_v, cache_b_k, cache_b_v, sink_row, *, t_new):
    n_seq, la, nh, hd = cache_a_k.shape
    w = nh * hd
    assert t_new <= NEW_ROWS
    nb = 2 if n_seq % 2 == 0 else 1
    padded = [_pad_new_rows(a, n_seq, t_new) for a in (qa, ka, va, qb, kb, vb)]
    dim_major = lambda c: jnp.transpose(c, (0, 2, 3, 1))
    new_a = pl.BlockSpec((nb * NEW_ROWS, w), lambda i: (i, 0))
    new_b = pl.BlockSpec((nb * NEW_ROWS, KVW), lambda i: (i, 0))
    spec_ca = pl.BlockSpec((nb, nh, hd, la), lambda i: (i, 0, 0, 0))
    spec_cb = pl.BlockSpec((nb, BAND, KVW), lambda i: (i, 0, 0))
    sds = jax.ShapeDtypeStruct
    oa, ob = pl.pallas_call(
        functools.partial(_sample_attn_kernel, nb=nb, t_new=t_new),
        grid=(n_seq // nb,),
        in_specs=[new_a, new_a, new_a, new_a, new_b, new_b, spec_ca, spec_ca, spec_cb, spec_cb,
                  pl.BlockSpec((1, LANES), lambda i: (0, 0))],
        out_specs=[new_a, new_a],
        out_shape=[sds((n_seq * NEW_ROWS, w), BF16), sds((n_seq * NEW_ROWS, w), BF16)],
        compiler_params=_cparams("arbitrary"),
        name="sample_attn",
    )(*padded, dim_major(cache_a_k), dim_major(cache_a_v), cache_b_k, cache_b_v, sink_row)
    unpad = lambda o: o.reshape(n_seq, NEW_ROWS, w)[:, :t_new].reshape(n_seq * t_new, w)
    return unpad(oa), unpad(ob)


def _silu(a):
    return a * (1.0 / (1.0 + jnp.exp(-a)))


def _merge_branches(o_refs, lse_refs):
    lses = [r[...] for r in lse_refs]
    top = functools.reduce(jnp.maximum, lses)
    ws = [jnp.exp(l - top) for l in lses]
    inv_den = 1.0 / functools.reduce(lambda a, b: a + b, ws)
    head = lax.broadcasted_iota(jnp.int32, (LANES, AW), 0)
    lane_head = lax.broadcasted_iota(jnp.int32, (LANES, AW), 1) // HEAD_DIM
    expand = (head == lane_head).astype(BF16)
    oa = None
    for w, o_ref in zip(ws, o_refs):
        w = w * inv_den
        w_hi = w.astype(BF16)
        w_lo = (w - w_hi.astype(F32)).astype(BF16)
        part = (_dot(w_hi, expand) + _dot(w_lo, expand)) * o_ref[...].astype(F32)
        oa = part if oa is None else oa + part
    return oa.astype(BF16)


def _mix_out_ffn_kernel(x_ref, *refs, n_branches):
    o_refs, refs = refs[:n_branches], refs[n_branches:]
    if n_branches > 1:
        lse_refs, refs = refs[:n_branches], refs[n_branches:]
    ob_ref, wo_ref, g_ref, w1_ref, w3_ref, w2_ref, out_ref, x1_ref, h_ref, acc_ref = refs
    j = pl.program_id(1)

    @pl.when(j == 0)
    def _():
        oa = _merge_branches(o_refs, lse_refs) if n_branches > 1 else o_refs[0][...]
        x1 = x_ref[...] + _dot(oa, wo_ref[0:AW, :]) + _dot(ob_ref[...], wo_ref[AW:, :])
        x1_ref[...] = x1
        h_ref[...] = _rms_norm(x1, g_ref[...]).astype(BF16)
        acc_ref[...] = jnp.zeros_like(acc_ref)

    h = h_ref[...]
    gate = (_silu(_dot(h, w1_ref[...])) * _dot(h, w3_ref[...])).astype(BF16)
    acc_ref[...] += _dot(gate, w2_ref[...])

    @pl.when(j == pl.num_programs(1) - 1)
    def _():
        out_ref[...] = x1_ref[...] + acc_ref[...]


def _ff_tile(ff, target):
    best = LANES
    for k in range(1, ff // LANES + 1):
        if ff % (k * LANES) == 0 and k * LANES <= target:
            best = k * LANES
    return best


def _mix_out_ffn(x, oa_branches, lse_branches, ob, wo, g, w1, w3, w2, *, tm):
    t, d = x.shape
    ff = w1.shape[1]
    tf = _ff_tile(ff, 1408)
    row = lambda i, j: (i, 0)
    fixed = lambda i, j: (0, 0)
    n_branches = len(oa_branches)
    assert len(lse_branches) == (n_branches if n_branches > 1 else 0)
    return pl.pallas_call(
        functools.partial(_mix_out_ffn_kernel, n_branches=n_branches),
        grid=(t // tm, ff // tf),
        in_specs=[pl.BlockSpec((tm, d), row)] + [pl.BlockSpec((tm, AW), row)] * n_branches
                 + [pl.BlockSpec((tm, LANES), row)] * len(lse_branches)
                 + [pl.BlockSpec((tm, AW), row),
                  pl.BlockSpec((2 * AW, d), fixed), pl.BlockSpec((1, d), fixed),
                  pl.BlockSpec((d, tf), lambda i, j: (0, j)), pl.BlockSpec((d, tf), lambda i, j: (0, j)),
                  pl.BlockSpec((tf, d), lambda i, j: (j, 0))],
        out_specs=pl.BlockSpec((tm, d), row),
        out_shape=jax.ShapeDtypeStruct((t, d), F32),
        scratch_shapes=[pltpu.VMEM((tm, d), F32), pltpu.VMEM((tm, d), BF16), pltpu.VMEM((tm, d), F32)],
        compiler_params=_cparams("arbitrary", "arbitrary"),
        name="mix_out_ffn",
    )(x, *oa_branches, *lse_branches, ob, wo, g, w1, w3, w2)


def _gelu_exact(z):
    return 0.5 * z * (1.0 + lax.erf(z * (2.0 ** -0.5)))


def _gmlp_kernel(x_ref, g_ref, win_ref, lng_ref, lnb_ref, mix_ref, bias_ref, wout_ref, gffn_ref, wr_ref,
                 *out_refs, write_v):
    if write_v:
        xo_ref, h2_ref, logit_ref, v_ref = out_refs[:4]
    else:
        xo_ref, h2_ref, logit_ref = out_refs[:3]
    gate_ref = out_refs[-1]
    x = x_ref[...]
    tm, cd = x.shape
    h = _rms_norm(x, g_ref[...]).astype(BF16)
    z = _gelu_exact(_dot(h, win_ref[...]))
    u = z[:, :cd]
    v = z[:, cd:]
    mu = jnp.mean(v, axis=-1, keepdims=True)
    vc = v - mu
    v = vc * lax.rsqrt(jnp.mean(vc * vc, axis=-1, keepdims=True) + NORM_EPS) * lng_ref[...] + lnb_ref[...]
    if write_v:
        v_ref[...] = v
    vb = v.astype(BF16)
    bias = bias_ref[...]
    gw = cd // C_GROUPS
    for c in range(tm // CHUNK):
        rs = slice(c * CHUNK, (c + 1) * CHUNK)
        for g in range(C_GROUPS):
            cs = slice(g * gw, (g + 1) * gw)
            f = _dot(mix_ref[g], vb[rs, cs]) + bias[:, cs]
            gate_ref[rs, cs] = (u[rs, cs] * f).astype(BF16)
    xo = x + _dot(gate_ref[...], wout_ref[...])
    xo_ref[...] = xo
    h2 = _rms_norm(xo, gffn_ref[...])
    h2_ref[...] = h2
    h_hi = h2.astype(BF16)
    h_lo = (h2 - h_hi.astype(F32)).astype(BF16)
    wr = wr_ref[...]
    both = _dot(h_hi, wr)
    logit_ref[...] = both[:, :LANES] + both[:, LANES:] + _dot(h_lo, wr[:, :LANES])


def _gmlp_block(x, g, w_in, ln_g, ln_b, mix, bias_full, w_out, g_ffn, wr_hi_lo, *, tm, write_v):
    t, d = x.shape
    cd = w_out.shape[0]
    row = lambda i: (i, 0)
    fixed = lambda i: (0, 0)
    sds = jax.ShapeDtypeStruct
    out_specs = [pl.BlockSpec((tm, d), row), pl.BlockSpec((tm, d), row), pl.BlockSpec((tm, LANES), row)]
    out_shape = [sds((t, d), F32), sds((t, d), F32), sds((t, LANES), F32)]
    if write_v:
        out_specs.append(pl.BlockSpec((tm, cd), row))
        out_shape.append(sds((t, cd), F32))
    return pl.pallas_call(
        functools.partial(_gmlp_kernel, write_v=write_v),
        grid=(t // tm,),
        in_specs=[pl.BlockSpec((tm, d), row), pl.BlockSpec((1, d), fixed), pl.BlockSpec((d, 2 * cd), fixed),
                  pl.BlockSpec((1, cd), fixed), pl.BlockSpec((1, cd), fixed),
                  pl.BlockSpec((C_GROUPS, CHUNK, CHUNK), lambda i: (0, 0, 0)),
                  pl.BlockSpec((CHUNK, cd), fixed), pl.BlockSpec((cd, d), fixed), pl.BlockSpec((1, d), fixed),
                  pl.BlockSpec((d, 2 * LANES), fixed)],
        out_specs=out_specs, out_shape=out_shape,
        scratch_shapes=[pltpu.VMEM((tm, cd), BF16)],
        compiler_params=_cparams("arbitrary"),
        name="gmlp_block",
    )(x, g, w_in, ln_g, ln_b, mix, bias_full, w_out, g_ffn, wr_hi_lo)


MOE_BLOCK = 1024
MOE_SUB = 256
DISPATCH_ROWS = 512


def _dispatch_kernel(dest_ref, h_ref, xs_in_ref, xs_ref, sem):
    del xs_in_ref
    rows = h_ref.shape[0]

    def issue(g, carry):
        for s in range(SUBLANES):
            r = g * SUBLANES + s
            for k in range(TOP_K):
                d = dest_ref[0, 0, r * TOP_K + k]
                pltpu.make_async_copy(h_ref.at[pl.ds(r, 1), :], xs_ref.at[pl.ds(d, 1), :], sem).start(priority=k)
        return carry

    lax.fori_loop(0, rows // SUBLANES, issue, 0)
    for _ in range(TOP_K):
        pltpu.make_async_copy(h_ref, xs_ref.at[pl.ds(0, rows), :], sem).wait()


def _dispatch(h, dest, xs):
    t, d = h.shape
    tg = DISPATCH_ROWS
    return pl.pallas_call(
        _dispatch_kernel,
        grid=(t // tg,),
        in_specs=[pl.BlockSpec((1, 1, tg * TOP_K), lambda i: (i, 0, 0), memory_space=pltpu.SMEM),
                  pl.BlockSpec((tg, d), lambda i: (i, 0)),
                  pl.BlockSpec(memory_space=pl.ANY)],
        out_specs=pl.BlockSpec(memory_space=pl.ANY),
        out_shape=jax.ShapeDtypeStruct(xs.shape, xs.dtype),
        scratch_shapes=[pltpu.SemaphoreType.DMA],
        input_output_aliases={2: 0},
        compiler_params=_cparams("arbitrary"),
        name="moe_dispatch",
    )(dest.reshape(t // tg, 1, tg * TOP_K), h, xs)


def _expert_kernel(be_ref, na_ref, ns_ref, xs_ref, w1_ref, w3_ref, w2_ref, y_ref, h_ref, acc_ref):
    del be_ref, na_ref
    b = pl.program_id(0)
    j = pl.program_id(1)
    n_sub = ns_ref[b]

    @pl.when((n_sub > 0) & (j == 0))
    def _():
        h_ref[...] = xs_ref[...].astype(BF16)
        acc_ref[...] = jnp.zeros_like(acc_ref)

    for k in range(1, MOE_BLOCK // MOE_SUB + 1):
        @pl.when(n_sub == k)
        def _(rows=k * MOE_SUB):
            h = h_ref[0:rows, :]
            w1 = w1_ref[...].astype(BF16)
            w3 = w3_ref[...].astype(BF16)
            gate = (_silu(_dot(h, w1)) * _dot(h, w3)).astype(BF16)
            acc_ref[0:rows, :] += _dot(gate, w2_ref[...].astype(BF16))

    @pl.when(j == pl.num_programs(1) - 1)
    def _():
        y_ref[...] = jnp.where(n_sub > 0, acc_ref[...], 0.0)


def _experts(xs, block_expert, n_active, n_sub, w1, w3, w2):
    n_rows, d = xs.shape
    ff = w1.shape[2]
    tf = _ff_tile(ff, 512)
    n_blocks = n_rows // MOE_BLOCK
    nj = ff // tf
    def wcol(b, j, be, na, ns):
        live = b < na[0]
        return (be[b], 0, jnp.where(live, j, nj - 1))

    def wrow(b, j, be, na, ns):
        live = b < na[0]
        return (be[b], jnp.where(live, j, nj - 1), 0)

    def xrow(b, j, be, na, ns):
        return (jnp.minimum(b, na[0] - 1), 0)

    grid_spec = pltpu.PrefetchScalarGridSpec(
        num_scalar_prefetch=3,
        grid=(n_blocks, nj),
        in_specs=[pl.BlockSpec((MOE_BLOCK, d), xrow),
                  pl.BlockSpec((None, d, tf), wcol), pl.BlockSpec((None, d, tf), wcol),
                  pl.BlockSpec((None, tf, d), wrow)],
        out_specs=pl.BlockSpec((MOE_BLOCK, d), lambda b, j, be, na, ns: (b, 0)),
        scratch_shapes=[pltpu.VMEM((MOE_BLOCK, d), BF16), pltpu.VMEM((MOE_BLOCK, d), F32)],
    )
    return pl.pallas_call(
        _expert_kernel,
        grid_spec=grid_spec,
        out_shape=jax.ShapeDtypeStruct((n_rows, d), F32),
        compiler_params=_cparams("arbitrary", "arbitrary"),
        name="moe_experts",
    )(block_expert, n_active, n_sub, xs, w1, w3, w2)


def _combine_kernel(dest_ref, x_ref, gate_ref, y_ref, out_ref, buf, sem):
    rows = x_ref.shape[0]

    def issue(g, carry):
        for s in range(SUBLANES):
            r = g * SUBLANES + s
            for k in range(TOP_K):
                d = dest_ref[0, 0, r * TOP_K + k]
                pltpu.make_async_copy(y_ref.at[pl.ds(d, 1), :], buf.at[k, pl.ds(r, 1), :], sem).start(priority=k)
        return carry

    lax.fori_loop(0, rows // SUBLANES, issue, 0)
    for k in range(TOP_K):
        pltpu.make_async_copy(y_ref.at[pl.ds(0, rows), :], buf.at[k], sem).wait()
    gates = gate_ref[...]
    y = buf[0] * gates[:, 0:1]
    for k in range(1, TOP_K):
        y = y + buf[k] * gates[:, k:k + 1]
    out_ref[...] = x_ref[...] + y


def _combine(x, gates, dest, y):
    t, d = x.shape
    tg = DISPATCH_ROWS
    return pl.pallas_call(
        _combine_kernel,
        grid=(t // tg,),
        in_specs=[pl.BlockSpec((1, 1, tg * TOP_K), lambda i: (i, 0, 0), memory_space=pltpu.SMEM),
                  pl.BlockSpec((tg, d), lambda i: (i, 0)),
                  pl.BlockSpec((tg, TOP_K), lambda i: (i, 0)),
                  pl.BlockSpec(memory_space=pl.ANY)],
        out_specs=pl.BlockSpec((tg, d), lambda i: (i, 0)),
        out_shape=jax.ShapeDtypeStruct((t, d), F32),
        scratch_shapes=[pltpu.VMEM((TOP_K, tg, d), F32), pltpu.SemaphoreType.DMA],
        compiler_params=_cparams("arbitrary"),
        name="moe_combine",
    )(dest.reshape(t // tg, 1, tg * TOP_K), x, gates, y)


def _route(logits):
    t = logits.shape[0]
    top_val, top_idx = lax.top_k(logits, TOP_K)
    gates = jax.nn.softmax(top_val, axis=-1)
    expert = top_idx.reshape(-1)
    one_hot = (expert[:, None] == jnp.arange(N_EXPERTS, dtype=expert.dtype)[None, :]).astype(jnp.int32)
    running = jnp.cumsum(one_hot, axis=0)
    rank = jnp.sum(running * one_hot, axis=1) - 1
    counts = running[-1]
    padded = (counts + MOE_BLOCK - 1) // MOE_BLOCK * MOE_BLOCK
    pad_ends = jnp.cumsum(padded)
    pad_starts = pad_ends - padded
    dest = (jnp.sum(pad_starts[None, :] * one_hot, axis=1) + rank).astype(jnp.int32)
    n_blocks = -(-(t * TOP_K) // MOE_BLOCK) + N_EXPERTS
    block_start = jnp.arange(n_blocks, dtype=jnp.int32) * MOE_BLOCK
    block_expert = jnp.sum((block_start[:, None] >= pad_ends[None, :]).astype(jnp.int32), axis=1)
    block_expert = jnp.minimum(block_expert, N_EXPERTS - 1).astype(jnp.int32)
    used_end = (pad_starts + counts)[block_expert]
    used_rows = jnp.clip(used_end - block_start, 0, MOE_BLOCK)
    used_rows = jnp.where(block_start < pad_ends[-1], used_rows, 0)
    n_sub = ((used_rows + MOE_SUB - 1) // MOE_SUB).astype(jnp.int32)
    n_active = (pad_ends[-1:] // MOE_BLOCK).astype(jnp.int32)
    return gates, dest.reshape(t, TOP_K), block_expert, n_sub, n_active, n_blocks * MOE_BLOCK


def _moe(groups, w1, w3, w2):
    sizes = [x.shape[0] for x, _, _ in groups]
    gates, dest, block_expert, n_sub, n_active, n_rows = _route(jnp.concatenate([l for _, _, l in groups]))
    xs = jnp.zeros((n_rows, w1.shape[1]), F32)
    start = 0
    for (_, h2, _), size in zip(groups, sizes):
        xs = _dispatch(h2, dest[start:start + size].reshape(-1), xs)
        start += size
    y = _experts(xs, block_expert, n_active, n_sub, w1, w3, w2)
    outs, start = [], 0
    for (x, _, _), size in zip(groups, sizes):
        outs.append(_combine(x, gates[start:start + size], dest[start:start + size].reshape(-1), y))
        start += size
    return outs


def _rope_tables(pos):
    inv_freq = ROPE_THETA ** (-jnp.arange(HALF_DIM, dtype=F32) / HALF_DIM)
    ang = pos.astype(F32)[:, None] * inv_freq[None, :]
    cos, sin = jnp.cos(ang), jnp.sin(ang)
    reps = LANES // HEAD_DIM
    return jnp.tile(cos, (1, 2 * reps)), jnp.tile(jnp.concatenate([-sin, sin], axis=1), (1, reps))


def _qk_gain_row(qk_gain):
    ones_a = jnp.ones((AW,), F32)
    ones_b = jnp.ones((KVW,), F32)
    return jnp.concatenate([jnp.tile(qk_gain[0], N_HEADS), jnp.tile(qk_gain[1], N_HEADS), ones_a,
                            jnp.tile(qk_gain[2], N_HEADS), jnp.tile(qk_gain[3], B_KV_HEADS), ones_b])[None, :]


def _row_block(t):
    return 512 if t % 512 == 0 else t


def kernel(x_prompt, x_sample, cache_a_k, cache_a_v, cache_b_k, cache_b_v, norm_mix, norm_ffn, ab_w_in,
           ab_qk_gain, ab_sinks, ab_w_out, ffn_w1, ffn_w3, ffn_w2, c_w_in, c_ln_g, c_ln_b, c_w_s, c_b_s,
           c_w_out, moe_router, moe_w1, moe_w3, moe_w2):
    n_p, s_p, d = x_prompt.shape
    n_s, t_new, _ = x_sample.shape
    depth = norm_mix.shape[0]
    past_len = 8192
    xp = x_prompt.reshape(n_p * s_p, d)
    xs = x_sample.reshape(n_s * t_new, d)
    pos_p = jnp.tile(jnp.arange(s_p), n_p)
    pos_s = jnp.tile(past_len + jnp.arange(t_new), n_s)
    cos_p, sin_p = _rope_tables(pos_p)
    cos_s, sin_s = _rope_tables(pos_s)
    row = lambda a: a[None, :]
    akp, avp, bkp, bvp, aks, avs, bks, bvs, cvs = [], [], [], [], [], [], [], [], []
    la_p, lb_p = min(BAND * A_DILATIONS[-1], s_p), min(BAND, s_p)
    for layer in range(depth):
        i = layer // 2
        if layer % 2 == 0:
            w_in = ab_w_in[i].astype(BF16)
            gain_row = _qk_gain_row(ab_qk_gain[i])
            w_out = ab_w_out[i].astype(BF16)
            w1, w3, w2 = ffn_w1[i].astype(BF16), ffn_w3[i].astype(BF16), ffn_w2[i].astype(BF16)
            sinks = ab_sinks[i].astype(F32)
            sink_rows = jnp.broadcast_to(sinks[:, None], (N_HEADS, LANES))
            sink_lane_row = jnp.pad(sinks, (0, LANES - N_HEADS))[None, :]
            qa, ka, va, qb, kb, vb, ka16, va16, kvb16 = _qkv_proj(
                xp, row(norm_mix[layer]), w_in, gain_row, cos_p, sin_p,
                tm=_row_block(xp.shape[0]), q_dtype=BF16, bf16_kv=True)
            seq = lambda a: a.reshape(n_p, s_p, a.shape[-1])
            branches = [_attn_a_branch(seq(qa), seq(ka16), seq(va16), dilation=r) for r in A_DILATIONS]
            oas = [o.reshape(n_p * s_p, AW) for o, _ in branches]
            lses = [l.reshape(n_p * s_p, LANES) for _, l in branches]
            ob = _attn_b(seq(qb), seq(kvb16), sink_rows).reshape(n_p * s_p, AW)
            xp = _mix_out_ffn(xp, oas, lses, ob, w_out, row(norm_ffn[layer]), w1, w3, w2,
                              tm=_row_block(xp.shape[0]))
            tail = lambda a, rows, nh: seq(a)[:, s_p - rows:].reshape(n_p, rows, nh, HEAD_DIM)
            akp.append(tail(ka, la_p, N_HEADS))
            avp.append(tail(va, la_p, N_HEADS))
            bkp.append(tail(kb, lb_p, B_KV_HEADS))
            bvp.append(tail(vb, lb_p, B_KV_HEADS))
            qa, ka, va, qb, kb, vb = _qkv_proj(xs, row(norm_mix[layer]), w_in, gain_row, cos_s, sin_s,
                                               tm=_row_block(xs.shape[0]), q_dtype=F32, bf16_kv=False)
            flat = lambda c: c.reshape(c.shape[0], c.shape[1], c.shape[2] * c.shape[3])
            oa, ob = _sample_attn(qa, ka, va, qb, kb, vb, cache_a_k[i], cache_a_v[i],
                                  flat(cache_b_k[i]), flat(cache_b_v[i]), sink_lane_row, t_new=t_new)
            xs = _mix_out_ffn(xs, [oa], [], ob, w_out, row(norm_ffn[layer]), w1, w3, w2,
                              tm=_row_block(xs.shape[0]))
            heads_s = lambda a, nh: a.reshape(n_s, t_new, nh, HEAD_DIM)
            aks.append(heads_s(ka, N_HEADS))
            avs.append(heads_s(va, N_HEADS))
            bks.append(heads_s(kb, B_KV_HEADS))
            bvs.append(heads_s(vb, B_KV_HEADS))
        else:
            w_in = c_w_in[i].astype(BF16)
            w_out = c_w_out[i].astype(BF16)
            tril = jnp.tril(jnp.ones((CHUNK, CHUNK), F32))
            mix_p = (c_w_s[i] * tril).astype(BF16)
            gw = w_out.shape[0] // C_GROUPS
            bias_p = jnp.repeat(c_b_s[i].T, gw, axis=1)
            per_tile = CHUNK // t_new
            mix_s = jnp.einsum("ab,gij->gaibj", jnp.eye(per_tile, dtype=F32), (c_w_s[i] * tril)[:, :t_new, :t_new])
            mix_s = mix_s.reshape(C_GROUPS, CHUNK, CHUNK).astype(BF16)
            bias_s = jnp.tile(bias_p[:t_new], (per_tile, 1))
            wr = moe_router[i]
            wr_hi = wr.astype(BF16)
            wr_lo = (wr - wr_hi.astype(F32)).astype(BF16)
            pad_e = lambda a: jnp.pad(a, ((0, 0), (0, LANES - N_EXPERTS)))
            wr_hi_lo = jnp.concatenate([pad_e(wr_hi), pad_e(wr_lo)], axis=1)
            common = (row(c_ln_g[i]), row(c_ln_b[i]))
            xp1, hp, lp = _gmlp_block(xp, row(norm_mix[layer]), w_in, *common, mix_p, bias_p, w_out,
                                      row(norm_ffn[layer]), wr_hi_lo, tm=_row_block(xp.shape[0]), write_v=False)
            xs1, hs, ls, v_new = _gmlp_block(xs, row(norm_mix[layer]), w_in, *common, mix_s, bias_s, w_out,
                                             row(norm_ffn[layer]), wr_hi_lo, tm=_row_block(xs.shape[0]),
                                             write_v=True)
            cvs.append(v_new.reshape(n_s, t_new, v_new.shape[-1]))
            xp, xs = _moe([(xp1, hp, lp[:, :N_EXPERTS]), (xs1, hs, ls[:, :N_EXPERTS])],
                          moe_w1[i], moe_w3[i], moe_w2[i])
    return (xp.reshape(n_p, s_p, d), xs.reshape(n_s, t_new, d),
            jnp.stack(akp), jnp.stack(avp), jnp.stack(bkp), jnp.stack(bvp),
            jnp.stack(aks), jnp.stack(avs), jnp.stack(bks), jnp.stack(bvs),
            jnp.stack(cvs))
```

```python
import functools

import jax
import jax.numpy as jnp
from jax import lax
from jax.experimental import pallas as pl
from jax.experimental.pallas import tpu as pltpu

F32 = jnp.float32
BF16 = jnp.bfloat16

HEAD_DIM = 64
HALF_DIM = HEAD_DIM // 2
N_HEADS = 8
B_KV_HEADS = 2
A_DILATIONS = (1, 4, 16)
BAND = 128
ROPE_THETA = 10000.0
NORM_EPS = 1e-6
NEG_INF = -1e30
CHUNK = 128
C_GROUPS = 8
N_EXPERTS = 8
TOP_K = 2

LANES = 128
SUBLANES = 8
VMEM_LIMIT_BYTES = 56 * 1024 * 1024

AW = N_HEADS * HEAD_DIM
KVW = B_KV_HEADS * HEAD_DIM
HEADS_PER_VREG = LANES // HEAD_DIM
N_HEAD_BLOCKS = AW // LANES


def _cparams(*sem):
    return pltpu.CompilerParams(dimension_semantics=sem, vmem_limit_bytes=VMEM_LIMIT_BYTES)


def _rms_norm(x, g):
    return x * lax.rsqrt(jnp.mean(x * x, axis=-1, keepdims=True) + NORM_EPS) * g


def _dot(a, b):
    return jnp.dot(a, b, preferred_element_type=F32)


def _dot_nt(a, b):
    return lax.dot_general(a, b, (((1,), (1,)), ((), ())), preferred_element_type=F32)


_QA_BLOCKS = range(0, 4)
_KA_BLOCKS = range(4, 8)
_VA_BLOCKS = range(8, 12)
_QB_BLOCKS = range(12, 16)
_KB_BLOCK = 16
_VB_BLOCK = 17


def _proj_kernel(x_ref, g_ref, w_ref, gain_ref, cos_ref, sin_ref,
                 qa_ref, ka_ref, va_ref, qb_ref, kb_ref, vb_ref, *bf16_kv_refs):
    h = _rms_norm(x_ref[...], g_ref[...]).astype(BF16)
    p = _dot(h, w_ref[...])
    tm = p.shape[0]
    cos = cos_ref[...]
    sin = sin_ref[...]
    lane = lax.broadcasted_iota(jnp.int32, (tm, LANES), 1)
    first_half = (lane & HALF_DIM) == 0
    li = lax.broadcasted_iota(jnp.int32, (LANES, LANES), 0) // HEAD_DIM
    lj = lax.broadcasted_iota(jnp.int32, (LANES, LANES), 1) // HEAD_DIM
    head_mean = jnp.where(li == lj, 1.0 / HEAD_DIM, 0.0).astype(BF16)

    def norm_rope(blk):
        pc = p[:, blk * LANES:(blk + 1) * LANES]
        ms = _dot((pc * pc).astype(BF16), head_mean)
        y = pc * lax.rsqrt(ms + NORM_EPS) * gain_ref[:, blk * LANES:(blk + 1) * LANES]
        partner = jnp.where(first_half, pltpu.roll(y, LANES - HALF_DIM, 1), pltpu.roll(y, HALF_DIM, 1))
        return y * cos + partner * sin

    scale = HEAD_DIM ** -0.5
    for j, blk in enumerate(_QA_BLOCKS):
        qa_ref[:, j * LANES:(j + 1) * LANES] = (norm_rope(blk) * scale).astype(qa_ref.dtype)
    for j, blk in enumerate(_KA_BLOCKS):
        ka_ref[:, j * LANES:(j + 1) * LANES] = norm_rope(blk)
    for j, blk in enumerate(_VA_BLOCKS):
        va_ref[:, j * LANES:(j + 1) * LANES] = p[:, blk * LANES:(blk + 1) * LANES]
    for j, blk in enumerate(_QB_BLOCKS):
        qb_ref[:, j * LANES:(j + 1) * LANES] = (norm_rope(blk) * scale).astype(qb_ref.dtype)
    kb_ref[...] = norm_rope(_KB_BLOCK)
    vb_ref[...] = p[:, _VB_BLOCK * LANES:(_VB_BLOCK + 1) * LANES]
    if bf16_kv_refs:
        ka16_ref, va16_ref, kvb16_ref = bf16_kv_refs
        ka16_ref[...] = ka_ref[...].astype(BF16)
        va16_ref[...] = va_ref[...].astype(BF16)
        kb = kb_ref[...]
        vb = vb_ref[...]
        parts = (kb, pltpu.roll(kb, HEAD_DIM, 1), vb, pltpu.roll(vb, HEAD_DIM, 1))
        for j, part in enumerate(parts):
            kvb16_ref[:, j * KVW:(j + 1) * KVW] = part.astype(BF16)


def _qkv_proj(x, g, w, gain_row, cos, sin, *, tm, q_dtype, bf16_kv):
    t, d = x.shape
    n_out = w.shape[1]
    row = lambda i: (i, 0)
    fixed = lambda i: (0, 0)
    sds = jax.ShapeDtypeStruct
    widths = (AW, AW, AW, AW, KVW, KVW) + ((AW, AW, 4 * KVW) if bf16_kv else ())
    dtypes = (q_dtype, F32, F32, q_dtype, F32, F32) + ((BF16,) * 3 if bf16_kv else ())
    return pl.pallas_call(
        _proj_kernel,
        grid=(t // tm,),
        in_specs=[pl.BlockSpec((tm, d), row), pl.BlockSpec((1, d), fixed), pl.BlockSpec((d, n_out), fixed),
                  pl.BlockSpec((1, n_out), fixed), pl.BlockSpec((tm, LANES), row), pl.BlockSpec((tm, LANES), row)],
        out_specs=[pl.BlockSpec((tm, wd), row) for wd in widths],
        out_shape=[sds((t, wd), dt) for wd, dt in zip(widths, dtypes)],
        compiler_params=_cparams("arbitrary"),
        name="qkv_proj",
    )(x, g, w, gain_row, cos, sin)


def _fill_kv(kbuf, vbuf, kp_ref, kc_ref, vp_ref, vc_ref):
    kbuf[0:BAND, :] = kp_ref[...].astype(BF16)
    kbuf[BAND:, :] = kc_ref[...].astype(BF16)
    vbuf[0:BAND, :] = vp_ref[...].astype(BF16)
    vbuf[BAND:, :] = vc_ref[...].astype(BF16)


def _band_masks(rows, first_block):
    assert BAND == LANES
    row = lax.broadcasted_iota(jnp.int32, (rows, BAND), 0) & (BAND - 1)
    col = lax.broadcasted_iota(jnp.int32, (rows, BAND), 1)
    shifted = row + jnp.where(first_block, 2 * BAND, 0)
    return col > shifted, col == shifted, col <= row


def _banded_softmax_pv(qm, kp, kc, vp, vc, masks, floor_row=None):
    upper, diag, lower = masks
    s_prev = _dot_nt(qm, kp)
    s = jnp.where(upper, s_prev, jnp.where(lower, _dot_nt(qm, kc), NEG_INF))
    back = jnp.where(diag, s_prev, NEG_INF if floor_row is None else floor_row)
    m = jnp.max(jnp.maximum(s, back), axis=-1, keepdims=True)
    p = jnp.exp(s - m)
    p_back = jnp.where(diag, jnp.exp(s_prev - m), 0.0)
    l = jnp.sum(p + p_back, axis=-1, keepdims=True)
    p_prev = jnp.where(upper, p, p_back).astype(BF16)
    p_own = jnp.where(lower, p, 0.0).astype(BF16)
    return m, l, _dot(p_prev, vp) + _dot(p_own, vc)


def _attn_a_kernel(q_ref, kc_ref, kp_ref, vc_ref, vp_ref, o_ref, lse_ref, kbuf, vbuf, *, tq):
    blk = pl.program_id(2)
    _fill_kv(kbuf, vbuf, kp_ref, kc_ref, vp_ref, vc_ref)
    lane = lax.broadcasted_iota(jnp.int32, (BAND, LANES), 1)
    low_head = lane < HEAD_DIM

    def sub_block(sb, carry):
        r0 = pl.multiple_of(sb * BAND, BAND)
        masks = _band_masks(HEADS_PER_VREG * BAND, (blk + sb) == 0)
        lse_tile = jnp.zeros((BAND, LANES), F32)
        for hb in range(N_HEAD_BLOCKS):
            cs = slice(hb * LANES, (hb + 1) * LANES)
            q2 = q_ref[pl.ds(r0, BAND), cs]
            zero = jnp.zeros_like(q2)
            qm = jnp.concatenate([jnp.where(low_head, q2, zero), jnp.where(low_head, zero, q2)], axis=0)
            m, l, pv = _banded_softmax_pv(qm, kbuf[pl.ds(r0, BAND), cs], kbuf[pl.ds(r0 + BAND, BAND), cs],
                                          vbuf[pl.ds(r0, BAND), cs], vbuf[pl.ds(r0 + BAND, BAND), cs], masks)
            o = pv * (1.0 / l)
            o_ref[pl.ds(r0, BAND), cs] = jnp.where(low_head, o[:BAND], o[BAND:]).astype(o_ref.dtype)
            lse = m + jnp.log(l)
            for hh in range(HEADS_PER_VREG):
                lse_tile = jnp.where(lane == hb * HEADS_PER_VREG + hh, lse[hh * BAND:(hh + 1) * BAND], lse_tile)
        lse_ref[pl.ds(r0, BAND), :] = lse_tile
        return carry

    lax.fori_loop(0, tq // BAND, sub_block, 0)


def _attn_a_branch(q, k, v, *, dilation):
    n, s, w = q.shape
    r = dilation
    sub_len = s // r
    tq = min(512, sub_len)
    view = lambda a: a.reshape(n, sub_len, r * a.shape[-1])
    cur = lambda b, c, i: (b, i, c)
    prv = lambda b, c, i: (b, jnp.maximum(i * (tq // BAND) - 1, 0), c)
    blk_cur = pl.BlockSpec((None, tq, w), cur)
    blk_prv = pl.BlockSpec((None, BAND, w), prv)
    sds = jax.ShapeDtypeStruct
    o, lse = pl.pallas_call(
        functools.partial(_attn_a_kernel, tq=tq),
        grid=(n, r, sub_len // tq),
        in_specs=[blk_cur, blk_cur, blk_prv, blk_cur, blk_prv],
        out_specs=[blk_cur, pl.BlockSpec((None, tq, LANES), cur)],
        out_shape=[sds((n, sub_len, r * w), BF16), sds((n, sub_len, r * LANES), F32)],
        scratch_shapes=[pltpu.VMEM((tq + BAND, w), BF16), pltpu.VMEM((tq + BAND, w), BF16)],
        compiler_params=_cparams("arbitrary", "arbitrary", "arbitrary"),
        name=f"attn_a_dil{r}",
    )(view(q), view(k), view(k), view(v), view(v))
    return o.reshape(n, s, w), lse.reshape(n, s, LANES)


_KVB_K, _KVB_K_SWAPPED, _KVB_V, _KVB_V_SWAPPED = range(4)
_KVB_WIDTH = 4 * KVW


def _attn_b_kernel(q_ref, kvc_ref, kvp_ref, sink_ref, o_ref, kvbuf, *, tq):
    blk = pl.program_id(1)
    kvbuf[0:BAND, :] = kvp_ref[...]
    kvbuf[BAND:, :] = kvc_ref[...]
    lane = lax.broadcasted_iota(jnp.int32, (BAND, LANES), 1)
    low_head = lane < HEAD_DIM
    group = N_HEADS // B_KV_HEADS
    plain = [h for h in range(N_HEADS) if h % HEADS_PER_VREG == h // group]
    swapped = [h for h in range(N_HEADS) if h % HEADS_PER_VREG != h // group]

    def sub_block(sb, carry):
        r0 = pl.multiple_of(sb * BAND, BAND)
        masks = _band_masks(len(plain) * BAND, (blk + sb) == 0)
        blocks = [q_ref[pl.ds(r0, BAND), hb * LANES:(hb + 1) * LANES] for hb in range(N_HEAD_BLOCKS)]
        outs = [None] * N_HEADS
        for heads, k_blk, v_blk in ((plain, _KVB_K, _KVB_V), (swapped, _KVB_K_SWAPPED, _KVB_V_SWAPPED)):
            ks = slice(k_blk * KVW, (k_blk + 1) * KVW)
            vs = slice(v_blk * KVW, (v_blk + 1) * KVW)
            q_parts, sink_parts = [], []
            for h in heads:
                q2 = blocks[h // HEADS_PER_VREG]
                own_half = low_head if h % HEADS_PER_VREG == 0 else jnp.logical_not(low_head)
                q_parts.append(jnp.where(own_half, q2, jnp.zeros_like(q2)))
                sink_parts.append(jnp.broadcast_to(sink_ref[h:h + 1, :], (BAND, LANES)))
            sink = jnp.concatenate(sink_parts, axis=0)
            m, l, pv = _banded_softmax_pv(jnp.concatenate(q_parts, axis=0),
                                          kvbuf[pl.ds(r0, BAND), ks], kvbuf[pl.ds(r0 + BAND, BAND), ks],
                                          kvbuf[pl.ds(r0, BAND), vs], kvbuf[pl.ds(r0 + BAND, BAND), vs],
                                          masks, floor_row=sink)
            o = pv * (1.0 / (l + jnp.exp(sink - m)))
            for idx, h in enumerate(heads):
                outs[h] = o[idx * BAND:(idx + 1) * BAND]
        for hb in range(N_HEAD_BLOCKS):
            o_ref[pl.ds(r0, BAND), hb * LANES:(hb + 1) * LANES] = jnp.where(
                low_head, outs[hb * HEADS_PER_VREG], outs[hb * HEADS_PER_VREG + 1]).astype(o_ref.dtype)
        return carry

    lax.fori_loop(0, tq // BAND, sub_block, 0)


def _attn_b(q, kv, sinks_rows):
    n, s, w = q.shape
    tq = min(512, s)
    cur = lambda b, i: (b, i, 0)
    prv = lambda b, i: (b, jnp.maximum(i * (tq // BAND) - 1, 0), 0)
    return pl.pallas_call(
        functools.partial(_attn_b_kernel, tq=tq),
        grid=(n, s // tq),
        in_specs=[pl.BlockSpec((None, tq, w), cur), pl.BlockSpec((None, tq, _KVB_WIDTH), cur),
                  pl.BlockSpec((None, BAND, _KVB_WIDTH), prv),
                  pl.BlockSpec((N_HEADS, LANES), lambda b, i: (0, 0))],
        out_specs=pl.BlockSpec((None, tq, w), cur),
        out_shape=jax.ShapeDtypeStruct((n, s, w), BF16),
        scratch_shapes=[pltpu.VMEM((tq + BAND, _KVB_WIDTH), BF16)],
        compiler_params=_cparams("arbitrary", "arbitrary"),
        name="attn_b",
    )(q, kv, kv, sinks_rows)


NEW_ROWS = 8


def _branch_multiplicity(t_query, key_pos, la):
    back = la + t_query - key_pos
    mult = jnp.zeros(back.shape, F32)
    for r in A_DILATIONS:
        mult = mult + ((back >= 0) & (back <= BAND * r) & (back % r == 0)).astype(F32)
    return mult


def _sample_attn_kernel(qa_ref, kna_ref, vna_ref, qb_ref, knb_ref, vnb_ref,
                        kt_ref, vt_ref, cbk_ref, cbv_ref, sink_ref,
                        oa_ref, ob_ref, *, nb, t_new):
    la = kt_ref.shape[-1]
    t_c = lax.broadcasted_iota(jnp.int32, (NEW_ROWS, la), 0)
    t_c = jnp.where(t_c < t_new, t_c, 0)
    mult_c = _branch_multiplicity(t_c, lax.broadcasted_iota(jnp.int32, (NEW_ROWS, la), 1), la)
    t_n = lax.broadcasted_iota(jnp.int32, (NEW_ROWS, NEW_ROWS), 0)
    t_n = jnp.where(t_n < t_new, t_n, 0)
    key_n = lax.broadcasted_iota(jnp.int32, (NEW_ROWS, NEW_ROWS), 1)
    mult_n = jnp.where(key_n < t_new, _branch_multiplicity(t_n, la + key_n, la), 0.0)

    def head_attn_a(q_h, kt, vt, kn_h, vn_h):
        s_c = jnp.where(mult_c > 0.0, _dot(q_h, kt), NEG_INF)
        s_n = jnp.where(mult_n > 0.0, _dot_nt(q_h, kn_h), NEG_INF)
        m = jnp.maximum(jnp.max(s_c, axis=-1, keepdims=True), jnp.max(s_n, axis=-1, keepdims=True))
        p_c = jnp.exp(s_c - m) * mult_c
        p_n = jnp.exp(s_n - m) * mult_n
        l = jnp.sum(p_c, axis=-1, keepdims=True) + jnp.sum(p_n, axis=-1, keepdims=True)
        return (_dot_nt(p_c.astype(BF16), vt) + _dot(p_n.astype(BF16), vn_h)) / l

    lane_w = lax.broadcasted_iota(jnp.int32, (AW, LANES), 0) // HEAD_DIM
    head_w = lax.broadcasted_iota(jnp.int32, (AW, LANES), 1)
    head_sum = (lane_w == head_w).astype(BF16)
    head_e = lax.broadcasted_iota(jnp.int32, (LANES, AW), 0)
    lane_e = lax.broadcasted_iota(jnp.int32, (LANES, AW), 1) // HEAD_DIM
    head_expand = (head_e == lane_e).astype(BF16)
    row_c = lax.broadcasted_iota(jnp.int32, (BAND, 1), 0)
    row_n = lax.broadcasted_iota(jnp.int32, (NEW_ROWS, 1), 0)
    lane = lax.broadcasted_iota(jnp.int32, (BAND, LANES), 1)
    low_head = lane < HEAD_DIM
    low_head_n = low_head[:NEW_ROWS]
    sink_row = sink_ref[...]

    def scores(k, q_row):
        return _dot((k * q_row).astype(BF16), head_sum)

    def attend(q_row, blocks, sink=None):
        ss = []
        m = None
        for k, _, valid, _ in blocks:
            s = scores(k, q_row)
            if valid is not None:
                s = jnp.where(valid, s, NEG_INF)
            ss.append(s)
            bm = jnp.max(s, axis=0, keepdims=True)
            m = bm if m is None else jnp.maximum(m, bm)
        if sink is not None:
            m = jnp.maximum(m, sink)
        ps = []
        l = jnp.exp(sink - m) if sink is not None else 0.0
        for s, (_, _, valid, mult) in zip(ss, blocks):
            p = jnp.exp(s - m)
            if mult is not None:
                p = p * mult
            ps.append(p)
            l = l + jnp.sum(p, axis=0, keepdims=True)
        inv_l = 1.0 / l
        acc = None
        for p, (_, v, _, _) in zip(ps, blocks):
            pe = _dot((p * inv_l).astype(BF16), head_expand)
            part = jnp.sum(pe * v, axis=0, keepdims=True)
            acc = part if acc is None else acc + part
        return acc

    def expand_kv(x, low):
        sw = pltpu.roll(x, HEAD_DIM, 1)
        kv0 = jnp.where(low, x, sw)
        kv1 = jnp.where(low, sw, x)
        return jnp.concatenate([kv0, kv0, kv1, kv1], axis=1)

    def one_sequence(n, carry):
        new0 = pl.multiple_of(n * NEW_ROWS, NEW_ROWS)
        qa = qa_ref[pl.ds(new0, NEW_ROWS), :].astype(BF16)
        kna = kna_ref[pl.ds(new0, NEW_ROWS), :].astype(BF16)
        vna = vna_ref[pl.ds(new0, NEW_ROWS), :].astype(BF16)
        heads_out = []
        for h in range(N_HEADS):
            hs = slice(h * HEAD_DIM, (h + 1) * HEAD_DIM)
            heads_out.append(head_attn_a(qa[:, hs], kt_ref[n, h].astype(BF16), vt_ref[n, h].astype(BF16),
                                         kna[:, hs], vna[:, hs]))
        oa_ref[pl.ds(new0, NEW_ROWS), :] = jnp.concatenate(heads_out, axis=1).astype(oa_ref.dtype)
        kb_c = expand_kv(cbk_ref[n], low_head)
        vb_c = expand_kv(cbv_ref[n], low_head)
        kb_n = expand_kv(knb_ref[pl.ds(new0, NEW_ROWS), :], low_head_n)
        vb_n = expand_kv(vnb_ref[pl.ds(new0, NEW_ROWS), :], low_head_n)
        qb = qb_ref[pl.ds(new0, NEW_ROWS), :]
        out_b = jnp.zeros((NEW_ROWS, AW), F32)
        for t in range(t_new):
            acc_b = attend(qb[t:t + 1, :], [
                (kb_c, vb_c, row_c >= t, None),
                (kb_n, vb_n, row_n <= t, None),
            ], sink=sink_row)
            out_b = jnp.where(row_n == t, acc_b, out_b)
        ob_ref[pl.ds(new0, NEW_ROWS), :] = out_b.astype(ob_ref.dtype)
        return carry

    lax.fori_loop(0, nb, one_sequence, 0)


def _pad_new_rows(a, n_seq, t_new):
    w = a.shape[-1]
    a = a.reshape(n_seq, t_new, w)
    a = jnp.pad(a, ((0, 0), (0, NEW_ROWS - t_new), (0, 0)))
    return a.reshape(n_seq * NEW_ROWS, w)


def _sample_attn(qa, ka, va, qb, kb, vb, cache_a_k, cache_a_v, cache_b_k, cache_b_v, sink_row, *, t_new):
    n_seq, la, nh, hd = cache_a_k.shape
    w = nh * hd
    assert t_new <= NEW_ROWS
    nb = 2 if n_seq % 2 == 0 else 1
    padded = [_pad_new_rows(a, n_seq, t_new) for a in (qa, ka, va, qb, kb, vb)]
    dim_major = lambda c: jnp.transpose(c, (0, 2, 3, 1))
    new_a = pl.BlockSpec((nb * NEW_ROWS, w), lambda i: (i, 0))
    new_b = pl.BlockSpec((nb * NEW_ROWS, KVW), lambda i: (i, 0))
    spec_ca = pl.BlockSpec((nb, nh, hd, la), lambda i: (i, 0, 0, 0))
    spec_cb = pl.BlockSpec((nb, BAND, KVW), lambda i: (i, 0, 0))
    sds = jax.ShapeDtypeStruct
    oa, ob = pl.pallas_call(
        functools.partial(_sample_attn_kernel, nb=nb, t_new=t_new),
        grid=(n_seq // nb,),
        in_specs=[new_a, new_a, new_a, new_a, new_b, new_b, spec_ca, spec_ca, spec_cb, spec_cb,
                  pl.BlockSpec((1, LANES), lambda i: (0, 0))],
        out_specs=[new_a, new_a],
        out_shape=[sds((n_seq * NEW_ROWS, w), BF16), sds((n_seq * NEW_ROWS, w), BF16)],
        compiler_params=_cparams("arbitrary"),
        name="sample_attn",
    )(*padded, dim_major(cache_a_k), dim_major(cache_a_v), cache_b_k, cache_b_v, sink_row)
    unpad = lambda o: o.reshape(n_seq, NEW_ROWS, w)[:, :t_new].reshape(n_seq * t_new, w)
    return unpad(oa), unpad(ob)


def _silu(a):
    return a * (1.0 / (1.0 + jnp.exp(-a)))


def _merge_branches(o_refs, lse_refs):
    lses = [r[...] for r in lse_refs]
    top = functools.reduce(jnp.maximum, lses)
    ws = [jnp.exp(l - top) for l in lses]
    inv_den = 1.0 / functools.reduce(lambda a, b: a + b, ws)
    head = lax.broadcasted_iota(jnp.int32, (LANES, AW), 0)
    lane_head = lax.broadcasted_iota(jnp.int32, (LANES, AW), 1) // HEAD_DIM
    expand = (head == lane_head).astype(BF16)
    oa = None
    for w, o_ref in zip(ws, o_refs):
        w = w * inv_den
        w_hi = w.astype(BF16)
        w_lo = (w - w_hi.astype(F32)).astype(BF16)
        part = (_dot(w_hi, expand) + _dot(w_lo, expand)) * o_ref[...].astype(F32)
        oa = part if oa is None else oa + part
    return oa.astype(BF16)


def _mix_out_ffn_kernel(x_ref, *refs, n_branches):
    o_refs, refs = refs[:n_branches], refs[n_branches:]
    if n_branches > 1:
        lse_refs, refs = refs[:n_branches], refs[n_branches:]
    ob_ref, wo_ref, g_ref, w1_ref, w3_ref, w2_ref, out_ref, x1_ref, h_ref, acc_ref = refs
    j = pl.program_id(1)

    @pl.when(j == 0)
    def _():
        oa = _merge_branches(o_refs, lse_refs) if n_branches > 1 else o_refs[0][...]
        x1 = x_ref[...] + _dot(oa, wo_ref[0:AW, :]) + _dot(ob_ref[...], wo_ref[AW:, :])
        x1_ref[...] = x1
        h_ref[...] = _rms_norm(x1, g_ref[...]).astype(BF16)
        acc_ref[...] = jnp.zeros_like(acc_ref)

    h = h_ref[...]
    gate = (_silu(_dot(h, w1_ref[...])) * _dot(h, w3_ref[...])).astype(BF16)
    acc_ref[...] += _dot(gate, w2_ref[...])

    @pl.when(j == pl.num_programs(1) - 1)
    def _():
        out_ref[...] = x1_ref[...] + acc_ref[...]


def _ff_tile(ff, target):
    best = LANES
    for k in range(1, ff // LANES + 1):
        if ff % (k * LANES) == 0 and k * LANES <= target:
            best = k * LANES
    return best


def _mix_out_ffn(x, oa_branches, lse_branches, ob, wo, g, w1, w3, w2, *, tm):
    t, d = x.shape
    ff = w1.shape[1]
    tf = _ff_tile(ff, 1408)
    row = lambda i, j: (i, 0)
    fixed = lambda i, j: (0, 0)
    n_branches = len(oa_branches)
    assert len(lse_branches) == (n_branches if n_branches > 1 else 0)
    return pl.pallas_call(
        functools.partial(_mix_out_ffn_kernel, n_branches=n_branches),
        grid=(t // tm, ff // tf),
        in_specs=[pl.BlockSpec((tm, d), row)] + [pl.BlockSpec((tm, AW), row)] * n_branches
                 + [pl.BlockSpec((tm, LANES), row)] * len(lse_branches)
                 + [pl.BlockSpec((tm, AW), row),
                  pl.BlockSpec((2 * AW, d), fixed), pl.BlockSpec((1, d), fixed),
                  pl.BlockSpec((d, tf), lambda i, j: (0, j)), pl.BlockSpec((d, tf), lambda i, j: (0, j)),
                  pl.BlockSpec((tf, d), lambda i, j: (j, 0))],
        out_specs=pl.BlockSpec((tm, d), row),
        out_shape=jax.ShapeDtypeStruct((t, d), F32),
        scratch_shapes=[pltpu.VMEM((tm, d), F32), pltpu.VMEM((tm, d), BF16), pltpu.VMEM((tm, d), F32)],
        compiler_params=_cparams("arbitrary", "arbitrary"),
        name="mix_out_ffn",
    )(x, *oa_branches, *lse_branches, ob, wo, g, w1, w3, w2)


def _gelu_exact(z):
    return 0.5 * z * (1.0 + lax.erf(z * (2.0 ** -0.5)))


def _gmlp_kernel(x_ref, g_ref, win_ref, lng_ref, lnb_ref, mix_ref, bias_ref, wout_ref, gffn_ref, wr_ref,
                 *out_refs, write_v):
    if write_v:
        xo_ref, h2_ref, logit_ref, v_ref = out_refs[:4]
    else:
        xo_ref, h2_ref, logit_ref = out_refs[:3]
    gate_ref = out_refs[-1]
    x = x_ref[...]
    tm, cd = x.shape
    h = _rms_norm(x, g_ref[...]).astype(BF16)
    z = _gelu_exact(_dot(h, win_ref[...]))
    u = z[:, :cd]
    v = z[:, cd:]
    mu = jnp.mean(v, axis=-1, keepdims=True)
    vc = v - mu
    v = vc * lax.rsqrt(jnp.mean(vc * vc, axis=-1, keepdims=True) + NORM_EPS) * lng_ref[...] + lnb_ref[...]
    if write_v:
        v_ref[...] = v
    vb = v.astype(BF16)
    bias = bias_ref[...]
    gw = cd // C_GROUPS
    for c in range(tm // CHUNK):
        rs = slice(c * CHUNK, (c + 1) * CHUNK)
        for g in range(C_GROUPS):
            cs = slice(g * gw, (g + 1) * gw)
            f = _dot(mix_ref[g], vb[rs, cs]) + bias[:, cs]
            gate_ref[rs, cs] = (u[rs, cs] * f).astype(BF16)
    xo = x + _dot(gate_ref[...], wout_ref[...])
    xo_ref[...] = xo
    h2 = _rms_norm(xo, gffn_ref[...])
    h2_ref[...] = h2
    h_hi = h2.astype(BF16)
    h_lo = (h2 - h_hi.astype(F32)).astype(BF16)
    wr = wr_ref[...]
    both = _dot(h_hi, wr)
    logit_ref[...] = both[:, :LANES] + both[:, LANES:] + _dot(h_lo, wr[:, :LANES])


def _gmlp_block(x, g, w_in, ln_g, ln_b, mix, bias_full, w_out, g_ffn, wr_hi_lo, *, tm, write_v):
    t, d = x.shape
    cd = w_out.shape[0]
    row = lambda i: (i, 0)
    fixed = lambda i: (0, 0)
    sds = jax.ShapeDtypeStruct
    out_specs = [pl.BlockSpec((tm, d), row), pl.BlockSpec((tm, d), row), pl.BlockSpec((tm, LANES), row)]
    out_shape = [sds((t, d), F32), sds((t, d), F32), sds((t, LANES), F32)]
    if write_v:
        out_specs.append(pl.BlockSpec((tm, cd), row))
        out_shape.append(sds((t, cd), F32))
    return pl.pallas_call(
        functools.partial(_gmlp_kernel, write_v=write_v),
        grid=(t // tm,),
        in_specs=[pl.BlockSpec((tm, d), row), pl.BlockSpec((1, d), fixed), pl.BlockSpec((d, 2 * cd), fixed),
                  pl.BlockSpec((1, cd), fixed), pl.BlockSpec((1, cd), fixed),
                  pl.BlockSpec((C_GROUPS, CHUNK, CHUNK), lambda i: (0, 0, 0)),
                  pl.BlockSpec((CHUNK, cd), fixed), pl.BlockSpec((cd, d), fixed), pl.BlockSpec((1, d), fixed),
                  pl.BlockSpec((d, 2 * LANES), fixed)],
        out_specs=out_specs, out_shape=out_shape,
        scratch_shapes=[pltpu.VMEM((tm, cd), BF16)],
        compiler_params=_cparams("arbitrary"),
        name="gmlp_block",
    )(x, g, w_in, ln_g, ln_b, mix, bias_full, w_out, g_ffn, wr_hi_lo)


MOE_BLOCK = 1024
MOE_SUB = 256
DISPATCH_ROWS = 512


def _dispatch_kernel(dest_ref, h_ref, xs_in_ref, xs_ref, sem):
    del xs_in_ref
    rows = h_ref.shape[0]

    def issue(g, carry):
        for s in range(SUBLANES):
            r = g * SUBLANES + s
            for k in range(TOP_K):
                d = dest_ref[0, 0, r * TOP_K + k]
                pltpu.make_async_copy(h_ref.at[pl.ds(r, 1), :], xs_ref.at[pl.ds(d, 1), :], sem).start(priority=k)
        return carry

    lax.fori_loop(0, rows // SUBLANES, issue, 0)
    for _ in range(TOP_K):
        pltpu.make_async_copy(h_ref, xs_ref.at[pl.ds(0, rows), :], sem).wait()


def _dispatch(h, dest, xs):
    t, d = h.shape
    tg = DISPATCH_ROWS
    return pl.pallas_call(
        _dispatch_kernel,
        grid=(t // tg,),
        in_specs=[pl.BlockSpec((1, 1, tg * TOP_K), lambda i: (i, 0, 0), memory_space=pltpu.SMEM),
                  pl.BlockSpec((tg, d), lambda i: (i, 0)),
                  pl.BlockSpec(memory_space=pl.ANY)],
        out_specs=pl.BlockSpec(memory_space=pl.ANY),
        out_shape=jax.ShapeDtypeStruct(xs.shape, xs.dtype),
        scratch_shapes=[pltpu.SemaphoreType.DMA],
        input_output_aliases={2: 0},
        compiler_params=_cparams("arbitrary"),
        name="moe_dispatch",
    )(dest.reshape(t // tg, 1, tg * TOP_K), h, xs)


def _expert_kernel(be_ref, na_ref, ns_ref, xs_ref, w1_ref, w3_ref, w2_ref, y_ref, h_ref, acc_ref):
    del be_ref, na_ref
    b = pl.program_id(0)
    j = pl.program_id(1)
    n_sub = ns_ref[b]

    @pl.when((n_sub > 0) & (j == 0))
    def _():
        h_ref[...] = xs_ref[...].astype(BF16)
        acc_ref[...] = jnp.zeros_like(acc_ref)

    for k in range(1, MOE_BLOCK // MOE_SUB + 1):
        @pl.when(n_sub == k)
        def _(rows=k * MOE_SUB):
            h = h_ref[0:rows, :]
            w1 = w1_ref[...].astype(BF16)
            w3 = w3_ref[...].astype(BF16)
            gate = (_silu(_dot(h, w1)) * _dot(h, w3)).astype(BF16)
            acc_ref[0:rows, :] += _dot(gate, w2_ref[...].astype(BF16))

    @pl.when(j == pl.num_programs(1) - 1)
    def _():
        y_ref[...] = jnp.where(n_sub > 0, acc_ref[...], 0.0)


def _experts(xs, block_expert, n_active, n_sub, w1, w3, w2):
    n_rows, d = xs.shape
    ff = w1.shape[2]
    tf = _ff_tile(ff, 896)
    n_blocks = n_rows // MOE_BLOCK
    nj = ff // tf
    def wcol(b, j, be, na, ns):
        live = b < na[0]
        return (be[b], 0, jnp.where(live, j, nj - 1))

    def wrow(b, j, be, na, ns):
        live = b < na[0]
        return (be[b], jnp.where(live, j, nj - 1), 0)

    def xrow(b, j, be, na, ns):
        return (jnp.minimum(b, na[0] - 1), 0)

    grid_spec = pltpu.PrefetchScalarGridSpec(
        num_scalar_prefetch=3,
        grid=(n_blocks, nj),
        in_specs=[pl.BlockSpec((MOE_BLOCK, d), xrow),
                  pl.BlockSpec((None, d, tf), wcol), pl.BlockSpec((None, d, tf), wcol),
                  pl.BlockSpec((None, tf, d), wrow)],
        out_specs=pl.BlockSpec((MOE_BLOCK, d), lambda b, j, be, na, ns: (b, 0)),
        scratch_shapes=[pltpu.VMEM((MOE_BLOCK, d), BF16), pltpu.VMEM((MOE_BLOCK, d), F32)],
    )
    return pl.pallas_call(
        _expert_kernel,
        grid_spec=grid_spec,
        out_shape=jax.ShapeDtypeStruct((n_rows, d), F32),
        compiler_params=_cparams("arbitrary", "arbitrary"),
        name="moe_experts",
    )(block_expert, n_active, n_sub, xs, w1, w3, w2)


def _combine_kernel(dest_ref, x_ref, gate_ref, y_ref, out_ref, buf, sem):
    rows = x_ref.shape[0]

    def issue(g, carry):
        for s in range(SUBLANES):
            r = g * SUBLANES + s
            for k in range(TOP_K):
                d = dest_ref[0, 0, r * TOP_K + k]
                pltpu.make_async_copy(y_ref.at[pl.ds(d, 1), :], buf.at[k, pl.ds(r, 1), :], sem).start(priority=k)
        return carry

    lax.fori_loop(0, rows // SUBLANES, issue, 0)
    for k in range(TOP_K):
        pltpu.make_async_copy(y_ref.at[pl.ds(0, rows), :], buf.at[k], sem).wait()
    gates = gate_ref[...]
    y = buf[0] * gates[:, 0:1]
    for k in range(1, TOP_K):
        y = y + buf[k] * gates[:, k:k + 1]
    out_ref[...] = x_ref[...] + y


def _combine(x, gates, dest, y):
    t, d = x.shape
    tg = DISPATCH_ROWS
    return pl.pallas_call(
        _combine_kernel,
        grid=(t // tg,),
        in_specs=[pl.BlockSpec((1, 1, tg * TOP_K), lambda i: (i, 0, 0), memory_space=pltpu.SMEM),
                  pl.BlockSpec((tg, d), lambda i: (i, 0)),
                  pl.BlockSpec((tg, TOP_K), lambda i: (i, 0)),
                  pl.BlockSpec(memory_space=pl.ANY)],
        out_specs=pl.BlockSpec((tg, d), lambda i: (i, 0)),
        out_shape=jax.ShapeDtypeStruct((t, d), F32),
        scratch_shapes=[pltpu.VMEM((TOP_K, tg, d), F32), pltpu.SemaphoreType.DMA],
        compiler_params=_cparams("arbitrary"),
        name="moe_combine",
    )(dest.reshape(t // tg, 1, tg * TOP_K), x, gates, y)


def _route(logits):
    t = logits.shape[0]
    top_val, top_idx = lax.top_k(logits, TOP_K)
    gates = jax.nn.softmax(top_val, axis=-1)
    expert = top_idx.reshape(-1)
    one_hot = (expert[:, None] == jnp.arange(N_EXPERTS, dtype=expert.dtype)[None, :]).astype(jnp.int32)
    running = jnp.cumsum(one_hot, axis=0)
    rank = jnp.sum(running * one_hot, axis=1) - 1
    counts = running[-1]
    padded = (counts + MOE_BLOCK - 1) // MOE_BLOCK * MOE_BLOCK
    pad_ends = jnp.cumsum(padded)
    pad_starts = pad_ends - padded
    dest = (jnp.sum(pad_starts[None, :] * one_hot, axis=1) + rank).astype(jnp.int32)
    n_blocks = -(-(t * TOP_K) // MOE_BLOCK) + N_EXPERTS
    block_start = jnp.arange(n_blocks, dtype=jnp.int32) * MOE_BLOCK
    block_expert = jnp.sum((block_start[:, None] >= pad_ends[None, :]).astype(jnp.int32), axis=1)
    block_expert = jnp.minimum(block_expert, N_EXPERTS - 1).astype(jnp.int32)
    used_end = (pad_starts + counts)[block_expert]
    used_rows = jnp.clip(used_end - block_start, 0, MOE_BLOCK)
    used_rows = jnp.where(block_start < pad_ends[-1], used_rows, 0)
    n_sub = ((used_rows + MOE_SUB - 1) // MOE_SUB).astype(jnp.int32)
    n_active = (pad_ends[-1:] // MOE_BLOCK).astype(jnp.int32)
    return gates, dest.reshape(t, TOP_K), block_expert, n_sub, n_active, n_blocks * MOE_BLOCK


def _moe(groups, w1, w3, w2):
    sizes = [x.shape[0] for x, _, _ in groups]
    gates, dest, block_expert, n_sub, n_active, n_rows = _route(jnp.concatenate([l for _, _, l in groups]))
    xs = jnp.zeros((n_rows, w1.shape[1]), F32)
    start = 0
    for (_, h2, _), size in zip(groups, sizes):
        xs = _dispatch(h2, dest[start:start + size].reshape(-1), xs)
        start += size
    y = _experts(xs, block_expert, n_active, n_sub, w1, w3, w2)
    outs, start = [], 0
    for (x, _, _), size in zip(groups, sizes):
        outs.append(_combine(x, gates[start:start + size], dest[start:start + size].reshape(-1), y))
        start += size
    return outs


def _rope_tables(pos):
    inv_freq = ROPE_THETA ** (-jnp.arange(HALF_DIM, dtype=F32) / HALF_DIM)
    ang = pos.astype(F32)[:, None] * inv_freq[None, :]
    cos, sin = jnp.cos(ang), jnp.sin(ang)
    reps = LANES // HEAD_DIM
    return jnp.tile(cos, (1, 2 * reps)), jnp.tile(jnp.concatenate([-sin, sin], axis=1), (1, reps))


def _qk_gain_row(qk_gain):
    ones_a = jnp.ones((AW,), F32)
    ones_b = jnp.ones((KVW,), F32)
    return jnp.concatenate([jnp.tile(qk_gain[0], N_HEADS), jnp.tile(qk_gain[1], N_HEADS), ones_a,
                            jnp.tile(qk_gain[2], N_HEADS), jnp.tile(qk_gain[3], B_KV_HEADS), ones_b])[None, :]


def _row_block(t):
    return 512 if t % 512 == 0 else t


def kernel(x_prompt, x_sample, cache_a_k, cache_a_v, cache_b_k, cache_b_v, norm_mix, norm_ffn, ab_w_in,
           ab_qk_gain, ab_sinks, ab_w_out, ffn_w1, ffn_w3, ffn_w2, c_w_in, c_ln_g, c_ln_b, c_w_s, c_b_s,
           c_w_out, moe_router, moe_w1, moe_w3, moe_w2):
    n_p, s_p, d = x_prompt.shape
    n_s, t_new, _ = x_sample.shape
    depth = norm_mix.shape[0]
    past_len = 8192
    xp = x_prompt.reshape(n_p * s_p, d)
    xs = x_sample.reshape(n_s * t_new, d)
    pos_p = jnp.tile(jnp.arange(s_p), n_p)
    pos_s = jnp.tile(past_len + jnp.arange(t_new), n_s)
    cos_p, sin_p = _rope_tables(pos_p)
    cos_s, sin_s = _rope_tables(pos_s)
    row = lambda a: a[None, :]
    akp, avp, bkp, bvp, aks, avs, bks, bvs, cvs = [], [], [], [], [], [], [], [], []
    la_p, lb_p = min(BAND * A_DILATIONS[-1], s_p), min(BAND, s_p)
    for layer in range(depth):
        i = layer // 2
        if layer % 2 == 0:
            w_in = ab_w_in[i].astype(BF16)
            gain_row = _qk_gain_row(ab_qk_gain[i])
            w_out = ab_w_out[i].astype(BF16)
            w1, w3, w2 = ffn_w1[i].astype(BF16), ffn_w3[i].astype(BF16), ffn_w2[i].astype(BF16)
            sinks = ab_sinks[i].astype(F32)
            sink_rows = jnp.broadcast_to(sinks[:, None], (N_HEADS, LANES))
            sink_lane_row = jnp.pad(sinks, (0, LANES - N_HEADS))[None, :]
            qa, ka, va, qb, kb, vb, ka16, va16, kvb16 = _qkv_proj(
                xp, row(norm_mix[layer]), w_in, gain_row, cos_p, sin_p,
                tm=_row_block(xp.shape[0]), q_dtype=BF16, bf16_kv=True)
            seq = lambda a: a.reshape(n_p, s_p, a.shape[-1])
            branches = [_attn_a_branch(seq(qa), seq(ka16), seq(va16), dilation=r) for r in A_DILATIONS]
            oas = [o.reshape(n_p * s_p, AW) for o, _ in branches]
            lses = [l.reshape(n_p * s_p, LANES) for _, l in branches]
            ob = _attn_b(seq(qb), seq(kvb16), sink_rows).reshape(n_p * s_p, AW)
            xp = _mix_out_ffn(xp, oas, lses, ob, w_out, row(norm_ffn[layer]), w1, w3, w2,
                              tm=_row_block(xp.shape[0]))
            tail = lambda a, rows, nh: seq(a)[:, s_p - rows:].reshape(n_p, rows, nh, HEAD_DIM)
            akp.append(tail(ka, la_p, N_HEADS))
            avp.append(tail(va, la_p, N_HEADS))
            bkp.append(tail(kb, lb_p, B_KV_HEADS))
            bvp.append(tail(vb, lb_p, B_KV_HEADS))
            qa, ka, va, qb, kb, vb = _qkv_proj(xs, row(norm_mix[layer]), w_in, gain_row, cos_s, sin_s,
                                               tm=_row_block(xs.shape[0]), q_dtype=F32, bf16_kv=False)
            flat = lambda c: c.reshape(c.shape[0], c.shape[1], c.shape[2] * c.shape[3])
            oa, ob = _sample_attn(qa, ka, va, qb, kb, vb, cache_a_k[i], cache_a_v[i],
                                  flat(cache_b_k[i]), flat(cache_b_v[i]), sink_lane_row, t_new=t_new)
            xs = _mix_out_ffn(xs, [oa], [], ob, w_out, row(norm_ffn[layer]), w1, w3, w2,
                              tm=_row_block(xs.shape[0]))
            heads_s = lambda a, nh: a.reshape(n_s, t_new, nh, HEAD_DIM)
            aks.append(heads_s(ka, N_HEADS))
            avs.append(heads_s(va, N_HEADS))
            bks.append(heads_s(kb, B_KV_HEADS))
            bvs.append(heads_s(vb, B_KV_HEADS))
        else:
            w_in = c_w_in[i].astype(BF16)
            w_out = c_w_out[i].astype(BF16)
            tril = jnp.tril(jnp.ones((CHUNK, CHUNK), F32))
            mix_p = (c_w_s[i] * tril).astype(BF16)
            gw = w_out.shape[0] // C_GROUPS
            bias_p = jnp.repeat(c_b_s[i].T, gw, axis=1)
            per_tile = CHUNK // t_new
            mix_s = jnp.einsum("ab,gij->gaibj", jnp.eye(per_tile, dtype=F32), (c_w_s[i] * tril)[:, :t_new, :t_new])
            mix_s = mix_s.reshape(C_GROUPS, CHUNK, CHUNK).astype(BF16)
            bias_s = jnp.tile(bias_p[:t_new], (per_tile, 1))
            wr = moe_router[i]
            wr_hi = wr.astype(BF16)
            wr_lo = (wr - wr_hi.astype(F32)).astype(BF16)
            pad_e = lambda a: jnp.pad(a, ((0, 0), (0, LANES - N_EXPERTS)))
            wr_hi_lo = jnp.concatenate([pad_e(wr_hi), pad_e(wr_lo)], axis=1)
            common = (row(c_ln_g[i]), row(c_ln_b[i]))
            xp1, hp, lp = _gmlp_block(xp, row(norm_mix[layer]), w_in, *common, mix_p, bias_p, w_out,
                                      row(norm_ffn[layer]), wr_hi_lo, tm=_row_block(xp.shape[0]), write_v=False)
            xs1, hs, ls, v_new = _gmlp_block(xs, row(norm_mix[layer]), w_in, *common, mix_s, bias_s, w_out,
                                             row(norm_ffn[layer]), wr_hi_lo, tm=_row_block(xs.shape[0]),
                                             write_v=True)
            cvs.append(v_new.reshape(n_s, t_new, v_new.shape[-1]))
            xp, xs = _moe([(xp1, hp, lp[:, :N_EXPERTS]), (xs1, hs, ls[:, :N_EXPERTS])],
                          moe_w1[i], moe_w3[i], moe_w2[i])
    return (xp.reshape(n_p, s_p, d), xs.reshape(n_s, t_new, d),
            jnp.stack(akp), jnp.stack(avp), jnp.stack(bkp), jnp.stack(bvp),
            jnp.stack(aks), jnp.stack(avs), jnp.stack(bks), jnp.stack(bvs),
            jnp.stack(cvs))
```

```python
import functools

import jax
import jax.numpy as jnp
from jax import lax
from jax.experimental import pallas as pl
from jax.experimental.pallas import tpu as pltpu

F32 = jnp.float32
BF16 = jnp.bfloat16

HEAD_DIM = 64
HALF_DIM = HEAD_DIM // 2
N_HEADS = 8
B_KV_HEADS = 2
A_DILATIONS = (1, 4, 16)
BAND = 128
ROPE_THETA = 10000.0
NORM_EPS = 1e-6
NEG_INF = -1e30
CHUNK = 128
C_GROUPS = 8
N_EXPERTS = 8
TOP_K = 2

LANES = 128
SUBLANES = 8
VMEM_LIMIT_BYTES = 56 * 1024 * 1024

AW = N_HEADS * HEAD_DIM
KVW = B_KV_HEADS * HEAD_DIM
HEADS_PER_VREG = LANES // HEAD_DIM
N_HEAD_BLOCKS = AW // LANES


def _cparams(*sem):
    return pltpu.CompilerParams(dimension_semantics=sem, vmem_limit_bytes=VMEM_LIMIT_BYTES)


def _rms_norm(x, g):
    return x * lax.rsqrt(jnp.mean(x * x, axis=-1, keepdims=True) + NORM_EPS) * g


def _dot(a, b):
    return jnp.dot(a, b, preferred_element_type=F32)


def _dot_nt(a, b):
    return lax.dot_general(a, b, (((1,), (1,)), ((), ())), preferred_element_type=F32)


_QA_BLOCKS = range(0, 4)
_KA_BLOCKS = range(4, 8)
_VA_BLOCKS = range(8, 12)
_QB_BLOCKS = range(12, 16)
_KB_BLOCK = 16
_VB_BLOCK = 17


def _proj_kernel(x_ref, g_ref, w_ref, gain_ref, cos_ref, sin_ref,
                 qa_ref, ka_ref, va_ref, qb_ref, kb_ref, vb_ref, *prompt_refs):
    stage = prompt_refs[-1] if prompt_refs else None
    h = _rms_norm(x_ref[...], g_ref[...]).astype(BF16)
    p = _dot(h, w_ref[...])
    tm = p.shape[0]
    cos = cos_ref[...]
    sin = sin_ref[...]
    lane = lax.broadcasted_iota(jnp.int32, (tm, LANES), 1)
    first_half = (lane & HALF_DIM) == 0
    li = lax.broadcasted_iota(jnp.int32, (LANES, LANES), 0) // HEAD_DIM
    lj = lax.broadcasted_iota(jnp.int32, (LANES, LANES), 1) // HEAD_DIM
    head_mean = jnp.where(li == lj, 1.0 / HEAD_DIM, 0.0).astype(BF16)

    def norm_rope(blk):
        pc = p[:, blk * LANES:(blk + 1) * LANES]
        ms = _dot((pc * pc).astype(BF16), head_mean)
        y = pc * lax.rsqrt(ms + NORM_EPS) * gain_ref[:, blk * LANES:(blk + 1) * LANES]
        partner = jnp.where(first_half, pltpu.roll(y, LANES - HALF_DIM, 1), pltpu.roll(y, HALF_DIM, 1))
        return y * cos + partner * sin

    scale = HEAD_DIM ** -0.5
    for j, blk in enumerate(_QA_BLOCKS):
        q = norm_rope(blk) * scale
        qa_ref[:, j * LANES:(j + 1) * LANES] = q.astype(qa_ref.dtype)
        if stage is not None:
            stage[j] = q
    for j, blk in enumerate(_KA_BLOCKS):
        k = norm_rope(blk)
        ka_ref[:, j * LANES:(j + 1) * LANES] = k
        if stage is not None:
            stage[N_HEAD_BLOCKS + j] = k
    for j, blk in enumerate(_VA_BLOCKS):
        v = p[:, blk * LANES:(blk + 1) * LANES]
        va_ref[:, j * LANES:(j + 1) * LANES] = v
        if stage is not None:
            stage[2 * N_HEAD_BLOCKS + j] = v
    for j, blk in enumerate(_QB_BLOCKS):
        qb_ref[:, j * LANES:(j + 1) * LANES] = (norm_rope(blk) * scale).astype(qb_ref.dtype)
    kb_ref[...] = norm_rope(_KB_BLOCK)
    vb_ref[...] = p[:, _VB_BLOCK * LANES:(_VB_BLOCK + 1) * LANES]
    if prompt_refs:
        ka16_ref, va16_ref, kvb16_ref = prompt_refs[:3]
        ka16_ref[...] = ka_ref[...].astype(BF16)
        va16_ref[...] = va_ref[...].astype(BF16)
        kb = kb_ref[...]
        vb = vb_ref[...]
        parts = (kb, pltpu.roll(kb, HEAD_DIM, 1), vb, pltpu.roll(vb, HEAD_DIM, 1))
        for j, part in enumerate(parts):
            kvb16_ref[:, j * KVW:(j + 1) * KVW] = part.astype(BF16)
        for idx, r in enumerate(A_DILATIONS[1:]):
            for which, dst in enumerate(prompt_refs[3 + 3 * idx:6 + 3 * idx]):
                for c in range(r):
                    for cb in range(N_HEAD_BLOCKS):
                        rows = stage[which * N_HEAD_BLOCKS + cb, pl.ds(c, tm // r, stride=r), :]
                        dst[c, :, cb * LANES:(cb + 1) * LANES] = rows.astype(BF16)


def _qkv_proj(x, g, w, gain_row, cos, sin, *, tm, q_dtype, seq_len=None):
    t, d = x.shape
    n_out = w.shape[1]
    row = lambda i: (i, 0)
    fixed = lambda i: (0, 0)
    sds = jax.ShapeDtypeStruct
    out_specs = [pl.BlockSpec((tm, wd), row) for wd in (AW, AW, AW, AW, KVW, KVW)]
    out_shape = [sds((t, wd), dt) for wd, dt in zip((AW, AW, AW, AW, KVW, KVW), (q_dtype, F32, F32, q_dtype, F32, F32))]
    scratch = []
    if seq_len is not None:
        out_specs += [pl.BlockSpec((tm, wd), row) for wd in (AW, AW, 4 * KVW)]
        out_shape += [sds((t, wd), BF16) for wd in (AW, AW, 4 * KVW)]
        tiles = seq_len // tm
        for r in A_DILATIONS[1:]:
            assert seq_len % tm == 0 and tm % (r * 16) == 0
            out_specs += [pl.BlockSpec((None, r, tm // r, AW), lambda i: (i // tiles, 0, i % tiles, 0))] * 3
            out_shape += [sds((t // seq_len, r, seq_len // r, AW), BF16)] * 3
        scratch = [pltpu.VMEM((3 * N_HEAD_BLOCKS, tm, LANES), F32)]
    return pl.pallas_call(
        _proj_kernel,
        grid=(t // tm,),
        in_specs=[pl.BlockSpec((tm, d), row), pl.BlockSpec((1, d), fixed), pl.BlockSpec((d, n_out), fixed),
                  pl.BlockSpec((1, n_out), fixed), pl.BlockSpec((tm, LANES), row), pl.BlockSpec((tm, LANES), row)],
        out_specs=out_specs, out_shape=out_shape, scratch_shapes=scratch,
        compiler_params=_cparams("arbitrary"),
        name="qkv_proj",
    )(x, g, w, gain_row, cos, sin)


def _fill_kv(kbuf, vbuf, kp_ref, kc_ref, vp_ref, vc_ref):
    kbuf[0:BAND, :] = kp_ref[...].astype(BF16)
    kbuf[BAND:, :] = kc_ref[...].astype(BF16)
    vbuf[0:BAND, :] = vp_ref[...].astype(BF16)
    vbuf[BAND:, :] = vc_ref[...].astype(BF16)


def _band_masks(rows, first_block):
    assert BAND == LANES
    row = lax.broadcasted_iota(jnp.int32, (rows, BAND), 0) & (BAND - 1)
    col = lax.broadcasted_iota(jnp.int32, (rows, BAND), 1)
    shifted = row + jnp.where(first_block, 2 * BAND, 0)
    return col > shifted, col == shifted, col <= row


def _banded_softmax_pv(qm, kp, kc, vp, vc, masks, floor_row=None):
    upper, diag, lower = masks
    s_prev = _dot_nt(qm, kp)
    s = jnp.where(upper, s_prev, jnp.where(lower, _dot_nt(qm, kc), NEG_INF))
    back = jnp.where(diag, s_prev, NEG_INF if floor_row is None else floor_row)
    m = jnp.max(jnp.maximum(s, back), axis=-1, keepdims=True)
    p = jnp.exp(s - m)
    p_back = jnp.where(diag, jnp.exp(s_prev - m), 0.0)
    l = jnp.sum(p + p_back, axis=-1, keepdims=True)
    p_prev = jnp.where(upper, p, p_back).astype(BF16)
    p_own = jnp.where(lower, p, 0.0).astype(BF16)
    return m, l, _dot(p_prev, vp) + _dot(p_own, vc)


def _attn_a_kernel(q_ref, kc_ref, kp_ref, vc_ref, vp_ref, o_ref, lse_ref, kbuf, vbuf, *, tq):
    blk = pl.program_id(2)
    _fill_kv(kbuf, vbuf, kp_ref, kc_ref, vp_ref, vc_ref)
    lane = lax.broadcasted_iota(jnp.int32, (BAND, LANES), 1)
    low_head = lane < HEAD_DIM

    def sub_block(sb, carry):
        r0 = pl.multiple_of(sb * BAND, BAND)
        masks = _band_masks(HEADS_PER_VREG * BAND, (blk + sb) == 0)
        lse_tile = jnp.zeros((BAND, LANES), F32)
        for hb in range(N_HEAD_BLOCKS):
            cs = slice(hb * LANES, (hb + 1) * LANES)
            q2 = q_ref[pl.ds(r0, BAND), cs]
            zero = jnp.zeros_like(q2)
            qm = jnp.concatenate([jnp.where(low_head, q2, zero), jnp.where(low_head, zero, q2)], axis=0)
            m, l, pv = _banded_softmax_pv(qm, kbuf[pl.ds(r0, BAND), cs], kbuf[pl.ds(r0 + BAND, BAND), cs],
                                          vbuf[pl.ds(r0, BAND), cs], vbuf[pl.ds(r0 + BAND, BAND), cs], masks)
            o = pv * (1.0 / l)
            o_ref[pl.ds(r0, BAND), cs] = jnp.where(low_head, o[:BAND], o[BAND:]).astype(o_ref.dtype)
            lse = m + jnp.log(l)
            for hh in range(HEADS_PER_VREG):
                lse_tile = jnp.where(lane == hb * HEADS_PER_VREG + hh, lse[hh * BAND:(hh + 1) * BAND], lse_tile)
        lse_ref[pl.ds(r0, BAND), :] = lse_tile
        return carry

    lax.fori_loop(0, tq // BAND, sub_block, 0)


def _attn_a_branch(q, k, v):
    n, r, sub_len, w = q.shape
    tq = min(512, sub_len)
    cur = lambda b, c, i: (b, c, i, 0)
    prv = lambda b, c, i: (b, c, jnp.maximum(i * (tq // BAND) - 1, 0), 0)
    blk_cur = pl.BlockSpec((None, None, tq, w), cur)
    blk_prv = pl.BlockSpec((None, None, BAND, w), prv)
    sds = jax.ShapeDtypeStruct
    return pl.pallas_call(
        functools.partial(_attn_a_kernel, tq=tq),
        grid=(n, r, sub_len // tq),
        in_specs=[blk_cur, blk_cur, blk_prv, blk_cur, blk_prv],
        out_specs=[blk_cur, pl.BlockSpec((None, None, tq, LANES), cur)],
        out_shape=[sds((n, r, sub_len, w), BF16), sds((n, r, sub_len, LANES), F32)],
        scratch_shapes=[pltpu.VMEM((tq + BAND, w), BF16), pltpu.VMEM((tq + BAND, w), BF16)],
        compiler_params=_cparams("arbitrary", "arbitrary", "arbitrary"),
        name=f"attn_a_dil{r}",
    )(q, k, k, v, v)


_KVB_K, _KVB_K_SWAPPED, _KVB_V, _KVB_V_SWAPPED = range(4)
_KVB_WIDTH = 4 * KVW


def _attn_b_kernel(q_ref, kvc_ref, kvp_ref, sink_ref, o_ref, kvbuf, *, tq):
    blk = pl.program_id(1)
    kvbuf[0:BAND, :] = kvp_ref[...]
    kvbuf[BAND:, :] = kvc_ref[...]
    lane = lax.broadcasted_iota(jnp.int32, (BAND, LANES), 1)
    low_head = lane < HEAD_DIM
    group = N_HEADS // B_KV_HEADS
    plain = [h for h in range(N_HEADS) if h % HEADS_PER_VREG == h // group]
    swapped = [h for h in range(N_HEADS) if h % HEADS_PER_VREG != h // group]

    def sub_block(sb, carry):
        r0 = pl.multiple_of(sb * BAND, BAND)
        masks = _band_masks(len(plain) * BAND, (blk + sb) == 0)
        blocks = [q_ref[pl.ds(r0, BAND), hb * LANES:(hb + 1) * LANES] for hb in range(N_HEAD_BLOCKS)]
        outs = [None] * N_HEADS
        for heads, k_blk, v_blk in ((plain, _KVB_K, _KVB_V), (swapped, _KVB_K_SWAPPED, _KVB_V_SWAPPED)):
            ks = slice(k_blk * KVW, (k_blk + 1) * KVW)
            vs = slice(v_blk * KVW, (v_blk + 1) * KVW)
            q_parts, sink_parts = [], []
            for h in heads:
                q2 = blocks[h // HEADS_PER_VREG]
                own_half = low_head if h % HEADS_PER_VREG == 0 else jnp.logical_not(low_head)
                q_parts.append(jnp.where(own_half, q2, jnp.zeros_like(q2)))
                sink_parts.append(jnp.broadcast_to(sink_ref[h:h + 1, :], (BAND, LANES)))
            sink = jnp.concatenate(sink_parts, axis=0)
            m, l, pv = _banded_softmax_pv(jnp.concatenate(q_parts, axis=0),
                                          kvbuf[pl.ds(r0, BAND), ks], kvbuf[pl.ds(r0 + BAND, BAND), ks],
                                          kvbuf[pl.ds(r0, BAND), vs], kvbuf[pl.ds(r0 + BAND, BAND), vs],
                                          masks, floor_row=sink)
            o = pv * (1.0 / (l + jnp.exp(sink - m)))
            for idx, h in enumerate(heads):
                outs[h] = o[idx * BAND:(idx + 1) * BAND]
        for hb in range(N_HEAD_BLOCKS):
            o_ref[pl.ds(r0, BAND), hb * LANES:(hb + 1) * LANES] = jnp.where(
                low_head, outs[hb * HEADS_PER_VREG], outs[hb * HEADS_PER_VREG + 1]).astype(o_ref.dtype)
        return carry

    lax.fori_loop(0, tq // BAND, sub_block, 0)


def _attn_b(q, kv, sinks_rows):
    n, s, w = q.shape
    tq = min(512, s)
    cur = lambda b, i: (b, i, 0)
    prv = lambda b, i: (b, jnp.maximum(i * (tq // BAND) - 1, 0), 0)
    return pl.pallas_call(
        functools.partial(_attn_b_kernel, tq=tq),
        grid=(n, s // tq),
        in_specs=[pl.BlockSpec((None, tq, w), cur), pl.BlockSpec((None, tq, _KVB_WIDTH), cur),
                  pl.BlockSpec((None, BAND, _KVB_WIDTH), prv),
                  pl.BlockSpec((N_HEADS, LANES), lambda b, i: (0, 0))],
        out_specs=pl.BlockSpec((None, tq, w), cur),
        out_shape=jax.ShapeDtypeStruct((n, s, w), BF16),
        scratch_shapes=[pltpu.VMEM((tq + BAND, _KVB_WIDTH), BF16)],
        compiler_params=_cparams("arbitrary", "arbitrary"),
        name="attn_b",
    )(q, kv, kv, sinks_rows)


NEW_ROWS = 8


def _branch_multiplicity(t_query, key_pos, la):
    back = la + t_query - key_pos
    mult = jnp.zeros(back.shape, F32)
    for r in A_DILATIONS:
        mult = mult + ((back >= 0) & (back <= BAND * r) & (back % r == 0)).astype(F32)
    return mult


def _sample_attn_kernel(qa_ref, kna_ref, vna_ref, qb_ref, knb_ref, vnb_ref,
                        kt_ref, vt_ref, cbk_ref, cbv_ref, sink_ref,
                        oa_ref, ob_ref, *, nb, t_new):
    la = kt_ref.shape[-1]
    t_c = lax.broadcasted_iota(jnp.int32, (NEW_ROWS, la), 0)
    t_c = jnp.where(t_c < t_new, t_c, 0)
    mult_c = _branch_multiplicity(t_c, lax.broadcasted_iota(jnp.int32, (NEW_ROWS, la), 1), la)
    t_n = lax.broadcasted_iota(jnp.int32, (NEW_ROWS, NEW_ROWS), 0)
    t_n = jnp.where(t_n < t_new, t_n, 0)
    key_n = lax.broadcasted_iota(jnp.int32, (NEW_ROWS, NEW_ROWS), 1)
    mult_n = jnp.where(key_n < t_new, _branch_multiplicity(t_n, la + key_n, la), 0.0)

    def head_attn_a(q_h, kt, vt, kn_h, vn_h):
        s_c = jnp.where(mult_c > 0.0, _dot(q_h, kt), NEG_INF)
        s_n = jnp.where(mult_n > 0.0, _dot_nt(q_h, kn_h), NEG_INF)
        m = jnp.maximum(jnp.max(s_c, axis=-1, keepdims=True), jnp.max(s_n, axis=-1, keepdims=True))
        p_c = jnp.exp(s_c - m) * mult_c
        p_n = jnp.exp(s_n - m) * mult_n
        l = jnp.sum(p_c, axis=-1, keepdims=True) + jnp.sum(p_n, axis=-1, keepdims=True)
        return (_dot_nt(p_c.astype(BF16), vt) + _dot(p_n.astype(BF16), vn_h)) / l

    lane_w = lax.broadcasted_iota(jnp.int32, (AW, LANES), 0) // HEAD_DIM
    head_w = lax.broadcasted_iota(jnp.int32, (AW, LANES), 1)
    head_sum = (lane_w == head_w).astype(BF16)
    head_e = lax.broadcasted_iota(jnp.int32, (LANES, AW), 0)
    lane_e = lax.broadcasted_iota(jnp.int32, (LANES, AW), 1) // HEAD_DIM
    head_expand = (head_e == lane_e).astype(BF16)
    row_c = lax.broadcasted_iota(jnp.int32, (BAND, 1), 0)
    row_n = lax.broadcasted_iota(jnp.int32, (NEW_ROWS, 1), 0)
    lane = lax.broadcasted_iota(jnp.int32, (BAND, LANES), 1)
    low_head = lane < HEAD_DIM
    low_head_n = low_head[:NEW_ROWS]
    sink_row = sink_ref[...]

    def scores(k, q_row):
        return _dot((k * q_row).astype(BF16), head_sum)

    def attend(q_row, blocks, sink=None):
        ss = []
        m = None
        for k, _, valid, _ in blocks:
            s = scores(k, q_row)
            if valid is not None:
                s = jnp.where(valid, s, NEG_INF)
            ss.append(s)
            bm = jnp.max(s, axis=0, keepdims=True)
            m = bm if m is None else jnp.maximum(m, bm)
        if sink is not None:
            m = jnp.maximum(m, sink)
        ps = []
        l = jnp.exp(sink - m) if sink is not None else 0.0
        for s, (_, _, valid, mult) in zip(ss, blocks):
            p = jnp.exp(s - m)
            if mult is not None:
                p = p * mult
            ps.append(p)
            l = l + jnp.sum(p, axis=0, keepdims=True)
        inv_l = 1.0 / l
        acc = None
        for p, (_, v, _, _) in zip(ps, blocks):
            pe = _dot((p * inv_l).astype(BF16), head_expand)
            part = jnp.sum(pe * v, axis=0, keepdims=True)
            acc = part if acc is None else acc + part
        return acc

    def expand_kv(x, low):
        sw = pltpu.roll(x, HEAD_DIM, 1)
        kv0 = jnp.where(low, x, sw)
        kv1 = jnp.where(low, sw, x)
        return jnp.concatenate([kv0, kv0, kv1, kv1], axis=1)

    def one_sequence(n, carry):
        new0 = pl.multiple_of(n * NEW_ROWS, NEW_ROWS)
        qa = qa_ref[pl.ds(new0, NEW_ROWS), :].astype(BF16)
        kna = kna_ref[pl.ds(new0, NEW_ROWS), :].astype(BF16)
        vna = vna_ref[pl.ds(new0, NEW_ROWS), :].astype(BF16)
        heads_out = []
        for h in range(N_HEADS):
            hs = slice(h * HEAD_DIM, (h + 1) * HEAD_DIM)
            heads_out.append(head_attn_a(qa[:, hs], kt_ref[n, h].astype(BF16), vt_ref[n, h].astype(BF16),
                                         kna[:, hs], vna[:, hs]))
        oa_ref[pl.ds(new0, NEW_ROWS), :] = jnp.concatenate(heads_out, axis=1).astype(oa_ref.dtype)
        kb_c = expand_kv(cbk_ref[n], low_head)
        vb_c = expand_kv(cbv_ref[n], low_head)
        kb_n = expand_kv(knb_ref[pl.ds(new0, NEW_ROWS), :], low_head_n)
        vb_n = expand_kv(vnb_ref[pl.ds(new0, NEW_ROWS), :], low_head_n)
        qb = qb_ref[pl.ds(new0, NEW_ROWS), :]
        out_b = jnp.zeros((NEW_ROWS, AW), F32)
        for t in range(t_new):
            acc_b = attend(qb[t:t + 1, :], [
                (kb_c, vb_c, row_c >= t, None),
                (kb_n, vb_n, row_n <= t, None),
            ], sink=sink_row)
            out_b = jnp.where(row_n == t, acc_b, out_b)
        ob_ref[pl.ds(new0, NEW_ROWS), :] = out_b.astype(ob_ref.dtype)
        return carry

    lax.fori_loop(0, nb, one_sequence, 0)


def _pad_new_rows(a, n_seq, t_new):
    w = a.shape[-1]
    a = a.reshape(n_seq, t_new, w)
    a = jnp.pad(a, ((0, 0), (0, NEW_ROWS - t_new), (0, 0)))
    return a.reshape(n_seq * NEW_ROWS, w)


def _sample_attn(qa, ka, va, qb, kb, vb, cache_a_k, cache_a_v, cache_b_k, cache_b_v, sink_row, *, t_new):
    n_seq, la, nh, hd = cache_a_k.shape
    w = nh * hd
    assert t_new <= NEW_ROWS
    nb = 2 if n_seq % 2 == 0 else 1
    padded = [_pad_new_rows(a, n_seq, t_new) for a in (qa, ka, va, qb, kb, vb)]
    dim_major = lambda c: jnp.transpose(c, (0, 2, 3, 1))
    new_a = pl.BlockSpec((nb * NEW_ROWS, w), lambda i: (i, 0))
    new_b = pl.BlockSpec((nb * NEW_ROWS, KVW), lambda i: (i, 0))
    spec_ca = pl.BlockSpec((nb, nh, hd, la), lambda i: (i, 0, 0, 0))
    spec_cb = pl.BlockSpec((nb, BAND, KVW), lambda i: (i, 0, 0))
    sds = jax.ShapeDtypeStruct
    oa, ob = pl.pallas_call(
        functools.partial(_sample_attn_kernel, nb=nb, t_new=t_new),
        grid=(n_seq // nb,),
        in_specs=[new_a, new_a, new_a, new_a, new_b, new_b, spec_ca, spec_ca, spec_cb, spec_cb,
                  pl.BlockSpec((1, LANES), lambda i: (0, 0))],
        out_specs=[new_a, new_a],
        out_shape=[sds((n_seq * NEW_ROWS, w), BF16), sds((n_seq * NEW_ROWS, w), BF16)],
        compiler_params=_cparams("arbitrary"),
        name="sample_attn",
    )(*padded, dim_major(cache_a_k), dim_major(cache_a_v), cache_b_k, cache_b_v, sink_row)
    unpad = lambda o: o.reshape(n_seq, NEW_ROWS, w)[:, :t_new].reshape(n_seq * t_new, w)
    return unpad(oa), unpad(ob)


def _silu(a):
    return a * (1.0 / (1.0 + jnp.exp(-a)))


def _natural_order(o_ref, lse_ref, o_stage, lse_stage):
    r, sub, _ = o_ref.shape
    if r == 1:
        return o_ref[0].astype(F32), lse_ref[0]
    for c in range(r):
        lse_stage[pl.ds(c, sub, stride=r), :] = lse_ref[c]
        for cb in range(N_HEAD_BLOCKS):
            o_stage[cb, pl.ds(c, sub, stride=r), :] = o_ref[c, :, cb * LANES:(cb + 1) * LANES].astype(F32)
    return jnp.concatenate([o_stage[cb] for cb in range(N_HEAD_BLOCKS)], axis=1), lse_stage[...]


def _merge_branches(outs, lses):
    top = functools.reduce(jnp.maximum, lses)
    ws = [jnp.exp(l - top) for l in lses]
    inv_den = 1.0 / functools.reduce(lambda a, b: a + b, ws)
    head = lax.broadcasted_iota(jnp.int32, (LANES, AW), 0)
    lane_head = lax.broadcasted_iota(jnp.int32, (LANES, AW), 1) // HEAD_DIM
    expand = (head == lane_head).astype(BF16)
    oa = None
    for w, o in zip(ws, outs):
        w = w * inv_den
        w_hi = w.astype(BF16)
        w_lo = (w - w_hi.astype(F32)).astype(BF16)
        part = (_dot(w_hi, expand) + _dot(w_lo, expand)) * o
        oa = part if oa is None else oa + part
    return oa.astype(BF16)


def _mix_out_ffn_kernel(x_ref, *refs, n_branches):
    if n_branches:
        branch_refs, refs = refs[:2 * n_branches], refs[2 * n_branches:]
        refs, (o_stage, lse_stage) = refs[:-2], refs[-2:]
    else:
        oa_ref, refs = refs[0], refs[1:]
    ob_ref, wo_ref, g_ref, w1_ref, w3_ref, w2_ref, out_ref, x1_ref, h_ref, acc_ref = refs
    j = pl.program_id(1)

    @pl.when(j == 0)
    def _():
        if n_branches:
            outs, lses = zip(*[_natural_order(branch_refs[2 * i], branch_refs[2 * i + 1], o_stage, lse_stage)
                               for i in range(n_branches)])
            oa = _merge_branches(outs, lses)
        else:
            oa = oa_ref[...]
        x1 = x_ref[...] + _dot(oa, wo_ref[0:AW, :]) + _dot(ob_ref[...], wo_ref[AW:, :])
        x1_ref[...] = x1
        h_ref[...] = _rms_norm(x1, g_ref[...]).astype(BF16)
        acc_ref[...] = jnp.zeros_like(acc_ref)

    h = h_ref[...]
    gate = (_silu(_dot(h, w1_ref[...])) * _dot(h, w3_ref[...])).astype(BF16)
    acc_ref[...] += _dot(gate, w2_ref[...])

    @pl.when(j == pl.num_programs(1) - 1)
    def _():
        out_ref[...] = x1_ref[...] + acc_ref[...]


def _ff_tile(ff, target):
    best = LANES
    for k in range(1, ff // LANES + 1):
        if ff % (k * LANES) == 0 and k * LANES <= target:
            best = k * LANES
    return best


def _mix_out_ffn(x, oa, ob, wo, g, w1, w3, w2, *, tm):
    t, d = x.shape
    ff = w1.shape[1]
    tf = _ff_tile(ff, 1408)
    row = lambda i, j: (i, 0)
    fixed = lambda i, j: (0, 0)
    if isinstance(oa, (list, tuple)):
        n_branches = len(oa)
        tiles = t // oa[0][0].shape[0] // tm
        a_specs, a_args = [], []
        for o, lse in oa:
            r = o.shape[1]
            assert tm % (r * SUBLANES) == 0
            tile = lambda i, j: (i // tiles, 0, i % tiles, 0)
            a_specs += [pl.BlockSpec((None, r, tm // r, AW), tile), pl.BlockSpec((None, r, tm // r, LANES), tile)]
            a_args += [o, lse]
        stage = [pltpu.VMEM((N_HEAD_BLOCKS, tm, LANES), F32), pltpu.VMEM((tm, LANES), F32)]
    else:
        n_branches, a_specs, a_args, stage = 0, [pl.BlockSpec((tm, AW), row)], [oa], []
    return pl.pallas_call(
        functools.partial(_mix_out_ffn_kernel, n_branches=n_branches),
        grid=(t // tm, ff // tf),
        in_specs=[pl.BlockSpec((tm, d), row)] + a_specs
                 + [pl.BlockSpec((tm, AW), row),
                  pl.BlockSpec((2 * AW, d), fixed), pl.BlockSpec((1, d), fixed),
                  pl.BlockSpec((d, tf), lambda i, j: (0, j)), pl.BlockSpec((d, tf), lambda i, j: (0, j)),
                  pl.BlockSpec((tf, d), lambda i, j: (j, 0))],
        out_specs=pl.BlockSpec((tm, d), row),
        out_shape=jax.ShapeDtypeStruct((t, d), F32),
        scratch_shapes=[pltpu.VMEM((tm, d), F32), pltpu.VMEM((tm, d), BF16), pltpu.VMEM((tm, d), F32)] + stage,
        compiler_params=_cparams("arbitrary", "arbitrary"),
        name="mix_out_ffn",
    )(x, *a_args, ob, wo, g, w1, w3, w2)


def _gelu_exact(z):
    return 0.5 * z * (1.0 + lax.erf(z * (2.0 ** -0.5)))


def _gmlp_kernel(x_ref, g_ref, win_ref, lng_ref, lnb_ref, mix_ref, bias_ref, wout_ref, gffn_ref, wr_ref,
                 *out_refs, write_v):
    if write_v:
        xo_ref, h2_ref, logit_ref, v_ref = out_refs[:4]
    else:
        xo_ref, h2_ref, logit_ref = out_refs[:3]
    gate_ref = out_refs[-1]
    x = x_ref[...]
    tm, cd = x.shape
    h = _rms_norm(x, g_ref[...]).astype(BF16)
    z = _gelu_exact(_dot(h, win_ref[...]))
    u = z[:, :cd]
    v = z[:, cd:]
    mu = jnp.mean(v, axis=-1, keepdims=True)
    vc = v - mu
    v = vc * lax.rsqrt(jnp.mean(vc * vc, axis=-1, keepdims=True) + NORM_EPS) * lng_ref[...] + lnb_ref[...]
    if write_v:
        v_ref[...] = v
    vb = v.astype(BF16)
    bias = bias_ref[...]
    gw = cd // C_GROUPS
    for c in range(tm // CHUNK):
        rs = slice(c * CHUNK, (c + 1) * CHUNK)
        for g in range(C_GROUPS):
            cs = slice(g * gw, (g + 1) * gw)
            f = _dot(mix_ref[g], vb[rs, cs]) + bias[:, cs]
            gate_ref[rs, cs] = (u[rs, cs] * f).astype(BF16)
    xo = x + _dot(gate_ref[...], wout_ref[...])
    xo_ref[...] = xo
    h2 = _rms_norm(xo, gffn_ref[...])
    h2_ref[...] = h2
    h_hi = h2.astype(BF16)
    h_lo = (h2 - h_hi.astype(F32)).astype(BF16)
    wr = wr_ref[...]
    both = _dot(h_hi, wr)
    logit_ref[...] = both[:, :LANES] + both[:, LANES:] + _dot(h_lo, wr[:, :LANES])


def _gmlp_block(x, g, w_in, ln_g, ln_b, mix, bias_full, w_out, g_ffn, wr_hi_lo, *, tm, write_v):
    t, d = x.shape
    cd = w_out.shape[0]
    row = lambda i: (i, 0)
    fixed = lambda i: (0, 0)
    sds = jax.ShapeDtypeStruct
    out_specs = [pl.BlockSpec((tm, d), row), pl.BlockSpec((tm, d), row), pl.BlockSpec((tm, LANES), row)]
    out_shape = [sds((t, d), F32), sds((t, d), F32), sds((t, LANES), F32)]
    if write_v:
        out_specs.append(pl.BlockSpec((tm, cd), row))
        out_shape.append(sds((t, cd), F32))
    return pl.pallas_call(
        functools.partial(_gmlp_kernel, write_v=write_v),
        grid=(t // tm,),
        in_specs=[pl.BlockSpec((tm, d), row), pl.BlockSpec((1, d), fixed), pl.BlockSpec((d, 2 * cd), fixed),
                  pl.BlockSpec((1, cd), fixed), pl.BlockSpec((1, cd), fixed),
                  pl.BlockSpec((C_GROUPS, CHUNK, CHUNK), lambda i: (0, 0, 0)),
                  pl.BlockSpec((CHUNK, cd), fixed), pl.BlockSpec((cd, d), fixed), pl.BlockSpec((1, d), fixed),
                  pl.BlockSpec((d, 2 * LANES), fixed)],
        out_specs=out_specs, out_shape=out_shape,
        scratch_shapes=[pltpu.VMEM((tm, cd), BF16)],
        compiler_params=_cparams("arbitrary"),
        name="gmlp_block",
    )(x, g, w_in, ln_g, ln_b, mix, bias_full, w_out, g_ffn, wr_hi_lo)


MOE_BLOCK = 1024
MOE_SUB = 256
DISPATCH_ROWS = 512


def _dispatch_kernel(dest_ref, h_ref, xs_in_ref, xs_ref, sem):
    del xs_in_ref
    rows = h_ref.shape[0]

    def issue(g, carry):
        for s in range(SUBLANES):
            r = g * SUBLANES + s
            for k in range(TOP_K):
                d = dest_ref[0, 0, r * TOP_K + k]
                pltpu.make_async_copy(h_ref.at[pl.ds(r, 1), :], xs_ref.at[pl.ds(d, 1), :], sem).start(priority=k)
        return carry

    lax.fori_loop(0, rows // SUBLANES, issue, 0)
    for _ in range(TOP_K):
        pltpu.make_async_copy(h_ref, xs_ref.at[pl.ds(0, rows), :], sem).wait()


def _dispatch(h, dest, xs):
    t, d = h.shape
    tg = DISPATCH_ROWS
    return pl.pallas_call(
        _dispatch_kernel,
        grid=(t // tg,),
        in_specs=[pl.BlockSpec((1, 1, tg * TOP_K), lambda i: (i, 0, 0), memory_space=pltpu.SMEM),
                  pl.BlockSpec((tg, d), lambda i: (i, 0)),
                  pl.BlockSpec(memory_space=pl.ANY)],
        out_specs=pl.BlockSpec(memory_space=pl.ANY),
        out_shape=jax.ShapeDtypeStruct(xs.shape, xs.dtype),
        scratch_shapes=[pltpu.SemaphoreType.DMA],
        input_output_aliases={2: 0},
        compiler_params=_cparams("arbitrary"),
        name="moe_dispatch",
    )(dest.reshape(t // tg, 1, tg * TOP_K), h, xs)


def _expert_kernel(be_ref, na_ref, ns_ref, xs_ref, w1_ref, w3_ref, w2_ref, y_ref, h_ref, acc_ref):
    del be_ref, na_ref
    b = pl.program_id(0)
    j = pl.program_id(1)
    n_sub = ns_ref[b]

    @pl.when((n_sub > 0) & (j == 0))
    def _():
        h_ref[...] = xs_ref[...].astype(BF16)
        acc_ref[...] = jnp.zeros_like(acc_ref)

    for k in range(1, MOE_BLOCK // MOE_SUB + 1):
        @pl.when(n_sub == k)
        def _(rows=k * MOE_SUB):
            h = h_ref[0:rows, :]
            w1 = w1_ref[...].astype(BF16)
            w3 = w3_ref[...].astype(BF16)
            gate = (_silu(_dot(h, w1)) * _dot(h, w3)).astype(BF16)
            acc_ref[0:rows, :] += _dot(gate, w2_ref[...].astype(BF16))

    @pl.when(j == pl.num_programs(1) - 1)
    def _():
        y_ref[...] = jnp.where(n_sub > 0, acc_ref[...], 0.0)


def _experts(xs, block_expert, n_active, n_sub, w1, w3, w2):
    n_rows, d = xs.shape
    ff = w1.shape[2]
    tf = _ff_tile(ff, 512)
    n_blocks = n_rows // MOE_BLOCK
    nj = ff // tf
    def wcol(b, j, be, na, ns):
        live = b < na[0]
        return (be[b], 0, jnp.where(live, j, nj - 1))

    def wrow(b, j, be, na, ns):
        live = b < na[0]
        return (be[b], jnp.where(live, j, nj - 1), 0)

    def xrow(b, j, be, na, ns):
        return (jnp.minimum(b, na[0] - 1), 0)

    grid_spec = pltpu.PrefetchScalarGridSpec(
        num_scalar_prefetch=3,
        grid=(n_blocks, nj),
        in_specs=[pl.BlockSpec((MOE_BLOCK, d), xrow),
                  pl.BlockSpec((None, d, tf), wcol), pl.BlockSpec((None, d, tf), wcol),
                  pl.BlockSpec((None, tf, d), wrow)],
        out_specs=pl.BlockSpec((MOE_BLOCK, d), lambda b, j, be, na, ns: (b, 0)),
        scratch_shapes=[pltpu.VMEM((MOE_BLOCK, d), BF16), pltpu.VMEM((MOE_BLOCK, d), F32)],
    )
    return pl.pallas_call(
        _expert_kernel,
        grid_spec=grid_spec,
        out_shape=jax.ShapeDtypeStruct((n_rows, d), F32),
        compiler_params=_cparams("arbitrary", "arbitrary"),
        name="moe_experts",
    )(block_expert, n_active, n_sub, xs, w1, w3, w2)


def _combine_kernel(dest_ref, x_ref, gate_ref, y_ref, out_ref, buf, sem):
    rows = x_ref.shape[0]

    def issue(g, carry):
        for s in range(SUBLANES):
            r = g * SUBLANES + s
            for k in range(TOP_K):
                d = dest_ref[0, 0, r * TOP_K + k]
                pltpu.make_async_copy(y_ref.at[pl.ds(d, 1), :], buf.at[k, pl.ds(r, 1), :], sem).start(priority=k)
        return carry

    lax.fori_loop(0, rows // SUBLANES, issue, 0)
    for k in range(TOP_K):
        pltpu.make_async_copy(y_ref.at[pl.ds(0, rows), :], buf.at[k], sem).wait()
    gates = gate_ref[...]
    y = buf[0] * gates[:, 0:1]
    for k in range(1, TOP_K):
        y = y + buf[k] * gates[:, k:k + 1]
    out_ref[...] = x_ref[...] + y


def _combine(x, gates, dest, y):
    t, d = x.shape
    tg = DISPATCH_ROWS
    return pl.pallas_call(
        _combine_kernel,
        grid=(t // tg,),
        in_specs=[pl.BlockSpec((1, 1, tg * TOP_K), lambda i: (i, 0, 0), memory_space=pltpu.SMEM),
                  pl.BlockSpec((tg, d), lambda i: (i, 0)),
                  pl.BlockSpec((tg, TOP_K), lambda i: (i, 0)),
                  pl.BlockSpec(memory_space=pl.ANY)],
        out_specs=pl.BlockSpec((tg, d), lambda i: (i, 0)),
        out_shape=jax.ShapeDtypeStruct((t, d), F32),
        scratch_shapes=[pltpu.VMEM((TOP_K, tg, d), F32), pltpu.SemaphoreType.DMA],
        compiler_params=_cparams("arbitrary"),
        name="moe_combine",
    )(dest.reshape(t // tg, 1, tg * TOP_K), x, gates, y)


def _route(logits):
    t = logits.shape[0]
    top_val, top_idx = lax.top_k(logits, TOP_K)
    gates = jax.nn.softmax(top_val, axis=-1)
    expert = top_idx.reshape(-1)
    one_hot = (expert[:, None] == jnp.arange(N_EXPERTS, dtype=expert.dtype)[None, :]).astype(jnp.int32)
    running = jnp.cumsum(one_hot, axis=0)
    rank = jnp.sum(running * one_hot, axis=1) - 1
    counts = running[-1]
    padded = (counts + MOE_BLOCK - 1) // MOE_BLOCK * MOE_BLOCK
    pad_ends = jnp.cumsum(padded)
    pad_starts = pad_ends - padded
    dest = (jnp.sum(pad_starts[None, :] * one_hot, axis=1) + rank).astype(jnp.int32)
    n_blocks = -(-(t * TOP_K) // MOE_BLOCK) + N_EXPERTS
    block_start = jnp.arange(n_blocks, dtype=jnp.int32) * MOE_BLOCK
    block_expert = jnp.sum((block_start[:, None] >= pad_ends[None, :]).astype(jnp.int32), axis=1)
    block_expert = jnp.minimum(block_expert, N_EXPERTS - 1).astype(jnp.int32)
    used_end = (pad_starts + counts)[block_expert]
    used_rows = jnp.clip(used_end - block_start, 0, MOE_BLOCK)
    used_rows = jnp.where(block_start < pad_ends[-1], used_rows, 0)
    n_sub = ((used_rows + MOE_SUB - 1) // MOE_SUB).astype(jnp.int32)
    n_active = (pad_ends[-1:] // MOE_BLOCK).astype(jnp.int32)
    return gates, dest.reshape(t, TOP_K), block_expert, n_sub, n_active, n_blocks * MOE_BLOCK


def _moe(groups, w1, w3, w2):
    sizes = [x.shape[0] for x, _, _ in groups]
    gates, dest, block_expert, n_sub, n_active, n_rows = _route(jnp.concatenate([l for _, _, l in groups]))
    xs = jnp.zeros((n_rows, w1.shape[1]), F32)
    start = 0
    for (_, h2, _), size in zip(groups, sizes):
        xs = _dispatch(h2, dest[start:start + size].reshape(-1), xs)
        start += size
    y = _experts(xs, block_expert, n_active, n_sub, w1, w3, w2)
    outs, start = [], 0
    for (x, _, _), size in zip(groups, sizes):
        outs.append(_combine(x, gates[start:start + size], dest[start:start + size].reshape(-1), y))
        start += size
    return outs


def _rope_tables(pos):
    inv_freq = ROPE_THETA ** (-jnp.arange(HALF_DIM, dtype=F32) / HALF_DIM)
    ang = pos.astype(F32)[:, None] * inv_freq[None, :]
    cos, sin = jnp.cos(ang), jnp.sin(ang)
    reps = LANES // HEAD_DIM
    return jnp.tile(cos, (1, 2 * reps)), jnp.tile(jnp.concatenate([-sin, sin], axis=1), (1, reps))


def _qk_gain_row(qk_gain):
    ones_a = jnp.ones((AW,), F32)
    ones_b = jnp.ones((KVW,), F32)
    return jnp.concatenate([jnp.tile(qk_gain[0], N_HEADS), jnp.tile(qk_gain[1], N_HEADS), ones_a,
                            jnp.tile(qk_gain[2], N_HEADS), jnp.tile(qk_gain[3], B_KV_HEADS), ones_b])[None, :]


def _row_block(t):
    return 512 if t % 512 == 0 else t


def kernel(x_prompt, x_sample, cache_a_k, cache_a_v, cache_b_k, cache_b_v, norm_mix, norm_ffn, ab_w_in,
           ab_qk_gain, ab_sinks, ab_w_out, ffn_w1, ffn_w3, ffn_w2, c_w_in, c_ln_g, c_ln_b, c_w_s, c_b_s,
           c_w_out, moe_router, moe_w1, moe_w3, moe_w2):
    n_p, s_p, d = x_prompt.shape
    n_s, t_new, _ = x_sample.shape
    depth = norm_mix.shape[0]
    past_len = 8192
    xp = x_prompt.reshape(n_p * s_p, d)
    xs = x_sample.reshape(n_s * t_new, d)
    pos_p = jnp.tile(jnp.arange(s_p), n_p)
    pos_s = jnp.tile(past_len + jnp.arange(t_new), n_s)
    cos_p, sin_p = _rope_tables(pos_p)
    cos_s, sin_s = _rope_tables(pos_s)
    row = lambda a: a[None, :]
    akp, avp, bkp, bvp, aks, avs, bks, bvs, cvs = [], [], [], [], [], [], [], [], []
    la_p, lb_p = min(BAND * A_DILATIONS[-1], s_p), min(BAND, s_p)
    for layer in range(depth):
        i = layer // 2
        if layer % 2 == 0:
            w_in = ab_w_in[i].astype(BF16)
            gain_row = _qk_gain_row(ab_qk_gain[i])
            w_out = ab_w_out[i].astype(BF16)
            w1, w3, w2 = ffn_w1[i].astype(BF16), ffn_w3[i].astype(BF16), ffn_w2[i].astype(BF16)
            sinks = ab_sinks[i].astype(F32)
            sink_rows = jnp.broadcast_to(sinks[:, None], (N_HEADS, LANES))
            sink_lane_row = jnp.pad(sinks, (0, LANES - N_HEADS))[None, :]
            qa, ka, va, qb, kb, vb, ka16, va16, kvb16, *dilated = _qkv_proj(
                xp, row(norm_mix[layer]), w_in, gain_row, cos_p, sin_p,
                tm=_row_block(s_p), q_dtype=BF16, seq_len=s_p)
            seq = lambda a: a.reshape(n_p, s_p, a.shape[-1])
            qkv_by_branch = [[seq(a)[:, None] for a in (qa, ka16, va16)]] + [dilated[3 * i:3 * i + 3]
                                                                            for i in range(len(A_DILATIONS) - 1)]
            branches = [_attn_a_branch(q, k, v) for q, k, v in qkv_by_branch]
            ob = _attn_b(seq(qb), seq(kvb16), sink_rows).reshape(n_p * s_p, AW)
            xp = _mix_out_ffn(xp, branches, ob, w_out, row(norm_ffn[layer]), w1, w3, w2, tm=_row_block(s_p))
            tail = lambda a, rows, nh: seq(a)[:, s_p - rows:].reshape(n_p, rows, nh, HEAD_DIM)
            akp.append(tail(ka, la_p, N_HEADS))
            avp.append(tail(va, la_p, N_HEADS))
            bkp.append(tail(kb, lb_p, B_KV_HEADS))
            bvp.append(tail(vb, lb_p, B_KV_HEADS))
            qa, ka, va, qb, kb, vb = _qkv_proj(xs, row(norm_mix[layer]), w_in, gain_row, cos_s, sin_s,
                                               tm=_row_block(xs.shape[0]), q_dtype=F32)
            flat = lambda c: c.reshape(c.shape[0], c.shape[1], c.shape[2] * c.shape[3])
            oa, ob = _sample_attn(qa, ka, va, qb, kb, vb, cache_a_k[i], cache_a_v[i],
                                  flat(cache_b_k[i]), flat(cache_b_v[i]), sink_lane_row, t_new=t_new)
            xs = _mix_out_ffn(xs, oa, ob, w_out, row(norm_ffn[layer]), w1, w3, w2, tm=_row_block(xs.shape[0]))
            heads_s = lambda a, nh: a.reshape(n_s, t_new, nh, HEAD_DIM)
            aks.append(heads_s(ka, N_HEADS))
            avs.append(heads_s(va, N_HEADS))
            bks.append(heads_s(kb, B_KV_HEADS))
            bvs.append(heads_s(vb, B_KV_HEADS))
        else:
            w_in = c_w_in[i].astype(BF16)
            w_out = c_w_out[i].astype(BF16)
            tril = jnp.tril(jnp.ones((CHUNK, CHUNK), F32))
            mix_p = (c_w_s[i] * tril).astype(BF16)
            gw = w_out.shape[0] // C_GROUPS
            bias_p = jnp.repeat(c_b_s[i].T, gw, axis=1)
            per_tile = CHUNK // t_new
            mix_s = jnp.einsum("ab,gij->gaibj", jnp.eye(per_tile, dtype=F32), (c_w_s[i] * tril)[:, :t_new, :t_new])
            mix_s = mix_s.reshape(C_GROUPS, CHUNK, CHUNK).astype(BF16)
            bias_s = jnp.tile(bias_p[:t_new], (per_tile, 1))
            wr = moe_router[i]
            wr_hi = wr.astype(BF16)
            wr_lo = (wr - wr_hi.astype(F32)).astype(BF16)
            pad_e = lambda a: jnp.pad(a, ((0, 0), (0, LANES - N_EXPERTS)))
            wr_hi_lo = jnp.concatenate([pad_e(wr_hi), pad_e(wr_lo)], axis=1)
            common = (row(c_ln_g[i]), row(c_ln_b[i]))
            xp1, hp, lp = _gmlp_block(xp, row(norm_mix[layer]), w_in, *common, mix_p, bias_p, w_out,
                                      row(norm_ffn[layer]), wr_hi_lo, tm=_row_block(xp.shape[0]), write_v=False)
            xs1, hs, ls, v_new = _gmlp_block(xs, row(norm_mix[layer]), w_in, *common, mix_s, bias_s, w_out,
                                             row(norm_ffn[layer]), wr_hi_lo, tm=_row_block(xs.shape[0]),
                                             write_v=True)
            cvs.append(v_new.reshape(n_s, t_new, v_new.shape[-1]))
            xp, xs = _moe([(xp1, hp, lp[:, :N_EXPERTS]), (xs1, hs, ls[:, :N_EXPERTS])],
                          moe_w1[i], moe_w3[i], moe_w2[i])
    return (xp.reshape(n_p, s_p, d), xs.reshape(n_s, t_new, d),
            jnp.stack(akp), jnp.stack(avp), jnp.stack(bkp), jnp.stack(bvp),
            jnp.stack(aks), jnp.stack(avs), jnp.stack(bks), jnp.stack(bvs),
            jnp.stack(cvs))
```

```python
import functools

import jax
import jax.numpy as jnp
from jax import lax
from jax.experimental import pallas as pl
from jax.experimental.pallas import tpu as pltpu

F32 = jnp.float32
BF16 = jnp.bfloat16

HEAD_DIM = 64
HALF_DIM = HEAD_DIM // 2
N_HEADS = 8
B_KV_HEADS = 2
A_DILATIONS = (1, 4, 16)
BAND = 128
ROPE_THETA = 10000.0
NORM_EPS = 1e-6
NEG_INF = -1e30
CHUNK = 128
C_GROUPS = 8
N_EXPERTS = 8
TOP_K = 2

LANES = 128
SUBLANES = 8
VMEM_LIMIT_BYTES = 56 * 1024 * 1024

AW = N_HEADS * HEAD_DIM
KVW = B_KV_HEADS * HEAD_DIM
HEADS_PER_VREG = LANES // HEAD_DIM
N_HEAD_BLOCKS = AW // LANES


def _cparams(*sem):
    return pltpu.CompilerParams(dimension_semantics=sem, vmem_limit_bytes=VMEM_LIMIT_BYTES)


def _rms_norm(x, g):
    return x * lax.rsqrt(jnp.mean(x * x, axis=-1, keepdims=True) + NORM_EPS) * g


def _dot(a, b):
    return jnp.dot(a, b, preferred_element_type=F32)


def _dot_nt(a, b):
    return lax.dot_general(a, b, (((1,), (1,)), ((), ())), preferred_element_type=F32)


_QA_BLOCKS = range(0, 4)
_KA_BLOCKS = range(4, 8)
_VA_BLOCKS = range(8, 12)
_QB_BLOCKS = range(12, 16)
_KB_BLOCK = 16
_VB_BLOCK = 17


def _proj_kernel(x_ref, g_ref, w_ref, gain_ref, cos_ref, sin_ref,
                 qa_ref, ka_ref, va_ref, qb_ref, kb_ref, vb_ref, *prompt_refs):
    stage = prompt_refs[-1] if prompt_refs else None
    h = _rms_norm(x_ref[...], g_ref[...]).astype(BF16)
    p = _dot(h, w_ref[...])
    tm = p.shape[0]
    cos = cos_ref[...]
    sin = sin_ref[...]
    lane = lax.broadcasted_iota(jnp.int32, (tm, LANES), 1)
    first_half = (lane & HALF_DIM) == 0
    li = lax.broadcasted_iota(jnp.int32, (LANES, LANES), 0) // HEAD_DIM
    lj = lax.broadcasted_iota(jnp.int32, (LANES, LANES), 1) // HEAD_DIM
    head_mean = jnp.where(li == lj, 1.0 / HEAD_DIM, 0.0).astype(BF16)

    def norm_rope(blk):
        pc = p[:, blk * LANES:(blk + 1) * LANES]
        ms = _dot((pc * pc).astype(BF16), head_mean)
        y = pc * lax.rsqrt(ms + NORM_EPS) * gain_ref[:, blk * LANES:(blk + 1) * LANES]
        partner = jnp.where(first_half, pltpu.roll(y, LANES - HALF_DIM, 1), pltpu.roll(y, HALF_DIM, 1))
        return y * cos + partner * sin

    scale = HEAD_DIM ** -0.5
    for j, blk in enumerate(_QA_BLOCKS):
        q = norm_rope(blk) * scale
        qa_ref[:, j * LANES:(j + 1) * LANES] = q.astype(qa_ref.dtype)
        if stage is not None:
            stage[j] = q
    for j, blk in enumerate(_KA_BLOCKS):
        k = norm_rope(blk)
        ka_ref[:, j * LANES:(j + 1) * LANES] = k
        if stage is not None:
            stage[N_HEAD_BLOCKS + j] = k
    for j, blk in enumerate(_VA_BLOCKS):
        v = p[:, blk * LANES:(blk + 1) * LANES]
        va_ref[:, j * LANES:(j + 1) * LANES] = v
        if stage is not None:
            stage[2 * N_HEAD_BLOCKS + j] = v
    for j, blk in enumerate(_QB_BLOCKS):
        qb_ref[:, j * LANES:(j + 1) * LANES] = (norm_rope(blk) * scale).astype(qb_ref.dtype)
    kb_ref[...] = norm_rope(_KB_BLOCK)
    vb_ref[...] = p[:, _VB_BLOCK * LANES:(_VB_BLOCK + 1) * LANES]
    if prompt_refs:
        ka16_ref, va16_ref, kvb16_ref = prompt_refs[:3]
        ka16_ref[...] = ka_ref[...].astype(BF16)
        va16_ref[...] = va_ref[...].astype(BF16)
        kb = kb_ref[...]
        vb = vb_ref[...]
        parts = (kb, pltpu.roll(kb, HEAD_DIM, 1), vb, pltpu.roll(vb, HEAD_DIM, 1))
        for j, part in enumerate(parts):
            kvb16_ref[:, j * KVW:(j + 1) * KVW] = part.astype(BF16)
        for idx, r in enumerate(A_DILATIONS[1:]):
            for which, dst in enumerate(prompt_refs[3 + 3 * idx:6 + 3 * idx]):
                for c in range(r):
                    for cb in range(N_HEAD_BLOCKS):
                        rows = stage[which * N_HEAD_BLOCKS + cb, pl.ds(c, tm // r, stride=r), :]
                        dst[c, :, cb * LANES:(cb + 1) * LANES] = rows.astype(BF16)


def _qkv_proj(x, g, w, gain_row, cos, sin, *, tm, q_dtype, seq_len=None):
    t, d = x.shape
    n_out = w.shape[1]
    row = lambda i: (i, 0)
    fixed = lambda i: (0, 0)
    sds = jax.ShapeDtypeStruct
    out_specs = [pl.BlockSpec((tm, wd), row) for wd in (AW, AW, AW, AW, KVW, KVW)]
    out_shape = [sds((t, wd), dt) for wd, dt in zip((AW, AW, AW, AW, KVW, KVW), (q_dtype, F32, F32, q_dtype, F32, F32))]
    scratch = []
    if seq_len is not None:
        out_specs += [pl.BlockSpec((tm, wd), row) for wd in (AW, AW, 4 * KVW)]
        out_shape += [sds((t, wd), BF16) for wd in (AW, AW, 4 * KVW)]
        tiles = seq_len // tm
        for r in A_DILATIONS[1:]:
            assert seq_len % tm == 0 and tm % (r * 16) == 0
            out_specs += [pl.BlockSpec((None, r, tm // r, AW), lambda i: (i // tiles, 0, i % tiles, 0))] * 3
            out_shape += [sds((t // seq_len, r, seq_len // r, AW), BF16)] * 3
        scratch = [pltpu.VMEM((3 * N_HEAD_BLOCKS, tm, LANES), F32)]
    return pl.pallas_call(
        _proj_kernel,
        grid=(t // tm,),
        in_specs=[pl.BlockSpec((tm, d), row), pl.BlockSpec((1, d), fixed), pl.BlockSpec((d, n_out), fixed),
                  pl.BlockSpec((1, n_out), fixed), pl.BlockSpec((tm, LANES), row), pl.BlockSpec((tm, LANES), row)],
        out_specs=out_specs, out_shape=out_shape, scratch_shapes=scratch,
        compiler_params=_cparams("arbitrary"),
        name="qkv_proj",
    )(x, g, w, gain_row, cos, sin)


def _fill_kv(kbuf, vbuf, kp_ref, kc_ref, vp_ref, vc_ref):
    kbuf[0:BAND, :] = kp_ref[...].astype(BF16)
    kbuf[BAND:, :] = kc_ref[...].astype(BF16)
    vbuf[0:BAND, :] = vp_ref[...].astype(BF16)
    vbuf[BAND:, :] = vc_ref[...].astype(BF16)


def _band_masks(rows, first_block):
    assert BAND == LANES
    row = lax.broadcasted_iota(jnp.int32, (rows, BAND), 0) & (BAND - 1)
    col = lax.broadcasted_iota(jnp.int32, (rows, BAND), 1)
    shifted = row + jnp.where(first_block, 2 * BAND, 0)
    return col > shifted, col == shifted, col <= row


def _banded_softmax_pv(qm, kp, kc, vp, vc, masks, floor_row=None):
    upper, diag, lower = masks
    s_prev = _dot_nt(qm, kp)
    s = jnp.where(upper, s_prev, jnp.where(lower, _dot_nt(qm, kc), NEG_INF))
    back = jnp.where(diag, s_prev, NEG_INF if floor_row is None else floor_row)
    m = jnp.max(jnp.maximum(s, back), axis=-1, keepdims=True)
    p = jnp.exp(s - m)
    p_back = jnp.where(diag, jnp.exp(s_prev - m), 0.0)
    l = jnp.sum(p + p_back, axis=-1, keepdims=True)
    p_prev = jnp.where(upper, p, p_back).astype(BF16)
    p_own = jnp.where(lower, p, 0.0).astype(BF16)
    return m, l, _dot(p_prev, vp) + _dot(p_own, vc)


def _attn_a_kernel(q_ref, kc_ref, kp_ref, vc_ref, vp_ref, o_ref, lse_ref, kbuf, vbuf, *, tq):
    blk = pl.program_id(2)
    _fill_kv(kbuf, vbuf, kp_ref, kc_ref, vp_ref, vc_ref)
    lane = lax.broadcasted_iota(jnp.int32, (BAND, LANES), 1)
    low_head = lane < HEAD_DIM

    def sub_block(sb, carry):
        r0 = pl.multiple_of(sb * BAND, BAND)
        masks = _band_masks(HEADS_PER_VREG * BAND, (blk + sb) == 0)
        lse_tile = jnp.zeros((BAND, LANES), F32)
        for hb in range(N_HEAD_BLOCKS):
            cs = slice(hb * LANES, (hb + 1) * LANES)
            q2 = q_ref[pl.ds(r0, BAND), cs]
            zero = jnp.zeros_like(q2)
            qm = jnp.concatenate([jnp.where(low_head, q2, zero), jnp.where(low_head, zero, q2)], axis=0)
            m, l, pv = _banded_softmax_pv(qm, kbuf[pl.ds(r0, BAND), cs], kbuf[pl.ds(r0 + BAND, BAND), cs],
                                          vbuf[pl.ds(r0, BAND), cs], vbuf[pl.ds(r0 + BAND, BAND), cs], masks)
            o = pv * (1.0 / l)
            o_ref[pl.ds(r0, BAND), cs] = jnp.where(low_head, o[:BAND], o[BAND:]).astype(o_ref.dtype)
            lse = m + jnp.log(l)
            for hh in range(HEADS_PER_VREG):
                lse_tile = jnp.where(lane == hb * HEADS_PER_VREG + hh, lse[hh * BAND:(hh + 1) * BAND], lse_tile)
        lse_ref[pl.ds(r0, BAND), :] = lse_tile
        return carry

    lax.fori_loop(0, tq // BAND, sub_block, 0)


def _attn_a_branch(q, k, v):
    n, r, sub_len, w = q.shape
    tq = min(512, sub_len)
    cur = lambda b, c, i: (b, c, i, 0)
    prv = lambda b, c, i: (b, c, jnp.maximum(i * (tq // BAND) - 1, 0), 0)
    blk_cur = pl.BlockSpec((None, None, tq, w), cur)
    blk_prv = pl.BlockSpec((None, None, BAND, w), prv)
    sds = jax.ShapeDtypeStruct
    return pl.pallas_call(
        functools.partial(_attn_a_kernel, tq=tq),
        grid=(n, r, sub_len // tq),
        in_specs=[blk_cur, blk_cur, blk_prv, blk_cur, blk_prv],
        out_specs=[blk_cur, pl.BlockSpec((None, None, tq, LANES), cur)],
        out_shape=[sds((n, r, sub_len, w), BF16), sds((n, r, sub_len, LANES), F32)],
        scratch_shapes=[pltpu.VMEM((tq + BAND, w), BF16), pltpu.VMEM((tq + BAND, w), BF16)],
        compiler_params=_cparams("arbitrary", "arbitrary", "arbitrary"),
        name=f"attn_a_dil{r}",
    )(q, k, k, v, v)


_KVB_K, _KVB_K_SWAPPED, _KVB_V, _KVB_V_SWAPPED = range(4)
_KVB_WIDTH = 4 * KVW


def _attn_b_kernel(q_ref, kvc_ref, kvp_ref, sink_ref, o_ref, kvbuf, *, tq):
    blk = pl.program_id(1)
    kvbuf[0:BAND, :] = kvp_ref[...]
    kvbuf[BAND:, :] = kvc_ref[...]
    lane = lax.broadcasted_iota(jnp.int32, (BAND, LANES), 1)
    low_head = lane < HEAD_DIM
    group = N_HEADS // B_KV_HEADS
    plain = [h for h in range(N_HEADS) if h % HEADS_PER_VREG == h // group]
    swapped = [h for h in range(N_HEADS) if h % HEADS_PER_VREG != h // group]

    def sub_block(sb, carry):
        r0 = pl.multiple_of(sb * BAND, BAND)
        masks = _band_masks(len(plain) * BAND, (blk + sb) == 0)
        blocks = [q_ref[pl.ds(r0, BAND), hb * LANES:(hb + 1) * LANES] for hb in range(N_HEAD_BLOCKS)]
        outs = [None] * N_HEADS
        for heads, k_blk, v_blk in ((plain, _KVB_K, _KVB_V), (swapped, _KVB_K_SWAPPED, _KVB_V_SWAPPED)):
            ks = slice(k_blk * KVW, (k_blk + 1) * KVW)
            vs = slice(v_blk * KVW, (v_blk + 1) * KVW)
            q_parts, sink_parts = [], []
            for h in heads:
                q2 = blocks[h // HEADS_PER_VREG]
                own_half = low_head if h % HEADS_PER_VREG == 0 else jnp.logical_not(low_head)
                q_parts.append(jnp.where(own_half, q2, jnp.zeros_like(q2)))
                sink_parts.append(jnp.broadcast_to(sink_ref[h:h + 1, :], (BAND, LANES)))
            sink = jnp.concatenate(sink_parts, axis=0)
            m, l, pv = _banded_softmax_pv(jnp.concatenate(q_parts, axis=0),
                                          kvbuf[pl.ds(r0, BAND), ks], kvbuf[pl.ds(r0 + BAND, BAND), ks],
                                          kvbuf[pl.ds(r0, BAND), vs], kvbuf[pl.ds(r0 + BAND, BAND), vs],
                                          masks, floor_row=sink)
            o = pv * (1.0 / (l + jnp.exp(sink - m)))
            for idx, h in enumerate(heads):
                outs[h] = o[idx * BAND:(idx + 1) * BAND]
        for hb in range(N_HEAD_BLOCKS):
            o_ref[pl.ds(r0, BAND), hb * LANES:(hb + 1) * LANES] = jnp.where(
                low_head, outs[hb * HEADS_PER_VREG], outs[hb * HEADS_PER_VREG + 1]).astype(o_ref.dtype)
        return carry

    lax.fori_loop(0, tq // BAND, sub_block, 0)


def _attn_b(q, kv, sinks_rows):
    n, s, w = q.shape
    tq = min(512, s)
    cur = lambda b, i: (b, i, 0)
    prv = lambda b, i: (b, jnp.maximum(i * (tq // BAND) - 1, 0), 0)
    return pl.pallas_call(
        functools.partial(_attn_b_kernel, tq=tq),
        grid=(n, s // tq),
        in_specs=[pl.BlockSpec((None, tq, w), cur), pl.BlockSpec((None, tq, _KVB_WIDTH), cur),
                  pl.BlockSpec((None, BAND, _KVB_WIDTH), prv),
                  pl.BlockSpec((N_HEADS, LANES), lambda b, i: (0, 0))],
        out_specs=pl.BlockSpec((None, tq, w), cur),
        out_shape=jax.ShapeDtypeStruct((n, s, w), BF16),
        scratch_shapes=[pltpu.VMEM((tq + BAND, _KVB_WIDTH), BF16)],
        compiler_params=_cparams("arbitrary", "arbitrary"),
        name="attn_b",
    )(q, kv, kv, sinks_rows)


NEW_ROWS = 8


def _branch_multiplicity(t_query, key_pos, la):
    back = la + t_query - key_pos
    mult = jnp.zeros(back.shape, F32)
    for r in A_DILATIONS:
        mult = mult + ((back >= 0) & (back <= BAND * r) & (back % r == 0)).astype(F32)
    return mult


def _sample_attn_kernel(qa_ref, kna_ref, vna_ref, qb_ref, knb_ref, vnb_ref,
                        kt_ref, vt_ref, cbk_ref, cbv_ref, sink_ref,
                        oa_ref, ob_ref, *, nb, t_new):
    la = kt_ref.shape[-1]
    t_c = lax.broadcasted_iota(jnp.int32, (NEW_ROWS, la), 0)
    t_c = jnp.where(t_c < t_new, t_c, 0)
    mult_c = _branch_multiplicity(t_c, lax.broadcasted_iota(jnp.int32, (NEW_ROWS, la), 1), la)
    t_n = lax.broadcasted_iota(jnp.int32, (NEW_ROWS, NEW_ROWS), 0)
    t_n = jnp.where(t_n < t_new, t_n, 0)
    key_n = lax.broadcasted_iota(jnp.int32, (NEW_ROWS, NEW_ROWS), 1)
    mult_n = jnp.where(key_n < t_new, _branch_multiplicity(t_n, la + key_n, la), 0.0)

    def head_attn_a(q_h, kt, vt, kn_h, vn_h):
        s_c = jnp.where(mult_c > 0.0, _dot(q_h, kt), NEG_INF)
        s_n = jnp.where(mult_n > 0.0, _dot_nt(q_h, kn_h), NEG_INF)
        m = jnp.maximum(jnp.max(s_c, axis=-1, keepdims=True), jnp.max(s_n, axis=-1, keepdims=True))
        p_c = jnp.exp(s_c - m) * mult_c
        p_n = jnp.exp(s_n - m) * mult_n
        l = jnp.sum(p_c, axis=-1, keepdims=True) + jnp.sum(p_n, axis=-1, keepdims=True)
        return (_dot_nt(p_c.astype(BF16), vt) + _dot(p_n.astype(BF16), vn_h)) / l

    lane_w = lax.broadcasted_iota(jnp.int32, (AW, LANES), 0) // HEAD_DIM
    head_w = lax.broadcasted_iota(jnp.int32, (AW, LANES), 1)
    head_sum = (lane_w == head_w).astype(BF16)
    head_e = lax.broadcasted_iota(jnp.int32, (LANES, AW), 0)
    lane_e = lax.broadcasted_iota(jnp.int32, (LANES, AW), 1) // HEAD_DIM
    head_expand = (head_e == lane_e).astype(BF16)
    row_c = lax.broadcasted_iota(jnp.int32, (BAND, 1), 0)
    row_n = lax.broadcasted_iota(jnp.int32, (NEW_ROWS, 1), 0)
    lane = lax.broadcasted_iota(jnp.int32, (BAND, LANES), 1)
    low_head = lane < HEAD_DIM
    low_head_n = low_head[:NEW_ROWS]
    sink_row = sink_ref[...]

    def scores(k, q_row):
        return _dot((k * q_row).astype(BF16), head_sum)

    def attend(q_row, blocks, sink=None):
        ss = []
        m = None
        for k, _, valid, _ in blocks:
            s = scores(k, q_row)
            if valid is not None:
                s = jnp.where(valid, s, NEG_INF)
            ss.append(s)
            bm = jnp.max(s, axis=0, keepdims=True)
            m = bm if m is None else jnp.maximum(m, bm)
        if sink is not None:
            m = jnp.maximum(m, sink)
        ps = []
        l = jnp.exp(sink - m) if sink is not None else 0.0
        for s, (_, _, valid, mult) in zip(ss, blocks):
            p = jnp.exp(s - m)
            if mult is not None:
                p = p * mult
            ps.append(p)
            l = l + jnp.sum(p, axis=0, keepdims=True)
        inv_l = 1.0 / l
        acc = None
        for p, (_, v, _, _) in zip(ps, blocks):
            pe = _dot((p * inv_l).astype(BF16), head_expand)
            part = jnp.sum(pe * v, axis=0, keepdims=True)
            acc = part if acc is None else acc + part
        return acc

    def expand_kv(x, low):
        sw = pltpu.roll(x, HEAD_DIM, 1)
        kv0 = jnp.where(low, x, sw)
        kv1 = jnp.where(low, sw, x)
        return jnp.concatenate([kv0, kv0, kv1, kv1], axis=1)

    def one_sequence(n, carry):
        new0 = pl.multiple_of(n * NEW_ROWS, NEW_ROWS)
        qa = qa_ref[pl.ds(new0, NEW_ROWS), :].astype(BF16)
        kna = kna_ref[pl.ds(new0, NEW_ROWS), :].astype(BF16)
        vna = vna_ref[pl.ds(new0, NEW_ROWS), :].astype(BF16)
        heads_out = []
        for h in range(N_HEADS):
            hs = slice(h * HEAD_DIM, (h + 1) * HEAD_DIM)
            heads_out.append(head_attn_a(qa[:, hs], kt_ref[n, h].astype(BF16), vt_ref[n, h].astype(BF16),
                                         kna[:, hs], vna[:, hs]))
        oa_ref[pl.ds(new0, NEW_ROWS), :] = jnp.concatenate(heads_out, axis=1).astype(oa_ref.dtype)
        kb_c = expand_kv(cbk_ref[n], low_head)
        vb_c = expand_kv(cbv_ref[n], low_head)
        kb_n = expand_kv(knb_ref[pl.ds(new0, NEW_ROWS), :], low_head_n)
        vb_n = expand_kv(vnb_ref[pl.ds(new0, NEW_ROWS), :], low_head_n)
        qb = qb_ref[pl.ds(new0, NEW_ROWS), :]
        out_b = jnp.zeros((NEW_ROWS, AW), F32)
        for t in range(t_new):
            acc_b = attend(qb[t:t + 1, :], [
                (kb_c, vb_c, row_c >= t, None),
                (kb_n, vb_n, row_n <= t, None),
            ], sink=sink_row)
            out_b = jnp.where(row_n == t, acc_b, out_b)
        ob_ref[pl.ds(new0, NEW_ROWS), :] = out_b.astype(ob_ref.dtype)
        return carry

    lax.fori_loop(0, nb, one_sequence, 0)


def _pad_new_rows(a, n_seq, t_new):
    w = a.shape[-1]
    a = a.reshape(n_seq, t_new, w)
    a = jnp.pad(a, ((0, 0), (0, NEW_ROWS - t_new), (0, 0)))
    return a.reshape(n_seq * NEW_ROWS, w)


def _sample_attn(qa, ka, va, qb, kb, vb, cache_a_k, cache_a_v, cache_b_k, cache_b_v, sink_row, *, t_new):
    n_seq, la, nh, hd = cache_a_k.shape
    w = nh * hd
    assert t_new <= NEW_ROWS
    nb = 2 if n_seq % 2 == 0 else 1
    padded = [_pad_new_rows(a, n_seq, t_new) for a in (qa, ka, va, qb, kb, vb)]
    dim_major = lambda c: jnp.transpose(c, (0, 2, 3, 1))
    new_a = pl.BlockSpec((nb * NEW_ROWS, w), lambda i: (i, 0))
    new_b = pl.BlockSpec((nb * NEW_ROWS, KVW), lambda i: (i, 0))
    spec_ca = pl.BlockSpec((nb, nh, hd, la), lambda i: (i, 0, 0, 0))
    spec_cb = pl.BlockSpec((nb, BAND, KVW), lambda i: (i, 0, 0))
    sds = jax.ShapeDtypeStruct
    oa, ob = pl.pallas_call(
        functools.partial(_sample_attn_kernel, nb=nb, t_new=t_new),
        grid=(n_seq // nb,),
        in_specs=[new_a, new_a, new_a, new_a, new_b, new_b, spec_ca, spec_ca, spec_cb, spec_cb,
                  pl.BlockSpec((1, LANES), lambda i: (0, 0))],
        out_specs=[new_a, new_a],
        out_shape=[sds((n_seq * NEW_ROWS, w), BF16), sds((n_seq * NEW_ROWS, w), BF16)],
        compiler_params=_cparams("arbitrary"),
        name="sample_attn",
    )(*padded, dim_major(cache_a_k), dim_major(cache_a_v), cache_b_k, cache_b_v, sink_row)
    unpad = lambda o: o.reshape(n_seq, NEW_ROWS, w)[:, :t_new].reshape(n_seq * t_new, w)
    return unpad(oa), unpad(ob)


def _silu(a):
    return a * (1.0 / (1.0 + jnp.exp(-a)))


def _natural_order(o_ref, lse_ref, o_stage, lse_stage):
    r, sub, _ = o_ref.shape
    if r == 1:
        return o_ref[0].astype(F32), lse_ref[0]
    for c in range(r):
        lse_stage[pl.ds(c, sub, stride=r), :] = lse_ref[c]
        for cb in range(N_HEAD_BLOCKS):
            o_stage[cb, pl.ds(c, sub, stride=r), :] = o_ref[c, :, cb * LANES:(cb + 1) * LANES].astype(F32)
    return jnp.concatenate([o_stage[cb] for cb in range(N_HEAD_BLOCKS)], axis=1), lse_stage[...]


def _merge_branches(outs, lses):
    top = functools.reduce(jnp.maximum, lses)
    ws = [jnp.exp(l - top) for l in lses]
    inv_den = 1.0 / functools.reduce(lambda a, b: a + b, ws)
    head = lax.broadcasted_iota(jnp.int32, (LANES, AW), 0)
    lane_head = lax.broadcasted_iota(jnp.int32, (LANES, AW), 1) // HEAD_DIM
    expand = (head == lane_head).astype(BF16)
    oa = None
    for w, o in zip(ws, outs):
        w = w * inv_den
        w_hi = w.astype(BF16)
        w_lo = (w - w_hi.astype(F32)).astype(BF16)
        part = (_dot(w_hi, expand) + _dot(w_lo, expand)) * o
        oa = part if oa is None else oa + part
    return oa.astype(BF16)


def _mix_out_ffn_kernel(x_ref, *refs, n_branches):
    if n_branches:
        branch_refs, refs = refs[:2 * n_branches], refs[2 * n_branches:]
        refs, (o_stage, lse_stage) = refs[:-2], refs[-2:]
    else:
        oa_ref, refs = refs[0], refs[1:]
    ob_ref, wo_ref, g_ref, w1_ref, w3_ref, w2_ref, out_ref, x1_ref, h_ref, acc_ref = refs
    j = pl.program_id(1)

    @pl.when(j == 0)
    def _():
        if n_branches:
            outs, lses = zip(*[_natural_order(branch_refs[2 * i], branch_refs[2 * i + 1], o_stage, lse_stage)
                               for i in range(n_branches)])
            oa = _merge_branches(outs, lses)
        else:
            oa = oa_ref[...]
        x1 = x_ref[...] + _dot(oa, wo_ref[0:AW, :]) + _dot(ob_ref[...], wo_ref[AW:, :])
        x1_ref[...] = x1
        h_ref[...] = _rms_norm(x1, g_ref[...]).astype(BF16)
        acc_ref[...] = jnp.zeros_like(acc_ref)

    h = h_ref[...]
    gate = (_silu(_dot(h, w1_ref[...])) * _dot(h, w3_ref[...])).astype(BF16)
    acc_ref[...] += _dot(gate, w2_ref[...])

    @pl.when(j == pl.num_programs(1) - 1)
    def _():
        out_ref[...] = x1_ref[...] + acc_ref[...]


def _ff_tile(ff, target):
    best = LANES
    for k in range(1, ff // LANES + 1):
        if ff % (k * LANES) == 0 and k * LANES <= target:
            best = k * LANES
    return best


def _mix_out_ffn(x, oa, ob, wo, g, w1, w3, w2, *, tm):
    t, d = x.shape
    ff = w1.shape[1]
    tf = ff
    row = lambda i, j: (i, 0)
    fixed = lambda i, j: (0, 0)
    once = pl.Buffered(1)
    if isinstance(oa, (list, tuple)):
        n_branches = len(oa)
        tiles = t // oa[0][0].shape[0] // tm
        a_specs, a_args = [], []
        for o, lse in oa:
            r = o.shape[1]
            assert tm % (r * SUBLANES) == 0
            tile = lambda i, j: (i // tiles, 0, i % tiles, 0)
            a_specs += [pl.BlockSpec((None, r, tm // r, AW), tile), pl.BlockSpec((None, r, tm // r, LANES), tile)]
            a_args += [o, lse]
        stage = [pltpu.VMEM((N_HEAD_BLOCKS, tm, LANES), F32), pltpu.VMEM((tm, LANES), F32)]
    else:
        n_branches, a_specs, a_args, stage = 0, [pl.BlockSpec((tm, AW), row)], [oa], []
    return pl.pallas_call(
        functools.partial(_mix_out_ffn_kernel, n_branches=n_branches),
        grid=(t // tm, ff // tf),
        in_specs=[pl.BlockSpec((tm, d), row)] + a_specs
                 + [pl.BlockSpec((tm, AW), row),
                  pl.BlockSpec((2 * AW, d), fixed, pipeline_mode=once), pl.BlockSpec((1, d), fixed),
                  pl.BlockSpec((d, tf), lambda i, j: (0, j), pipeline_mode=once),
                  pl.BlockSpec((d, tf), lambda i, j: (0, j), pipeline_mode=once),
                  pl.BlockSpec((tf, d), lambda i, j: (j, 0), pipeline_mode=once)],
        out_specs=pl.BlockSpec((tm, d), row),
        out_shape=jax.ShapeDtypeStruct((t, d), F32),
        scratch_shapes=[pltpu.VMEM((tm, d), F32), pltpu.VMEM((tm, d), BF16), pltpu.VMEM((tm, d), F32)] + stage,
        compiler_params=_cparams("arbitrary", "arbitrary"),
        name="mix_out_ffn",
    )(x, *a_args, ob, wo, g, w1, w3, w2)


def _gelu_exact(z):
    return 0.5 * z * (1.0 + lax.erf(z * (2.0 ** -0.5)))


def _gmlp_kernel(x_ref, g_ref, win_ref, lng_ref, lnb_ref, mix_ref, bias_ref, wout_ref, gffn_ref, wr_ref,
                 *out_refs, write_v):
    if write_v:
        xo_ref, h2_ref, logit_ref, v_ref = out_refs[:4]
    else:
        xo_ref, h2_ref, logit_ref = out_refs[:3]
    gate_ref = out_refs[-1]
    x = x_ref[...]
    tm, cd = x.shape
    h = _rms_norm(x, g_ref[...]).astype(BF16)
    z = _gelu_exact(_dot(h, win_ref[...]))
    u = z[:, :cd]
    v = z[:, cd:]
    mu = jnp.mean(v, axis=-1, keepdims=True)
    vc = v - mu
    v = vc * lax.rsqrt(jnp.mean(vc * vc, axis=-1, keepdims=True) + NORM_EPS) * lng_ref[...] + lnb_ref[...]
    if write_v:
        v_ref[...] = v
    vb = v.astype(BF16)
    bias = bias_ref[...]
    gw = cd // C_GROUPS
    for c in range(tm // CHUNK):
        rs = slice(c * CHUNK, (c + 1) * CHUNK)
        for g in range(C_GROUPS):
            cs = slice(g * gw, (g + 1) * gw)
            f = _dot(mix_ref[g], vb[rs, cs]) + bias[:, cs]
            gate_ref[rs, cs] = (u[rs, cs] * f).astype(BF16)
    xo = x + _dot(gate_ref[...], wout_ref[...])
    xo_ref[...] = xo
    h2 = _rms_norm(xo, gffn_ref[...])
    h2_ref[...] = h2
    h_hi = h2.astype(BF16)
    h_lo = (h2 - h_hi.astype(F32)).astype(BF16)
    wr = wr_ref[...]
    both = _dot(h_hi, wr)
    logit_ref[...] = both[:, :LANES] + both[:, LANES:] + _dot(h_lo, wr[:, :LANES])


def _gmlp_block(x, g, w_in, ln_g, ln_b, mix, bias_full, w_out, g_ffn, wr_hi_lo, *, tm, write_v):
    t, d = x.shape
    cd = w_out.shape[0]
    row = lambda i: (i, 0)
    fixed = lambda i: (0, 0)
    sds = jax.ShapeDtypeStruct
    out_specs = [pl.BlockSpec((tm, d), row), pl.BlockSpec((tm, d), row), pl.BlockSpec((tm, LANES), row)]
    out_shape = [sds((t, d), F32), sds((t, d), F32), sds((t, LANES), F32)]
    if write_v:
        out_specs.append(pl.BlockSpec((tm, cd), row))
        out_shape.append(sds((t, cd), F32))
    return pl.pallas_call(
        functools.partial(_gmlp_kernel, write_v=write_v),
        grid=(t // tm,),
        in_specs=[pl.BlockSpec((tm, d), row), pl.BlockSpec((1, d), fixed), pl.BlockSpec((d, 2 * cd), fixed),
                  pl.BlockSpec((1, cd), fixed), pl.BlockSpec((1, cd), fixed),
                  pl.BlockSpec((C_GROUPS, CHUNK, CHUNK), lambda i: (0, 0, 0)),
                  pl.BlockSpec((CHUNK, cd), fixed), pl.BlockSpec((cd, d), fixed), pl.BlockSpec((1, d), fixed),
                  pl.BlockSpec((d, 2 * LANES), fixed)],
        out_specs=out_specs, out_shape=out_shape,
        scratch_shapes=[pltpu.VMEM((tm, cd), BF16)],
        compiler_params=_cparams("arbitrary"),
        name="gmlp_block",
    )(x, g, w_in, ln_g, ln_b, mix, bias_full, w_out, g_ffn, wr_hi_lo)


MOE_BLOCK = 1024
MOE_SUB = 256
DISPATCH_ROWS = 512


def _dispatch_kernel(dest_ref, h_ref, xs_in_ref, xs_ref, sem):
    del xs_in_ref
    rows = h_ref.shape[0]

    def issue(g, carry):
        for s in range(SUBLANES):
            r = g * SUBLANES + s
            for k in range(TOP_K):
                d = dest_ref[0, 0, r * TOP_K + k]
                pltpu.make_async_copy(h_ref.at[pl.ds(r, 1), :], xs_ref.at[pl.ds(d, 1), :], sem).start(priority=k)
        return carry

    lax.fori_loop(0, rows // SUBLANES, issue, 0)
    for _ in range(TOP_K):
        pltpu.make_async_copy(h_ref, xs_ref.at[pl.ds(0, rows), :], sem).wait()


def _dispatch(h, dest, xs):
    t, d = h.shape
    tg = DISPATCH_ROWS
    return pl.pallas_call(
        _dispatch_kernel,
        grid=(t // tg,),
        in_specs=[pl.BlockSpec((1, 1, tg * TOP_K), lambda i: (i, 0, 0), memory_space=pltpu.SMEM),
                  pl.BlockSpec((tg, d), lambda i: (i, 0)),
                  pl.BlockSpec(memory_space=pl.ANY)],
        out_specs=pl.BlockSpec(memory_space=pl.ANY),
        out_shape=jax.ShapeDtypeStruct(xs.shape, xs.dtype),
        scratch_shapes=[pltpu.SemaphoreType.DMA],
        input_output_aliases={2: 0},
        compiler_params=_cparams("arbitrary"),
        name="moe_dispatch",
    )(dest.reshape(t // tg, 1, tg * TOP_K), h, xs)


def _expert_kernel(be_ref, na_ref, ns_ref, xs_ref, w1_ref, w3_ref, w2_ref, y_ref, h_ref, acc_ref):
    del be_ref, na_ref
    b = pl.program_id(0)
    j = pl.program_id(1)
    n_sub = ns_ref[b]

    @pl.when((n_sub > 0) & (j == 0))
    def _():
        h_ref[...] = xs_ref[...].astype(BF16)
        acc_ref[...] = jnp.zeros_like(acc_ref)

    for k in range(1, MOE_BLOCK // MOE_SUB + 1):
        @pl.when(n_sub == k)
        def _(rows=k * MOE_SUB):
            h = h_ref[0:rows, :]
            w1 = w1_ref[...].astype(BF16)
            w3 = w3_ref[...].astype(BF16)
            gate = (_silu(_dot(h, w1)) * _dot(h, w3)).astype(BF16)
            acc_ref[0:rows, :] += _dot(gate, w2_ref[...].astype(BF16))

    @pl.when(j == pl.num_programs(1) - 1)
    def _():
        y_ref[...] = jnp.where(n_sub > 0, acc_ref[...], 0.0)


def _experts(xs, block_expert, n_active, n_sub, w1, w3, w2):
    n_rows, d = xs.shape
    ff = w1.shape[2]
    tf = _ff_tile(ff, 512)
    n_blocks = n_rows // MOE_BLOCK
    nj = ff // tf
    def wcol(b, j, be, na, ns):
        live = b < na[0]
        return (be[b], 0, jnp.where(live, j, nj - 1))

    def wrow(b, j, be, na, ns):
        live = b < na[0]
        return (be[b], jnp.where(live, j, nj - 1), 0)

    def xrow(b, j, be, na, ns):
        return (jnp.minimum(b, na[0] - 1), 0)

    grid_spec = pltpu.PrefetchScalarGridSpec(
        num_scalar_prefetch=3,
        grid=(n_blocks, nj),
        in_specs=[pl.BlockSpec((MOE_BLOCK, d), xrow),
                  pl.BlockSpec((None, d, tf), wcol), pl.BlockSpec((None, d, tf), wcol),
                  pl.BlockSpec((None, tf, d), wrow)],
        out_specs=pl.BlockSpec((MOE_BLOCK, d), lambda b, j, be, na, ns: (b, 0)),
        scratch_shapes=[pltpu.VMEM((MOE_BLOCK, d), BF16), pltpu.VMEM((MOE_BLOCK, d), F32)],
    )
    return pl.pallas_call(
        _expert_kernel,
        grid_spec=grid_spec,
        out_shape=jax.ShapeDtypeStruct((n_rows, d), F32),
        compiler_params=_cparams("arbitrary", "arbitrary"),
        name="moe_experts",
    )(block_expert, n_active, n_sub, xs, w1, w3, w2)


def _combine_kernel(dest_ref, x_ref, gate_ref, y_ref, out_ref, buf, sem):
    rows = x_ref.shape[0]

    def issue(g, carry):
        for s in range(SUBLANES):
            r = g * SUBLANES + s
            for k in range(TOP_K):
                d = dest_ref[0, 0, r * TOP_K + k]
                pltpu.make_async_copy(y_ref.at[pl.ds(d, 1), :], buf.at[k, pl.ds(r, 1), :], sem).start(priority=k)
        return carry

    lax.fori_loop(0, rows // SUBLANES, issue, 0)
    for k in range(TOP_K):
        pltpu.make_async_copy(y_ref.at[pl.ds(0, rows), :], buf.at[k], sem).wait()
    gates = gate_ref[...]
    y = buf[0] * gates[:, 0:1]
    for k in range(1, TOP_K):
        y = y + buf[k] * gates[:, k:k + 1]
    out_ref[...] = x_ref[...] + y


def _combine(x, gates, dest, y):
    t, d = x.shape
    tg = DISPATCH_ROWS
    return pl.pallas_call(
        _combine_kernel,
        grid=(t // tg,),
        in_specs=[pl.BlockSpec((1, 1, tg * TOP_K), lambda i: (i, 0, 0), memory_space=pltpu.SMEM),
                  pl.BlockSpec((tg, d), lambda i: (i, 0)),
                  pl.BlockSpec((tg, TOP_K), lambda i: (i, 0)),
                  pl.BlockSpec(memory_space=pl.ANY)],
        out_specs=pl.BlockSpec((tg, d), lambda i: (i, 0)),
        out_shape=jax.ShapeDtypeStruct((t, d), F32),
        scratch_shapes=[pltpu.VMEM((TOP_K, tg, d), F32), pltpu.SemaphoreType.DMA],
        compiler_params=_cparams("arbitrary"),
        name="moe_combine",
    )(dest.reshape(t // tg, 1, tg * TOP_K), x, gates, y)


def _route(logits):
    t = logits.shape[0]
    top_val, top_idx = lax.top_k(logits, TOP_K)
    gates = jax.nn.softmax(top_val, axis=-1)
    expert = top_idx.reshape(-1)
    one_hot = (expert[:, None] == jnp.arange(N_EXPERTS, dtype=expert.dtype)[None, :]).astype(jnp.int32)
    running = jnp.cumsum(one_hot, axis=0)
    rank = jnp.sum(running * one_hot, axis=1) - 1
    counts = running[-1]
    padded = (counts + MOE_BLOCK - 1) // MOE_BLOCK * MOE_BLOCK
    pad_ends = jnp.cumsum(padded)
    pad_starts = pad_ends - padded
    dest = (jnp.sum(pad_starts[None, :] * one_hot, axis=1) + rank).astype(jnp.int32)
    n_blocks = -(-(t * TOP_K) // MOE_BLOCK) + N_EXPERTS
    block_start = jnp.arange(n_blocks, dtype=jnp.int32) * MOE_BLOCK
    block_expert = jnp.sum((block_start[:, None] >= pad_ends[None, :]).astype(jnp.int32), axis=1)
    block_expert = jnp.minimum(block_expert, N_EXPERTS - 1).astype(jnp.int32)
    used_end = (pad_starts + counts)[block_expert]
    used_rows = jnp.clip(used_end - block_start, 0, MOE_BLOCK)
    used_rows = jnp.where(block_start < pad_ends[-1], used_rows, 0)
    n_sub = ((used_rows + MOE_SUB - 1) // MOE_SUB).astype(jnp.int32)
    n_active = (pad_ends[-1:] // MOE_BLOCK).astype(jnp.int32)
    return gates, dest.reshape(t, TOP_K), block_expert, n_sub, n_active, n_blocks * MOE_BLOCK


def _moe(groups, w1, w3, w2):
    sizes = [x.shape[0] for x, _, _ in groups]
    gates, dest, block_expert, n_sub, n_active, n_rows = _route(jnp.concatenate([l for _, _, l in groups]))
    xs = jnp.zeros((n_rows, w1.shape[1]), F32)
    start = 0
    for (_, h2, _), size in zip(groups, sizes):
        xs = _dispatch(h2, dest[start:start + size].reshape(-1), xs)
        start += size
    y = _experts(xs, block_expert, n_active, n_sub, w1, w3, w2)
    outs, start = [], 0
    for (x, _, _), size in zip(groups, sizes):
        outs.append(_combine(x, gates[start:start + size], dest[start:start + size].reshape(-1), y))
        start += size
    return outs


def _rope_tables(pos):
    inv_freq = ROPE_THETA ** (-jnp.arange(HALF_DIM, dtype=F32) / HALF_DIM)
    ang = pos.astype(F32)[:, None] * inv_freq[None, :]
    cos, sin = jnp.cos(ang), jnp.sin(ang)
    reps = LANES // HEAD_DIM
    return jnp.tile(cos, (1, 2 * reps)), jnp.tile(jnp.concatenate([-sin, sin], axis=1), (1, reps))


def _qk_gain_row(qk_gain):
    ones_a = jnp.ones((AW,), F32)
    ones_b = jnp.ones((KVW,), F32)
    return jnp.concatenate([jnp.tile(qk_gain[0], N_HEADS), jnp.tile(qk_gain[1], N_HEADS), ones_a,
                            jnp.tile(qk_gain[2], N_HEADS), jnp.tile(qk_gain[3], B_KV_HEADS), ones_b])[None, :]


def _row_block(t):
    return 512 if t % 512 == 0 else t


def kernel(x_prompt, x_sample, cache_a_k, cache_a_v, cache_b_k, cache_b_v, norm_mix, norm_ffn, ab_w_in,
           ab_qk_gain, ab_sinks, ab_w_out, ffn_w1, ffn_w3, ffn_w2, c_w_in, c_ln_g, c_ln_b, c_w_s, c_b_s,
           c_w_out, moe_router, moe_w1, moe_w3, moe_w2):
    n_p, s_p, d = x_prompt.shape
    n_s, t_new, _ = x_sample.shape
    depth = norm_mix.shape[0]
    past_len = 8192
    xp = x_prompt.reshape(n_p * s_p, d)
    xs = x_sample.reshape(n_s * t_new, d)
    pos_p = jnp.tile(jnp.arange(s_p), n_p)
    pos_s = jnp.tile(past_len + jnp.arange(t_new), n_s)
    cos_p, sin_p = _rope_tables(pos_p)
    cos_s, sin_s = _rope_tables(pos_s)
    row = lambda a: a[None, :]
    akp, avp, bkp, bvp, aks, avs, bks, bvs, cvs = [], [], [], [], [], [], [], [], []
    la_p, lb_p = min(BAND * A_DILATIONS[-1], s_p), min(BAND, s_p)
    for layer in range(depth):
        i = layer // 2
        if layer % 2 == 0:
            w_in = ab_w_in[i].astype(BF16)
            gain_row = _qk_gain_row(ab_qk_gain[i])
            w_out = ab_w_out[i].astype(BF16)
            w1, w3, w2 = ffn_w1[i].astype(BF16), ffn_w3[i].astype(BF16), ffn_w2[i].astype(BF16)
            sinks = ab_sinks[i].astype(F32)
            sink_rows = jnp.broadcast_to(sinks[:, None], (N_HEADS, LANES))
            sink_lane_row = jnp.pad(sinks, (0, LANES - N_HEADS))[None, :]
            qa, ka, va, qb, kb, vb, ka16, va16, kvb16, *dilated = _qkv_proj(
                xp, row(norm_mix[layer]), w_in, gain_row, cos_p, sin_p,
                tm=_row_block(s_p), q_dtype=BF16, seq_len=s_p)
            seq = lambda a: a.reshape(n_p, s_p, a.shape[-1])
            qkv_by_branch = [[seq(a)[:, None] for a in (qa, ka16, va16)]] + [dilated[3 * i:3 * i + 3]
                                                                            for i in range(len(A_DILATIONS) - 1)]
            branches = [_attn_a_branch(q, k, v) for q, k, v in qkv_by_branch]
            ob = _attn_b(seq(qb), seq(kvb16), sink_rows).reshape(n_p * s_p, AW)
            xp = _mix_out_ffn(xp, branches, ob, w_out, row(norm_ffn[layer]), w1, w3, w2, tm=_row_block(s_p))
            tail = lambda a, rows, nh: seq(a)[:, s_p - rows:].reshape(n_p, rows, nh, HEAD_DIM)
            akp.append(tail(ka, la_p, N_HEADS))
            avp.append(tail(va, la_p, N_HEADS))
            bkp.append(tail(kb, lb_p, B_KV_HEADS))
            bvp.append(tail(vb, lb_p, B_KV_HEADS))
            qa, ka, va, qb, kb, vb = _qkv_proj(xs, row(norm_mix[layer]), w_in, gain_row, cos_s, sin_s,
                                               tm=_row_block(xs.shape[0]), q_dtype=F32)
            flat = lambda c: c.reshape(c.shape[0], c.shape[1], c.shape[2] * c.shape[3])
            oa, ob = _sample_attn(qa, ka, va, qb, kb, vb, cache_a_k[i], cache_a_v[i],
                                  flat(cache_b_k[i]), flat(cache_b_v[i]), sink_lane_row, t_new=t_new)
            xs = _mix_out_ffn(xs, oa, ob, w_out, row(norm_ffn[layer]), w1, w3, w2, tm=_row_block(xs.shape[0]))
            heads_s = lambda a, nh: a.reshape(n_s, t_new, nh, HEAD_DIM)
            aks.append(heads_s(ka, N_HEADS))
            avs.append(heads_s(va, N_HEADS))
            bks.append(heads_s(kb, B_KV_HEADS))
            bvs.append(heads_s(vb, B_KV_HEADS))
        else:
            w_in = c_w_in[i].astype(BF16)
            w_out = c_w_out[i].astype(BF16)
            tril = jnp.tril(jnp.ones((CHUNK, CHUNK), F32))
            mix_p = (c_w_s[i] * tril).astype(BF16)
            gw = w_out.shape[0] // C_GROUPS
            bias_p = jnp.repeat(c_b_s[i].T, gw, axis=1)
            per_tile = CHUNK // t_new
            mix_s = jnp.einsum("ab,gij->gaibj", jnp.eye(per_tile, dtype=F32), (c_w_s[i] * tril)[:, :t_new, :t_new])
            mix_s = mix_s.reshape(C_GROUPS, CHUNK, CHUNK).astype(BF16)
            bias_s = jnp.tile(bias_p[:t_new], (per_tile, 1))
            wr = moe_router[i]
            wr_hi = wr.astype(BF16)
            wr_lo = (wr - wr_hi.astype(F32)).astype(BF16)
            pad_e = lambda a: jnp.pad(a, ((0, 0), (0, LANES - N_EXPERTS)))
            wr_hi_lo = jnp.concatenate([pad_e(wr_hi), pad_e(wr_lo)], axis=1)
            common = (row(c_ln_g[i]), row(c_ln_b[i]))
            xp1, hp, lp = _gmlp_block(xp, row(norm_mix[layer]), w_in, *common, mix_p, bias_p, w_out,
                                      row(norm_ffn[layer]), wr_hi_lo, tm=_row_block(xp.shape[0]), write_v=False)
            xs1, hs, ls, v_new = _gmlp_block(xs, row(norm_mix[layer]), w_in, *common, mix_s, bias_s, w_out,
                                             row(norm_ffn[layer]), wr_hi_lo, tm=_row_block(xs.shape[0]),
                                             write_v=True)
            cvs.append(v_new.reshape(n_s, t_new, v_new.shape[-1]))
            xp, xs = _moe([(xp1, hp, lp[:, :N_EXPERTS]), (xs1, hs, ls[:, :N_EXPERTS])],
                          moe_w1[i], moe_w3[i], moe_w2[i])
    return (xp.reshape(n_p, s_p, d), xs.reshape(n_s, t_new, d),
            jnp.stack(akp), jnp.stack(avp), jnp.stack(bkp), jnp.stack(bvp),
            jnp.stack(aks), jnp.stack(avs), jnp.stack(bks), jnp.stack(bvs),
            jnp.stack(cvs))
```

```python
import functools

import jax
import jax.numpy as jnp
from jax import lax
from jax.experimental import pallas as pl
from jax.experimental.pallas import tpu as pltpu

F32 = jnp.float32
BF16 = jnp.bfloat16

HEAD_DIM = 64
HALF_DIM = HEAD_DIM // 2
N_HEADS = 8
B_KV_HEADS = 2
A_DILATIONS = (1, 4, 16)
BAND = 128
ROPE_THETA = 10000.0
NORM_EPS = 1e-6
NEG_INF = -1e30
CHUNK = 128
C_GROUPS = 8
N_EXPERTS = 8
TOP_K = 2

LANES = 128
SUBLANES = 8
VMEM_LIMIT_BYTES = 56 * 1024 * 1024

AW = N_HEADS * HEAD_DIM
KVW = B_KV_HEADS * HEAD_DIM
HEADS_PER_VREG = LANES // HEAD_DIM
N_HEAD_BLOCKS = AW // LANES


def _cparams(*sem):
    return pltpu.CompilerParams(dimension_semantics=sem, vmem_limit_bytes=VMEM_LIMIT_BYTES)


def _rms_norm(x, g):
    return x * lax.rsqrt(jnp.mean(x * x, axis=-1, keepdims=True) + NORM_EPS) * g


def _dot(a, b):
    return jnp.dot(a, b, preferred_element_type=F32)


def _dot_nt(a, b):
    return lax.dot_general(a, b, (((1,), (1,)), ((), ())), preferred_element_type=F32)


_QA_BLOCKS = range(0, 4)
_KA_BLOCKS = range(4, 8)
_VA_BLOCKS = range(8, 12)
_QB_BLOCKS = range(12, 16)
_KB_BLOCK = 16
_VB_BLOCK = 17


def _proj_kernel(x_ref, g_ref, w_ref, gain_ref, cos_ref, sin_ref,
                 qa_ref, ka_ref, va_ref, qb_ref, kb_ref, vb_ref, *prompt_refs):
    stage = prompt_refs[-1] if prompt_refs else None
    h = _rms_norm(x_ref[...], g_ref[...]).astype(BF16)
    p = _dot(h, w_ref[...])
    tm = p.shape[0]
    cos = cos_ref[...]
    sin = sin_ref[...]
    lane = lax.broadcasted_iota(jnp.int32, (tm, LANES), 1)
    first_half = (lane & HALF_DIM) == 0
    li = lax.broadcasted_iota(jnp.int32, (LANES, LANES), 0) // HEAD_DIM
    lj = lax.broadcasted_iota(jnp.int32, (LANES, LANES), 1) // HEAD_DIM
    head_mean = jnp.where(li == lj, 1.0 / HEAD_DIM, 0.0).astype(BF16)

    def norm_rope(blk):
        pc = p[:, blk * LANES:(blk + 1) * LANES]
        ms = _dot((pc * pc).astype(BF16), head_mean)
        y = pc * lax.rsqrt(ms + NORM_EPS) * gain_ref[:, blk * LANES:(blk + 1) * LANES]
        partner = jnp.where(first_half, pltpu.roll(y, LANES - HALF_DIM, 1), pltpu.roll(y, HALF_DIM, 1))
        return y * cos + partner * sin

    scale = HEAD_DIM ** -0.5
    for j, blk in enumerate(_QA_BLOCKS):
        q = norm_rope(blk) * scale
        qa_ref[:, j * LANES:(j + 1) * LANES] = q.astype(qa_ref.dtype)
        if stage is not None:
            stage[j] = q
    for j, blk in enumerate(_KA_BLOCKS):
        k = norm_rope(blk)
        ka_ref[:, j * LANES:(j + 1) * LANES] = k
        if stage is not None:
            stage[N_HEAD_BLOCKS + j] = k
    for j, blk in enumerate(_VA_BLOCKS):
        v = p[:, blk * LANES:(blk + 1) * LANES]
        va_ref[:, j * LANES:(j + 1) * LANES] = v
        if stage is not None:
            stage[2 * N_HEAD_BLOCKS + j] = v
    for j, blk in enumerate(_QB_BLOCKS):
        qb_ref[:, j * LANES:(j + 1) * LANES] = (norm_rope(blk) * scale).astype(qb_ref.dtype)
    kb_ref[...] = norm_rope(_KB_BLOCK)
    vb_ref[...] = p[:, _VB_BLOCK * LANES:(_VB_BLOCK + 1) * LANES]
    if prompt_refs:
        ka16_ref, va16_ref, kvb16_ref = prompt_refs[:3]
        ka16_ref[...] = ka_ref[...].astype(BF16)
        va16_ref[...] = va_ref[...].astype(BF16)
        kb = kb_ref[...]
        vb = vb_ref[...]
        parts = (kb, pltpu.roll(kb, HEAD_DIM, 1), vb, pltpu.roll(vb, HEAD_DIM, 1))
        for j, part in enumerate(parts):
            kvb16_ref[:, j * KVW:(j + 1) * KVW] = part.astype(BF16)
        for idx, r in enumerate(A_DILATIONS[1:]):
            for which, dst in enumerate(prompt_refs[3 + 3 * idx:6 + 3 * idx]):
                for c in range(r):
                    for cb in range(N_HEAD_BLOCKS):
                        rows = stage[which * N_HEAD_BLOCKS + cb, pl.ds(c, tm // r, stride=r), :]
                        dst[c, :, cb * LANES:(cb + 1) * LANES] = rows.astype(BF16)


def _qkv_proj(x, g, w, gain_row, cos, sin, *, tm, q_dtype, seq_len=None):
    t, d = x.shape
    n_out = w.shape[1]
    row = lambda i: (i, 0)
    fixed = lambda i: (0, 0)
    sds = jax.ShapeDtypeStruct
    out_specs = [pl.BlockSpec((tm, wd), row) for wd in (AW, AW, AW, AW, KVW, KVW)]
    out_shape = [sds((t, wd), dt) for wd, dt in zip((AW, AW, AW, AW, KVW, KVW), (q_dtype, F32, F32, q_dtype, F32, F32))]
    scratch = []
    if seq_len is not None:
        out_specs += [pl.BlockSpec((tm, wd), row) for wd in (AW, AW, 4 * KVW)]
        out_shape += [sds((t, wd), BF16) for wd in (AW, AW, 4 * KVW)]
        tiles = seq_len // tm
        for r in A_DILATIONS[1:]:
            assert seq_len % tm == 0 and tm % (r * 16) == 0
            out_specs += [pl.BlockSpec((None, r, tm // r, AW), lambda i: (i // tiles, 0, i % tiles, 0))] * 3
            out_shape += [sds((t // seq_len, r, seq_len // r, AW), BF16)] * 3
        scratch = [pltpu.VMEM((3 * N_HEAD_BLOCKS, tm, LANES), F32)]
    return pl.pallas_call(
        _proj_kernel,
        grid=(t // tm,),
        in_specs=[pl.BlockSpec((tm, d), row), pl.BlockSpec((1, d), fixed), pl.BlockSpec((d, n_out), fixed),
                  pl.BlockSpec((1, n_out), fixed), pl.BlockSpec((tm, LANES), row), pl.BlockSpec((tm, LANES), row)],
        out_specs=out_specs, out_shape=out_shape, scratch_shapes=scratch,
        compiler_params=_cparams("arbitrary"),
        name="qkv_proj",
    )(x, g, w, gain_row, cos, sin)


def _fill_kv(kbuf, vbuf, kp_ref, kc_ref, vp_ref, vc_ref):
    kbuf[0:BAND, :] = kp_ref[...].astype(BF16)
    kbuf[BAND:, :] = kc_ref[...].astype(BF16)
    vbuf[0:BAND, :] = vp_ref[...].astype(BF16)
    vbuf[BAND:, :] = vc_ref[...].astype(BF16)


def _band_masks(rows, first_block):
    assert BAND == LANES
    row = lax.broadcasted_iota(jnp.int32, (rows, BAND), 0) & (BAND - 1)
    col = lax.broadcasted_iota(jnp.int32, (rows, BAND), 1)
    shifted = row + jnp.where(first_block, 2 * BAND, 0)
    return col > shifted, col == shifted, col <= row


def _banded_softmax_pv(qm, kp, kc, vp, vc, masks, floor_row=None):
    upper, diag, lower = masks
    s_prev = _dot_nt(qm, kp)
    s = jnp.where(upper, s_prev, jnp.where(lower, _dot_nt(qm, kc), NEG_INF))
    back = jnp.where(diag, s_prev, NEG_INF if floor_row is None else floor_row)
    m = jnp.max(jnp.maximum(s, back), axis=-1, keepdims=True)
    p = jnp.exp(s - m)
    p_back = jnp.where(diag, jnp.exp(s_prev - m), 0.0)
    l = jnp.sum(p + p_back, axis=-1, keepdims=True)
    p_prev = jnp.where(upper, p, p_back).astype(BF16)
    p_own = jnp.where(lower, p, 0.0).astype(BF16)
    return m, l, _dot(p_prev, vp) + _dot(p_own, vc)


def _attn_a_kernel(q_ref, kc_ref, kp_ref, vc_ref, vp_ref, o_ref, lse_ref, kbuf, vbuf, *, tq):
    blk = pl.program_id(2)
    _fill_kv(kbuf, vbuf, kp_ref, kc_ref, vp_ref, vc_ref)
    lane = lax.broadcasted_iota(jnp.int32, (BAND, LANES), 1)
    low_head = lane < HEAD_DIM

    def sub_block(sb, carry):
        r0 = pl.multiple_of(sb * BAND, BAND)
        masks = _band_masks(HEADS_PER_VREG * BAND, (blk + sb) == 0)
        lse_tile = jnp.zeros((BAND, LANES), F32)
        for hb in range(N_HEAD_BLOCKS):
            cs = slice(hb * LANES, (hb + 1) * LANES)
            q2 = q_ref[pl.ds(r0, BAND), cs]
            zero = jnp.zeros_like(q2)
            qm = jnp.concatenate([jnp.where(low_head, q2, zero), jnp.where(low_head, zero, q2)], axis=0)
            m, l, pv = _banded_softmax_pv(qm, kbuf[pl.ds(r0, BAND), cs], kbuf[pl.ds(r0 + BAND, BAND), cs],
                                          vbuf[pl.ds(r0, BAND), cs], vbuf[pl.ds(r0 + BAND, BAND), cs], masks)
            o = pv * (1.0 / l)
            o_ref[pl.ds(r0, BAND), cs] = jnp.where(low_head, o[:BAND], o[BAND:]).astype(o_ref.dtype)
            lse = m + jnp.log(l)
            for hh in range(HEADS_PER_VREG):
                lse_tile = jnp.where(lane == hb * HEADS_PER_VREG + hh, lse[hh * BAND:(hh + 1) * BAND], lse_tile)
        lse_ref[pl.ds(r0, BAND), :] = lse_tile
        return carry

    lax.fori_loop(0, tq // BAND, sub_block, 0)


def _attn_a_branch(q, k, v):
    n, r, sub_len, w = q.shape
    tq = min(512, sub_len)
    cur = lambda b, c, i: (b, c, i, 0)
    prv = lambda b, c, i: (b, c, jnp.maximum(i * (tq // BAND) - 1, 0), 0)
    blk_cur = pl.BlockSpec((None, None, tq, w), cur)
    blk_prv = pl.BlockSpec((None, None, BAND, w), prv)
    sds = jax.ShapeDtypeStruct
    return pl.pallas_call(
        functools.partial(_attn_a_kernel, tq=tq),
        grid=(n, r, sub_len // tq),
        in_specs=[blk_cur, blk_cur, blk_prv, blk_cur, blk_prv],
        out_specs=[blk_cur, pl.BlockSpec((None, None, tq, LANES), cur)],
        out_shape=[sds((n, r, sub_len, w), BF16), sds((n, r, sub_len, LANES), F32)],
        scratch_shapes=[pltpu.VMEM((tq + BAND, w), BF16), pltpu.VMEM((tq + BAND, w), BF16)],
        compiler_params=_cparams("arbitrary", "arbitrary", "arbitrary"),
        name=f"attn_a_dil{r}",
    )(q, k, k, v, v)


_KVB_K, _KVB_K_SWAPPED, _KVB_V, _KVB_V_SWAPPED = range(4)
_KVB_WIDTH = 4 * KVW


def _attn_b_kernel(q_ref, kvc_ref, kvp_ref, sink_ref, o_ref, kvbuf, *, tq):
    blk = pl.program_id(1)
    kvbuf[0:BAND, :] = kvp_ref[...]
    kvbuf[BAND:, :] = kvc_ref[...]
    lane = lax.broadcasted_iota(jnp.int32, (BAND, LANES), 1)
    low_head = lane < HEAD_DIM
    group = N_HEADS // B_KV_HEADS
    plain = [h for h in range(N_HEADS) if h % HEADS_PER_VREG == h // group]
    swapped = [h for h in range(N_HEADS) if h % HEADS_PER_VREG != h // group]

    def sub_block(sb, carry):
        r0 = pl.multiple_of(sb * BAND, BAND)
        masks = _band_masks(len(plain) * BAND, (blk + sb) == 0)
        blocks = [q_ref[pl.ds(r0, BAND), hb * LANES:(hb + 1) * LANES] for hb in range(N_HEAD_BLOCKS)]
        outs = [None] * N_HEADS
        for heads, k_blk, v_blk in ((plain, _KVB_K, _KVB_V), (swapped, _KVB_K_SWAPPED, _KVB_V_SWAPPED)):
            ks = slice(k_blk * KVW, (k_blk + 1) * KVW)
            vs = slice(v_blk * KVW, (v_blk + 1) * KVW)
            q_parts, sink_parts = [], []
            for h in heads:
                q2 = blocks[h // HEADS_PER_VREG]
                own_half = low_head if h % HEADS_PER_VREG == 0 else jnp.logical_not(low_head)
                q_parts.append(jnp.where(own_half, q2, jnp.zeros_like(q2)))
                sink_parts.append(jnp.broadcast_to(sink_ref[h:h + 1, :], (BAND, LANES)))
            sink = jnp.concatenate(sink_parts, axis=0)
            m, l, pv = _banded_softmax_pv(jnp.concatenate(q_parts, axis=0),
                                          kvbuf[pl.ds(r0, BAND), ks], kvbuf[pl.ds(r0 + BAND, BAND), ks],
                                          kvbuf[pl.ds(r0, BAND), vs], kvbuf[pl.ds(r0 + BAND, BAND), vs],
                                          masks, floor_row=sink)
            o = pv * (1.0 / (l + jnp.exp(sink - m)))
            for idx, h in enumerate(heads):
                outs[h] = o[idx * BAND:(idx + 1) * BAND]
        for hb in range(N_HEAD_BLOCKS):
            o_ref[pl.ds(r0, BAND), hb * LANES:(hb + 1) * LANES] = jnp.where(
                low_head, outs[hb * HEADS_PER_VREG], outs[hb * HEADS_PER_VREG + 1]).astype(o_ref.dtype)
        return carry

    lax.fori_loop(0, tq // BAND, sub_block, 0)


def _attn_b(q, kv, sinks_rows):
    n, s, w = q.shape
    tq = min(512, s)
    cur = lambda b, i: (b, i, 0)
    prv = lambda b, i: (b, jnp.maximum(i * (tq // BAND) - 1, 0), 0)
    return pl.pallas_call(
        functools.partial(_attn_b_kernel, tq=tq),
        grid=(n, s // tq),
        in_specs=[pl.BlockSpec((None, tq, w), cur), pl.BlockSpec((None, tq, _KVB_WIDTH), cur),
                  pl.BlockSpec((None, BAND, _KVB_WIDTH), prv),
                  pl.BlockSpec((N_HEADS, LANES), lambda b, i: (0, 0))],
        out_specs=pl.BlockSpec((None, tq, w), cur),
        out_shape=jax.ShapeDtypeStruct((n, s, w), BF16),
        scratch_shapes=[pltpu.VMEM((tq + BAND, _KVB_WIDTH), BF16)],
        compiler_params=_cparams("arbitrary", "arbitrary"),
        name="attn_b",
    )(q, kv, kv, sinks_rows)


NEW_ROWS = 8


def _branch_multiplicity(t_query, key_pos, la):
    back = la + t_query - key_pos
    mult = jnp.zeros(back.shape, F32)
    for r in A_DILATIONS:
        mult = mult + ((back >= 0) & (back <= BAND * r) & (back % r == 0)).astype(F32)
    return mult


def _sample_attn_kernel(qa_ref, kna_ref, vna_ref, qb_ref, knb_ref, vnb_ref,
                        kt_ref, vt_ref, cbk_ref, cbv_ref, sink_ref,
                        oa_ref, ob_ref, *, nb, t_new):
    la = kt_ref.shape[-1]
    t_c = lax.broadcasted_iota(jnp.int32, (NEW_ROWS, la), 0)
    t_c = jnp.where(t_c < t_new, t_c, 0)
    mult_c = _branch_multiplicity(t_c, lax.broadcasted_iota(jnp.int32, (NEW_ROWS, la), 1), la)
    t_n = lax.broadcasted_iota(jnp.int32, (NEW_ROWS, NEW_ROWS), 0)
    t_n = jnp.where(t_n < t_new, t_n, 0)
    key_n = lax.broadcasted_iota(jnp.int32, (NEW_ROWS, NEW_ROWS), 1)
    mult_n = jnp.where(key_n < t_new, _branch_multiplicity(t_n, la + key_n, la), 0.0)

    def head_attn_a(q_h, kt, vt, kn_h, vn_h):
        s_c = jnp.where(mult_c > 0.0, _dot(q_h, kt), NEG_INF)
        s_n = jnp.where(mult_n > 0.0, _dot_nt(q_h, kn_h), NEG_INF)
        m = jnp.maximum(jnp.max(s_c, axis=-1, keepdims=True), jnp.max(s_n, axis=-1, keepdims=True))
        p_c = jnp.exp(s_c - m) * mult_c
        p_n = jnp.exp(s_n - m) * mult_n
        l = jnp.sum(p_c, axis=-1, keepdims=True) + jnp.sum(p_n, axis=-1, keepdims=True)
        return (_dot_nt(p_c.astype(BF16), vt) + _dot(p_n.astype(BF16), vn_h)) / l

    lane_w = lax.broadcasted_iota(jnp.int32, (AW, LANES), 0) // HEAD_DIM
    head_w = lax.broadcasted_iota(jnp.int32, (AW, LANES), 1)
    head_sum = (lane_w == head_w).astype(BF16)
    head_e = lax.broadcasted_iota(jnp.int32, (LANES, AW), 0)
    lane_e = lax.broadcasted_iota(jnp.int32, (LANES, AW), 1) // HEAD_DIM
    head_expand = (head_e == lane_e).astype(BF16)
    row_c = lax.broadcasted_iota(jnp.int32, (BAND, 1), 0)
    row_n = lax.broadcasted_iota(jnp.int32, (NEW_ROWS, 1), 0)
    lane = lax.broadcasted_iota(jnp.int32, (BAND, LANES), 1)
    low_head = lane < HEAD_DIM
    low_head_n = low_head[:NEW_ROWS]
    sink_row = sink_ref[...]

    def scores(k, q_row):
        return _dot((k * q_row).astype(BF16), head_sum)

    def attend(q_row, blocks, sink=None):
        ss = []
        m = None
        for k, _, valid, _ in blocks:
            s = scores(k, q_row)
            if valid is not None:
                s = jnp.where(valid, s, NEG_INF)
            ss.append(s)
            bm = jnp.max(s, axis=0, keepdims=True)
            m = bm if m is None else jnp.maximum(m, bm)
        if sink is not None:
            m = jnp.maximum(m, sink)
        ps = []
        l = jnp.exp(sink - m) if sink is not None else 0.0
        for s, (_, _, valid, mult) in zip(ss, blocks):
            p = jnp.exp(s - m)
            if mult is not None:
                p = p * mult
            ps.append(p)
            l = l + jnp.sum(p, axis=0, keepdims=True)
        inv_l = 1.0 / l
        acc = None
        for p, (_, v, _, _) in zip(ps, blocks):
            pe = _dot((p * inv_l).astype(BF16), head_expand)
            part = jnp.sum(pe * v, axis=0, keepdims=True)
            acc = part if acc is None else acc + part
        return acc

    def expand_kv(x, low):
        sw = pltpu.roll(x, HEAD_DIM, 1)
        kv0 = jnp.where(low, x, sw)
        kv1 = jnp.where(low, sw, x)
        return jnp.concatenate([kv0, kv0, kv1, kv1], axis=1)

    def one_sequence(n, carry):
        new0 = pl.multiple_of(n * NEW_ROWS, NEW_ROWS)
        qa = qa_ref[pl.ds(new0, NEW_ROWS), :].astype(BF16)
        kna = kna_ref[pl.ds(new0, NEW_ROWS), :].astype(BF16)
        vna = vna_ref[pl.ds(new0, NEW_ROWS), :].astype(BF16)
        heads_out = []
        for h in range(N_HEADS):
            hs = slice(h * HEAD_DIM, (h + 1) * HEAD_DIM)
            heads_out.append(head_attn_a(qa[:, hs], kt_ref[n, h].astype(BF16), vt_ref[n, h].astype(BF16),
                                         kna[:, hs], vna[:, hs]))
        oa_ref[pl.ds(new0, NEW_ROWS), :] = jnp.concatenate(heads_out, axis=1).astype(oa_ref.dtype)
        kb_c = expand_kv(cbk_ref[n], low_head)
        vb_c = expand_kv(cbv_ref[n], low_head)
        kb_n = expand_kv(knb_ref[pl.ds(new0, NEW_ROWS), :], low_head_n)
        vb_n = expand_kv(vnb_ref[pl.ds(new0, NEW_ROWS), :], low_head_n)
        qb = qb_ref[pl.ds(new0, NEW_ROWS), :]
        out_b = jnp.zeros((NEW_ROWS, AW), F32)
        for t in range(t_new):
            acc_b = attend(qb[t:t + 1, :], [
                (kb_c, vb_c, row_c >= t, None),
                (kb_n, vb_n, row_n <= t, None),
            ], sink=sink_row)
            out_b = jnp.where(row_n == t, acc_b, out_b)
        ob_ref[pl.ds(new0, NEW_ROWS), :] = out_b.astype(ob_ref.dtype)
        return carry

    lax.fori_loop(0, nb, one_sequence, 0)


def _pad_new_rows(a, n_seq, t_new):
    w = a.shape[-1]
    a = a.reshape(n_seq, t_new, w)
    a = jnp.pad(a, ((0, 0), (0, NEW_ROWS - t_new), (0, 0)))
    return a.reshape(n_seq * NEW_ROWS, w)


def _sample_attn(qa, ka, va, qb, kb, vb, cache_a_k, cache_a_v, cache_b_k, cache_b_v, sink_row, *, t_new):
    n_seq, la, nh, hd = cache_a_k.shape
    w = nh * hd
    assert t_new <= NEW_ROWS
    nb = 2 if n_seq % 2 == 0 else 1
    padded = [_pad_new_rows(a, n_seq, t_new) for a in (qa, ka, va, qb, kb, vb)]
    dim_major = lambda c: jnp.transpose(c, (0, 2, 3, 1))
    new_a = pl.BlockSpec((nb * NEW_ROWS, w), lambda i: (i, 0))
    new_b = pl.BlockSpec((nb * NEW_ROWS, KVW), lambda i: (i, 0))
    spec_ca = pl.BlockSpec((nb, nh, hd, la), lambda i: (i, 0, 0, 0))
    spec_cb = pl.BlockSpec((nb, BAND, KVW), lambda i: (i, 0, 0))
    sds = jax.ShapeDtypeStruct
    oa, ob = pl.pallas_call(
        functools.partial(_sample_attn_kernel, nb=nb, t_new=t_new),
        grid=(n_seq // nb,),
        in_specs=[new_a, new_a, new_a, new_a, new_b, new_b, spec_ca, spec_ca, spec_cb, spec_cb,
                  pl.BlockSpec((1, LANES), lambda i: (0, 0))],
        out_specs=[new_a, new_a],
        out_shape=[sds((n_seq * NEW_ROWS, w), BF16), sds((n_seq * NEW_ROWS, w), BF16)],
        compiler_params=_cparams("arbitrary"),
        name="sample_attn",
    )(*padded, dim_major(cache_a_k), dim_major(cache_a_v), cache_b_k, cache_b_v, sink_row)
    unpad = lambda o: o.reshape(n_seq, NEW_ROWS, w)[:, :t_new].reshape(n_seq * t_new, w)
    return unpad(oa), unpad(ob)


def _silu(a):
    return a * (1.0 / (1.0 + jnp.exp(-a)))


def _natural_order(o_ref, lse_ref, o_stage, lse_stage):
    r, sub, _ = o_ref.shape
    if r == 1:
        return o_ref[0].astype(F32), lse_ref[0]
    for c in range(r):
        lse_stage[pl.ds(c, sub, stride=r), :] = lse_ref[c]
        for cb in range(N_HEAD_BLOCKS):
            o_stage[cb, pl.ds(c, sub, stride=r), :] = o_ref[c, :, cb * LANES:(cb + 1) * LANES].astype(F32)
    return jnp.concatenate([o_stage[cb] for cb in range(N_HEAD_BLOCKS)], axis=1), lse_stage[...]


def _merge_branches(outs, lses):
    top = functools.reduce(jnp.maximum, lses)
    ws = [jnp.exp(l - top) for l in lses]
    inv_den = 1.0 / functools.reduce(lambda a, b: a + b, ws)
    head = lax.broadcasted_iota(jnp.int32, (LANES, AW), 0)
    lane_head = lax.broadcasted_iota(jnp.int32, (LANES, AW), 1) // HEAD_DIM
    expand = (head == lane_head).astype(BF16)
    oa = None
    for w, o in zip(ws, outs):
        w = w * inv_den
        w_hi = w.astype(BF16)
        w_lo = (w - w_hi.astype(F32)).astype(BF16)
        part = (_dot(w_hi, expand) + _dot(w_lo, expand)) * o
        oa = part if oa is None else oa + part
    return oa.astype(BF16)


def _mix_out_ffn_kernel(x_ref, *refs, n_branches):
    if n_branches:
        branch_refs, refs = refs[:2 * n_branches], refs[2 * n_branches:]
        refs, (o_stage, lse_stage) = refs[:-2], refs[-2:]
    else:
        oa_ref, refs = refs[0], refs[1:]
    ob_ref, wo_ref, g_ref, w1_ref, w3_ref, w2_ref, out_ref, x1_ref, h_ref, acc_ref = refs
    j = pl.program_id(1)

    @pl.when(j == 0)
    def _():
        if n_branches:
            outs, lses = zip(*[_natural_order(branch_refs[2 * i], branch_refs[2 * i + 1], o_stage, lse_stage)
                               for i in range(n_branches)])
            oa = _merge_branches(outs, lses)
        else:
            oa = oa_ref[...]
        x1 = x_ref[...] + _dot(oa, wo_ref[0:AW, :]) + _dot(ob_ref[...], wo_ref[AW:, :])
        x1_ref[...] = x1
        h_ref[...] = _rms_norm(x1, g_ref[...]).astype(BF16)
        acc_ref[...] = jnp.zeros_like(acc_ref)

    h = h_ref[...]
    gate = (_silu(_dot(h, w1_ref[...])) * _dot(h, w3_ref[...])).astype(BF16)
    acc_ref[...] += _dot(gate, w2_ref[...])

    @pl.when(j == pl.num_programs(1) - 1)
    def _():
        out_ref[...] = x1_ref[...] + acc_ref[...]


def _ff_tile(ff, target):
    best = LANES
    for k in range(1, ff // LANES + 1):
        if ff % (k * LANES) == 0 and k * LANES <= target:
            best = k * LANES
    return best


def _mix_out_ffn(x, oa, ob, wo, g, w1, w3, w2, *, tm):
    t, d = x.shape
    ff = w1.shape[1]
    tf = ff
    row = lambda i, j: (i, 0)
    fixed = lambda i, j: (0, 0)
    once = pl.Buffered(1)
    if isinstance(oa, (list, tuple)):
        n_branches = len(oa)
        tiles = t // oa[0][0].shape[0] // tm
        a_specs, a_args = [], []
        for o, lse in oa:
            r = o.shape[1]
            assert tm % (r * SUBLANES) == 0
            tile = lambda i, j: (i // tiles, 0, i % tiles, 0)
            a_specs += [pl.BlockSpec((None, r, tm // r, AW), tile), pl.BlockSpec((None, r, tm // r, LANES), tile)]
            a_args += [o, lse]
        stage = [pltpu.VMEM((N_HEAD_BLOCKS, tm, LANES), F32), pltpu.VMEM((tm, LANES), F32)]
    else:
        n_branches, a_specs, a_args, stage = 0, [pl.BlockSpec((tm, AW), row)], [oa], []
    return pl.pallas_call(
        functools.partial(_mix_out_ffn_kernel, n_branches=n_branches),
        grid=(t // tm, ff // tf),
        in_specs=[pl.BlockSpec((tm, d), row)] + a_specs
                 + [pl.BlockSpec((tm, AW), row),
                  pl.BlockSpec((2 * AW, d), fixed, pipeline_mode=once), pl.BlockSpec((1, d), fixed),
                  pl.BlockSpec((d, tf), lambda i, j: (0, j), pipeline_mode=once),
                  pl.BlockSpec((d, tf), lambda i, j: (0, j), pipeline_mode=once),
                  pl.BlockSpec((tf, d), lambda i, j: (j, 0), pipeline_mode=once)],
        out_specs=pl.BlockSpec((tm, d), row),
        out_shape=jax.ShapeDtypeStruct((t, d), F32),
        scratch_shapes=[pltpu.VMEM((tm, d), F32), pltpu.VMEM((tm, d), BF16), pltpu.VMEM((tm, d), F32)] + stage,
        compiler_params=_cparams("arbitrary", "arbitrary"),
        name="mix_out_ffn",
    )(x, *a_args, ob, wo, g, w1, w3, w2)


def _gelu_exact(z):
    return 0.5 * z * (1.0 + lax.erf(z * (2.0 ** -0.5)))


def _gmlp_kernel(x_ref, g_ref, win_ref, lng_ref, lnb_ref, mix_ref, bias_ref, wout_ref, gffn_ref, wr_ref,
                 *out_refs, write_v, shared_h):
    if shared_h:
        out_refs = out_refs[1:]
    if write_v:
        xo_ref, h2_ref, logit_ref, v_ref = out_refs[:4]
    else:
        xo_ref, h2_ref, logit_ref = out_refs[:3]
    gate_ref = out_refs[-1]
    x = x_ref[...]
    tm, cd = x.shape
    h = _rms_norm(x, g_ref[...]).astype(BF16)
    z = _gelu_exact(_dot(h, win_ref[...]))
    u = z[:, :cd]
    v = z[:, cd:]
    mu = jnp.mean(v, axis=-1, keepdims=True)
    vc = v - mu
    v = vc * lax.rsqrt(jnp.mean(vc * vc, axis=-1, keepdims=True) + NORM_EPS) * lng_ref[...] + lnb_ref[...]
    if write_v:
        v_ref[...] = v
    vb = v.astype(BF16)
    bias = bias_ref[...]
    gw = cd // C_GROUPS
    for c in range(tm // CHUNK):
        rs = slice(c * CHUNK, (c + 1) * CHUNK)
        for g in range(C_GROUPS):
            cs = slice(g * gw, (g + 1) * gw)
            f = _dot(mix_ref[g], vb[rs, cs]) + bias[:, cs]
            gate_ref[rs, cs] = (u[rs, cs] * f).astype(BF16)
    xo = x + _dot(gate_ref[...], wout_ref[...])
    xo_ref[...] = xo
    h2 = _rms_norm(xo, gffn_ref[...])
    h2_ref[...] = h2
    h_hi = h2.astype(BF16)
    h_lo = (h2 - h_hi.astype(F32)).astype(BF16)
    wr = wr_ref[...]
    both = _dot(h_hi, wr)
    logit_ref[...] = both[:, :LANES] + both[:, LANES:] + _dot(h_lo, wr[:, :LANES])


def _gmlp_block(x, g, w_in, ln_g, ln_b, mix, bias_full, w_out, g_ffn, wr_hi_lo, *, tm, write_v,
                h_rows=None, h_row0=0, h_shared=None):
    t, d = x.shape
    cd = w_out.shape[0]
    h_rows = t if h_rows is None else h_rows
    assert h_row0 % tm == 0
    row = lambda i: (i, 0)
    fixed = lambda i: (0, 0)
    sds = jax.ShapeDtypeStruct
    out_specs = [pl.BlockSpec((tm, d), row), pl.BlockSpec((tm, d), lambda i: (i + h_row0 // tm, 0)),
                 pl.BlockSpec((tm, LANES), row)]
    out_shape = [sds((t, d), F32), sds((h_rows, d), F32), sds((t, LANES), F32)]
    if write_v:
        out_specs.append(pl.BlockSpec((tm, cd), row))
        out_shape.append(sds((t, cd), F32))
    in_specs = [pl.BlockSpec((tm, d), row), pl.BlockSpec((1, d), fixed), pl.BlockSpec((d, 2 * cd), fixed),
                pl.BlockSpec((1, cd), fixed), pl.BlockSpec((1, cd), fixed),
                pl.BlockSpec((C_GROUPS, CHUNK, CHUNK), lambda i: (0, 0, 0)),
                pl.BlockSpec((CHUNK, cd), fixed), pl.BlockSpec((cd, d), fixed), pl.BlockSpec((1, d), fixed),
                pl.BlockSpec((d, 2 * LANES), fixed)]
    args = [x, g, w_in, ln_g, ln_b, mix, bias_full, w_out, g_ffn, wr_hi_lo]
    aliases = {}
    if h_shared is not None:
        aliases = {len(args): 1}
        in_specs.append(pl.BlockSpec(memory_space=pl.ANY))
        args.append(h_shared)
    return pl.pallas_call(
        functools.partial(_gmlp_kernel, write_v=write_v, shared_h=h_shared is not None),
        grid=(t // tm,),
        in_specs=in_specs, out_specs=out_specs, out_shape=out_shape,
        scratch_shapes=[pltpu.VMEM((tm, cd), BF16)],
        input_output_aliases=aliases,
        compiler_params=_cparams("arbitrary"),
        name="gmlp_block",
    )(*args)


MOE_BLOCK = 1024
MOE_SUB = 256
DISPATCH_ROWS = 512


def _gather_chunk(nj):
    return -(-MOE_BLOCK // (nj * SUBLANES)) * SUBLANES


def _expert_kernel(be_ref, na_ref, ns_ref, tok_ref, tok_next_ref, h_hbm, w1_ref, w3_ref, w2_ref, y_ref,
                   xbuf, h_ref, acc_ref, sems, *, chunk):
    del be_ref, na_ref
    b = pl.program_id(0)
    j = pl.program_id(1)
    nb = pl.num_programs(0)
    nj = pl.num_programs(1)
    n_sub = ns_ref[b]
    rows_buf = xbuf.shape[1]

    def start_row(tokens, r, slot):
        tok = tokens[0, 0, jnp.minimum(r, MOE_BLOCK - 1)]
        pltpu.make_async_copy(h_hbm.at[pl.ds(tok, 1), :], xbuf.at[slot, pl.ds(r, 1), :], sems.at[slot]).start()

    def wait_block(slot):
        pltpu.make_async_copy(h_hbm.at[pl.ds(0, rows_buf), :], xbuf.at[slot], sems.at[slot]).wait()

    @pl.when((b == 0) & (j == 0))
    def _():
        def issue(g, carry):
            for s in range(SUBLANES):
                start_row(tok_ref, g * SUBLANES + s, 0)
            return carry
        lax.fori_loop(0, rows_buf // SUBLANES, issue, 0)

    fetched = (b == 0) | (ns_ref[jnp.maximum(b - 1, 0)] > 0)

    @pl.when(fetched & (j == 0))
    def _():
        wait_block(b % 2)

    @pl.when((n_sub > 0) & (j == 0))
    def _():
        h_ref[...] = xbuf[b % 2, 0:MOE_BLOCK, :].astype(BF16)
        acc_ref[...] = jnp.zeros_like(acc_ref)

    for k in range(1, MOE_BLOCK // MOE_SUB + 1):
        @pl.when(n_sub == k)
        def _(rows=k * MOE_SUB):
            h = h_ref[0:rows, :]
            w1 = w1_ref[...].astype(BF16)
            w3 = w3_ref[...].astype(BF16)
            gate = (_silu(_dot(h, w1)) * _dot(h, w3)).astype(BF16)
            acc_ref[0:rows, :] += _dot(gate, w2_ref[...].astype(BF16))
            for s in range(chunk):
                start_row(tok_next_ref, j * chunk + s, (b + 1) % 2)

    @pl.when(j == nj - 1)
    def _():
        y_ref[...] = jnp.where(n_sub > 0, acc_ref[...], 0.0)

    @pl.when((b == nb - 1) & (j == nj - 1) & (n_sub > 0))
    def _():
        wait_block((b + 1) % 2)


def _experts(h, row_token, block_expert, n_active, n_sub, w1, w3, w2):
    d = h.shape[1]
    n_rows = row_token.shape[0]
    ff = w1.shape[2]
    tf = _ff_tile(ff, 512)
    n_blocks = n_rows // MOE_BLOCK
    nj = ff // tf
    chunk = _gather_chunk(nj)
    def wcol(b, j, be, na, ns):
        live = b < na[0]
        return (be[b], 0, jnp.where(live, j, nj - 1))

    def wrow(b, j, be, na, ns):
        live = b < na[0]
        return (be[b], jnp.where(live, j, nj - 1), 0)

    tokens = row_token.reshape(n_blocks, 1, MOE_BLOCK)
    grid_spec = pltpu.PrefetchScalarGridSpec(
        num_scalar_prefetch=3,
        grid=(n_blocks, nj),
        in_specs=[pl.BlockSpec((1, 1, MOE_BLOCK), lambda b, j, be, na, ns: (b, 0, 0), memory_space=pltpu.SMEM),
                  pl.BlockSpec((1, 1, MOE_BLOCK), lambda b, j, be, na, ns: (jnp.minimum(b + 1, n_blocks - 1), 0, 0),
                               memory_space=pltpu.SMEM),
                  pl.BlockSpec(memory_space=pl.ANY),
                  pl.BlockSpec((None, d, tf), wcol), pl.BlockSpec((None, d, tf), wcol),
                  pl.BlockSpec((None, tf, d), wrow)],
        out_specs=pl.BlockSpec((MOE_BLOCK, d), lambda b, j, be, na, ns: (b, 0)),
        scratch_shapes=[pltpu.VMEM((2, nj * chunk, d), F32), pltpu.VMEM((MOE_BLOCK, d), BF16),
                        pltpu.VMEM((MOE_BLOCK, d), F32), pltpu.SemaphoreType.DMA((2,))],
    )
    return pl.pallas_call(
        functools.partial(_expert_kernel, chunk=chunk),
        grid_spec=grid_spec,
        out_shape=jax.ShapeDtypeStruct((n_rows, d), F32),
        compiler_params=_cparams("arbitrary", "arbitrary"),
        name="moe_experts",
    )(block_expert, n_active, n_sub, tokens, tokens, h, w1, w3, w2)


def _combine_kernel(dest_ref, x_ref, gate_ref, y_ref, out_ref, buf, sem):
    rows = x_ref.shape[0]

    def issue(g, carry):
        for s in range(SUBLANES):
            r = g * SUBLANES + s
            for k in range(TOP_K):
                d = dest_ref[0, 0, r * TOP_K + k]
                pltpu.make_async_copy(y_ref.at[pl.ds(d, 1), :], buf.at[k, pl.ds(r, 1), :], sem).start(priority=k)
        return carry

    lax.fori_loop(0, rows // SUBLANES, issue, 0)
    for k in range(TOP_K):
        pltpu.make_async_copy(y_ref.at[pl.ds(0, rows), :], buf.at[k], sem).wait()
    gates = gate_ref[...]
    y = buf[0] * gates[:, 0:1]
    for k in range(1, TOP_K):
        y = y + buf[k] * gates[:, k:k + 1]
    out_ref[...] = x_ref[...] + y


def _combine(x, gates, dest, y):
    t, d = x.shape
    tg = DISPATCH_ROWS
    return pl.pallas_call(
        _combine_kernel,
        grid=(t // tg,),
        in_specs=[pl.BlockSpec((1, 1, tg * TOP_K), lambda i: (i, 0, 0), memory_space=pltpu.SMEM),
                  pl.BlockSpec((tg, d), lambda i: (i, 0)),
                  pl.BlockSpec((tg, TOP_K), lambda i: (i, 0)),
                  pl.BlockSpec(memory_space=pl.ANY)],
        out_specs=pl.BlockSpec((tg, d), lambda i: (i, 0)),
        out_shape=jax.ShapeDtypeStruct((t, d), F32),
        scratch_shapes=[pltpu.VMEM((TOP_K, tg, d), F32), pltpu.SemaphoreType.DMA],
        compiler_params=_cparams("arbitrary"),
        name="moe_combine",
    )(dest.reshape(t // tg, 1, tg * TOP_K), x, gates, y)


def _route(logits):
    t = logits.shape[0]
    top_val, top_idx = lax.top_k(logits, TOP_K)
    gates = jax.nn.softmax(top_val, axis=-1)
    expert = top_idx.reshape(-1)
    one_hot = (expert[:, None] == jnp.arange(N_EXPERTS, dtype=expert.dtype)[None, :]).astype(jnp.int32)
    running = jnp.cumsum(one_hot, axis=0)
    rank = jnp.sum(running * one_hot, axis=1) - 1
    counts = running[-1]
    padded = (counts + MOE_BLOCK - 1) // MOE_BLOCK * MOE_BLOCK
    pad_ends = jnp.cumsum(padded)
    pad_starts = pad_ends - padded
    dest = (jnp.sum(pad_starts[None, :] * one_hot, axis=1) + rank).astype(jnp.int32)
    n_blocks = -(-(t * TOP_K) // MOE_BLOCK) + N_EXPERTS
    block_start = jnp.arange(n_blocks, dtype=jnp.int32) * MOE_BLOCK
    block_expert = jnp.sum((block_start[:, None] >= pad_ends[None, :]).astype(jnp.int32), axis=1)
    block_expert = jnp.minimum(block_expert, N_EXPERTS - 1).astype(jnp.int32)
    used_end = (pad_starts + counts)[block_expert]
    used_rows = jnp.clip(used_end - block_start, 0, MOE_BLOCK)
    used_rows = jnp.where(block_start < pad_ends[-1], used_rows, 0)
    n_sub = ((used_rows + MOE_SUB - 1) // MOE_SUB).astype(jnp.int32)
    n_active = (pad_ends[-1:] // MOE_BLOCK).astype(jnp.int32)
    row_token = jnp.zeros((n_blocks * MOE_BLOCK,), jnp.int32).at[dest].set(
        jnp.arange(t * TOP_K, dtype=jnp.int32) // TOP_K, unique_indices=True)
    return gates, dest.reshape(t, TOP_K), row_token, block_expert, n_sub, n_active


def _moe(xs, h2, logits, w1, w3, w2):
    gates, dest, row_token, block_expert, n_sub, n_active = _route(jnp.concatenate(logits))
    y = _experts(h2, row_token, block_expert, n_active, n_sub, w1, w3, w2)
    outs, start = [], 0
    for x in xs:
        size = x.shape[0]
        outs.append(_combine(x, gates[start:start + size], dest[start:start + size].reshape(-1), y))
        start += size
    return outs


def _rope_tables(pos):
    inv_freq = ROPE_THETA ** (-jnp.arange(HALF_DIM, dtype=F32) / HALF_DIM)
    ang = pos.astype(F32)[:, None] * inv_freq[None, :]
    cos, sin = jnp.cos(ang), jnp.sin(ang)
    reps = LANES // HEAD_DIM
    return jnp.tile(cos, (1, 2 * reps)), jnp.tile(jnp.concatenate([-sin, sin], axis=1), (1, reps))


def _qk_gain_row(qk_gain):
    ones_a = jnp.ones((AW,), F32)
    ones_b = jnp.ones((KVW,), F32)
    return jnp.concatenate([jnp.tile(qk_gain[0], N_HEADS), jnp.tile(qk_gain[1], N_HEADS), ones_a,
                            jnp.tile(qk_gain[2], N_HEADS), jnp.tile(qk_gain[3], B_KV_HEADS), ones_b])[None, :]


def _row_block(t):
    return 512 if t % 512 == 0 else t


def kernel(x_prompt, x_sample, cache_a_k, cache_a_v, cache_b_k, cache_b_v, norm_mix, norm_ffn, ab_w_in,
           ab_qk_gain, ab_sinks, ab_w_out, ffn_w1, ffn_w3, ffn_w2, c_w_in, c_ln_g, c_ln_b, c_w_s, c_b_s,
           c_w_out, moe_router, moe_w1, moe_w3, moe_w2):
    n_p, s_p, d = x_prompt.shape
    n_s, t_new, _ = x_sample.shape
    depth = norm_mix.shape[0]
    past_len = 8192
    xp = x_prompt.reshape(n_p * s_p, d)
    xs = x_sample.reshape(n_s * t_new, d)
    pos_p = jnp.tile(jnp.arange(s_p), n_p)
    pos_s = jnp.tile(past_len + jnp.arange(t_new), n_s)
    cos_p, sin_p = _rope_tables(pos_p)
    cos_s, sin_s = _rope_tables(pos_s)
    row = lambda a: a[None, :]
    akp, avp, bkp, bvp, aks, avs, bks, bvs, cvs = [], [], [], [], [], [], [], [], []
    la_p, lb_p = min(BAND * A_DILATIONS[-1], s_p), min(BAND, s_p)
    for layer in range(depth):
        i = layer // 2
        if layer % 2 == 0:
            w_in = ab_w_in[i].astype(BF16)
            gain_row = _qk_gain_row(ab_qk_gain[i])
            w_out = ab_w_out[i].astype(BF16)
            w1, w3, w2 = ffn_w1[i].astype(BF16), ffn_w3[i].astype(BF16), ffn_w2[i].astype(BF16)
            sinks = ab_sinks[i].astype(F32)
            sink_rows = jnp.broadcast_to(sinks[:, None], (N_HEADS, LANES))
            sink_lane_row = jnp.pad(sinks, (0, LANES - N_HEADS))[None, :]
            qa, ka, va, qb, kb, vb, ka16, va16, kvb16, *dilated = _qkv_proj(
                xp, row(norm_mix[layer]), w_in, gain_row, cos_p, sin_p,
                tm=_row_block(s_p), q_dtype=BF16, seq_len=s_p)
            seq = lambda a: a.reshape(n_p, s_p, a.shape[-1])
            qkv_by_branch = [[seq(a)[:, None] for a in (qa, ka16, va16)]] + [dilated[3 * i:3 * i + 3]
                                                                            for i in range(len(A_DILATIONS) - 1)]
            branches = [_attn_a_branch(q, k, v) for q, k, v in qkv_by_branch]
            ob = _attn_b(seq(qb), seq(kvb16), sink_rows).reshape(n_p * s_p, AW)
            xp = _mix_out_ffn(xp, branches, ob, w_out, row(norm_ffn[layer]), w1, w3, w2, tm=_row_block(s_p))
            tail = lambda a, rows, nh: seq(a)[:, s_p - rows:].reshape(n_p, rows, nh, HEAD_DIM)
            akp.append(tail(ka, la_p, N_HEADS))
            avp.append(tail(va, la_p, N_HEADS))
            bkp.append(tail(kb, lb_p, B_KV_HEADS))
            bvp.append(tail(vb, lb_p, B_KV_HEADS))
            qa, ka, va, qb, kb, vb = _qkv_proj(xs, row(norm_mix[layer]), w_in, gain_row, cos_s, sin_s,
                                               tm=_row_block(xs.shape[0]), q_dtype=F32)
            flat = lambda c: c.reshape(c.shape[0], c.shape[1], c.shape[2] * c.shape[3])
            oa, ob = _sample_attn(qa, ka, va, qb, kb, vb, cache_a_k[i], cache_a_v[i],
                                  flat(cache_b_k[i]), flat(cache_b_v[i]), sink_lane_row, t_new=t_new)
            xs = _mix_out_ffn(xs, oa, ob, w_out, row(norm_ffn[layer]), w1, w3, w2, tm=_row_block(xs.shape[0]))
            heads_s = lambda a, nh: a.reshape(n_s, t_new, nh, HEAD_DIM)
            aks.append(heads_s(ka, N_HEADS))
            avs.append(heads_s(va, N_HEADS))
            bks.append(heads_s(kb, B_KV_HEADS))
            bvs.append(heads_s(vb, B_KV_HEADS))
        else:
            w_in = c_w_in[i].astype(BF16)
            w_out = c_w_out[i].astype(BF16)
            tril = jnp.tril(jnp.ones((CHUNK, CHUNK), F32))
            mix_p = (c_w_s[i] * tril).astype(BF16)
            gw = w_out.shape[0] // C_GROUPS
            bias_p = jnp.repeat(c_b_s[i].T, gw, axis=1)
            per_tile = CHUNK // t_new
            mix_s = jnp.einsum("ab,gij->gaibj", jnp.eye(per_tile, dtype=F32), (c_w_s[i] * tril)[:, :t_new, :t_new])
            mix_s = mix_s.reshape(C_GROUPS, CHUNK, CHUNK).astype(BF16)
            bias_s = jnp.tile(bias_p[:t_new], (per_tile, 1))
            wr = moe_router[i]
            wr_hi = wr.astype(BF16)
            wr_lo = (wr - wr_hi.astype(F32)).astype(BF16)
            pad_e = lambda a: jnp.pad(a, ((0, 0), (0, LANES - N_EXPERTS)))
            wr_hi_lo = jnp.concatenate([pad_e(wr_hi), pad_e(wr_lo)], axis=1)
            common = (row(c_ln_g[i]), row(c_ln_b[i]))
            t_all = xp.shape[0] + xs.shape[0]
            xp1, h_all, lp = _gmlp_block(xp, row(norm_mix[layer]), w_in, *common, mix_p, bias_p, w_out,
                                         row(norm_ffn[layer]), wr_hi_lo, tm=_row_block(xp.shape[0]),
                                         write_v=False, h_rows=t_all, h_shared=jnp.zeros((t_all, d), F32))
            xs1, h_all, ls, v_new = _gmlp_block(xs, row(norm_mix[layer]), w_in, *common, mix_s, bias_s, w_out,
                                                row(norm_ffn[layer]), wr_hi_lo, tm=_row_block(xs.shape[0]),
                                                write_v=True, h_rows=t_all, h_row0=xp.shape[0], h_shared=h_all)
            cvs.append(v_new.reshape(n_s, t_new, v_new.shape[-1]))
            xp, xs = _moe([xp1, xs1], h_all, [lp[:, :N_EXPERTS], ls[:, :N_EXPERTS]],
                          moe_w1[i], moe_w3[i], moe_w2[i])
    return (xp.reshape(n_p, s_p, d), xs.reshape(n_s, t_new, d),
            jnp.stack(akp), jnp.stack(avp), jnp.stack(bkp), jnp.stack(bvp),
            jnp.stack(aks), jnp.stack(avs), jnp.stack(bks), jnp.stack(bvs),
            jnp.stack(cvs))
```

```python
import functools

import jax
import jax.numpy as jnp
from jax import lax
from jax.experimental import pallas as pl
from jax.experimental.pallas import tpu as pltpu

F32 = jnp.float32
BF16 = jnp.bfloat16

HEAD_DIM = 64
HALF_DIM = HEAD_DIM // 2
N_HEADS = 8
B_KV_HEADS = 2
A_DILATIONS = (1, 4, 16)
BAND = 128
ROPE_THETA = 10000.0
NORM_EPS = 1e-6
NEG_INF = -1e30
CHUNK = 128
C_GROUPS = 8
N_EXPERTS = 8
TOP_K = 2

LANES = 128
SUBLANES = 8
VMEM_LIMIT_BYTES = 56 * 1024 * 1024

AW = N_HEADS * HEAD_DIM
KVW = B_KV_HEADS * HEAD_DIM
HEADS_PER_VREG = LANES // HEAD_DIM
N_HEAD_BLOCKS = AW // LANES


def _cparams(*sem):
    return pltpu.CompilerParams(dimension_semantics=sem, vmem_limit_bytes=VMEM_LIMIT_BYTES)


def _rms_norm(x, g):
    return x * lax.rsqrt(jnp.mean(x * x, axis=-1, keepdims=True) + NORM_EPS) * g


def _dot(a, b):
    return jnp.dot(a, b, preferred_element_type=F32)


def _dot_nt(a, b):
    return lax.dot_general(a, b, (((1,), (1,)), ((), ())), preferred_element_type=F32)


_QA_BLOCKS = range(0, 4)
_KA_BLOCKS = range(4, 8)
_VA_BLOCKS = range(8, 12)
_QB_BLOCKS = range(12, 16)
_KB_BLOCK = 16
_VB_BLOCK = 17


def _proj_kernel(x_ref, g_ref, w_ref, gain_ref, cos_ref, sin_ref,
                 qa_ref, ka_ref, va_ref, qb_ref, kb_ref, vb_ref, *prompt_refs):
    stage = prompt_refs[-1] if prompt_refs else None
    h = _rms_norm(x_ref[...], g_ref[...]).astype(BF16)
    p = _dot(h, w_ref[...])
    tm = p.shape[0]
    cos = cos_ref[...]
    sin = sin_ref[...]
    lane = lax.broadcasted_iota(jnp.int32, (tm, LANES), 1)
    first_half = (lane & HALF_DIM) == 0
    li = lax.broadcasted_iota(jnp.int32, (LANES, LANES), 0) // HEAD_DIM
    lj = lax.broadcasted_iota(jnp.int32, (LANES, LANES), 1) // HEAD_DIM
    head_mean = jnp.where(li == lj, 1.0 / HEAD_DIM, 0.0).astype(BF16)

    def norm_rope(blk):
        pc = p[:, blk * LANES:(blk + 1) * LANES]
        ms = _dot((pc * pc).astype(BF16), head_mean)
        y = pc * lax.rsqrt(ms + NORM_EPS) * gain_ref[:, blk * LANES:(blk + 1) * LANES]
        partner = jnp.where(first_half, pltpu.roll(y, LANES - HALF_DIM, 1), pltpu.roll(y, HALF_DIM, 1))
        return y * cos + partner * sin

    scale = HEAD_DIM ** -0.5
    for j, blk in enumerate(_QA_BLOCKS):
        q = norm_rope(blk) * scale
        qa_ref[:, j * LANES:(j + 1) * LANES] = q.astype(qa_ref.dtype)
        if stage is not None:
            stage[j] = q
    for j, blk in enumerate(_KA_BLOCKS):
        k = norm_rope(blk)
        ka_ref[:, j * LANES:(j + 1) * LANES] = k
        if stage is not None:
            stage[N_HEAD_BLOCKS + j] = k
    for j, blk in enumerate(_VA_BLOCKS):
        v = p[:, blk * LANES:(blk + 1) * LANES]
        va_ref[:, j * LANES:(j + 1) * LANES] = v
        if stage is not None:
            stage[2 * N_HEAD_BLOCKS + j] = v
    for j, blk in enumerate(_QB_BLOCKS):
        qb_ref[:, j * LANES:(j + 1) * LANES] = (norm_rope(blk) * scale).astype(qb_ref.dtype)
    kb_ref[...] = norm_rope(_KB_BLOCK)
    vb_ref[...] = p[:, _VB_BLOCK * LANES:(_VB_BLOCK + 1) * LANES]
    if prompt_refs:
        ka16_ref, va16_ref, kvb16_ref = prompt_refs[:3]
        ka16_ref[...] = ka_ref[...].astype(BF16)
        va16_ref[...] = va_ref[...].astype(BF16)
        kb = kb_ref[...]
        vb = vb_ref[...]
        parts = (kb, pltpu.roll(kb, HEAD_DIM, 1), vb, pltpu.roll(vb, HEAD_DIM, 1))
        for j, part in enumerate(parts):
            kvb16_ref[:, j * KVW:(j + 1) * KVW] = part.astype(BF16)
        for idx, r in enumerate(A_DILATIONS[1:]):
            for which, dst in enumerate(prompt_refs[3 + 3 * idx:6 + 3 * idx]):
                for c in range(r):
                    for cb in range(N_HEAD_BLOCKS):
                        rows = stage[which * N_HEAD_BLOCKS + cb, pl.ds(c, tm // r, stride=r), :]
                        dst[c, :, cb * LANES:(cb + 1) * LANES] = rows.astype(BF16)


def _qkv_proj(x, g, w, gain_row, cos, sin, *, tm, q_dtype, seq_len=None):
    t, d = x.shape
    n_out = w.shape[1]
    row = lambda i: (i, 0)
    fixed = lambda i: (0, 0)
    sds = jax.ShapeDtypeStruct
    out_specs = [pl.BlockSpec((tm, wd), row) for wd in (AW, AW, AW, AW, KVW, KVW)]
    out_shape = [sds((t, wd), dt) for wd, dt in zip((AW, AW, AW, AW, KVW, KVW), (q_dtype, F32, F32, q_dtype, F32, F32))]
    scratch = []
    if seq_len is not None:
        out_specs += [pl.BlockSpec((tm, wd), row) for wd in (AW, AW, 4 * KVW)]
        out_shape += [sds((t, wd), BF16) for wd in (AW, AW, 4 * KVW)]
        tiles = seq_len // tm
        for r in A_DILATIONS[1:]:
            assert seq_len % tm == 0 and tm % (r * 16) == 0
            out_specs += [pl.BlockSpec((None, r, tm // r, AW), lambda i: (i // tiles, 0, i % tiles, 0))] * 3
            out_shape += [sds((t // seq_len, r, seq_len // r, AW), BF16)] * 3
        scratch = [pltpu.VMEM((3 * N_HEAD_BLOCKS, tm, LANES), F32)]
    return pl.pallas_call(
        _proj_kernel,
        grid=(t // tm,),
        in_specs=[pl.BlockSpec((tm, d), row), pl.BlockSpec((1, d), fixed), pl.BlockSpec((d, n_out), fixed),
                  pl.BlockSpec((1, n_out), fixed), pl.BlockSpec((tm, LANES), row), pl.BlockSpec((tm, LANES), row)],
        out_specs=out_specs, out_shape=out_shape, scratch_shapes=scratch,
        compiler_params=_cparams("arbitrary"),
        name="qkv_proj",
    )(x, g, w, gain_row, cos, sin)


def _fill_kv(kbuf, vbuf, kp_ref, kc_ref, vp_ref, vc_ref):
    kbuf[0:BAND, :] = kp_ref[...].astype(BF16)
    kbuf[BAND:, :] = kc_ref[...].astype(BF16)
    vbuf[0:BAND, :] = vp_ref[...].astype(BF16)
    vbuf[BAND:, :] = vc_ref[...].astype(BF16)


def _band_masks(rows, first_block):
    assert BAND == LANES
    row = lax.broadcasted_iota(jnp.int32, (rows, BAND), 0) & (BAND - 1)
    col = lax.broadcasted_iota(jnp.int32, (rows, BAND), 1)
    shifted = row + jnp.where(first_block, 2 * BAND, 0)
    return col > shifted, col == shifted, col <= row


def _banded_softmax_pv(qm, kp, kc, vp, vc, masks, floor_row=None):
    upper, diag, lower = masks
    s_prev = _dot_nt(qm, kp)
    s = jnp.where(upper, s_prev, jnp.where(lower, _dot_nt(qm, kc), NEG_INF))
    back = jnp.where(diag, s_prev, NEG_INF if floor_row is None else floor_row)
    m = jnp.max(jnp.maximum(s, back), axis=-1, keepdims=True)
    p = jnp.exp(s - m)
    p_back = jnp.where(diag, jnp.exp(s_prev - m), 0.0)
    l = jnp.sum(p + p_back, axis=-1, keepdims=True)
    p_prev = jnp.where(upper, p, p_back).astype(BF16)
    p_own = jnp.where(lower, p, 0.0).astype(BF16)
    return m, l, _dot(p_prev, vp) + _dot(p_own, vc)


def _attn_a_kernel(q_ref, kc_ref, kp_ref, vc_ref, vp_ref, o_ref, lse_ref, kbuf, vbuf, *, tq):
    blk = pl.program_id(2)
    _fill_kv(kbuf, vbuf, kp_ref, kc_ref, vp_ref, vc_ref)
    lane = lax.broadcasted_iota(jnp.int32, (BAND, LANES), 1)
    low_head = lane < HEAD_DIM

    def sub_block(sb, carry):
        r0 = pl.multiple_of(sb * BAND, BAND)
        masks = _band_masks(HEADS_PER_VREG * BAND, (blk + sb) == 0)
        lse_tile = jnp.zeros((BAND, LANES), F32)
        for hb in range(N_HEAD_BLOCKS):
            cs = slice(hb * LANES, (hb + 1) * LANES)
            q2 = q_ref[pl.ds(r0, BAND), cs]
            zero = jnp.zeros_like(q2)
            qm = jnp.concatenate([jnp.where(low_head, q2, zero), jnp.where(low_head, zero, q2)], axis=0)
            m, l, pv = _banded_softmax_pv(qm, kbuf[pl.ds(r0, BAND), cs], kbuf[pl.ds(r0 + BAND, BAND), cs],
                                          vbuf[pl.ds(r0, BAND), cs], vbuf[pl.ds(r0 + BAND, BAND), cs], masks)
            o = pv * (1.0 / l)
            o_ref[pl.ds(r0, BAND), cs] = jnp.where(low_head, o[:BAND], o[BAND:]).astype(o_ref.dtype)
            lse = m + jnp.log(l)
            for hh in range(HEADS_PER_VREG):
                lse_tile = jnp.where(lane == hb * HEADS_PER_VREG + hh, lse[hh * BAND:(hh + 1) * BAND], lse_tile)
        lse_ref[pl.ds(r0, BAND), :] = lse_tile
        return carry

    lax.fori_loop(0, tq // BAND, sub_block, 0)


def _attn_a_branch(q, k, v):
    n, r, sub_len, w = q.shape
    tq = min(512, sub_len)
    cur = lambda b, c, i: (b, c, i, 0)
    prv = lambda b, c, i: (b, c, jnp.maximum(i * (tq // BAND) - 1, 0), 0)
    blk_cur = pl.BlockSpec((None, None, tq, w), cur)
    blk_prv = pl.BlockSpec((None, None, BAND, w), prv)
    sds = jax.ShapeDtypeStruct
    return pl.pallas_call(
        functools.partial(_attn_a_kernel, tq=tq),
        grid=(n, r, sub_len // tq),
        in_specs=[blk_cur, blk_cur, blk_prv, blk_cur, blk_prv],
        out_specs=[blk_cur, pl.BlockSpec((None, None, tq, LANES), cur)],
        out_shape=[sds((n, r, sub_len, w), BF16), sds((n, r, sub_len, LANES), F32)],
        scratch_shapes=[pltpu.VMEM((tq + BAND, w), BF16), pltpu.VMEM((tq + BAND, w), BF16)],
        compiler_params=_cparams("arbitrary", "arbitrary", "arbitrary"),
        name=f"attn_a_dil{r}",
    )(q, k, k, v, v)


_KVB_K, _KVB_K_SWAPPED, _KVB_V, _KVB_V_SWAPPED = range(4)
_KVB_WIDTH = 4 * KVW


def _attn_b_kernel(q_ref, kvc_ref, kvp_ref, sink_ref, o_ref, kvbuf, *, tq):
    blk = pl.program_id(1)
    kvbuf[0:BAND, :] = kvp_ref[...]
    kvbuf[BAND:, :] = kvc_ref[...]
    lane = lax.broadcasted_iota(jnp.int32, (BAND, LANES), 1)
    low_head = lane < HEAD_DIM
    group = N_HEADS // B_KV_HEADS
    plain = [h for h in range(N_HEADS) if h % HEADS_PER_VREG == h // group]
    swapped = [h for h in range(N_HEADS) if h % HEADS_PER_VREG != h // group]

    def sub_block(sb, carry):
        r0 = pl.multiple_of(sb * BAND, BAND)
        masks = _band_masks(len(plain) * BAND, (blk + sb) == 0)
        blocks = [q_ref[pl.ds(r0, BAND), hb * LANES:(hb + 1) * LANES] for hb in range(N_HEAD_BLOCKS)]
        outs = [None] * N_HEADS
        for heads, k_blk, v_blk in ((plain, _KVB_K, _KVB_V), (swapped, _KVB_K_SWAPPED, _KVB_V_SWAPPED)):
            ks = slice(k_blk * KVW, (k_blk + 1) * KVW)
            vs = slice(v_blk * KVW, (v_blk + 1) * KVW)
            q_parts, sink_parts = [], []
            for h in heads:
                q2 = blocks[h // HEADS_PER_VREG]
                own_half = low_head if h % HEADS_PER_VREG == 0 else jnp.logical_not(low_head)
                q_parts.append(jnp.where(own_half, q2, jnp.zeros_like(q2)))
                sink_parts.append(jnp.broadcast_to(sink_ref[h:h + 1, :], (BAND, LANES)))
            sink = jnp.concatenate(sink_parts, axis=0)
            m, l, pv = _banded_softmax_pv(jnp.concatenate(q_parts, axis=0),
                                          kvbuf[pl.ds(r0, BAND), ks], kvbuf[pl.ds(r0 + BAND, BAND), ks],
                                          kvbuf[pl.ds(r0, BAND), vs], kvbuf[pl.ds(r0 + BAND, BAND), vs],
                                          masks, floor_row=sink)
            o = pv * (1.0 / (l + jnp.exp(sink - m)))
            for idx, h in enumerate(heads):
                outs[h] = o[idx * BAND:(idx + 1) * BAND]
        for hb in range(N_HEAD_BLOCKS):
            o_ref[pl.ds(r0, BAND), hb * LANES:(hb + 1) * LANES] = jnp.where(
                low_head, outs[hb * HEADS_PER_VREG], outs[hb * HEADS_PER_VREG + 1]).astype(o_ref.dtype)
        return carry

    lax.fori_loop(0, tq // BAND, sub_block, 0)


def _attn_b(q, kv, sinks_rows):
    n, s, w = q.shape
    tq = min(512, s)
    cur = lambda b, i: (b, i, 0)
    prv = lambda b, i: (b, jnp.maximum(i * (tq // BAND) - 1, 0), 0)
    return pl.pallas_call(
        functools.partial(_attn_b_kernel, tq=tq),
        grid=(n, s // tq),
        in_specs=[pl.BlockSpec((None, tq, w), cur), pl.BlockSpec((None, tq, _KVB_WIDTH), cur),
                  pl.BlockSpec((None, BAND, _KVB_WIDTH), prv),
                  pl.BlockSpec((N_HEADS, LANES), lambda b, i: (0, 0))],
        out_specs=pl.BlockSpec((None, tq, w), cur),
        out_shape=jax.ShapeDtypeStruct((n, s, w), BF16),
        scratch_shapes=[pltpu.VMEM((tq + BAND, _KVB_WIDTH), BF16)],
        compiler_params=_cparams("arbitrary", "arbitrary"),
        name="attn_b",
    )(q, kv, kv, sinks_rows)


NEW_ROWS = 8


def _branch_multiplicity(t_query, key_pos, la):
    back = la + t_query - key_pos
    mult = jnp.zeros(back.shape, F32)
    for r in A_DILATIONS:
        mult = mult + ((back >= 0) & (back <= BAND * r) & (back % r == 0)).astype(F32)
    return mult


def _sample_attn_kernel(qa_ref, kna_ref, vna_ref, qb_ref, knb_ref, vnb_ref,
                        kt_ref, vt_ref, cbk_ref, cbv_ref, sink_ref,
                        oa_ref, ob_ref, *, nb, t_new):
    la = kt_ref.shape[-1]
    t_c = lax.broadcasted_iota(jnp.int32, (NEW_ROWS, la), 0)
    t_c = jnp.where(t_c < t_new, t_c, 0)
    mult_c = _branch_multiplicity(t_c, lax.broadcasted_iota(jnp.int32, (NEW_ROWS, la), 1), la)
    t_n = lax.broadcasted_iota(jnp.int32, (NEW_ROWS, NEW_ROWS), 0)
    t_n = jnp.where(t_n < t_new, t_n, 0)
    key_n = lax.broadcasted_iota(jnp.int32, (NEW_ROWS, NEW_ROWS), 1)
    mult_n = jnp.where(key_n < t_new, _branch_multiplicity(t_n, la + key_n, la), 0.0)

    def head_attn_a(q_h, kt, vt, kn_h, vn_h):
        s_c = jnp.where(mult_c > 0.0, _dot(q_h, kt), NEG_INF)
        s_n = jnp.where(mult_n > 0.0, _dot_nt(q_h, kn_h), NEG_INF)
        m = jnp.maximum(jnp.max(s_c, axis=-1, keepdims=True), jnp.max(s_n, axis=-1, keepdims=True))
        p_c = jnp.exp(s_c - m) * mult_c
        p_n = jnp.exp(s_n - m) * mult_n
        l = jnp.sum(p_c, axis=-1, keepdims=True) + jnp.sum(p_n, axis=-1, keepdims=True)
        return (_dot_nt(p_c.astype(BF16), vt) + _dot(p_n.astype(BF16), vn_h)) / l

    group = N_HEADS // B_KV_HEADS
    lb = cbk_ref.shape[-1]
    t_b = lax.broadcasted_iota(jnp.int32, (group * NEW_ROWS, lb), 0) & (NEW_ROWS - 1)
    t_b = jnp.where(t_b < t_new, t_b, 0)
    valid_bc = lax.broadcasted_iota(jnp.int32, (group * NEW_ROWS, lb), 1) >= lb + t_b - BAND
    t_bn = lax.broadcasted_iota(jnp.int32, (group * NEW_ROWS, NEW_ROWS), 0) & (NEW_ROWS - 1)
    t_bn = jnp.where(t_bn < t_new, t_bn, 0)
    key_bn = lax.broadcasted_iota(jnp.int32, (group * NEW_ROWS, NEW_ROWS), 1)
    valid_bn = (key_bn <= t_bn) & (key_bn < t_new)

    def kv_head_attn_b(q_g, kt, vt, kn_g, vn_g, sink):
        s_c = jnp.where(valid_bc, _dot(q_g, kt), NEG_INF)
        s_n = jnp.where(valid_bn, _dot_nt(q_g, kn_g), NEG_INF)
        m = jnp.maximum(jnp.max(s_c, axis=-1, keepdims=True), jnp.max(s_n, axis=-1, keepdims=True))
        m = jnp.maximum(m, sink)
        p_c = jnp.exp(s_c - m[:, :lb])
        p_n = jnp.exp(s_n - m[:, :NEW_ROWS])
        l = (jnp.sum(p_c, axis=-1, keepdims=True) + jnp.sum(p_n, axis=-1, keepdims=True) + jnp.exp(sink - m))
        return (_dot_nt(p_c.astype(BF16), vt) + _dot(p_n.astype(BF16), vn_g)) / l[:, :HEAD_DIM]

    def one_sequence(n, carry):
        new0 = pl.multiple_of(n * NEW_ROWS, NEW_ROWS)
        qa = qa_ref[pl.ds(new0, NEW_ROWS), :].astype(BF16)
        kna = kna_ref[pl.ds(new0, NEW_ROWS), :].astype(BF16)
        vna = vna_ref[pl.ds(new0, NEW_ROWS), :].astype(BF16)
        heads_out = []
        for h in range(N_HEADS):
            hs = slice(h * HEAD_DIM, (h + 1) * HEAD_DIM)
            heads_out.append(head_attn_a(qa[:, hs], kt_ref[n, h].astype(BF16), vt_ref[n, h].astype(BF16),
                                         kna[:, hs], vna[:, hs]))
        oa_ref[pl.ds(new0, NEW_ROWS), :] = jnp.concatenate(heads_out, axis=1).astype(oa_ref.dtype)
        qb = qb_ref[pl.ds(new0, NEW_ROWS), :].astype(BF16)
        knb = knb_ref[pl.ds(new0, NEW_ROWS), :].astype(BF16)
        vnb = vnb_ref[pl.ds(new0, NEW_ROWS), :].astype(BF16)
        heads_b = []
        for g in range(B_KV_HEADS):
            heads = range(g * group, (g + 1) * group)
            q_g = jnp.concatenate([qb[:, h * HEAD_DIM:(h + 1) * HEAD_DIM] for h in heads], axis=0)
            sink = jnp.concatenate([jnp.broadcast_to(sink_ref[h:h + 1, :], (NEW_ROWS, LANES)) for h in heads], axis=0)
            gs = slice(g * HEAD_DIM, (g + 1) * HEAD_DIM)
            o_g = kv_head_attn_b(q_g, cbk_ref[n, g].astype(BF16), cbv_ref[n, g].astype(BF16), knb[:, gs], vnb[:, gs], sink)
            heads_b += [o_g[hh * NEW_ROWS:(hh + 1) * NEW_ROWS] for hh in range(group)]
        ob_ref[pl.ds(new0, NEW_ROWS), :] = jnp.concatenate(heads_b, axis=1).astype(ob_ref.dtype)
        return carry

    lax.fori_loop(0, nb, one_sequence, 0)


def _pad_new_rows(a, n_seq, t_new):
    w = a.shape[-1]
    a = a.reshape(n_seq, t_new, w)
    a = jnp.pad(a, ((0, 0), (0, NEW_ROWS - t_new), (0, 0)))
    return a.reshape(n_seq * NEW_ROWS, w)


def _sample_attn(qa, ka, va, qb, kb, vb, cache_a_k, cache_a_v, cache_b_k, cache_b_v, sink_rows, *, t_new):
    n_seq, la, nh, hd = cache_a_k.shape
    lb, nkv = cache_b_k.shape[1:3]
    w = nh * hd
    assert t_new <= NEW_ROWS
    nb = 2 if n_seq % 2 == 0 else 1
    padded = [_pad_new_rows(a, n_seq, t_new) for a in (qa, ka, va, qb, kb, vb)]
    dim_major = lambda c: jnp.transpose(c, (0, 2, 3, 1))
    new_a = pl.BlockSpec((nb * NEW_ROWS, w), lambda i: (i, 0))
    new_b = pl.BlockSpec((nb * NEW_ROWS, KVW), lambda i: (i, 0))
    spec_ca = pl.BlockSpec((nb, nh, hd, la), lambda i: (i, 0, 0, 0))
    spec_cb = pl.BlockSpec((nb, nkv, hd, lb), lambda i: (i, 0, 0, 0))
    sds = jax.ShapeDtypeStruct
    oa, ob = pl.pallas_call(
        functools.partial(_sample_attn_kernel, nb=nb, t_new=t_new),
        grid=(n_seq // nb,),
        in_specs=[new_a, new_a, new_a, new_a, new_b, new_b, spec_ca, spec_ca, spec_cb, spec_cb,
                  pl.BlockSpec((N_HEADS, LANES), lambda i: (0, 0))],
        out_specs=[new_a, new_a],
        out_shape=[sds((n_seq * NEW_ROWS, w), BF16), sds((n_seq * NEW_ROWS, w), BF16)],
        compiler_params=_cparams("arbitrary"),
        name="sample_attn",
    )(*padded, dim_major(cache_a_k), dim_major(cache_a_v), dim_major(cache_b_k), dim_major(cache_b_v), sink_rows)
    unpad = lambda o: o.reshape(n_seq, NEW_ROWS, w)[:, :t_new].reshape(n_seq * t_new, w)
    return unpad(oa), unpad(ob)


def _silu(a):
    return a * (1.0 / (1.0 + jnp.exp(-a)))


def _natural_order(o_ref, lse_ref, o_stage, lse_stage):
    r, sub, _ = o_ref.shape
    if r == 1:
        return o_ref[0].astype(F32), lse_ref[0]
    for c in range(r):
        lse_stage[pl.ds(c, sub, stride=r), :] = lse_ref[c]
        for cb in range(N_HEAD_BLOCKS):
            o_stage[cb, pl.ds(c, sub, stride=r), :] = o_ref[c, :, cb * LANES:(cb + 1) * LANES].astype(F32)
    return jnp.concatenate([o_stage[cb] for cb in range(N_HEAD_BLOCKS)], axis=1), lse_stage[...]


def _merge_branches(outs, lses):
    top = functools.reduce(jnp.maximum, lses)
    ws = [jnp.exp(l - top) for l in lses]
    inv_den = 1.0 / functools.reduce(lambda a, b: a + b, ws)
    head = lax.broadcasted_iota(jnp.int32, (LANES, AW), 0)
    lane_head = lax.broadcasted_iota(jnp.int32, (LANES, AW), 1) // HEAD_DIM
    expand = (head == lane_head).astype(BF16)
    oa = None
    for w, o in zip(ws, outs):
        w = w * inv_den
        w_hi = w.astype(BF16)
        w_lo = (w - w_hi.astype(F32)).astype(BF16)
        part = (_dot(w_hi, expand) + _dot(w_lo, expand)) * o
        oa = part if oa is None else oa + part
    return oa.astype(BF16)


def _mix_out_ffn_kernel(x_ref, *refs, n_branches):
    if n_branches:
        branch_refs, refs = refs[:2 * n_branches], refs[2 * n_branches:]
        refs, (o_stage, lse_stage) = refs[:-2], refs[-2:]
    else:
        oa_ref, refs = refs[0], refs[1:]
    ob_ref, wo_ref, g_ref, w1_ref, w3_ref, w2_ref, out_ref, x1_ref, h_ref, acc_ref = refs
    j = pl.program_id(1)

    @pl.when(j == 0)
    def _():
        if n_branches:
            outs, lses = zip(*[_natural_order(branch_refs[2 * i], branch_refs[2 * i + 1], o_stage, lse_stage)
                               for i in range(n_branches)])
            oa = _merge_branches(outs, lses)
        else:
            oa = oa_ref[...]
        x1 = x_ref[...] + _dot(oa, wo_ref[0:AW, :]) + _dot(ob_ref[...], wo_ref[AW:, :])
        x1_ref[...] = x1
        h_ref[...] = _rms_norm(x1, g_ref[...]).astype(BF16)
        acc_ref[...] = jnp.zeros_like(acc_ref)

    h = h_ref[...]
    gate = (_silu(_dot(h, w1_ref[...])) * _dot(h, w3_ref[...])).astype(BF16)
    acc_ref[...] += _dot(gate, w2_ref[...])

    @pl.when(j == pl.num_programs(1) - 1)
    def _():
        out_ref[...] = x1_ref[...] + acc_ref[...]


def _ff_tile(ff, target):
    best = LANES
    for k in range(1, ff // LANES + 1):
        if ff % (k * LANES) == 0 and k * LANES <= target:
            best = k * LANES
    return best


def _mix_out_ffn(x, oa, ob, wo, g, w1, w3, w2, *, tm):
    t, d = x.shape
    ff = w1.shape[1]
    tf = ff
    row = lambda i, j: (i, 0)
    fixed = lambda i, j: (0, 0)
    once = pl.Buffered(1)
    if isinstance(oa, (list, tuple)):
        n_branches = len(oa)
        tiles = t // oa[0][0].shape[0] // tm
        a_specs, a_args = [], []
        for o, lse in oa:
            r = o.shape[1]
            assert tm % (r * SUBLANES) == 0
            tile = lambda i, j: (i // tiles, 0, i % tiles, 0)
            a_specs += [pl.BlockSpec((None, r, tm // r, AW), tile), pl.BlockSpec((None, r, tm // r, LANES), tile)]
            a_args += [o, lse]
        stage = [pltpu.VMEM((N_HEAD_BLOCKS, tm, LANES), F32), pltpu.VMEM((tm, LANES), F32)]
    else:
        n_branches, a_specs, a_args, stage = 0, [pl.BlockSpec((tm, AW), row)], [oa], []
    return pl.pallas_call(
        functools.partial(_mix_out_ffn_kernel, n_branches=n_branches),
        grid=(t // tm, ff // tf),
        in_specs=[pl.BlockSpec((tm, d), row)] + a_specs
                 + [pl.BlockSpec((tm, AW), row),
                  pl.BlockSpec((2 * AW, d), fixed, pipeline_mode=once), pl.BlockSpec((1, d), fixed),
                  pl.BlockSpec((d, tf), lambda i, j: (0, j), pipeline_mode=once),
                  pl.BlockSpec((d, tf), lambda i, j: (0, j), pipeline_mode=once),
                  pl.BlockSpec((tf, d), lambda i, j: (j, 0), pipeline_mode=once)],
        out_specs=pl.BlockSpec((tm, d), row),
        out_shape=jax.ShapeDtypeStruct((t, d), F32),
        scratch_shapes=[pltpu.VMEM((tm, d), F32), pltpu.VMEM((tm, d), BF16), pltpu.VMEM((tm, d), F32)] + stage,
        compiler_params=_cparams("arbitrary", "arbitrary"),
        name="mix_out_ffn",
    )(x, *a_args, ob, wo, g, w1, w3, w2)


def _gelu_exact(z):
    return 0.5 * z * (1.0 + lax.erf(z * (2.0 ** -0.5)))


def _gmlp_kernel(x_ref, g_ref, win_ref, lng_ref, lnb_ref, mix_ref, bias_ref, wout_ref, gffn_ref, wr_ref,
                 *out_refs, write_v):
    if write_v:
        xo_ref, h2_ref, logit_ref, v_ref = out_refs[:4]
    else:
        xo_ref, h2_ref, logit_ref = out_refs[:3]
    gate_ref = out_refs[-1]
    x = x_ref[...]
    tm, cd = x.shape
    h = _rms_norm(x, g_ref[...]).astype(BF16)
    z = _gelu_exact(_dot(h, win_ref[...]))
    u = z[:, :cd]
    v = z[:, cd:]
    mu = jnp.mean(v, axis=-1, keepdims=True)
    vc = v - mu
    v = vc * lax.rsqrt(jnp.mean(vc * vc, axis=-1, keepdims=True) + NORM_EPS) * lng_ref[...] + lnb_ref[...]
    if write_v:
        v_ref[...] = v
    vb = v.astype(BF16)
    bias = bias_ref[...]
    gw = cd // C_GROUPS
    for c in range(tm // CHUNK):
        rs = slice(c * CHUNK, (c + 1) * CHUNK)
        for g in range(C_GROUPS):
            cs = slice(g * gw, (g + 1) * gw)
            f = _dot(mix_ref[g], vb[rs, cs]) + bias[:, cs]
            gate_ref[rs, cs] = (u[rs, cs] * f).astype(BF16)
    xo = x + _dot(gate_ref[...], wout_ref[...])
    xo_ref[...] = xo
    h2 = _rms_norm(xo, gffn_ref[...])
    h2_ref[...] = h2
    h_hi = h2.astype(BF16)
    h_lo = (h2 - h_hi.astype(F32)).astype(BF16)
    wr = wr_ref[...]
    both = _dot(h_hi, wr)
    logit_ref[...] = both[:, :LANES] + both[:, LANES:] + _dot(h_lo, wr[:, :LANES])


def _gmlp_block(x, g, w_in, ln_g, ln_b, mix, bias_full, w_out, g_ffn, wr_hi_lo, *, tm, write_v):
    t, d = x.shape
    cd = w_out.shape[0]
    row = lambda i: (i, 0)
    fixed = lambda i: (0, 0)
    sds = jax.ShapeDtypeStruct
    out_specs = [pl.BlockSpec((tm, d), row), pl.BlockSpec((tm, d), row), pl.BlockSpec((tm, LANES), row)]
    out_shape = [sds((t, d), F32), sds((t, d), F32), sds((t, LANES), F32)]
    if write_v:
        out_specs.append(pl.BlockSpec((tm, cd), row))
        out_shape.append(sds((t, cd), F32))
    return pl.pallas_call(
        functools.partial(_gmlp_kernel, write_v=write_v),
        grid=(t // tm,),
        in_specs=[pl.BlockSpec((tm, d), row), pl.BlockSpec((1, d), fixed), pl.BlockSpec((d, 2 * cd), fixed),
                  pl.BlockSpec((1, cd), fixed), pl.BlockSpec((1, cd), fixed),
                  pl.BlockSpec((C_GROUPS, CHUNK, CHUNK), lambda i: (0, 0, 0)),
                  pl.BlockSpec((CHUNK, cd), fixed), pl.BlockSpec((cd, d), fixed), pl.BlockSpec((1, d), fixed),
                  pl.BlockSpec((d, 2 * LANES), fixed)],
        out_specs=out_specs, out_shape=out_shape,
        scratch_shapes=[pltpu.VMEM((tm, cd), BF16)],
        compiler_params=_cparams("arbitrary"),
        name="gmlp_block",
    )(x, g, w_in, ln_g, ln_b, mix, bias_full, w_out, g_ffn, wr_hi_lo)


MOE_BLOCK = 1024
MOE_SUB = 256
DISPATCH_ROWS = 512


def _dispatch_kernel(dest_ref, h_ref, xs_in_ref, xs_ref, sem):
    del xs_in_ref
    rows = h_ref.shape[0]

    def issue(g, carry):
        for s in range(SUBLANES):
            r = g * SUBLANES + s
            for k in range(TOP_K):
                d = dest_ref[0, 0, r * TOP_K + k]
                pltpu.make_async_copy(h_ref.at[pl.ds(r, 1), :], xs_ref.at[pl.ds(d, 1), :], sem).start(priority=k)
        return carry

    lax.fori_loop(0, rows // SUBLANES, issue, 0)
    for _ in range(TOP_K):
        pltpu.make_async_copy(h_ref, xs_ref.at[pl.ds(0, rows), :], sem).wait()


def _dispatch(h, dest, xs):
    t, d = h.shape
    tg = DISPATCH_ROWS
    return pl.pallas_call(
        _dispatch_kernel,
        grid=(t // tg,),
        in_specs=[pl.BlockSpec((1, 1, tg * TOP_K), lambda i: (i, 0, 0), memory_space=pltpu.SMEM),
                  pl.BlockSpec((tg, d), lambda i: (i, 0)),
                  pl.BlockSpec(memory_space=pl.ANY)],
        out_specs=pl.BlockSpec(memory_space=pl.ANY),
        out_shape=jax.ShapeDtypeStruct(xs.shape, xs.dtype),
        scratch_shapes=[pltpu.SemaphoreType.DMA],
        input_output_aliases={2: 0},
        compiler_params=_cparams("arbitrary"),
        name="moe_dispatch",
    )(dest.reshape(t // tg, 1, tg * TOP_K), h, xs)


def _expert_kernel(be_ref, na_ref, ns_ref, xs_ref, w1_ref, w3_ref, w2_ref, y_ref, h_ref, acc_ref):
    del be_ref, na_ref
    b = pl.program_id(0)
    j = pl.program_id(1)
    n_sub = ns_ref[b]

    @pl.when((n_sub > 0) & (j == 0))
    def _():
        h_ref[...] = xs_ref[...].astype(BF16)
        acc_ref[...] = jnp.zeros_like(acc_ref)

    for k in range(1, MOE_BLOCK // MOE_SUB + 1):
        @pl.when(n_sub == k)
        def _(rows=k * MOE_SUB):
            h = h_ref[0:rows, :]
            w1 = w1_ref[...].astype(BF16)
            w3 = w3_ref[...].astype(BF16)
            gate = (_silu(_dot(h, w1)) * _dot(h, w3)).astype(BF16)
            acc_ref[0:rows, :] += _dot(gate, w2_ref[...].astype(BF16))

    @pl.when(j == pl.num_programs(1) - 1)
    def _():
        y_ref[...] = jnp.where(n_sub > 0, acc_ref[...], 0.0)


def _experts(xs, block_expert, n_active, n_sub, w1, w3, w2):
    n_rows, d = xs.shape
    ff = w1.shape[2]
    tf = _ff_tile(ff, 512)
    n_blocks = n_rows // MOE_BLOCK
    nj = ff // tf
    def wcol(b, j, be, na, ns):
        live = b < na[0]
        return (be[b], 0, jnp.where(live, j, nj - 1))

    def wrow(b, j, be, na, ns):
        live = b < na[0]
        return (be[b], jnp.where(live, j, nj - 1), 0)

    def xrow(b, j, be, na, ns):
        return (jnp.minimum(b, na[0] - 1), 0)

    grid_spec = pltpu.PrefetchScalarGridSpec(
        num_scalar_prefetch=3,
        grid=(n_blocks, nj),
        in_specs=[pl.BlockSpec((MOE_BLOCK, d), xrow),
                  pl.BlockSpec((None, d, tf), wcol), pl.BlockSpec((None, d, tf), wcol),
                  pl.BlockSpec((None, tf, d), wrow)],
        out_specs=pl.BlockSpec((MOE_BLOCK, d), lambda b, j, be, na, ns: (b, 0)),
        scratch_shapes=[pltpu.VMEM((MOE_BLOCK, d), BF16), pltpu.VMEM((MOE_BLOCK, d), F32)],
    )
    return pl.pallas_call(
        _expert_kernel,
        grid_spec=grid_spec,
        out_shape=jax.ShapeDtypeStruct((n_rows, d), F32),
        compiler_params=_cparams("arbitrary", "arbitrary"),
        name="moe_experts",
    )(block_expert, n_active, n_sub, xs, w1, w3, w2)


def _combine_kernel(dest_ref, x_ref, gate_ref, y_ref, out_ref, buf, sem):
    rows = x_ref.shape[0]

    def issue(g, carry):
        for s in range(SUBLANES):
            r = g * SUBLANES + s
            for k in range(TOP_K):
                d = dest_ref[0, 0, r * TOP_K + k]
                pltpu.make_async_copy(y_ref.at[pl.ds(d, 1), :], buf.at[k, pl.ds(r, 1), :], sem).start(priority=k)
        return carry

    lax.fori_loop(0, rows // SUBLANES, issue, 0)
    for k in range(TOP_K):
        pltpu.make_async_copy(y_ref.at[pl.ds(0, rows), :], buf.at[k], sem).wait()
    gates = gate_ref[...]
    y = buf[0] * gates[:, 0:1]
    for k in range(1, TOP_K):
        y = y + buf[k] * gates[:, k:k + 1]
    out_ref[...] = x_ref[...] + y


def _combine(x, gates, dest, y):
    t, d = x.shape
    tg = DISPATCH_ROWS
    return pl.pallas_call(
        _combine_kernel,
        grid=(t // tg,),
        in_specs=[pl.BlockSpec((1, 1, tg * TOP_K), lambda i: (i, 0, 0), memory_space=pltpu.SMEM),
                  pl.BlockSpec((tg, d), lambda i: (i, 0)),
                  pl.BlockSpec((tg, TOP_K), lambda i: (i, 0)),
                  pl.BlockSpec(memory_space=pl.ANY)],
        out_specs=pl.BlockSpec((tg, d), lambda i: (i, 0)),
        out_shape=jax.ShapeDtypeStruct((t, d), F32),
        scratch_shapes=[pltpu.VMEM((TOP_K, tg, d), F32), pltpu.SemaphoreType.DMA],
        compiler_params=_cparams("arbitrary"),
        name="moe_combine",
    )(dest.reshape(t // tg, 1, tg * TOP_K), x, gates, y)


def _route(logits):
    t = logits.shape[0]
    top_val, top_idx = lax.top_k(logits, TOP_K)
    gates = jax.nn.softmax(top_val, axis=-1)
    expert = top_idx.reshape(-1)
    one_hot = (expert[:, None] == jnp.arange(N_EXPERTS, dtype=expert.dtype)[None, :]).astype(jnp.int32)
    running = jnp.cumsum(one_hot, axis=0)
    rank = jnp.sum(running * one_hot, axis=1) - 1
    counts = running[-1]
    padded = (counts + MOE_BLOCK - 1) // MOE_BLOCK * MOE_BLOCK
    pad_ends = jnp.cumsum(padded)
    pad_starts = pad_ends - padded
    dest = (jnp.sum(pad_starts[None, :] * one_hot, axis=1) + rank).astype(jnp.int32)
    n_blocks = -(-(t * TOP_K) // MOE_BLOCK) + N_EXPERTS
    block_start = jnp.arange(n_blocks, dtype=jnp.int32) * MOE_BLOCK
    block_expert = jnp.sum((block_start[:, None] >= pad_ends[None, :]).astype(jnp.int32), axis=1)
    block_expert = jnp.minimum(block_expert, N_EXPERTS - 1).astype(jnp.int32)
    used_end = (pad_starts + counts)[block_expert]
    used_rows = jnp.clip(used_end - block_start, 0, MOE_BLOCK)
    used_rows = jnp.where(block_start < pad_ends[-1], used_rows, 0)
    n_sub = ((used_rows + MOE_SUB - 1) // MOE_SUB).astype(jnp.int32)
    n_active = (pad_ends[-1:] // MOE_BLOCK).astype(jnp.int32)
    return gates, dest.reshape(t, TOP_K), block_expert, n_sub, n_active, n_blocks * MOE_BLOCK


def _moe(groups, w1, w3, w2):
    sizes = [x.shape[0] for x, _, _ in groups]
    gates, dest, block_expert, n_sub, n_active, n_rows = _route(jnp.concatenate([l for _, _, l in groups]))
    xs = jnp.zeros((n_rows, w1.shape[1]), F32)
    start = 0
    for (_, h2, _), size in zip(groups, sizes):
        xs = _dispatch(h2, dest[start:start + size].reshape(-1), xs)
        start += size
    y = _experts(xs, block_expert, n_active, n_sub, w1, w3, w2)
    outs, start = [], 0
    for (x, _, _), size in zip(groups, sizes):
        outs.append(_combine(x, gates[start:start + size], dest[start:start + size].reshape(-1), y))
        start += size
    return outs


def _rope_tables(pos):
    inv_freq = ROPE_THETA ** (-jnp.arange(HALF_DIM, dtype=F32) / HALF_DIM)
    ang = pos.astype(F32)[:, None] * inv_freq[None, :]
    cos, sin = jnp.cos(ang), jnp.sin(ang)
    reps = LANES // HEAD_DIM
    return jnp.tile(cos, (1, 2 * reps)), jnp.tile(jnp.concatenate([-sin, sin], axis=1), (1, reps))


def _qk_gain_row(qk_gain):
    ones_a = jnp.ones((AW,), F32)
    ones_b = jnp.ones((KVW,), F32)
    return jnp.concatenate([jnp.tile(qk_gain[0], N_HEADS), jnp.tile(qk_gain[1], N_HEADS), ones_a,
                            jnp.tile(qk_gain[2], N_HEADS), jnp.tile(qk_gain[3], B_KV_HEADS), ones_b])[None, :]


def _row_block(t):
    return 512 if t % 512 == 0 else t


def kernel(x_prompt, x_sample, cache_a_k, cache_a_v, cache_b_k, cache_b_v, norm_mix, norm_ffn, ab_w_in,
           ab_qk_gain, ab_sinks, ab_w_out, ffn_w1, ffn_w3, ffn_w2, c_w_in, c_ln_g, c_ln_b, c_w_s, c_b_s,
           c_w_out, moe_router, moe_w1, moe_w3, moe_w2):
    n_p, s_p, d = x_prompt.shape
    n_s, t_new, _ = x_sample.shape
    depth = norm_mix.shape[0]
    past_len = 8192
    xp = x_prompt.reshape(n_p * s_p, d)
    xs = x_sample.reshape(n_s * t_new, d)
    pos_p = jnp.tile(jnp.arange(s_p), n_p)
    pos_s = jnp.tile(past_len + jnp.arange(t_new), n_s)
    cos_p, sin_p = _rope_tables(pos_p)
    cos_s, sin_s = _rope_tables(pos_s)
    row = lambda a: a[None, :]
    akp, avp, bkp, bvp, aks, avs, bks, bvs, cvs = [], [], [], [], [], [], [], [], []
    la_p, lb_p = min(BAND * A_DILATIONS[-1], s_p), min(BAND, s_p)
    for layer in range(depth):
        i = layer // 2
        if layer % 2 == 0:
            w_in = ab_w_in[i].astype(BF16)
            gain_row = _qk_gain_row(ab_qk_gain[i])
            w_out = ab_w_out[i].astype(BF16)
            w1, w3, w2 = ffn_w1[i].astype(BF16), ffn_w3[i].astype(BF16), ffn_w2[i].astype(BF16)
            sinks = ab_sinks[i].astype(F32)
            sink_rows = jnp.broadcast_to(sinks[:, None], (N_HEADS, LANES))
            qa, ka, va, qb, kb, vb, ka16, va16, kvb16, *dilated = _qkv_proj(
                xp, row(norm_mix[layer]), w_in, gain_row, cos_p, sin_p,
                tm=_row_block(s_p), q_dtype=BF16, seq_len=s_p)
            seq = lambda a: a.reshape(n_p, s_p, a.shape[-1])
            qkv_by_branch = [[seq(a)[:, None] for a in (qa, ka16, va16)]] + [dilated[3 * i:3 * i + 3]
                                                                            for i in range(len(A_DILATIONS) - 1)]
            branches = [_attn_a_branch(q, k, v) for q, k, v in qkv_by_branch]
            ob = _attn_b(seq(qb), seq(kvb16), sink_rows).reshape(n_p * s_p, AW)
            xp = _mix_out_ffn(xp, branches, ob, w_out, row(norm_ffn[layer]), w1, w3, w2, tm=_row_block(s_p))
            tail = lambda a, rows, nh: seq(a)[:, s_p - rows:].reshape(n_p, rows, nh, HEAD_DIM)
            akp.append(tail(ka, la_p, N_HEADS))
            avp.append(tail(va, la_p, N_HEADS))
            bkp.append(tail(kb, lb_p, B_KV_HEADS))
            bvp.append(tail(vb, lb_p, B_KV_HEADS))
            qa, ka, va, qb, kb, vb = _qkv_proj(xs, row(norm_mix[layer]), w_in, gain_row, cos_s, sin_s,
                                               tm=_row_block(xs.shape[0]), q_dtype=F32)
            oa, ob = _sample_attn(qa, ka, va, qb, kb, vb, cache_a_k[i], cache_a_v[i],
                                  cache_b_k[i], cache_b_v[i], sink_rows, t_new=t_new)
            xs = _mix_out_ffn(xs, oa, ob, w_out, row(norm_ffn[layer]), w1, w3, w2, tm=_row_block(xs.shape[0]))
            heads_s = lambda a, nh: a.reshape(n_s, t_new, nh, HEAD_DIM)
            aks.append(heads_s(ka, N_HEADS))
            avs.append(heads_s(va, N_HEADS))
            bks.append(heads_s(kb, B_KV_HEADS))
            bvs.append(heads_s(vb, B_KV_HEADS))
        else:
            w_in = c_w_in[i].astype(BF16)
            w_out = c_w_out[i].astype(BF16)
            tril = jnp.tril(jnp.ones((CHUNK, CHUNK), F32))
            mix_p = (c_w_s[i] * tril).astype(BF16)
            gw = w_out.shape[0] // C_GROUPS
            bias_p = jnp.repeat(c_b_s[i].T, gw, axis=1)
            per_tile = CHUNK // t_new
            mix_s = jnp.einsum("ab,gij->gaibj", jnp.eye(per_tile, dtype=F32), (c_w_s[i] * tril)[:, :t_new, :t_new])
            mix_s = mix_s.reshape(C_GROUPS, CHUNK, CHUNK).astype(BF16)
            bias_s = jnp.tile(bias_p[:t_new], (per_tile, 1))
            wr = moe_router[i]
            wr_hi = wr.astype(BF16)
            wr_lo = (wr - wr_hi.astype(F32)).astype(BF16)
            pad_e = lambda a: jnp.pad(a, ((0, 0), (0, LANES - N_EXPERTS)))
            wr_hi_lo = jnp.concatenate([pad_e(wr_hi), pad_e(wr_lo)], axis=1)
            common = (row(c_ln_g[i]), row(c_ln_b[i]))
            xp1, hp, lp = _gmlp_block(xp, row(norm_mix[layer]), w_in, *common, mix_p, bias_p, w_out,
                                      row(norm_ffn[layer]), wr_hi_lo, tm=_row_block(xp.shape[0]), write_v=False)
            xs1, hs, ls, v_new = _gmlp_block(xs, row(norm_mix[layer]), w_in, *common, mix_s, bias_s, w_out,
                                             row(norm_ffn[layer]), wr_hi_lo, tm=_row_block(xs.shape[0]),
                                             write_v=True)
            cvs.append(v_new.reshape(n_s, t_new, v_new.shape[-1]))
            xp, xs = _moe([(xp1, hp, lp[:, :N_EXPERTS]), (xs1, hs, ls[:, :N_EXPERTS])],
                          moe_w1[i], moe_w3[i], moe_w2[i])
    return (xp.reshape(n_p, s_p, d), xs.reshape(n_s, t_new, d),
            jnp.stack(akp), jnp.stack(avp), jnp.stack(bkp), jnp.stack(bvp),
            jnp.stack(aks), jnp.stack(avs), jnp.stack(bks), jnp.stack(bvs),
            jnp.stack(cvs))
```

```python
import functools

import jax
import jax.numpy as jnp
from jax import lax
from jax.experimental import pallas as pl
from jax.experimental.pallas import tpu as pltpu

F32 = jnp.float32
BF16 = jnp.bfloat16

HEAD_DIM = 64
HALF_DIM = HEAD_DIM // 2
N_HEADS = 8
B_KV_HEADS = 2
A_DILATIONS = (1, 4, 16)
BAND = 128
ROPE_THETA = 10000.0
NORM_EPS = 1e-6
NEG_INF = -1e30
CHUNK = 128
C_GROUPS = 8
N_EXPERTS = 8
TOP_K = 2

LANES = 128
SUBLANES = 8
VMEM_LIMIT_BYTES = 56 * 1024 * 1024

AW = N_HEADS * HEAD_DIM
KVW = B_KV_HEADS * HEAD_DIM
HEADS_PER_VREG = LANES // HEAD_DIM
N_HEAD_BLOCKS = AW // LANES


def _cparams(*sem):
    return pltpu.CompilerParams(dimension_semantics=sem, vmem_limit_bytes=VMEM_LIMIT_BYTES)


def _rms_norm(x, g):
    return x * lax.rsqrt(jnp.mean(x * x, axis=-1, keepdims=True) + NORM_EPS) * g


def _dot(a, b):
    return jnp.dot(a, b, preferred_element_type=F32)


def _dot_nt(a, b):
    return lax.dot_general(a, b, (((1,), (1,)), ((), ())), preferred_element_type=F32)


_QA_BLOCKS = range(0, 4)
_KA_BLOCKS = range(4, 8)
_VA_BLOCKS = range(8, 12)
_QB_BLOCKS = range(12, 16)
_KB_BLOCK = 16
_VB_BLOCK = 17


def _proj_kernel(x_ref, g_ref, w_ref, gain_ref, cos_ref, sin_ref,
                 qa_ref, ka_ref, va_ref, qb_ref, kb_ref, vb_ref, *prompt_refs):
    stage = prompt_refs[-1] if prompt_refs else None
    h = _rms_norm(x_ref[...], g_ref[...]).astype(BF16)
    tm = h.shape[0]
    n_blocks = w_ref.shape[1] // LANES
    live = {}

    def proj_block(blk):
        group = blk // N_HEAD_BLOCKS
        if group not in live:
            live.clear()
            lo = group * N_HEAD_BLOCKS
            hi = min(lo + N_HEAD_BLOCKS, n_blocks)
            live[group] = _dot(h, w_ref[:, lo * LANES:hi * LANES])
        off = blk % N_HEAD_BLOCKS
        return live[group][:, off * LANES:(off + 1) * LANES]

    cos = cos_ref[...]
    sin = sin_ref[...]
    lane = lax.broadcasted_iota(jnp.int32, (tm, LANES), 1)
    first_half = (lane & HALF_DIM) == 0
    li = lax.broadcasted_iota(jnp.int32, (LANES, LANES), 0) // HEAD_DIM
    lj = lax.broadcasted_iota(jnp.int32, (LANES, LANES), 1) // HEAD_DIM
    head_mean = jnp.where(li == lj, 1.0 / HEAD_DIM, 0.0).astype(BF16)

    def norm_rope(blk):
        pc = proj_block(blk)
        ms = _dot((pc * pc).astype(BF16), head_mean)
        y = pc * lax.rsqrt(ms + NORM_EPS) * gain_ref[:, blk * LANES:(blk + 1) * LANES]
        partner = jnp.where(first_half, pltpu.roll(y, LANES - HALF_DIM, 1), pltpu.roll(y, HALF_DIM, 1))
        return y * cos + partner * sin

    scale = HEAD_DIM ** -0.5
    for j, blk in enumerate(_QA_BLOCKS):
        q = norm_rope(blk) * scale
        qa_ref[:, j * LANES:(j + 1) * LANES] = q.astype(qa_ref.dtype)
        if stage is not None:
            stage[j] = q
    for j, blk in enumerate(_KA_BLOCKS):
        k = norm_rope(blk)
        ka_ref[:, j * LANES:(j + 1) * LANES] = k
        if stage is not None:
            stage[N_HEAD_BLOCKS + j] = k
    for j, blk in enumerate(_VA_BLOCKS):
        v = proj_block(blk)
        va_ref[:, j * LANES:(j + 1) * LANES] = v
        if stage is not None:
            stage[2 * N_HEAD_BLOCKS + j] = v
    for j, blk in enumerate(_QB_BLOCKS):
        qb_ref[:, j * LANES:(j + 1) * LANES] = (norm_rope(blk) * scale).astype(qb_ref.dtype)
    kb_ref[...] = norm_rope(_KB_BLOCK)
    vb_ref[...] = proj_block(_VB_BLOCK)
    if prompt_refs:
        ka16_ref, va16_ref, kvb16_ref = prompt_refs[:3]
        ka16_ref[...] = ka_ref[...].astype(BF16)
        va16_ref[...] = va_ref[...].astype(BF16)
        kb = kb_ref[...]
        vb = vb_ref[...]
        parts = (kb, pltpu.roll(kb, HEAD_DIM, 1), vb, pltpu.roll(vb, HEAD_DIM, 1))
        for j, part in enumerate(parts):
            kvb16_ref[:, j * KVW:(j + 1) * KVW] = part.astype(BF16)
        for idx, r in enumerate(A_DILATIONS[1:]):
            for which, dst in enumerate(prompt_refs[3 + 3 * idx:6 + 3 * idx]):
                for c in range(r):
                    for cb in range(N_HEAD_BLOCKS):
                        rows = stage[which * N_HEAD_BLOCKS + cb, pl.ds(c, tm // r, stride=r), :]
                        dst[c, :, cb * LANES:(cb + 1) * LANES] = rows.astype(BF16)


def _qkv_proj(x, g, w, gain_row, cos, sin, *, tm, q_dtype, seq_len=None):
    t, d = x.shape
    n_out = w.shape[1]
    row = lambda i: (i, 0)
    fixed = lambda i: (0, 0)
    sds = jax.ShapeDtypeStruct
    out_specs = [pl.BlockSpec((tm, wd), row) for wd in (AW, AW, AW, AW, KVW, KVW)]
    out_shape = [sds((t, wd), dt) for wd, dt in zip((AW, AW, AW, AW, KVW, KVW), (q_dtype, F32, F32, q_dtype, F32, F32))]
    scratch = []
    if seq_len is not None:
        out_specs += [pl.BlockSpec((tm, wd), row) for wd in (AW, AW, 4 * KVW)]
        out_shape += [sds((t, wd), BF16) for wd in (AW, AW, 4 * KVW)]
        tiles = seq_len // tm
        for r in A_DILATIONS[1:]:
            assert seq_len % tm == 0 and tm % (r * 16) == 0
            out_specs += [pl.BlockSpec((None, r, tm // r, AW), lambda i: (i // tiles, 0, i % tiles, 0))] * 3
            out_shape += [sds((t // seq_len, r, seq_len // r, AW), BF16)] * 3
        scratch = [pltpu.VMEM((3 * N_HEAD_BLOCKS, tm, LANES), F32)]
    return pl.pallas_call(
        _proj_kernel,
        grid=(t // tm,),
        in_specs=[pl.BlockSpec((tm, d), row), pl.BlockSpec((1, d), fixed), pl.BlockSpec((d, n_out), fixed),
                  pl.BlockSpec((1, n_out), fixed), pl.BlockSpec((tm, LANES), row), pl.BlockSpec((tm, LANES), row)],
        out_specs=out_specs, out_shape=out_shape, scratch_shapes=scratch,
        compiler_params=_cparams("arbitrary"),
        name="qkv_proj",
    )(x, g, w, gain_row, cos, sin)


def _fill_kv(kbuf, vbuf, kp_ref, kc_ref, vp_ref, vc_ref):
    kbuf[0:BAND, :] = kp_ref[...].astype(BF16)
    kbuf[BAND:, :] = kc_ref[...].astype(BF16)
    vbuf[0:BAND, :] = vp_ref[...].astype(BF16)
    vbuf[BAND:, :] = vc_ref[...].astype(BF16)


def _band_masks(rows, first_block):
    assert BAND == LANES
    row = lax.broadcasted_iota(jnp.int32, (rows, BAND), 0) & (BAND - 1)
    col = lax.broadcasted_iota(jnp.int32, (rows, BAND), 1)
    shifted = row + jnp.where(first_block, 2 * BAND, 0)
    return col > shifted, col == shifted, col <= row


def _banded_softmax_pv(qm, kp, kc, vp, vc, masks, floor_row=None):
    upper, diag, lower = masks
    s_prev = _dot_nt(qm, kp)
    s = jnp.where(upper, s_prev, jnp.where(lower, _dot_nt(qm, kc), NEG_INF))
    back = jnp.where(diag, s_prev, NEG_INF if floor_row is None else floor_row)
    m = jnp.max(jnp.maximum(s, back), axis=-1, keepdims=True)
    p = jnp.exp(s - m)
    p_back = jnp.where(diag, jnp.exp(s_prev - m), 0.0)
    l = jnp.sum(p + p_back, axis=-1, keepdims=True)
    p_prev = jnp.where(upper, p, p_back).astype(BF16)
    p_own = jnp.where(lower, p, 0.0).astype(BF16)
    return m, l, _dot(p_prev, vp) + _dot(p_own, vc)


def _attn_a_kernel(q_ref, kc_ref, kp_ref, vc_ref, vp_ref, o_ref, lse_ref, kbuf, vbuf, *, tq):
    blk = pl.program_id(2)
    _fill_kv(kbuf, vbuf, kp_ref, kc_ref, vp_ref, vc_ref)
    lane = lax.broadcasted_iota(jnp.int32, (BAND, LANES), 1)
    low_head = lane < HEAD_DIM

    def sub_block(sb, carry):
        r0 = pl.multiple_of(sb * BAND, BAND)
        masks = _band_masks(HEADS_PER_VREG * BAND, (blk + sb) == 0)
        lse_tile = jnp.zeros((BAND, LANES), F32)
        for hb in range(N_HEAD_BLOCKS):
            cs = slice(hb * LANES, (hb + 1) * LANES)
            q2 = q_ref[pl.ds(r0, BAND), cs]
            zero = jnp.zeros_like(q2)
            qm = jnp.concatenate([jnp.where(low_head, q2, zero), jnp.where(low_head, zero, q2)], axis=0)
            m, l, pv = _banded_softmax_pv(qm, kbuf[pl.ds(r0, BAND), cs], kbuf[pl.ds(r0 + BAND, BAND), cs],
                                          vbuf[pl.ds(r0, BAND), cs], vbuf[pl.ds(r0 + BAND, BAND), cs], masks)
            o = pv * (1.0 / l)
            o_ref[pl.ds(r0, BAND), cs] = jnp.where(low_head, o[:BAND], o[BAND:]).astype(o_ref.dtype)
            lse = m + jnp.log(l)
            for hh in range(HEADS_PER_VREG):
                lse_tile = jnp.where(lane == hb * HEADS_PER_VREG + hh, lse[hh * BAND:(hh + 1) * BAND], lse_tile)
        lse_ref[pl.ds(r0, BAND), :] = lse_tile
        return carry

    lax.fori_loop(0, tq // BAND, sub_block, 0)


def _attn_a_branch(q, k, v):
    n, r, sub_len, w = q.shape
    tq = min(1024, sub_len)
    cur = lambda b, c, i: (b, c, i, 0)
    prv = lambda b, c, i: (b, c, jnp.maximum(i * (tq // BAND) - 1, 0), 0)
    blk_cur = pl.BlockSpec((None, None, tq, w), cur)
    blk_prv = pl.BlockSpec((None, None, BAND, w), prv)
    sds = jax.ShapeDtypeStruct
    return pl.pallas_call(
        functools.partial(_attn_a_kernel, tq=tq),
        grid=(n, r, sub_len // tq),
        in_specs=[blk_cur, blk_cur, blk_prv, blk_cur, blk_prv],
        out_specs=[blk_cur, pl.BlockSpec((None, None, tq, LANES), cur)],
        out_shape=[sds((n, r, sub_len, w), BF16), sds((n, r, sub_len, LANES), F32)],
        scratch_shapes=[pltpu.VMEM((tq + BAND, w), BF16), pltpu.VMEM((tq + BAND, w), BF16)],
        compiler_params=_cparams("arbitrary", "arbitrary", "arbitrary"),
        name=f"attn_a_dil{r}",
    )(q, k, k, v, v)


_KVB_K, _KVB_K_SWAPPED, _KVB_V, _KVB_V_SWAPPED = range(4)
_KVB_WIDTH = 4 * KVW


def _attn_b_kernel(q_ref, kvc_ref, kvp_ref, sink_ref, o_ref, kvbuf, *, tq):
    blk = pl.program_id(1)
    kvbuf[0:BAND, :] = kvp_ref[...]
    kvbuf[BAND:, :] = kvc_ref[...]
    lane = lax.broadcasted_iota(jnp.int32, (BAND, LANES), 1)
    low_head = lane < HEAD_DIM
    group = N_HEADS // B_KV_HEADS
    plain = [h for h in range(N_HEADS) if h % HEADS_PER_VREG == h // group]
    swapped = [h for h in range(N_HEADS) if h % HEADS_PER_VREG != h // group]

    def sub_block(sb, carry):
        r0 = pl.multiple_of(sb * BAND, BAND)
        masks = _band_masks(len(plain) * BAND, (blk + sb) == 0)
        blocks = [q_ref[pl.ds(r0, BAND), hb * LANES:(hb + 1) * LANES] for hb in range(N_HEAD_BLOCKS)]
        outs = [None] * N_HEADS
        for heads, k_blk, v_blk in ((plain, _KVB_K, _KVB_V), (swapped, _KVB_K_SWAPPED, _KVB_V_SWAPPED)):
            ks = slice(k_blk * KVW, (k_blk + 1) * KVW)
            vs = slice(v_blk * KVW, (v_blk + 1) * KVW)
            q_parts, sink_parts = [], []
            for h in heads:
                q2 = blocks[h // HEADS_PER_VREG]
                own_half = low_head if h % HEADS_PER_VREG == 0 else jnp.logical_not(low_head)
                q_parts.append(jnp.where(own_half, q2, jnp.zeros_like(q2)))
                sink_parts.append(jnp.broadcast_to(sink_ref[h:h + 1, :], (BAND, LANES)))
            sink = jnp.concatenate(sink_parts, axis=0)
            m, l, pv = _banded_softmax_pv(jnp.concatenate(q_parts, axis=0),
                                          kvbuf[pl.ds(r0, BAND), ks], kvbuf[pl.ds(r0 + BAND, BAND), ks],
                                          kvbuf[pl.ds(r0, BAND), vs], kvbuf[pl.ds(r0 + BAND, BAND), vs],
                                          masks, floor_row=sink)
            o = pv * (1.0 / (l + jnp.exp(sink - m)))
            for idx, h in enumerate(heads):
                outs[h] = o[idx * BAND:(idx + 1) * BAND]
        for hb in range(N_HEAD_BLOCKS):
            o_ref[pl.ds(r0, BAND), hb * LANES:(hb + 1) * LANES] = jnp.where(
                low_head, outs[hb * HEADS_PER_VREG], outs[hb * HEADS_PER_VREG + 1]).astype(o_ref.dtype)
        return carry

    lax.fori_loop(0, tq // BAND, sub_block, 0)


def _attn_b(q, kv, sinks_rows):
    n, s, w = q.shape
    tq = min(1024, s)
    cur = lambda b, i: (b, i, 0)
    prv = lambda b, i: (b, jnp.maximum(i * (tq // BAND) - 1, 0), 0)
    return pl.pallas_call(
        functools.partial(_attn_b_kernel, tq=tq),
        grid=(n, s // tq),
        in_specs=[pl.BlockSpec((None, tq, w), cur), pl.BlockSpec((None, tq, _KVB_WIDTH), cur),
                  pl.BlockSpec((None, BAND, _KVB_WIDTH), prv),
                  pl.BlockSpec((N_HEADS, LANES), lambda b, i: (0, 0))],
        out_specs=pl.BlockSpec((None, tq, w), cur),
        out_shape=jax.ShapeDtypeStruct((n, s, w), BF16),
        scratch_shapes=[pltpu.VMEM((tq + BAND, _KVB_WIDTH), BF16)],
        compiler_params=_cparams("arbitrary", "arbitrary"),
        name="attn_b",
    )(q, kv, kv, sinks_rows)


NEW_ROWS = 8


def _branch_multiplicity(t_query, key_pos, la):
    back = la + t_query - key_pos
    mult = jnp.zeros(back.shape, F32)
    for r in A_DILATIONS:
        mult = mult + ((back >= 0) & (back <= BAND * r) & (back % r == 0)).astype(F32)
    return mult


def _sample_attn_kernel(qa_ref, kna_ref, vna_ref, qb_ref, knb_ref, vnb_ref,
                        kt_ref, vt_ref, cbk_ref, cbv_ref, sink_ref,
                        oa_ref, ob_ref, *, nb, t_new):
    la = kt_ref.shape[-1]
    t_c = lax.broadcasted_iota(jnp.int32, (NEW_ROWS, la), 0)
    t_c = jnp.where(t_c < t_new, t_c, 0)
    mult_c = _branch_multiplicity(t_c, lax.broadcasted_iota(jnp.int32, (NEW_ROWS, la), 1), la)
    t_n = lax.broadcasted_iota(jnp.int32, (NEW_ROWS, NEW_ROWS), 0)
    t_n = jnp.where(t_n < t_new, t_n, 0)
    key_n = lax.broadcasted_iota(jnp.int32, (NEW_ROWS, NEW_ROWS), 1)
    mult_n = jnp.where(key_n < t_new, _branch_multiplicity(t_n, la + key_n, la), 0.0)

    def head_attn_a(q_h, kt, vt, kn_h, vn_h):
        s_c = jnp.where(mult_c > 0.0, _dot(q_h, kt), NEG_INF)
        s_n = jnp.where(mult_n > 0.0, _dot_nt(q_h, kn_h), NEG_INF)
        m = jnp.maximum(jnp.max(s_c, axis=-1, keepdims=True), jnp.max(s_n, axis=-1, keepdims=True))
        p_c = jnp.exp(s_c - m) * mult_c
        p_n = jnp.exp(s_n - m) * mult_n
        l = jnp.sum(p_c, axis=-1, keepdims=True) + jnp.sum(p_n, axis=-1, keepdims=True)
        return (_dot_nt(p_c.astype(BF16), vt) + _dot(p_n.astype(BF16), vn_h)) / l

    group = N_HEADS // B_KV_HEADS
    lb = cbk_ref.shape[-1]
    t_b = lax.broadcasted_iota(jnp.int32, (group * NEW_ROWS, lb), 0) & (NEW_ROWS - 1)
    t_b = jnp.where(t_b < t_new, t_b, 0)
    valid_bc = lax.broadcasted_iota(jnp.int32, (group * NEW_ROWS, lb), 1) >= lb + t_b - BAND
    t_bn = lax.broadcasted_iota(jnp.int32, (group * NEW_ROWS, NEW_ROWS), 0) & (NEW_ROWS - 1)
    t_bn = jnp.where(t_bn < t_new, t_bn, 0)
    key_bn = lax.broadcasted_iota(jnp.int32, (group * NEW_ROWS, NEW_ROWS), 1)
    valid_bn = (key_bn <= t_bn) & (key_bn < t_new)

    def kv_head_attn_b(q_g, kt, vt, kn_g, vn_g, sink):
        s_c = jnp.where(valid_bc, _dot(q_g, kt), NEG_INF)
        s_n = jnp.where(valid_bn, _dot_nt(q_g, kn_g), NEG_INF)
        m = jnp.maximum(jnp.max(s_c, axis=-1, keepdims=True), jnp.max(s_n, axis=-1, keepdims=True))
        m = jnp.maximum(m, sink)
        p_c = jnp.exp(s_c - m[:, :lb])
        p_n = jnp.exp(s_n - m[:, :NEW_ROWS])
        l = (jnp.sum(p_c, axis=-1, keepdims=True) + jnp.sum(p_n, axis=-1, keepdims=True) + jnp.exp(sink - m))
        return (_dot_nt(p_c.astype(BF16), vt) + _dot(p_n.astype(BF16), vn_g)) / l[:, :HEAD_DIM]

    def one_sequence(n, carry):
        new0 = pl.multiple_of(n * NEW_ROWS, NEW_ROWS)
        qa = qa_ref[pl.ds(new0, NEW_ROWS), :].astype(BF16)
        kna = kna_ref[pl.ds(new0, NEW_ROWS), :].astype(BF16)
        vna = vna_ref[pl.ds(new0, NEW_ROWS), :].astype(BF16)
        heads_out = []
        for h in range(N_HEADS):
            hs = slice(h * HEAD_DIM, (h + 1) * HEAD_DIM)
            heads_out.append(head_attn_a(qa[:, hs], kt_ref[n, h].astype(BF16), vt_ref[n, h].astype(BF16),
                                         kna[:, hs], vna[:, hs]))
        oa_ref[pl.ds(new0, NEW_ROWS), :] = jnp.concatenate(heads_out, axis=1).astype(oa_ref.dtype)
        qb = qb_ref[pl.ds(new0, NEW_ROWS), :].astype(BF16)
        knb = knb_ref[pl.ds(new0, NEW_ROWS), :].astype(BF16)
        vnb = vnb_ref[pl.ds(new0, NEW_ROWS), :].astype(BF16)
        heads_b = []
        for g in range(B_KV_HEADS):
            heads = range(g * group, (g + 1) * group)
            q_g = jnp.concatenate([qb[:, h * HEAD_DIM:(h + 1) * HEAD_DIM] for h in heads], axis=0)
            sink = jnp.concatenate([jnp.broadcast_to(sink_ref[h:h + 1, :], (NEW_ROWS, LANES)) for h in heads], axis=0)
            gs = slice(g * HEAD_DIM, (g + 1) * HEAD_DIM)
            o_g = kv_head_attn_b(q_g, cbk_ref[n, g].astype(BF16), cbv_ref[n, g].astype(BF16), knb[:, gs], vnb[:, gs], sink)
            heads_b += [o_g[hh * NEW_ROWS:(hh + 1) * NEW_ROWS] for hh in range(group)]
        ob_ref[pl.ds(new0, NEW_ROWS), :] = jnp.concatenate(heads_b, axis=1).astype(ob_ref.dtype)
        return carry

    lax.fori_loop(0, nb, one_sequence, 0)


def _pad_new_rows(a, n_seq, t_new):
    w = a.shape[-1]
    a = a.reshape(n_seq, t_new, w)
    a = jnp.pad(a, ((0, 0), (0, NEW_ROWS - t_new), (0, 0)))
    return a.reshape(n_seq * NEW_ROWS, w)


def _sample_attn(qa, ka, va, qb, kb, vb, cache_a_k, cache_a_v, cache_b_k, cache_b_v, sink_rows, *, t_new):
    n_seq, la, nh, hd = cache_a_k.shape
    lb, nkv = cache_b_k.shape[1:3]
    w = nh * hd
    assert t_new <= NEW_ROWS
    nb = 2 if n_seq % 2 == 0 else 1
    padded = [_pad_new_rows(a, n_seq, t_new) for a in (qa, ka, va, qb, kb, vb)]
    dim_major = lambda c: jnp.transpose(c, (0, 2, 3, 1))
    new_a = pl.BlockSpec((nb * NEW_ROWS, w), lambda i: (i, 0))
    new_b = pl.BlockSpec((nb * NEW_ROWS, KVW), lambda i: (i, 0))
    spec_ca = pl.BlockSpec((nb, nh, hd, la), lambda i: (i, 0, 0, 0))
    spec_cb = pl.BlockSpec((nb, nkv, hd, lb), lambda i: (i, 0, 0, 0))
    sds = jax.ShapeDtypeStruct
    oa, ob = pl.pallas_call(
        functools.partial(_sample_attn_kernel, nb=nb, t_new=t_new),
        grid=(n_seq // nb,),
        in_specs=[new_a, new_a, new_a, new_a, new_b, new_b, spec_ca, spec_ca, spec_cb, spec_cb,
                  pl.BlockSpec((N_HEADS, LANES), lambda i: (0, 0))],
        out_specs=[new_a, new_a],
        out_shape=[sds((n_seq * NEW_ROWS, w), BF16), sds((n_seq * NEW_ROWS, w), BF16)],
        compiler_params=_cparams("arbitrary"),
        name="sample_attn",
    )(*padded, dim_major(cache_a_k), dim_major(cache_a_v), dim_major(cache_b_k), dim_major(cache_b_v), sink_rows)
    unpad = lambda o: o.reshape(n_seq, NEW_ROWS, w)[:, :t_new].reshape(n_seq * t_new, w)
    return unpad(oa), unpad(ob)


def _silu(a):
    return a * (1.0 / (1.0 + jnp.exp(-a)))


def _natural_order(o_ref, lse_ref, o_stage, lse_stage):
    r, sub, _ = o_ref.shape
    if r == 1:
        return o_ref[0].astype(F32), lse_ref[0]
    for c in range(r):
        lse_stage[pl.ds(c, sub, stride=r), :] = lse_ref[c]
        for cb in range(N_HEAD_BLOCKS):
            o_stage[cb, pl.ds(c, sub, stride=r), :] = o_ref[c, :, cb * LANES:(cb + 1) * LANES].astype(F32)
    return jnp.concatenate([o_stage[cb] for cb in range(N_HEAD_BLOCKS)], axis=1), lse_stage[...]


def _merge_branches(outs, lses):
    top = functools.reduce(jnp.maximum, lses)
    ws = [jnp.exp(l - top) for l in lses]
    inv_den = 1.0 / functools.reduce(lambda a, b: a + b, ws)
    head = lax.broadcasted_iota(jnp.int32, (LANES, AW), 0)
    lane_head = lax.broadcasted_iota(jnp.int32, (LANES, AW), 1) // HEAD_DIM
    expand = (head == lane_head).astype(BF16)
    oa = None
    for w, o in zip(ws, outs):
        w = w * inv_den
        w_hi = w.astype(BF16)
        w_lo = (w - w_hi.astype(F32)).astype(BF16)
        part = (_dot(w_hi, expand) + _dot(w_lo, expand)) * o
        oa = part if oa is None else oa + part
    return oa.astype(BF16)


def _mix_out_ffn_kernel(x_ref, *refs, n_branches):
    if n_branches:
        branch_refs, refs = refs[:2 * n_branches], refs[2 * n_branches:]
        refs, (o_stage, lse_stage) = refs[:-2], refs[-2:]
    else:
        oa_ref, refs = refs[0], refs[1:]
    ob_ref, wo_ref, g_ref, w1_ref, w3_ref, w2_ref, out_ref, x1_ref, h_ref, acc_ref = refs
    j = pl.program_id(1)

    @pl.when(j == 0)
    def _():
        if n_branches:
            outs, lses = zip(*[_natural_order(branch_refs[2 * i], branch_refs[2 * i + 1], o_stage, lse_stage)
                               for i in range(n_branches)])
            oa = _merge_branches(outs, lses)
        else:
            oa = oa_ref[...]
        x1 = x_ref[...] + _dot(oa, wo_ref[0:AW, :]) + _dot(ob_ref[...], wo_ref[AW:, :])
        x1_ref[...] = x1
        h_ref[...] = _rms_norm(x1, g_ref[...]).astype(BF16)
        acc_ref[...] = jnp.zeros_like(acc_ref)

    h = h_ref[...]
    gate = (_silu(_dot(h, w1_ref[...])) * _dot(h, w3_ref[...])).astype(BF16)
    acc_ref[...] += _dot(gate, w2_ref[...])

    @pl.when(j == pl.num_programs(1) - 1)
    def _():
        out_ref[...] = x1_ref[...] + acc_ref[...]


def _ff_tile(ff, target):
    best = LANES
    for k in range(1, ff // LANES + 1):
        if ff % (k * LANES) == 0 and k * LANES <= target:
            best = k * LANES
    return best


def _mix_out_ffn(x, oa, ob, wo, g, w1, w3, w2, *, tm):
    t, d = x.shape
    ff = w1.shape[1]
    tf = ff
    row = lambda i, j: (i, 0)
    fixed = lambda i, j: (0, 0)
    once = pl.Buffered(1)
    if isinstance(oa, (list, tuple)):
        n_branches = len(oa)
        tiles = t // oa[0][0].shape[0] // tm
        a_specs, a_args = [], []
        for o, lse in oa:
            r = o.shape[1]
            assert tm % (r * SUBLANES) == 0
            tile = lambda i, j: (i // tiles, 0, i % tiles, 0)
            a_specs += [pl.BlockSpec((None, r, tm // r, AW), tile), pl.BlockSpec((None, r, tm // r, LANES), tile)]
            a_args += [o, lse]
        stage = [pltpu.VMEM((N_HEAD_BLOCKS, tm, LANES), F32), pltpu.VMEM((tm, LANES), F32)]
    else:
        n_branches, a_specs, a_args, stage = 0, [pl.BlockSpec((tm, AW), row)], [oa], []
    return pl.pallas_call(
        functools.partial(_mix_out_ffn_kernel, n_branches=n_branches),
        grid=(t // tm, ff // tf),
        in_specs=[pl.BlockSpec((tm, d), row)] + a_specs
                 + [pl.BlockSpec((tm, AW), row),
                  pl.BlockSpec((2 * AW, d), fixed, pipeline_mode=once), pl.BlockSpec((1, d), fixed),
                  pl.BlockSpec((d, tf), lambda i, j: (0, j), pipeline_mode=once),
                  pl.BlockSpec((d, tf), lambda i, j: (0, j), pipeline_mode=once),
                  pl.BlockSpec((tf, d), lambda i, j: (j, 0), pipeline_mode=once)],
        out_specs=pl.BlockSpec((tm, d), row),
        out_shape=jax.ShapeDtypeStruct((t, d), F32),
        scratch_shapes=[pltpu.VMEM((tm, d), F32), pltpu.VMEM((tm, d), BF16), pltpu.VMEM((tm, d), F32)] + stage,
        compiler_params=_cparams("arbitrary", "arbitrary"),
        name="mix_out_ffn",
    )(x, *a_args, ob, wo, g, w1, w3, w2)


def _gelu_exact(z):
    return 0.5 * z * (1.0 + lax.erf(z * (2.0 ** -0.5)))


def _gmlp_kernel(x_ref, g_ref, win_ref, lng_ref, lnb_ref, mix_ref, bias_ref, wout_ref, gffn_ref, wr_ref,
                 *out_refs, write_v):
    if write_v:
        xo_ref, h2_ref, logit_ref, v_ref = out_refs[:4]
    else:
        xo_ref, h2_ref, logit_ref = out_refs[:3]
    gate_ref = out_refs[-1]
    x = x_ref[...]
    tm, cd = x.shape
    h = _rms_norm(x, g_ref[...]).astype(BF16)
    z = _gelu_exact(_dot(h, win_ref[...]))
    u = z[:, :cd]
    v = z[:, cd:]
    mu = jnp.mean(v, axis=-1, keepdims=True)
    vc = v - mu
    v = vc * lax.rsqrt(jnp.mean(vc * vc, axis=-1, keepdims=True) + NORM_EPS) * lng_ref[...] + lnb_ref[...]
    if write_v:
        v_ref[...] = v
    vb = v.astype(BF16)
    bias = bias_ref[...]
    gw = cd // C_GROUPS
    for c in range(tm // CHUNK):
        rs = slice(c * CHUNK, (c + 1) * CHUNK)
        for g in range(C_GROUPS):
            cs = slice(g * gw, (g + 1) * gw)
            f = _dot(mix_ref[g], vb[rs, cs]) + bias[:, cs]
            gate_ref[rs, cs] = (u[rs, cs] * f).astype(BF16)
    xo = x + _dot(gate_ref[...], wout_ref[...])
    xo_ref[...] = xo
    h2 = _rms_norm(xo, gffn_ref[...])
    h2_ref[...] = h2
    h_hi = h2.astype(BF16)
    h_lo = (h2 - h_hi.astype(F32)).astype(BF16)
    wr = wr_ref[...]
    both = _dot(h_hi, wr)
    logit_ref[...] = both[:, :LANES] + both[:, LANES:] + _dot(h_lo, wr[:, :LANES])


def _gmlp_block(x, g, w_in, ln_g, ln_b, mix, bias_full, w_out, g_ffn, wr_hi_lo, *, tm, write_v):
    t, d = x.shape
    cd = w_out.shape[0]
    row = lambda i: (i, 0)
    fixed = lambda i: (0, 0)
    sds = jax.ShapeDtypeStruct
    out_specs = [pl.BlockSpec((tm, d), row), pl.BlockSpec((tm, d), row), pl.BlockSpec((tm, LANES), row)]
    out_shape = [sds((t, d), F32), sds((t, d), F32), sds((t, LANES), F32)]
    if write_v:
        out_specs.append(pl.BlockSpec((tm, cd), row))
        out_shape.append(sds((t, cd), F32))
    return pl.pallas_call(
        functools.partial(_gmlp_kernel, write_v=write_v),
        grid=(t // tm,),
        in_specs=[pl.BlockSpec((tm, d), row), pl.BlockSpec((1, d), fixed), pl.BlockSpec((d, 2 * cd), fixed),
                  pl.BlockSpec((1, cd), fixed), pl.BlockSpec((1, cd), fixed),
                  pl.BlockSpec((C_GROUPS, CHUNK, CHUNK), lambda i: (0, 0, 0)),
                  pl.BlockSpec((CHUNK, cd), fixed), pl.BlockSpec((cd, d), fixed), pl.BlockSpec((1, d), fixed),
                  pl.BlockSpec((d, 2 * LANES), fixed)],
        out_specs=out_specs, out_shape=out_shape,
        scratch_shapes=[pltpu.VMEM((tm, cd), BF16)],
        compiler_params=_cparams("arbitrary"),
        name="gmlp_block",
    )(x, g, w_in, ln_g, ln_b, mix, bias_full, w_out, g_ffn, wr_hi_lo)


MOE_BLOCK = 1024
MOE_SUB = 256
DISPATCH_ROWS = 512


def _dispatch_kernel(dest_ref, h_ref, xs_in_ref, xs_ref, sem):
    del xs_in_ref
    rows = h_ref.shape[0]

    def issue(g, carry):
        for s in range(SUBLANES):
            r = g * SUBLANES + s
            for k in range(TOP_K):
                d = dest_ref[0, 0, r * TOP_K + k]
                pltpu.make_async_copy(h_ref.at[pl.ds(r, 1), :], xs_ref.at[pl.ds(d, 1), :], sem).start(priority=k)
        return carry

    lax.fori_loop(0, rows // SUBLANES, issue, 0)
    for _ in range(TOP_K):
        pltpu.make_async_copy(h_ref, xs_ref.at[pl.ds(0, rows), :], sem).wait()


def _dispatch(h, dest, xs):
    t, d = h.shape
    tg = DISPATCH_ROWS
    return pl.pallas_call(
        _dispatch_kernel,
        grid=(t // tg,),
        in_specs=[pl.BlockSpec((1, 1, tg * TOP_K), lambda i: (i, 0, 0), memory_space=pltpu.SMEM),
                  pl.BlockSpec((tg, d), lambda i: (i, 0)),
                  pl.BlockSpec(memory_space=pl.ANY)],
        out_specs=pl.BlockSpec(memory_space=pl.ANY),
        out_shape=jax.ShapeDtypeStruct(xs.shape, xs.dtype),
        scratch_shapes=[pltpu.SemaphoreType.DMA],
        input_output_aliases={2: 0},
        compiler_params=_cparams("arbitrary"),
        name="moe_dispatch",
    )(dest.reshape(t // tg, 1, tg * TOP_K), h, xs)


def _expert_kernel(be_ref, na_ref, ns_ref, xs_ref, w1_ref, w3_ref, w2_ref, y_ref, h_ref, acc_ref):
    del be_ref, na_ref
    b = pl.program_id(0)
    j = pl.program_id(1)
    n_sub = ns_ref[b]

    @pl.when((n_sub > 0) & (j == 0))
    def _():
        h_ref[...] = xs_ref[...].astype(BF16)
        acc_ref[...] = jnp.zeros_like(acc_ref)

    for k in range(1, MOE_BLOCK // MOE_SUB + 1):
        @pl.when(n_sub == k)
        def _(rows=k * MOE_SUB):
            h = h_ref[0:rows, :]
            w1 = w1_ref[...].astype(BF16)
            w3 = w3_ref[...].astype(BF16)
            gate = (_silu(_dot(h, w1)) * _dot(h, w3)).astype(BF16)
            acc_ref[0:rows, :] += _dot(gate, w2_ref[...].astype(BF16))

    @pl.when(j == pl.num_programs(1) - 1)
    def _():
        y_ref[...] = jnp.where(n_sub > 0, acc_ref[...], 0.0)


def _experts(xs, block_expert, n_active, n_sub, w1, w3, w2):
    n_rows, d = xs.shape
    ff = w1.shape[2]
    tf = _ff_tile(ff, 512)
    n_blocks = n_rows // MOE_BLOCK
    nj = ff // tf
    def wcol(b, j, be, na, ns):
        live = b < na[0]
        return (be[b], 0, jnp.where(live, j, nj - 1))

    def wrow(b, j, be, na, ns):
        live = b < na[0]
        return (be[b], jnp.where(live, j, nj - 1), 0)

    def xrow(b, j, be, na, ns):
        return (jnp.minimum(b, na[0] - 1), 0)

    grid_spec = pltpu.PrefetchScalarGridSpec(
        num_scalar_prefetch=3,
        grid=(n_blocks, nj),
        in_specs=[pl.BlockSpec((MOE_BLOCK, d), xrow),
                  pl.BlockSpec((None, d, tf), wcol), pl.BlockSpec((None, d, tf), wcol),
                  pl.BlockSpec((None, tf, d), wrow)],
        out_specs=pl.BlockSpec((MOE_BLOCK, d), lambda b, j, be, na, ns: (b, 0)),
        scratch_shapes=[pltpu.VMEM((MOE_BLOCK, d), BF16), pltpu.VMEM((MOE_BLOCK, d), F32)],
    )
    return pl.pallas_call(
        _expert_kernel,
        grid_spec=grid_spec,
        out_shape=jax.ShapeDtypeStruct((n_rows, d), F32),
        compiler_params=_cparams("arbitrary", "arbitrary"),
        name="moe_experts",
    )(block_expert, n_active, n_sub, xs, w1, w3, w2)


def _combine_kernel(dest_ref, x_ref, gate_ref, y_ref, out_ref, buf, sem):
    rows = x_ref.shape[0]

    def issue(g, carry):
        for s in range(SUBLANES):
            r = g * SUBLANES + s
            for k in range(TOP_K):
                d = dest_ref[0, 0, r * TOP_K + k]
                pltpu.make_async_copy(y_ref.at[pl.ds(d, 1), :], buf.at[k, pl.ds(r, 1), :], sem).start(priority=k)
        return carry

    lax.fori_loop(0, rows // SUBLANES, issue, 0)
    for k in range(TOP_K):
        pltpu.make_async_copy(y_ref.at[pl.ds(0, rows), :], buf.at[k], sem).wait()
    gates = gate_ref[...]
    y = buf[0] * gates[:, 0:1]
    for k in range(1, TOP_K):
        y = y + buf[k] * gates[:, k:k + 1]
    out_ref[...] = x_ref[...] + y


def _combine(x, gates, dest, y):
    t, d = x.shape
    tg = DISPATCH_ROWS
    return pl.pallas_call(
        _combine_kernel,
        grid=(t // tg,),
        in_specs=[pl.BlockSpec((1, 1, tg * TOP_K), lambda i: (i, 0, 0), memory_space=pltpu.SMEM),
                  pl.BlockSpec((tg, d), lambda i: (i, 0)),
                  pl.BlockSpec((tg, TOP_K), lambda i: (i, 0)),
                  pl.BlockSpec(memory_space=pl.ANY)],
        out_specs=pl.BlockSpec((tg, d), lambda i: (i, 0)),
        out_shape=jax.ShapeDtypeStruct((t, d), F32),
        scratch_shapes=[pltpu.VMEM((TOP_K, tg, d), F32), pltpu.SemaphoreType.DMA],
        compiler_params=_cparams("arbitrary"),
        name="moe_combine",
    )(dest.reshape(t // tg, 1, tg * TOP_K), x, gates, y)


def _route(logits):
    t = logits.shape[0]
    top_val, top_idx = lax.top_k(logits, TOP_K)
    gates = jax.nn.softmax(top_val, axis=-1)
    expert = top_idx.reshape(-1)
    one_hot = (expert[:, None] == jnp.arange(N_EXPERTS, dtype=expert.dtype)[None, :]).astype(jnp.int32)
    running = jnp.cumsum(one_hot, axis=0)
    rank = jnp.sum(running * one_hot, axis=1) - 1
    counts = running[-1]
    padded = (counts + MOE_BLOCK - 1) // MOE_BLOCK * MOE_BLOCK
    pad_ends = jnp.cumsum(padded)
    pad_starts = pad_ends - padded
    dest = (jnp.sum(pad_starts[None, :] * one_hot, axis=1) + rank).astype(jnp.int32)
    n_blocks = -(-(t * TOP_K) // MOE_BLOCK) + N_EXPERTS
    block_start = jnp.arange(n_blocks, dtype=jnp.int32) * MOE_BLOCK
    block_expert = jnp.sum((block_start[:, None] >= pad_ends[None, :]).astype(jnp.int32), axis=1)
    block_expert = jnp.minimum(block_expert, N_EXPERTS - 1).astype(jnp.int32)
    used_end = (pad_starts + counts)[block_expert]
    used_rows = jnp.clip(used_end - block_start, 0, MOE_BLOCK)
    used_rows = jnp.where(block_start < pad_ends[-1], used_rows, 0)
    n_sub = ((used_rows + MOE_SUB - 1) // MOE_SUB).astype(jnp.int32)
    n_active = (pad_ends[-1:] // MOE_BLOCK).astype(jnp.int32)
    return gates, dest.reshape(t, TOP_K), block_expert, n_sub, n_active, n_blocks * MOE_BLOCK


def _moe(groups, w1, w3, w2):
    sizes = [x.shape[0] for x, _, _ in groups]
    gates, dest, block_expert, n_sub, n_active, n_rows = _route(jnp.concatenate([l for _, _, l in groups]))
    xs = jnp.zeros((n_rows, w1.shape[1]), F32)
    start = 0
    for (_, h2, _), size in zip(groups, sizes):
        xs = _dispatch(h2, dest[start:start + size].reshape(-1), xs)
        start += size
    y = _experts(xs, block_expert, n_active, n_sub, w1, w3, w2)
    outs, start = [], 0
    for (x, _, _), size in zip(groups, sizes):
        outs.append(_combine(x, gates[start:start + size], dest[start:start + size].reshape(-1), y))
        start += size
    return outs


def _rope_tables(pos):
    inv_freq = ROPE_THETA ** (-jnp.arange(HALF_DIM, dtype=F32) / HALF_DIM)
    ang = pos.astype(F32)[:, None] * inv_freq[None, :]
    cos, sin = jnp.cos(ang), jnp.sin(ang)
    reps = LANES // HEAD_DIM
    return jnp.tile(cos, (1, 2 * reps)), jnp.tile(jnp.concatenate([-sin, sin], axis=1), (1, reps))


def _qk_gain_row(qk_gain):
    ones_a = jnp.ones((AW,), F32)
    ones_b = jnp.ones((KVW,), F32)
    return jnp.concatenate([jnp.tile(qk_gain[0], N_HEADS), jnp.tile(qk_gain[1], N_HEADS), ones_a,
                            jnp.tile(qk_gain[2], N_HEADS), jnp.tile(qk_gain[3], B_KV_HEADS), ones_b])[None, :]


def _row_block(t):
    return 512 if t % 512 == 0 else t


def kernel(x_prompt, x_sample, cache_a_k, cache_a_v, cache_b_k, cache_b_v, norm_mix, norm_ffn, ab_w_in,
           ab_qk_gain, ab_sinks, ab_w_out, ffn_w1, ffn_w3, ffn_w2, c_w_in, c_ln_g, c_ln_b, c_w_s, c_b_s,
           c_w_out, moe_router, moe_w1, moe_w3, moe_w2):
    n_p, s_p, d = x_prompt.shape
    n_s, t_new, _ = x_sample.shape
    depth = norm_mix.shape[0]
    past_len = 8192
    xp = x_prompt.reshape(n_p * s_p, d)
    xs = x_sample.reshape(n_s * t_new, d)
    pos_p = jnp.tile(jnp.arange(s_p), n_p)
    pos_s = jnp.tile(past_len + jnp.arange(t_new), n_s)
    cos_p, sin_p = _rope_tables(pos_p)
    cos_s, sin_s = _rope_tables(pos_s)
    row = lambda a: a[None, :]
    akp, avp, bkp, bvp, aks, avs, bks, bvs, cvs = [], [], [], [], [], [], [], [], []
    la_p, lb_p = min(BAND * A_DILATIONS[-1], s_p), min(BAND, s_p)
    for layer in range(depth):
        i = layer // 2
        if layer % 2 == 0:
            w_in = ab_w_in[i].astype(BF16)
            gain_row = _qk_gain_row(ab_qk_gain[i])
            w_out = ab_w_out[i].astype(BF16)
            w1, w3, w2 = ffn_w1[i].astype(BF16), ffn_w3[i].astype(BF16), ffn_w2[i].astype(BF16)
            sinks = ab_sinks[i].astype(F32)
            sink_rows = jnp.broadcast_to(sinks[:, None], (N_HEADS, LANES))
            qa, ka, va, qb, kb, vb, ka16, va16, kvb16, *dilated = _qkv_proj(
                xp, row(norm_mix[layer]), w_in, gain_row, cos_p, sin_p,
                tm=_row_block(s_p), q_dtype=BF16, seq_len=s_p)
            seq = lambda a: a.reshape(n_p, s_p, a.shape[-1])
            qkv_by_branch = [[seq(a)[:, None] for a in (qa, ka16, va16)]] + [dilated[3 * i:3 * i + 3]
                                                                            for i in range(len(A_DILATIONS) - 1)]
            branches = [_attn_a_branch(q, k, v) for q, k, v in qkv_by_branch]
            ob = _attn_b(seq(qb), seq(kvb16), sink_rows).reshape(n_p * s_p, AW)
            xp = _mix_out_ffn(xp, branches, ob, w_out, row(norm_ffn[layer]), w1, w3, w2, tm=_row_block(s_p))
            tail = lambda a, rows, nh: seq(a)[:, s_p - rows:].reshape(n_p, rows, nh, HEAD_DIM)
            akp.append(tail(ka, la_p, N_HEADS))
            avp.append(tail(va, la_p, N_HEADS))
            bkp.append(tail(kb, lb_p, B_KV_HEADS))
            bvp.append(tail(vb, lb_p, B_KV_HEADS))
            qa, ka, va, qb, kb, vb = _qkv_proj(xs, row(norm_mix[layer]), w_in, gain_row, cos_s, sin_s,
                                               tm=_row_block(xs.shape[0]), q_dtype=F32)
            oa, ob = _sample_attn(qa, ka, va, qb, kb, vb, cache_a_k[i], cache_a_v[i],
                                  cache_b_k[i], cache_b_v[i], sink_rows, t_new=t_new)
            xs = _mix_out_ffn(xs, oa, ob, w_out, row(norm_ffn[layer]), w1, w3, w2, tm=_row_block(xs.shape[0]))
            heads_s = lambda a, nh: a.reshape(n_s, t_new, nh, HEAD_DIM)
            aks.append(heads_s(ka, N_HEADS))
            avs.append(heads_s(va, N_HEADS))
            bks.append(heads_s(kb, B_KV_HEADS))
            bvs.append(heads_s(vb, B_KV_HEADS))
        else:
            w_in = c_w_in[i].astype(BF16)
            w_out = c_w_out[i].astype(BF16)
            tril = jnp.tril(jnp.ones((CHUNK, CHUNK), F32))
            mix_p = (c_w_s[i] * tril).astype(BF16)
            gw = w_out.shape[0] // C_GROUPS
            bias_p = jnp.repeat(c_b_s[i].T, gw, axis=1)
            per_tile = CHUNK // t_new
            mix_s = jnp.einsum("ab,gij->gaibj", jnp.eye(per_tile, dtype=F32), (c_w_s[i] * tril)[:, :t_new, :t_new])
            mix_s = mix_s.reshape(C_GROUPS, CHUNK, CHUNK).astype(BF16)
            bias_s = jnp.tile(bias_p[:t_new], (per_tile, 1))
            wr = moe_router[i]
            wr_hi = wr.astype(BF16)
            wr_lo = (wr - wr_hi.astype(F32)).astype(BF16)
            pad_e = lambda a: jnp.pad(a, ((0, 0), (0, LANES - N_EXPERTS)))
            wr_hi_lo = jnp.concatenate([pad_e(wr_hi), pad_e(wr_lo)], axis=1)
            common = (row(c_ln_g[i]), row(c_ln_b[i]))
            xp1, hp, lp = _gmlp_block(xp, row(norm_mix[layer]), w_in, *common, mix_p, bias_p, w_out,
                                      row(norm_ffn[layer]), wr_hi_lo, tm=_row_block(xp.shape[0]), write_v=False)
            xs1, hs, ls, v_new = _gmlp_block(xs, row(norm_mix[layer]), w_in, *common, mix_s, bias_s, w_out,
                                             row(norm_ffn[layer]), wr_hi_lo, tm=_row_block(xs.shape[0]),
                                             write_v=True)
            cvs.append(v_new.reshape(n_s, t_new, v_new.shape[-1]))
            xp, xs = _moe([(xp1, hp, lp[:, :N_EXPERTS]), (xs1, hs, ls[:, :N_EXPERTS])],
                          moe_w1[i], moe_w3[i], moe_w2[i])
    return (xp.reshape(n_p, s_p, d), xs.reshape(n_s, t_new, d),
            jnp.stack(akp), jnp.stack(avp), jnp.stack(bkp), jnp.stack(bvp),
            jnp.stack(aks), jnp.stack(avs), jnp.stack(bks), jnp.stack(bvs),
            jnp.stack(cvs))
```

```python
import functools

import jax
import jax.numpy as jnp
from jax import lax
from jax.experimental import pallas as pl
from jax.experimental.pallas import tpu as pltpu

F32 = jnp.float32
BF16 = jnp.bfloat16

HEAD_DIM = 64
HALF_DIM = HEAD_DIM // 2
N_HEADS = 8
B_KV_HEADS = 2
A_DILATIONS = (1, 4, 16)
BAND = 128
ROPE_THETA = 10000.0
NORM_EPS = 1e-6
NEG_INF = -1e30
CHUNK = 128
C_GROUPS = 8
N_EXPERTS = 8
TOP_K = 2

LANES = 128
SUBLANES = 8
VMEM_LIMIT_BYTES = 56 * 1024 * 1024

AW = N_HEADS * HEAD_DIM
KVW = B_KV_HEADS * HEAD_DIM
HEADS_PER_VREG = LANES // HEAD_DIM
N_HEAD_BLOCKS = AW // LANES


def _cparams(*sem):
    return pltpu.CompilerParams(dimension_semantics=sem, vmem_limit_bytes=VMEM_LIMIT_BYTES)


def _rms_norm(x, g):
    return x * lax.rsqrt(jnp.mean(x * x, axis=-1, keepdims=True) + NORM_EPS) * g


def _dot(a, b):
    return jnp.dot(a, b, preferred_element_type=F32)


def _dot_nt(a, b):
    return lax.dot_general(a, b, (((1,), (1,)), ((), ())), preferred_element_type=F32)


_QA_BLOCKS = range(0, 4)
_KA_BLOCKS = range(4, 8)
_VA_BLOCKS = range(8, 12)
_QB_BLOCKS = range(12, 16)
_KB_BLOCK = 16
_VB_BLOCK = 17


def _proj_kernel(x_ref, g_ref, w_ref, gain_ref, cos_ref, sin_ref,
                 qa_ref, ka_ref, va_ref, qb_ref, kb_ref, vb_ref, *prompt_refs):
    stage = prompt_refs[-1] if prompt_refs else None
    h = _rms_norm(x_ref[...], g_ref[...]).astype(BF16)
    tm = h.shape[0]
    n_blocks = w_ref.shape[1] // LANES
    live = {}

    def proj_block(blk):
        group = blk // N_HEAD_BLOCKS
        if group not in live:
            live.clear()
            lo = group * N_HEAD_BLOCKS
            hi = min(lo + N_HEAD_BLOCKS, n_blocks)
            live[group] = _dot(h, w_ref[:, lo * LANES:hi * LANES])
        off = blk % N_HEAD_BLOCKS
        return live[group][:, off * LANES:(off + 1) * LANES]

    cos = cos_ref[...]
    sin = sin_ref[...]
    lane = lax.broadcasted_iota(jnp.int32, (tm, LANES), 1)
    first_half = (lane & HALF_DIM) == 0
    li = lax.broadcasted_iota(jnp.int32, (LANES, LANES), 0) // HEAD_DIM
    lj = lax.broadcasted_iota(jnp.int32, (LANES, LANES), 1) // HEAD_DIM
    head_mean = jnp.where(li == lj, 1.0 / HEAD_DIM, 0.0).astype(BF16)

    def norm_rope(blk):
        pc = proj_block(blk)
        ms = _dot((pc * pc).astype(BF16), head_mean)
        y = pc * lax.rsqrt(ms + NORM_EPS) * gain_ref[:, blk * LANES:(blk + 1) * LANES]
        partner = jnp.where(first_half, pltpu.roll(y, LANES - HALF_DIM, 1), pltpu.roll(y, HALF_DIM, 1))
        return y * cos + partner * sin

    scale = HEAD_DIM ** -0.5
    for j, blk in enumerate(_QA_BLOCKS):
        q = norm_rope(blk) * scale
        qa_ref[:, j * LANES:(j + 1) * LANES] = q.astype(qa_ref.dtype)
        if stage is not None:
            stage[j] = q
    for j, blk in enumerate(_KA_BLOCKS):
        k = norm_rope(blk)
        ka_ref[:, j * LANES:(j + 1) * LANES] = k
        if stage is not None:
            stage[N_HEAD_BLOCKS + j] = k
    for j, blk in enumerate(_VA_BLOCKS):
        v = proj_block(blk)
        va_ref[:, j * LANES:(j + 1) * LANES] = v
        if stage is not None:
            stage[2 * N_HEAD_BLOCKS + j] = v
    for j, blk in enumerate(_QB_BLOCKS):
        qb_ref[:, j * LANES:(j + 1) * LANES] = (norm_rope(blk) * scale).astype(qb_ref.dtype)
    kb_ref[...] = norm_rope(_KB_BLOCK)
    vb_ref[...] = proj_block(_VB_BLOCK)
    if prompt_refs:
        ka16_ref, va16_ref, kvb16_ref = prompt_refs[:3]
        ka16_ref[...] = ka_ref[...].astype(BF16)
        va16_ref[...] = va_ref[...].astype(BF16)
        kb = kb_ref[...]
        vb = vb_ref[...]
        parts = (kb, pltpu.roll(kb, HEAD_DIM, 1), vb, pltpu.roll(vb, HEAD_DIM, 1))
        for j, part in enumerate(parts):
            kvb16_ref[:, j * KVW:(j + 1) * KVW] = part.astype(BF16)
        for idx, r in enumerate(A_DILATIONS[1:]):
            for which, dst in enumerate(prompt_refs[3 + 3 * idx:6 + 3 * idx]):
                for c in range(r):
                    for cb in range(N_HEAD_BLOCKS):
                        rows = stage[which * N_HEAD_BLOCKS + cb, pl.ds(c, tm // r, stride=r), :]
                        dst[c, :, cb * LANES:(cb + 1) * LANES] = rows.astype(BF16)


def _qkv_proj(x, g, w, gain_row, cos, sin, *, tm, q_dtype, seq_len=None):
    t, d = x.shape
    n_out = w.shape[1]
    row = lambda i: (i, 0)
    fixed = lambda i: (0, 0)
    sds = jax.ShapeDtypeStruct
    out_specs = [pl.BlockSpec((tm, wd), row) for wd in (AW, AW, AW, AW, KVW, KVW)]
    out_shape = [sds((t, wd), dt) for wd, dt in zip((AW, AW, AW, AW, KVW, KVW), (q_dtype, F32, F32, q_dtype, F32, F32))]
    scratch = []
    if seq_len is not None:
        out_specs += [pl.BlockSpec((tm, wd), row) for wd in (AW, AW, 4 * KVW)]
        out_shape += [sds((t, wd), BF16) for wd in (AW, AW, 4 * KVW)]
        tiles = seq_len // tm
        for r in A_DILATIONS[1:]:
            assert seq_len % tm == 0 and tm % (r * 16) == 0
            out_specs += [pl.BlockSpec((None, r, tm // r, AW), lambda i: (i // tiles, 0, i % tiles, 0))] * 3
            out_shape += [sds((t // seq_len, r, seq_len // r, AW), BF16)] * 3
        scratch = [pltpu.VMEM((3 * N_HEAD_BLOCKS, tm, LANES), F32)]
    return pl.pallas_call(
        _proj_kernel,
        grid=(t // tm,),
        in_specs=[pl.BlockSpec((tm, d), row), pl.BlockSpec((1, d), fixed), pl.BlockSpec((d, n_out), fixed),
                  pl.BlockSpec((1, n_out), fixed), pl.BlockSpec((tm, LANES), row), pl.BlockSpec((tm, LANES), row)],
        out_specs=out_specs, out_shape=out_shape, scratch_shapes=scratch,
        compiler_params=_cparams("arbitrary"),
        name="qkv_proj",
    )(x, g, w, gain_row, cos, sin)


def _fill_kv(kbuf, vbuf, kp_ref, kc_ref, vp_ref, vc_ref):
    kbuf[0:BAND, :] = kp_ref[...].astype(BF16)
    kbuf[BAND:, :] = kc_ref[...].astype(BF16)
    vbuf[0:BAND, :] = vp_ref[...].astype(BF16)
    vbuf[BAND:, :] = vc_ref[...].astype(BF16)


def _band_masks(rows, first_block):
    assert BAND == LANES
    row = lax.broadcasted_iota(jnp.int32, (rows, BAND), 0) & (BAND - 1)
    col = lax.broadcasted_iota(jnp.int32, (rows, BAND), 1)
    shifted = row + jnp.where(first_block, 2 * BAND, 0)
    return col > shifted, col == shifted, col <= row


def _banded_softmax_pv(qm, kp, kc, vp, vc, masks, floor_row=None):
    upper, diag, lower = masks
    s_prev = _dot_nt(qm, kp)
    s = jnp.where(upper, s_prev, jnp.where(lower, _dot_nt(qm, kc), NEG_INF))
    back = jnp.where(diag, s_prev, NEG_INF if floor_row is None else floor_row)
    m = jnp.max(jnp.maximum(s, back), axis=-1, keepdims=True)
    p = jnp.exp(s - m)
    p_back = jnp.where(diag, jnp.exp(s_prev - m), 0.0)
    l = jnp.sum(p + p_back, axis=-1, keepdims=True)
    p_prev = jnp.where(upper, p, p_back).astype(BF16)
    p_own = jnp.where(lower, p, 0.0).astype(BF16)
    return m, l, _dot(p_prev, vp) + _dot(p_own, vc)


def _attn_a_kernel(q_ref, kc_ref, kp_ref, vc_ref, vp_ref, o_ref, lse_ref, kbuf, vbuf, *, tq):
    blk = pl.program_id(2)
    _fill_kv(kbuf, vbuf, kp_ref, kc_ref, vp_ref, vc_ref)
    lane = lax.broadcasted_iota(jnp.int32, (BAND, LANES), 1)
    low_head = lane < HEAD_DIM

    def sub_block(sb, carry):
        r0 = pl.multiple_of(sb * BAND, BAND)
        masks = _band_masks(HEADS_PER_VREG * BAND, (blk + sb) == 0)
        lse_tile = jnp.zeros((BAND, LANES), F32)
        for hb in range(N_HEAD_BLOCKS):
            cs = slice(hb * LANES, (hb + 1) * LANES)
            q2 = q_ref[pl.ds(r0, BAND), cs]
            zero = jnp.zeros_like(q2)
            qm = jnp.concatenate([jnp.where(low_head, q2, zero), jnp.where(low_head, zero, q2)], axis=0)
            m, l, pv = _banded_softmax_pv(qm, kbuf[pl.ds(r0, BAND), cs], kbuf[pl.ds(r0 + BAND, BAND), cs],
                                          vbuf[pl.ds(r0, BAND), cs], vbuf[pl.ds(r0 + BAND, BAND), cs], masks)
            o = pv * (1.0 / l)
            o_ref[pl.ds(r0, BAND), cs] = jnp.where(low_head, o[:BAND], o[BAND:]).astype(o_ref.dtype)
            lse = m + jnp.log(l)
            for hh in range(HEADS_PER_VREG):
                lse_tile = jnp.where(lane == hb * HEADS_PER_VREG + hh, lse[hh * BAND:(hh + 1) * BAND], lse_tile)
        lse_ref[pl.ds(r0, BAND), :] = lse_tile
        return carry

    lax.fori_loop(0, tq // BAND, sub_block, 0)


def _attn_a_branch(q, k, v):
    n, r, sub_len, w = q.shape
    tq = min(1024, sub_len)
    cur = lambda b, c, i: (b, c, i, 0)
    prv = lambda b, c, i: (b, c, jnp.maximum(i * (tq // BAND) - 1, 0), 0)
    blk_cur = pl.BlockSpec((None, None, tq, w), cur)
    blk_prv = pl.BlockSpec((None, None, BAND, w), prv)
    sds = jax.ShapeDtypeStruct
    return pl.pallas_call(
        functools.partial(_attn_a_kernel, tq=tq),
        grid=(n, r, sub_len // tq),
        in_specs=[blk_cur, blk_cur, blk_prv, blk_cur, blk_prv],
        out_specs=[blk_cur, pl.BlockSpec((None, None, tq, LANES), cur)],
        out_shape=[sds((n, r, sub_len, w), BF16), sds((n, r, sub_len, LANES), F32)],
        scratch_shapes=[pltpu.VMEM((tq + BAND, w), BF16), pltpu.VMEM((tq + BAND, w), BF16)],
        compiler_params=_cparams("arbitrary", "arbitrary", "arbitrary"),
        name=f"attn_a_dil{r}",
    )(q, k, k, v, v)


_KVB_K, _KVB_K_SWAPPED, _KVB_V, _KVB_V_SWAPPED = range(4)
_KVB_WIDTH = 4 * KVW


def _attn_b_kernel(q_ref, kvc_ref, kvp_ref, sink_ref, o_ref, kvbuf, *, tq):
    blk = pl.program_id(1)
    kvbuf[0:BAND, :] = kvp_ref[...]
    kvbuf[BAND:, :] = kvc_ref[...]
    lane = lax.broadcasted_iota(jnp.int32, (BAND, LANES), 1)
    low_head = lane < HEAD_DIM
    group = N_HEADS // B_KV_HEADS
    plain = [h for h in range(N_HEADS) if h % HEADS_PER_VREG == h // group]
    swapped = [h for h in range(N_HEADS) if h % HEADS_PER_VREG != h // group]

    def sub_block(sb, carry):
        r0 = pl.multiple_of(sb * BAND, BAND)
        masks = _band_masks(len(plain) * BAND, (blk + sb) == 0)
        blocks = [q_ref[pl.ds(r0, BAND), hb * LANES:(hb + 1) * LANES] for hb in range(N_HEAD_BLOCKS)]
        outs = [None] * N_HEADS
        for heads, k_blk, v_blk in ((plain, _KVB_K, _KVB_V), (swapped, _KVB_K_SWAPPED, _KVB_V_SWAPPED)):
            ks = slice(k_blk * KVW, (k_blk + 1) * KVW)
            vs = slice(v_blk * KVW, (v_blk + 1) * KVW)
            q_parts, sink_parts = [], []
            for h in heads:
                q2 = blocks[h // HEADS_PER_VREG]
                own_half = low_head if h % HEADS_PER_VREG == 0 else jnp.logical_not(low_head)
                q_parts.append(jnp.where(own_half, q2, jnp.zeros_like(q2)))
                sink_parts.append(jnp.broadcast_to(sink_ref[h:h + 1, :], (BAND, LANES)))
            sink = jnp.concatenate(sink_parts, axis=0)
            m, l, pv = _banded_softmax_pv(jnp.concatenate(q_parts, axis=0),
                                          kvbuf[pl.ds(r0, BAND), ks], kvbuf[pl.ds(r0 + BAND, BAND), ks],
                                          kvbuf[pl.ds(r0, BAND), vs], kvbuf[pl.ds(r0 + BAND, BAND), vs],
                                          masks, floor_row=sink)
            o = pv * (1.0 / (l + jnp.exp(sink - m)))
            for idx, h in enumerate(heads):
                outs[h] = o[idx * BAND:(idx + 1) * BAND]
        for hb in range(N_HEAD_BLOCKS):
            o_ref[pl.ds(r0, BAND), hb * LANES:(hb + 1) * LANES] = jnp.where(
                low_head, outs[hb * HEADS_PER_VREG], outs[hb * HEADS_PER_VREG + 1]).astype(o_ref.dtype)
        return carry

    lax.fori_loop(0, tq // BAND, sub_block, 0)


def _attn_b(q, kv, sinks_rows):
    n, s, w = q.shape
    tq = min(1024, s)
    cur = lambda b, i: (b, i, 0)
    prv = lambda b, i: (b, jnp.maximum(i * (tq // BAND) - 1, 0), 0)
    return pl.pallas_call(
        functools.partial(_attn_b_kernel, tq=tq),
        grid=(n, s // tq),
        in_specs=[pl.BlockSpec((None, tq, w), cur), pl.BlockSpec((None, tq, _KVB_WIDTH), cur),
                  pl.BlockSpec((None, BAND, _KVB_WIDTH), prv),
                  pl.BlockSpec((N_HEADS, LANES), lambda b, i: (0, 0))],
        out_specs=pl.BlockSpec((None, tq, w), cur),
        out_shape=jax.ShapeDtypeStruct((n, s, w), BF16),
        scratch_shapes=[pltpu.VMEM((tq + BAND, _KVB_WIDTH), BF16)],
        compiler_params=_cparams("arbitrary", "arbitrary"),
        name="attn_b",
    )(q, kv, kv, sinks_rows)


NEW_ROWS = 8


def _branch_multiplicity(t_query, key_pos, la):
    back = la + t_query - key_pos
    mult = jnp.zeros(back.shape, F32)
    for r in A_DILATIONS:
        mult = mult + ((back >= 0) & (back <= BAND * r) & (back % r == 0)).astype(F32)
    return mult


def _sample_attn_kernel(qa_ref, kna_ref, vna_ref, qb_ref, knb_ref, vnb_ref,
                        kt_ref, vt_ref, cbk_ref, cbv_ref, sink_ref,
                        oa_ref, ob_ref, *, nb, t_new):
    la = kt_ref.shape[-1]
    t_c = lax.broadcasted_iota(jnp.int32, (NEW_ROWS, la), 0)
    t_c = jnp.where(t_c < t_new, t_c, 0)
    mult_c = _branch_multiplicity(t_c, lax.broadcasted_iota(jnp.int32, (NEW_ROWS, la), 1), la)
    t_n = lax.broadcasted_iota(jnp.int32, (NEW_ROWS, NEW_ROWS), 0)
    t_n = jnp.where(t_n < t_new, t_n, 0)
    key_n = lax.broadcasted_iota(jnp.int32, (NEW_ROWS, NEW_ROWS), 1)
    mult_n = jnp.where(key_n < t_new, _branch_multiplicity(t_n, la + key_n, la), 0.0)

    def head_attn_a(q_h, kt, vt, kn_h, vn_h):
        s_c = jnp.where(mult_c > 0.0, _dot(q_h, kt), NEG_INF)
        s_n = jnp.where(mult_n > 0.0, _dot_nt(q_h, kn_h), NEG_INF)
        m = jnp.maximum(jnp.max(s_c, axis=-1, keepdims=True), jnp.max(s_n, axis=-1, keepdims=True))
        p_c = jnp.exp(s_c - m) * mult_c
        p_n = jnp.exp(s_n - m) * mult_n
        l = jnp.sum(p_c, axis=-1, keepdims=True) + jnp.sum(p_n, axis=-1, keepdims=True)
        return (_dot_nt(p_c.astype(BF16), vt) + _dot(p_n.astype(BF16), vn_h)) / l

    group = N_HEADS // B_KV_HEADS
    lb = cbk_ref.shape[-1]
    t_b = lax.broadcasted_iota(jnp.int32, (group * NEW_ROWS, lb), 0) & (NEW_ROWS - 1)
    t_b = jnp.where(t_b < t_new, t_b, 0)
    valid_bc = lax.broadcasted_iota(jnp.int32, (group * NEW_ROWS, lb), 1) >= lb + t_b - BAND
    t_bn = lax.broadcasted_iota(jnp.int32, (group * NEW_ROWS, NEW_ROWS), 0) & (NEW_ROWS - 1)
    t_bn = jnp.where(t_bn < t_new, t_bn, 0)
    key_bn = lax.broadcasted_iota(jnp.int32, (group * NEW_ROWS, NEW_ROWS), 1)
    valid_bn = (key_bn <= t_bn) & (key_bn < t_new)

    def kv_head_attn_b(q_g, kt, vt, kn_g, vn_g, sink):
        s_c = jnp.where(valid_bc, _dot(q_g, kt), NEG_INF)
        s_n = jnp.where(valid_bn, _dot_nt(q_g, kn_g), NEG_INF)
        m = jnp.maximum(jnp.max(s_c, axis=-1, keepdims=True), jnp.max(s_n, axis=-1, keepdims=True))
        m = jnp.maximum(m, sink)
        p_c = jnp.exp(s_c - m[:, :lb])
        p_n = jnp.exp(s_n - m[:, :NEW_ROWS])
        l = (jnp.sum(p_c, axis=-1, keepdims=True) + jnp.sum(p_n, axis=-1, keepdims=True) + jnp.exp(sink - m))
        return (_dot_nt(p_c.astype(BF16), vt) + _dot(p_n.astype(BF16), vn_g)) / l[:, :HEAD_DIM]

    def one_sequence(n, carry):
        new0 = pl.multiple_of(n * NEW_ROWS, NEW_ROWS)
        qa = qa_ref[pl.ds(new0, NEW_ROWS), :].astype(BF16)
        kna = kna_ref[pl.ds(new0, NEW_ROWS), :].astype(BF16)
        vna = vna_ref[pl.ds(new0, NEW_ROWS), :].astype(BF16)
        heads_out = []
        for h in range(N_HEADS):
            hs = slice(h * HEAD_DIM, (h + 1) * HEAD_DIM)
            heads_out.append(head_attn_a(qa[:, hs], kt_ref[n, h].astype(BF16), vt_ref[n, h].astype(BF16),
                                         kna[:, hs], vna[:, hs]))
        oa_ref[pl.ds(new0, NEW_ROWS), :] = jnp.concatenate(heads_out, axis=1).astype(oa_ref.dtype)
        qb = qb_ref[pl.ds(new0, NEW_ROWS), :].astype(BF16)
        knb = knb_ref[pl.ds(new0, NEW_ROWS), :].astype(BF16)
        vnb = vnb_ref[pl.ds(new0, NEW_ROWS), :].astype(BF16)
        heads_b = []
        for g in range(B_KV_HEADS):
            heads = range(g * group, (g + 1) * group)
            q_g = jnp.concatenate([qb[:, h * HEAD_DIM:(h + 1) * HEAD_DIM] for h in heads], axis=0)
            sink = jnp.concatenate([jnp.broadcast_to(sink_ref[h:h + 1, :], (NEW_ROWS, LANES)) for h in heads], axis=0)
            gs = slice(g * HEAD_DIM, (g + 1) * HEAD_DIM)
            o_g = kv_head_attn_b(q_g, cbk_ref[n, g].astype(BF16), cbv_ref[n, g].astype(BF16), knb[:, gs], vnb[:, gs], sink)
            heads_b += [o_g[hh * NEW_ROWS:(hh + 1) * NEW_ROWS] for hh in range(group)]
        ob_ref[pl.ds(new0, NEW_ROWS), :] = jnp.concatenate(heads_b, axis=1).astype(ob_ref.dtype)
        return carry

    lax.fori_loop(0, nb, one_sequence, 0)


def _pad_new_rows(a, n_seq, t_new):
    w = a.shape[-1]
    a = a.reshape(n_seq, t_new, w)
    a = jnp.pad(a, ((0, 0), (0, NEW_ROWS - t_new), (0, 0)))
    return a.reshape(n_seq * NEW_ROWS, w)


def _sample_attn(qa, ka, va, qb, kb, vb, cache_a_k, cache_a_v, cache_b_k, cache_b_v, sink_rows, *, t_new):
    n_seq, la, nh, hd = cache_a_k.shape
    lb, nkv = cache_b_k.shape[1:3]
    w = nh * hd
    assert t_new <= NEW_ROWS
    nb = 2 if n_seq % 2 == 0 else 1
    padded = [_pad_new_rows(a, n_seq, t_new) for a in (qa, ka, va, qb, kb, vb)]
    dim_major = lambda c: jnp.transpose(c, (0, 2, 3, 1))
    new_a = pl.BlockSpec((nb * NEW_ROWS, w), lambda i: (i, 0))
    new_b = pl.BlockSpec((nb * NEW_ROWS, KVW), lambda i: (i, 0))
    spec_ca = pl.BlockSpec((nb, nh, hd, la), lambda i: (i, 0, 0, 0))
    spec_cb = pl.BlockSpec((nb, nkv, hd, lb), lambda i: (i, 0, 0, 0))
    sds = jax.ShapeDtypeStruct
    oa, ob = pl.pallas_call(
        functools.partial(_sample_attn_kernel, nb=nb, t_new=t_new),
        grid=(n_seq // nb,),
        in_specs=[new_a, new_a, new_a, new_a, new_b, new_b, spec_ca, spec_ca, spec_cb, spec_cb,
                  pl.BlockSpec((N_HEADS, LANES), lambda i: (0, 0))],
        out_specs=[new_a, new_a],
        out_shape=[sds((n_seq * NEW_ROWS, w), BF16), sds((n_seq * NEW_ROWS, w), BF16)],
        compiler_params=_cparams("arbitrary"),
        name="sample_attn",
    )(*padded, dim_major(cache_a_k), dim_major(cache_a_v), dim_major(cache_b_k), dim_major(cache_b_v), sink_rows)
    unpad = lambda o: o.reshape(n_seq, NEW_ROWS, w)[:, :t_new].reshape(n_seq * t_new, w)
    return unpad(oa), unpad(ob)


def _silu(a):
    return a * (1.0 / (1.0 + jnp.exp(-a)))


def _natural_order(o_ref, lse_ref, o_stage, lse_stage):
    r, sub, _ = o_ref.shape
    if r == 1:
        return o_ref[0].astype(F32), lse_ref[0]
    for c in range(r):
        lse_stage[pl.ds(c, sub, stride=r), :] = lse_ref[c]
        for cb in range(N_HEAD_BLOCKS):
            o_stage[cb, pl.ds(c, sub, stride=r), :] = o_ref[c, :, cb * LANES:(cb + 1) * LANES].astype(F32)
    return jnp.concatenate([o_stage[cb] for cb in range(N_HEAD_BLOCKS)], axis=1), lse_stage[...]


def _merge_branches(outs, lses):
    top = functools.reduce(jnp.maximum, lses)
    ws = [jnp.exp(l - top) for l in lses]
    inv_den = 1.0 / functools.reduce(lambda a, b: a + b, ws)
    head = lax.broadcasted_iota(jnp.int32, (LANES, AW), 0)
    lane_head = lax.broadcasted_iota(jnp.int32, (LANES, AW), 1) // HEAD_DIM
    expand = (head == lane_head).astype(BF16)
    oa = None
    for w, o in zip(ws, outs):
        w = w * inv_den
        w_hi = w.astype(BF16)
        w_lo = (w - w_hi.astype(F32)).astype(BF16)
        part = (_dot(w_hi, expand) + _dot(w_lo, expand)) * o
        oa = part if oa is None else oa + part
    return oa.astype(BF16)


def _mix_out_ffn_kernel(x_ref, *refs, n_branches):
    if n_branches:
        branch_refs, refs = refs[:2 * n_branches], refs[2 * n_branches:]
        refs, (o_stage, lse_stage) = refs[:-2], refs[-2:]
    else:
        oa_ref, refs = refs[0], refs[1:]
    ob_ref, wo_ref, g_ref, w1_ref, w3_ref, w2_ref, out_ref, x1_ref, h_ref, acc_ref = refs
    j = pl.program_id(1)

    @pl.when(j == 0)
    def _():
        if n_branches:
            outs, lses = zip(*[_natural_order(branch_refs[2 * i], branch_refs[2 * i + 1], o_stage, lse_stage)
                               for i in range(n_branches)])
            oa = _merge_branches(outs, lses)
        else:
            oa = oa_ref[...]
        x1 = x_ref[...] + _dot(oa, wo_ref[0:AW, :]) + _dot(ob_ref[...], wo_ref[AW:, :])
        x1_ref[...] = x1
        h_ref[...] = _rms_norm(x1, g_ref[...]).astype(BF16)
        acc_ref[...] = jnp.zeros_like(acc_ref)

    h = h_ref[...]
    gate = (_silu(_dot(h, w1_ref[...])) * _dot(h, w3_ref[...])).astype(BF16)
    acc_ref[...] += _dot(gate, w2_ref[...])

    @pl.when(j == pl.num_programs(1) - 1)
    def _():
        out_ref[...] = x1_ref[...] + acc_ref[...]


def _ff_tile(ff, target):
    best = LANES
    for k in range(1, ff // LANES + 1):
        if ff % (k * LANES) == 0 and k * LANES <= target:
            best = k * LANES
    return best


def _mix_out_ffn(x, oa, ob, wo, g, w1, w3, w2, *, tm):
    t, d = x.shape
    ff = w1.shape[1]
    tf = ff
    row = lambda i, j: (i, 0)
    fixed = lambda i, j: (0, 0)
    once = pl.Buffered(1)
    if isinstance(oa, (list, tuple)):
        n_branches = len(oa)
        tiles = t // oa[0][0].shape[0] // tm
        a_specs, a_args = [], []
        for o, lse in oa:
            r = o.shape[1]
            assert tm % (r * SUBLANES) == 0
            tile = lambda i, j: (i // tiles, 0, i % tiles, 0)
            a_specs += [pl.BlockSpec((None, r, tm // r, AW), tile), pl.BlockSpec((None, r, tm // r, LANES), tile)]
            a_args += [o, lse]
        stage = [pltpu.VMEM((N_HEAD_BLOCKS, tm, LANES), F32), pltpu.VMEM((tm, LANES), F32)]
    else:
        n_branches, a_specs, a_args, stage = 0, [pl.BlockSpec((tm, AW), row)], [oa], []
    return pl.pallas_call(
        functools.partial(_mix_out_ffn_kernel, n_branches=n_branches),
        grid=(t // tm, ff // tf),
        in_specs=[pl.BlockSpec((tm, d), row)] + a_specs
                 + [pl.BlockSpec((tm, AW), row),
                  pl.BlockSpec((2 * AW, d), fixed, pipeline_mode=once), pl.BlockSpec((1, d), fixed),
                  pl.BlockSpec((d, tf), lambda i, j: (0, j), pipeline_mode=once),
                  pl.BlockSpec((d, tf), lambda i, j: (0, j), pipeline_mode=once),
                  pl.BlockSpec((tf, d), lambda i, j: (j, 0), pipeline_mode=once)],
        out_specs=pl.BlockSpec((tm, d), row),
        out_shape=jax.ShapeDtypeStruct((t, d), F32),
        scratch_shapes=[pltpu.VMEM((tm, d), F32), pltpu.VMEM((tm, d), BF16), pltpu.VMEM((tm, d), F32)] + stage,
        compiler_params=_cparams("arbitrary", "arbitrary"),
        name="mix_out_ffn",
    )(x, *a_args, ob, wo, g, w1, w3, w2)


def _gelu_exact(z):
    return 0.5 * z * (1.0 + lax.erf(z * (2.0 ** -0.5)))


def _gmlp_kernel(x_ref, g_ref, win_ref, lng_ref, lnb_ref, mix_ref, bias_ref, wout_ref, gffn_ref, wr_ref,
                 *out_refs, write_v):
    if write_v:
        xo_ref, h2_ref, logit_ref, v_ref = out_refs[:4]
    else:
        xo_ref, h2_ref, logit_ref = out_refs[:3]
    gate_ref = out_refs[-1]
    x = x_ref[...]
    tm, cd = x.shape
    h = _rms_norm(x, g_ref[...]).astype(BF16)
    z = _gelu_exact(_dot(h, win_ref[...]))
    u = z[:, :cd]
    v = z[:, cd:]
    mu = jnp.mean(v, axis=-1, keepdims=True)
    vc = v - mu
    v = vc * lax.rsqrt(jnp.mean(vc * vc, axis=-1, keepdims=True) + NORM_EPS) * lng_ref[...] + lnb_ref[...]
    if write_v:
        v_ref[...] = v
    vb = v.astype(BF16)
    bias = bias_ref[...]
    gw = cd // C_GROUPS
    for c in range(tm // CHUNK):
        rs = slice(c * CHUNK, (c + 1) * CHUNK)
        for g in range(C_GROUPS):
            cs = slice(g * gw, (g + 1) * gw)
            f = _dot(mix_ref[g], vb[rs, cs]) + bias[:, cs]
            gate_ref[rs, cs] = (u[rs, cs] * f).astype(BF16)
    xo = x + _dot(gate_ref[...], wout_ref[...])
    xo_ref[...] = xo
    h2 = _rms_norm(xo, gffn_ref[...])
    h2_ref[...] = h2
    h_hi = h2.astype(BF16)
    h_lo = (h2 - h_hi.astype(F32)).astype(BF16)
    wr = wr_ref[...]
    both = _dot(h_hi, wr)
    logit_ref[...] = both[:, :LANES] + both[:, LANES:] + _dot(h_lo, wr[:, :LANES])


def _gmlp_block(x, g, w_in, ln_g, ln_b, mix, bias_full, w_out, g_ffn, wr_hi_lo, *, tm, write_v):
    t, d = x.shape
    cd = w_out.shape[0]
    row = lambda i: (i, 0)
    fixed = lambda i: (0, 0)
    sds = jax.ShapeDtypeStruct
    out_specs = [pl.BlockSpec((tm, d), row), pl.BlockSpec((tm, d), row), pl.BlockSpec((tm, LANES), row)]
    out_shape = [sds((t, d), F32), sds((t, d), F32), sds((t, LANES), F32)]
    if write_v:
        out_specs.append(pl.BlockSpec((tm, cd), row))
        out_shape.append(sds((t, cd), F32))
    return pl.pallas_call(
        functools.partial(_gmlp_kernel, write_v=write_v),
        grid=(t // tm,),
        in_specs=[pl.BlockSpec((tm, d), row), pl.BlockSpec((1, d), fixed), pl.BlockSpec((d, 2 * cd), fixed),
                  pl.BlockSpec((1, cd), fixed), pl.BlockSpec((1, cd), fixed),
                  pl.BlockSpec((C_GROUPS, CHUNK, CHUNK), lambda i: (0, 0, 0)),
                  pl.BlockSpec((CHUNK, cd), fixed), pl.BlockSpec((cd, d), fixed), pl.BlockSpec((1, d), fixed),
                  pl.BlockSpec((d, 2 * LANES), fixed)],
        out_specs=out_specs, out_shape=out_shape,
        scratch_shapes=[pltpu.VMEM((tm, cd), BF16)],
        compiler_params=_cparams("arbitrary"),
        name="gmlp_block",
    )(x, g, w_in, ln_g, ln_b, mix, bias_full, w_out, g_ffn, wr_hi_lo)


MOE_BLOCK = 1024
MOE_SUB = 256
DISPATCH_ROWS = 512


def _dispatch_kernel(dest_ref, h_ref, xs_in_ref, xs_ref, sem):
    del xs_in_ref
    rows = h_ref.shape[0]

    def issue(g, carry):
        for s in range(SUBLANES):
            r = g * SUBLANES + s
            for k in range(TOP_K):
                d = dest_ref[0, 0, r * TOP_K + k]
                pltpu.make_async_copy(h_ref.at[pl.ds(r, 1), :], xs_ref.at[pl.ds(d, 1), :], sem).start(priority=k)
        return carry

    lax.fori_loop(0, rows // SUBLANES, issue, 0)
    for _ in range(TOP_K):
        pltpu.make_async_copy(h_ref, xs_ref.at[pl.ds(0, rows), :], sem).wait()


def _dispatch(h, dest, xs):
    t, d = h.shape
    tg = DISPATCH_ROWS
    return pl.pallas_call(
        _dispatch_kernel,
        grid=(t // tg,),
        in_specs=[pl.BlockSpec((1, 1, tg * TOP_K), lambda i: (i, 0, 0), memory_space=pltpu.SMEM),
                  pl.BlockSpec((tg, d), lambda i: (i, 0)),
                  pl.BlockSpec(memory_space=pl.ANY)],
        out_specs=pl.BlockSpec(memory_space=pl.ANY),
        out_shape=jax.ShapeDtypeStruct(xs.shape, xs.dtype),
        scratch_shapes=[pltpu.SemaphoreType.DMA],
        input_output_aliases={2: 0},
        compiler_params=_cparams("arbitrary"),
        name="moe_dispatch",
    )(dest.reshape(t // tg, 1, tg * TOP_K), h, xs)


def _expert_kernel(be_ref, na_ref, ns_ref, xs_ref, w1_ref, w3_ref, w2_ref, y_ref, h_ref, acc_ref):
    del be_ref, na_ref
    b = pl.program_id(0)
    j = pl.program_id(1)
    n_sub = ns_ref[b]

    @pl.when((n_sub > 0) & (j == 0))
    def _():
        h_ref[...] = xs_ref[...].astype(BF16)
        acc_ref[...] = jnp.zeros_like(acc_ref)

    for k in range(1, MOE_BLOCK // MOE_SUB + 1):
        @pl.when(n_sub == k)
        def _(rows=k * MOE_SUB):
            h = h_ref[0:rows, :]
            w1 = w1_ref[...].astype(BF16)
            w3 = w3_ref[...].astype(BF16)
            gate = (_silu(_dot(h, w1)) * _dot(h, w3)).astype(BF16)
            acc_ref[0:rows, :] += _dot(gate, w2_ref[...].astype(BF16))

    @pl.when(j == pl.num_programs(1) - 1)
    def _():
        y_ref[...] = jnp.where(n_sub > 0, acc_ref[...], 0.0)


def _experts(xs, block_expert, n_active, n_sub, w1, w3, w2):
    n_rows, d = xs.shape
    ff = w1.shape[2]
    tf = _ff_tile(ff, 512)
    n_blocks = n_rows // MOE_BLOCK
    nj = ff // tf
    def wcol(b, j, be, na, ns):
        live = b < na[0]
        return (be[b], 0, jnp.where(live, j, nj - 1))

    def wrow(b, j, be, na, ns):
        live = b < na[0]
        return (be[b], jnp.where(live, j, nj - 1), 0)

    def xrow(b, j, be, na, ns):
        return (jnp.minimum(b, na[0] - 1), 0)

    grid_spec = pltpu.PrefetchScalarGridSpec(
        num_scalar_prefetch=3,
        grid=(n_blocks, nj),
        in_specs=[pl.BlockSpec((MOE_BLOCK, d), xrow),
                  pl.BlockSpec((None, d, tf), wcol), pl.BlockSpec((None, d, tf), wcol),
                  pl.BlockSpec((None, tf, d), wrow)],
        out_specs=pl.BlockSpec((MOE_BLOCK, d), lambda b, j, be, na, ns: (b, 0)),
        scratch_shapes=[pltpu.VMEM((MOE_BLOCK, d), BF16), pltpu.VMEM((MOE_BLOCK, d), F32)],
    )
    return pl.pallas_call(
        _expert_kernel,
        grid_spec=grid_spec,
        out_shape=jax.ShapeDtypeStruct((n_rows, d), F32),
        compiler_params=_cparams("arbitrary", "arbitrary"),
        name="moe_experts",
    )(block_expert, n_active, n_sub, xs, w1, w3, w2)


def _combine_kernel(dest_ref, dest_next_ref, x_ref, gate_ref, y_ref, out_ref, buf, sems, *, steps):
    rows = x_ref.shape[0]
    i = pl.program_id(0)

    def fetch(dests, slot):
        def issue(g, carry):
            for s in range(SUBLANES):
                r = g * SUBLANES + s
                for k in range(TOP_K):
                    d = dests[0, 0, r * TOP_K + k]
                    pltpu.make_async_copy(y_ref.at[pl.ds(d, 1), :], buf.at[slot, k, pl.ds(r, 1), :],
                                          sems.at[slot]).start(priority=k)
            return carry
        lax.fori_loop(0, rows // SUBLANES, issue, 0)

    @pl.when(i == 0)
    def _():
        fetch(dest_ref, 0)

    if steps > 1:
        @pl.when(i + 1 < steps)
        def _():
            fetch(dest_next_ref, (i + 1) % 2)

    slot = i % 2
    for k in range(TOP_K):
        pltpu.make_async_copy(y_ref.at[pl.ds(0, rows), :], buf.at[slot, k], sems.at[slot]).wait()
    gates = gate_ref[...]
    y = buf[slot, 0] * gates[:, 0:1]
    for k in range(1, TOP_K):
        y = y + buf[slot, k] * gates[:, k:k + 1]
    out_ref[...] = x_ref[...] + y


def _combine(x, gates, dest, y):
    t, d = x.shape
    tg = DISPATCH_ROWS
    steps = t // tg
    dests = dest.reshape(steps, 1, tg * TOP_K)
    return pl.pallas_call(
        functools.partial(_combine_kernel, steps=steps),
        grid=(steps,),
        in_specs=[pl.BlockSpec((1, 1, tg * TOP_K), lambda i: (i, 0, 0), memory_space=pltpu.SMEM),
                  pl.BlockSpec((1, 1, tg * TOP_K), lambda i: (jnp.minimum(i + 1, steps - 1), 0, 0),
                               memory_space=pltpu.SMEM),
                  pl.BlockSpec((tg, d), lambda i: (i, 0)),
                  pl.BlockSpec((tg, TOP_K), lambda i: (i, 0)),
                  pl.BlockSpec(memory_space=pl.ANY)],
        out_specs=pl.BlockSpec((tg, d), lambda i: (i, 0)),
        out_shape=jax.ShapeDtypeStruct((t, d), F32),
        scratch_shapes=[pltpu.VMEM((2, TOP_K, tg, d), F32), pltpu.SemaphoreType.DMA((2,))],
        compiler_params=_cparams("arbitrary"),
        name="moe_combine",
    )(dests, dests, x, gates, y)


def _route(logits):
    t = logits.shape[0]
    top_val, top_idx = lax.top_k(logits, TOP_K)
    gates = jax.nn.softmax(top_val, axis=-1)
    expert = top_idx.reshape(-1)
    one_hot = (expert[:, None] == jnp.arange(N_EXPERTS, dtype=expert.dtype)[None, :]).astype(jnp.int32)
    running = jnp.cumsum(one_hot, axis=0)
    rank = jnp.sum(running * one_hot, axis=1) - 1
    counts = running[-1]
    padded = (counts + MOE_BLOCK - 1) // MOE_BLOCK * MOE_BLOCK
    pad_ends = jnp.cumsum(padded)
    pad_starts = pad_ends - padded
    dest = (jnp.sum(pad_starts[None, :] * one_hot, axis=1) + rank).astype(jnp.int32)
    n_blocks = -(-(t * TOP_K) // MOE_BLOCK) + N_EXPERTS
    block_start = jnp.arange(n_blocks, dtype=jnp.int32) * MOE_BLOCK
    block_expert = jnp.sum((block_start[:, None] >= pad_ends[None, :]).astype(jnp.int32), axis=1)
    block_expert = jnp.minimum(block_expert, N_EXPERTS - 1).astype(jnp.int32)
    used_end = (pad_starts + counts)[block_expert]
    used_rows = jnp.clip(used_end - block_start, 0, MOE_BLOCK)
    used_rows = jnp.where(block_start < pad_ends[-1], used_rows, 0)
    n_sub = ((used_rows + MOE_SUB - 1) // MOE_SUB).astype(jnp.int32)
    n_active = (pad_ends[-1:] // MOE_BLOCK).astype(jnp.int32)
    return gates, dest.reshape(t, TOP_K), block_expert, n_sub, n_active, n_blocks * MOE_BLOCK


def _moe(groups, w1, w3, w2):
    sizes = [x.shape[0] for x, _, _ in groups]
    gates, dest, block_expert, n_sub, n_active, n_rows = _route(jnp.concatenate([l for _, _, l in groups]))
    xs = jnp.zeros((n_rows, w1.shape[1]), F32)
    start = 0
    for (_, h2, _), size in zip(groups, sizes):
        xs = _dispatch(h2, dest[start:start + size].reshape(-1), xs)
        start += size
    y = _experts(xs, block_expert, n_active, n_sub, w1, w3, w2)
    outs, start = [], 0
    for (x, _, _), size in zip(groups, sizes):
        outs.append(_combine(x, gates[start:start + size], dest[start:start + size].reshape(-1), y))
        start += size
    return outs


def _rope_tables(pos):
    inv_freq = ROPE_THETA ** (-jnp.arange(HALF_DIM, dtype=F32) / HALF_DIM)
    ang = pos.astype(F32)[:, None] * inv_freq[None, :]
    cos, sin = jnp.cos(ang), jnp.sin(ang)
    reps = LANES // HEAD_DIM
    return jnp.tile(cos, (1, 2 * reps)), jnp.tile(jnp.concatenate([-sin, sin], axis=1), (1, reps))


def _qk_gain_row(qk_gain):
    ones_a = jnp.ones((AW,), F32)
    ones_b = jnp.ones((KVW,), F32)
    return jnp.concatenate([jnp.tile(qk_gain[0], N_HEADS), jnp.tile(qk_gain[1], N_HEADS), ones_a,
                            jnp.tile(qk_gain[2], N_HEADS), jnp.tile(qk_gain[3], B_KV_HEADS), ones_b])[None, :]


def _row_block(t):
    return 512 if t % 512 == 0 else t


def kernel(x_prompt, x_sample, cache_a_k, cache_a_v, cache_b_k, cache_b_v, norm_mix, norm_ffn, ab_w_in,
           ab_qk_gain, ab_sinks, ab_w_out, ffn_w1, ffn_w3, ffn_w2, c_w_in, c_ln_g, c_ln_b, c_w_s, c_b_s,
           c_w_out, moe_router, moe_w1, moe_w3, moe_w2):
    n_p, s_p, d = x_prompt.shape
    n_s, t_new, _ = x_sample.shape
    depth = norm_mix.shape[0]
    past_len = 8192
    xp = x_prompt.reshape(n_p * s_p, d)
    xs = x_sample.reshape(n_s * t_new, d)
    pos_p = jnp.tile(jnp.arange(s_p), n_p)
    pos_s = jnp.tile(past_len + jnp.arange(t_new), n_s)
    cos_p, sin_p = _rope_tables(pos_p)
    cos_s, sin_s = _rope_tables(pos_s)
    row = lambda a: a[None, :]
    akp, avp, bkp, bvp, aks, avs, bks, bvs, cvs = [], [], [], [], [], [], [], [], []
    la_p, lb_p = min(BAND * A_DILATIONS[-1], s_p), min(BAND, s_p)
    for layer in range(depth):
        i = layer // 2
        if layer % 2 == 0:
            w_in = ab_w_in[i].astype(BF16)
            gain_row = _qk_gain_row(ab_qk_gain[i])
            w_out = ab_w_out[i].astype(BF16)
            w1, w3, w2 = ffn_w1[i].astype(BF16), ffn_w3[i].astype(BF16), ffn_w2[i].astype(BF16)
            sinks = ab_sinks[i].astype(F32)
            sink_rows = jnp.broadcast_to(sinks[:, None], (N_HEADS, LANES))
            qa, ka, va, qb, kb, vb, ka16, va16, kvb16, *dilated = _qkv_proj(
                xp, row(norm_mix[layer]), w_in, gain_row, cos_p, sin_p,
                tm=_row_block(s_p), q_dtype=BF16, seq_len=s_p)
            seq = lambda a: a.reshape(n_p, s_p, a.shape[-1])
            qkv_by_branch = [[seq(a)[:, None] for a in (qa, ka16, va16)]] + [dilated[3 * i:3 * i + 3]
                                                                            for i in range(len(A_DILATIONS) - 1)]
            branches = [_attn_a_branch(q, k, v) for q, k, v in qkv_by_branch]
            ob = _attn_b(seq(qb), seq(kvb16), sink_rows).reshape(n_p * s_p, AW)
            xp = _mix_out_ffn(xp, branches, ob, w_out, row(norm_ffn[layer]), w1, w3, w2, tm=_row_block(s_p))
            tail = lambda a, rows, nh: seq(a)[:, s_p - rows:].reshape(n_p, rows, nh, HEAD_DIM)
            akp.append(tail(ka, la_p, N_HEADS))
            avp.append(tail(va, la_p, N_HEADS))
            bkp.append(tail(kb, lb_p, B_KV_HEADS))
            bvp.append(tail(vb, lb_p, B_KV_HEADS))
            qa, ka, va, qb, kb, vb = _qkv_proj(xs, row(norm_mix[layer]), w_in, gain_row, cos_s, sin_s,
                                               tm=_row_block(xs.shape[0]), q_dtype=F32)
            oa, ob = _sample_attn(qa, ka, va, qb, kb, vb, cache_a_k[i], cache_a_v[i],
                                  cache_b_k[i], cache_b_v[i], sink_rows, t_new=t_new)
            xs = _mix_out_ffn(xs, oa, ob, w_out, row(norm_ffn[layer]), w1, w3, w2, tm=_row_block(xs.shape[0]))
            heads_s = lambda a, nh: a.reshape(n_s, t_new, nh, HEAD_DIM)
            aks.append(heads_s(ka, N_HEADS))
            avs.append(heads_s(va, N_HEADS))
            bks.append(heads_s(kb, B_KV_HEADS))
            bvs.append(heads_s(vb, B_KV_HEADS))
        else:
            w_in = c_w_in[i].astype(BF16)
            w_out = c_w_out[i].astype(BF16)
            tril = jnp.tril(jnp.ones((CHUNK, CHUNK), F32))
            mix_p = (c_w_s[i] * tril).astype(BF16)
            gw = w_out.shape[0] // C_GROUPS
            bias_p = jnp.repeat(c_b_s[i].T, gw, axis=1)
            per_tile = CHUNK // t_new
            mix_s = jnp.einsum("ab,gij->gaibj", jnp.eye(per_tile, dtype=F32), (c_w_s[i] * tril)[:, :t_new, :t_new])
            mix_s = mix_s.reshape(C_GROUPS, CHUNK, CHUNK).astype(BF16)
            bias_s = jnp.tile(bias_p[:t_new], (per_tile, 1))
            wr = moe_router[i]
            wr_hi = wr.astype(BF16)
            wr_lo = (wr - wr_hi.astype(F32)).astype(BF16)
            pad_e = lambda a: jnp.pad(a, ((0, 0), (0, LANES - N_EXPERTS)))
            wr_hi_lo = jnp.concatenate([pad_e(wr_hi), pad_e(wr_lo)], axis=1)
            common = (row(c_ln_g[i]), row(c_ln_b[i]))
            xp1, hp, lp = _gmlp_block(xp, row(norm_mix[layer]), w_in, *common, mix_p, bias_p, w_out,
                                      row(norm_ffn[layer]), wr_hi_lo, tm=_row_block(xp.shape[0]), write_v=False)
            xs1, hs, ls, v_new = _gmlp_block(xs, row(norm_mix[layer]), w_in, *common, mix_s, bias_s, w_out,
                                             row(norm_ffn[layer]), wr_hi_lo, tm=_row_block(xs.shape[0]),
                                             write_v=True)
            cvs.append(v_new.reshape(n_s, t_new, v_new.shape[-1]))
            xp, xs = _moe([(xp1, hp, lp[:, :N_EXPERTS]), (xs1, hs, ls[:, :N_EXPERTS])],
                          moe_w1[i], moe_w3[i], moe_w2[i])
    return (xp.reshape(n_p, s_p, d), xs.reshape(n_s, t_new, d),
            jnp.stack(akp), jnp.stack(avp), jnp.stack(bkp), jnp.stack(bvp),
            jnp.stack(aks), jnp.stack(avs), jnp.stack(bks), jnp.stack(bvs),
            jnp.stack(cvs))
```

```python
import functools

import jax
import jax.numpy as jnp
from jax import lax
from jax.experimental import pallas as pl
from jax.experimental.pallas import tpu as pltpu

F32 = jnp.float32
BF16 = jnp.bfloat16

HEAD_DIM = 64
HALF_DIM = HEAD_DIM // 2
N_HEADS = 8
B_KV_HEADS = 2
A_DILATIONS = (1, 4, 16)
BAND = 128
ROPE_THETA = 10000.0
NORM_EPS = 1e-6
NEG_INF = -1e30
CHUNK = 128
C_GROUPS = 8
N_EXPERTS = 8
TOP_K = 2

LANES = 128
SUBLANES = 8
VMEM_LIMIT_BYTES = 56 * 1024 * 1024

AW = N_HEADS * HEAD_DIM
KVW = B_KV_HEADS * HEAD_DIM
HEADS_PER_VREG = LANES // HEAD_DIM
N_HEAD_BLOCKS = AW // LANES


def _cparams(*sem):
    return pltpu.CompilerParams(dimension_semantics=sem, vmem_limit_bytes=VMEM_LIMIT_BYTES)


def _rms_norm(x, g):
    return x * lax.rsqrt(jnp.mean(x * x, axis=-1, keepdims=True) + NORM_EPS) * g


def _dot(a, b):
    return jnp.dot(a, b, preferred_element_type=F32)


def _dot_nt(a, b):
    return lax.dot_general(a, b, (((1,), (1,)), ((), ())), preferred_element_type=F32)


_QA_BLOCKS = range(0, 4)
_KA_BLOCKS = range(4, 8)
_VA_BLOCKS = range(8, 12)
_QB_BLOCKS = range(12, 16)
_KB_BLOCK = 16
_VB_BLOCK = 17


def _proj_kernel(x_ref, g_ref, w_ref, gain_ref, cos_ref, sin_ref,
                 qa_ref, ka_ref, va_ref, qb_ref, kb_ref, vb_ref, *prompt_refs):
    stage = prompt_refs[-1] if prompt_refs else None
    h = _rms_norm(x_ref[...], g_ref[...]).astype(BF16)
    tm = h.shape[0]
    n_blocks = w_ref.shape[1] // LANES
    live = {}

    def proj_block(blk):
        group = blk // N_HEAD_BLOCKS
        if group not in live:
            live.clear()
            lo = group * N_HEAD_BLOCKS
            hi = min(lo + N_HEAD_BLOCKS, n_blocks)
            live[group] = _dot(h, w_ref[:, lo * LANES:hi * LANES])
        off = blk % N_HEAD_BLOCKS
        return live[group][:, off * LANES:(off + 1) * LANES]

    cos = cos_ref[...]
    sin = sin_ref[...]
    lane = lax.broadcasted_iota(jnp.int32, (tm, LANES), 1)
    first_half = (lane & HALF_DIM) == 0
    li = lax.broadcasted_iota(jnp.int32, (LANES, LANES), 0) // HEAD_DIM
    lj = lax.broadcasted_iota(jnp.int32, (LANES, LANES), 1) // HEAD_DIM
    head_mean = jnp.where(li == lj, 1.0 / HEAD_DIM, 0.0).astype(BF16)

    def norm_rope(blk):
        pc = proj_block(blk)
        ms = _dot((pc * pc).astype(BF16), head_mean)
        y = pc * lax.rsqrt(ms + NORM_EPS) * gain_ref[:, blk * LANES:(blk + 1) * LANES]
        partner = jnp.where(first_half, pltpu.roll(y, LANES - HALF_DIM, 1), pltpu.roll(y, HALF_DIM, 1))
        return y * cos + partner * sin

    scale = HEAD_DIM ** -0.5
    for j, blk in enumerate(_QA_BLOCKS):
        q = norm_rope(blk) * scale
        qa_ref[:, j * LANES:(j + 1) * LANES] = q.astype(qa_ref.dtype)
        if stage is not None:
            stage[j] = q
    for j, blk in enumerate(_KA_BLOCKS):
        k = norm_rope(blk)
        ka_ref[:, j * LANES:(j + 1) * LANES] = k
        if stage is not None:
            stage[N_HEAD_BLOCKS + j] = k
    for j, blk in enumerate(_VA_BLOCKS):
        v = proj_block(blk)
        va_ref[:, j * LANES:(j + 1) * LANES] = v
        if stage is not None:
            stage[2 * N_HEAD_BLOCKS + j] = v
    for j, blk in enumerate(_QB_BLOCKS):
        qb_ref[:, j * LANES:(j + 1) * LANES] = (norm_rope(blk) * scale).astype(qb_ref.dtype)
    kb_ref[...] = norm_rope(_KB_BLOCK)
    vb_ref[...] = proj_block(_VB_BLOCK)
    if prompt_refs:
        ka16_ref, va16_ref, kvb16_ref = prompt_refs[:3]
        ka16_ref[...] = ka_ref[...].astype(BF16)
        va16_ref[...] = va_ref[...].astype(BF16)
        kb = kb_ref[...]
        vb = vb_ref[...]
        parts = (kb, pltpu.roll(kb, HEAD_DIM, 1), vb, pltpu.roll(vb, HEAD_DIM, 1))
        for j, part in enumerate(parts):
            kvb16_ref[:, j * KVW:(j + 1) * KVW] = part.astype(BF16)
        for idx, r in enumerate(A_DILATIONS[1:]):
            for which, dst in enumerate(prompt_refs[3 + 3 * idx:6 + 3 * idx]):
                for c in range(r):
                    for cb in range(N_HEAD_BLOCKS):
                        rows = stage[which * N_HEAD_BLOCKS + cb, pl.ds(c, tm // r, stride=r), :]
                        dst[c, :, cb * LANES:(cb + 1) * LANES] = rows.astype(BF16)


def _qkv_proj(x, g, w, gain_row, cos, sin, *, tm, q_dtype, seq_len=None):
    t, d = x.shape
    n_out = w.shape[1]
    row = lambda i: (i, 0)
    fixed = lambda i: (0, 0)
    sds = jax.ShapeDtypeStruct
    out_specs = [pl.BlockSpec((tm, wd), row) for wd in (AW, AW, AW, AW, KVW, KVW)]
    out_shape = [sds((t, wd), dt) for wd, dt in zip((AW, AW, AW, AW, KVW, KVW), (q_dtype, F32, F32, q_dtype, F32, F32))]
    scratch = []
    if seq_len is not None:
        out_specs += [pl.BlockSpec((tm, wd), row) for wd in (AW, AW, 4 * KVW)]
        out_shape += [sds((t, wd), BF16) for wd in (AW, AW, 4 * KVW)]
        tiles = seq_len // tm
        for r in A_DILATIONS[1:]:
            assert seq_len % tm == 0 and tm % (r * 16) == 0
            out_specs += [pl.BlockSpec((None, r, tm // r, AW), lambda i: (i // tiles, 0, i % tiles, 0))] * 3
            out_shape += [sds((t // seq_len, r, seq_len // r, AW), BF16)] * 3
        scratch = [pltpu.VMEM((3 * N_HEAD_BLOCKS, tm, LANES), F32)]
    return pl.pallas_call(
        _proj_kernel,
        grid=(t // tm,),
        in_specs=[pl.BlockSpec((tm, d), row), pl.BlockSpec((1, d), fixed), pl.BlockSpec((d, n_out), fixed),
                  pl.BlockSpec((1, n_out), fixed), pl.BlockSpec((tm, LANES), row), pl.BlockSpec((tm, LANES), row)],
        out_specs=out_specs, out_shape=out_shape, scratch_shapes=scratch,
        compiler_params=_cparams("arbitrary"),
        name="qkv_proj",
    )(x, g, w, gain_row, cos, sin)


def _fill_kv(kbuf, vbuf, kp_ref, kc_ref, vp_ref, vc_ref):
    kbuf[0:BAND, :] = kp_ref[...].astype(BF16)
    kbuf[BAND:, :] = kc_ref[...].astype(BF16)
    vbuf[0:BAND, :] = vp_ref[...].astype(BF16)
    vbuf[BAND:, :] = vc_ref[...].astype(BF16)


def _band_masks(rows, first_block):
    assert BAND == LANES
    row = lax.broadcasted_iota(jnp.int32, (rows, BAND), 0) & (BAND - 1)
    col = lax.broadcasted_iota(jnp.int32, (rows, BAND), 1)
    shifted = row + jnp.where(first_block, 2 * BAND, 0)
    return col > shifted, col == shifted, col <= row


def _banded_softmax_pv(qm, kp, kc, vp, vc, masks, floor_row=None):
    upper, diag, lower = masks
    s_prev = _dot_nt(qm, kp)
    s = jnp.where(upper, s_prev, jnp.where(lower, _dot_nt(qm, kc), NEG_INF))
    back = jnp.where(diag, s_prev, NEG_INF if floor_row is None else floor_row)
    m = jnp.max(jnp.maximum(s, back), axis=-1, keepdims=True)
    p = jnp.exp(s - m)
    p_back = jnp.where(diag, jnp.exp(s_prev - m), 0.0)
    l = jnp.sum(p + p_back, axis=-1, keepdims=True)
    p_prev = jnp.where(upper, p, p_back).astype(BF16)
    p_own = jnp.where(lower, p, 0.0).astype(BF16)
    return m, l, _dot(p_prev, vp) + _dot(p_own, vc)


def _attn_a_kernel(q_ref, kc_ref, kp_ref, vc_ref, vp_ref, o_ref, lse_ref, kbuf, vbuf, *, tq):
    blk = pl.program_id(2)
    _fill_kv(kbuf, vbuf, kp_ref, kc_ref, vp_ref, vc_ref)
    lane = lax.broadcasted_iota(jnp.int32, (BAND, LANES), 1)
    low_head = lane < HEAD_DIM

    def sub_block(sb, carry):
        r0 = pl.multiple_of(sb * BAND, BAND)
        masks = _band_masks(HEADS_PER_VREG * BAND, (blk + sb) == 0)
        lse_tile = jnp.zeros((BAND, LANES), F32)
        for hb in range(N_HEAD_BLOCKS):
            cs = slice(hb * LANES, (hb + 1) * LANES)
            q2 = q_ref[pl.ds(r0, BAND), cs]
            zero = jnp.zeros_like(q2)
            qm = jnp.concatenate([jnp.where(low_head, q2, zero), jnp.where(low_head, zero, q2)], axis=0)
            m, l, pv = _banded_softmax_pv(qm, kbuf[pl.ds(r0, BAND), cs], kbuf[pl.ds(r0 + BAND, BAND), cs],
                                          vbuf[pl.ds(r0, BAND), cs], vbuf[pl.ds(r0 + BAND, BAND), cs], masks)
            o = pv * (1.0 / l)
            o_ref[pl.ds(r0, BAND), cs] = jnp.where(low_head, o[:BAND], o[BAND:]).astype(o_ref.dtype)
            lse = m + jnp.log(l)
            for hh in range(HEADS_PER_VREG):
                lse_tile = jnp.where(lane == hb * HEADS_PER_VREG + hh, lse[hh * BAND:(hh + 1) * BAND], lse_tile)
        lse_ref[pl.ds(r0, BAND), :] = lse_tile
        return carry

    lax.fori_loop(0, tq // BAND, sub_block, 0)


def _attn_a_branch(q, k, v):
    n, r, sub_len, w = q.shape
    tq = min(1024, sub_len)
    cur = lambda b, c, i: (b, c, i, 0)
    prv = lambda b, c, i: (b, c, jnp.maximum(i * (tq // BAND) - 1, 0), 0)
    blk_cur = pl.BlockSpec((None, None, tq, w), cur)
    blk_prv = pl.BlockSpec((None, None, BAND, w), prv)
    sds = jax.ShapeDtypeStruct
    return pl.pallas_call(
        functools.partial(_attn_a_kernel, tq=tq),
        grid=(n, r, sub_len // tq),
        in_specs=[blk_cur, blk_cur, blk_prv, blk_cur, blk_prv],
        out_specs=[blk_cur, pl.BlockSpec((None, None, tq, LANES), cur)],
        out_shape=[sds((n, r, sub_len, w), BF16), sds((n, r, sub_len, LANES), F32)],
        scratch_shapes=[pltpu.VMEM((tq + BAND, w), BF16), pltpu.VMEM((tq + BAND, w), BF16)],
        compiler_params=_cparams("arbitrary", "arbitrary", "arbitrary"),
        name=f"attn_a_dil{r}",
    )(q, k, k, v, v)


_KVB_K, _KVB_K_SWAPPED, _KVB_V, _KVB_V_SWAPPED = range(4)
_KVB_WIDTH = 4 * KVW


def _attn_b_kernel(q_ref, kvc_ref, kvp_ref, sink_ref, o_ref, kvbuf, *, tq):
    blk = pl.program_id(1)
    kvbuf[0:BAND, :] = kvp_ref[...]
    kvbuf[BAND:, :] = kvc_ref[...]
    lane = lax.broadcasted_iota(jnp.int32, (BAND, LANES), 1)
    low_head = lane < HEAD_DIM
    group = N_HEADS // B_KV_HEADS
    plain = [h for h in range(N_HEADS) if h % HEADS_PER_VREG == h // group]
    swapped = [h for h in range(N_HEADS) if h % HEADS_PER_VREG != h // group]

    def sub_block(sb, carry):
        r0 = pl.multiple_of(sb * BAND, BAND)
        masks = _band_masks(len(plain) * BAND, (blk + sb) == 0)
        blocks = [q_ref[pl.ds(r0, BAND), hb * LANES:(hb + 1) * LANES] for hb in range(N_HEAD_BLOCKS)]
        outs = [None] * N_HEADS
        for heads, k_blk, v_blk in ((plain, _KVB_K, _KVB_V), (swapped, _KVB_K_SWAPPED, _KVB_V_SWAPPED)):
            ks = slice(k_blk * KVW, (k_blk + 1) * KVW)
            vs = slice(v_blk * KVW, (v_blk + 1) * KVW)
            q_parts, sink_parts = [], []
            for h in heads:
                q2 = blocks[h // HEADS_PER_VREG]
                own_half = low_head if h % HEADS_PER_VREG == 0 else jnp.logical_not(low_head)
                q_parts.append(jnp.where(own_half, q2, jnp.zeros_like(q2)))
                sink_parts.append(jnp.broadcast_to(sink_ref[h:h + 1, :], (BAND, LANES)))
            sink = jnp.concatenate(sink_parts, axis=0)
            m, l, pv = _banded_softmax_pv(jnp.concatenate(q_parts, axis=0),
                                          kvbuf[pl.ds(r0, BAND), ks], kvbuf[pl.ds(r0 + BAND, BAND), ks],
                                          kvbuf[pl.ds(r0, BAND), vs], kvbuf[pl.ds(r0 + BAND, BAND), vs],
                                          masks, floor_row=sink)
            o = pv * (1.0 / (l + jnp.exp(sink - m)))
            for idx, h in enumerate(heads):
                outs[h] = o[idx * BAND:(idx + 1) * BAND]
        for hb in range(N_HEAD_BLOCKS):
            o_ref[pl.ds(r0, BAND), hb * LANES:(hb + 1) * LANES] = jnp.where(
                low_head, outs[hb * HEADS_PER_VREG], outs[hb * HEADS_PER_VREG + 1]).astype(o_ref.dtype)
        return carry

    lax.fori_loop(0, tq // BAND, sub_block, 0)


def _attn_b(q, kv, sinks_rows):
    n, s, w = q.shape
    tq = min(1024, s)
    cur = lambda b, i: (b, i, 0)
    prv = lambda b, i: (b, jnp.maximum(i * (tq // BAND) - 1, 0), 0)
    return pl.pallas_call(
        functools.partial(_attn_b_kernel, tq=tq),
        grid=(n, s // tq),
        in_specs=[pl.BlockSpec((None, tq, w), cur), pl.BlockSpec((None, tq, _KVB_WIDTH), cur),
                  pl.BlockSpec((None, BAND, _KVB_WIDTH), prv),
                  pl.BlockSpec((N_HEADS, LANES), lambda b, i: (0, 0))],
        out_specs=pl.BlockSpec((None, tq, w), cur),
        out_shape=jax.ShapeDtypeStruct((n, s, w), BF16),
        scratch_shapes=[pltpu.VMEM((tq + BAND, _KVB_WIDTH), BF16)],
        compiler_params=_cparams("arbitrary", "arbitrary"),
        name="attn_b",
    )(q, kv, kv, sinks_rows)


NEW_ROWS = 8


def _branch_multiplicity(t_query, key_pos, la):
    back = la + t_query - key_pos
    mult = jnp.zeros(back.shape, F32)
    for r in A_DILATIONS:
        mult = mult + ((back >= 0) & (back <= BAND * r) & (back % r == 0)).astype(F32)
    return mult


def _sample_attn_kernel(qa_ref, kna_ref, vna_ref, qb_ref, knb_ref, vnb_ref,
                        kt_ref, vt_ref, cbk_ref, cbv_ref, sink_ref,
                        oa_ref, ob_ref, *, nb, t_new):
    la = kt_ref.shape[-1]
    t_c = lax.broadcasted_iota(jnp.int32, (NEW_ROWS, la), 0)
    t_c = jnp.where(t_c < t_new, t_c, 0)
    mult_c = _branch_multiplicity(t_c, lax.broadcasted_iota(jnp.int32, (NEW_ROWS, la), 1), la)
    t_n = lax.broadcasted_iota(jnp.int32, (NEW_ROWS, NEW_ROWS), 0)
    t_n = jnp.where(t_n < t_new, t_n, 0)
    key_n = lax.broadcasted_iota(jnp.int32, (NEW_ROWS, NEW_ROWS), 1)
    mult_n = jnp.where(key_n < t_new, _branch_multiplicity(t_n, la + key_n, la), 0.0)

    def head_attn_a(q_h, kt, vt, kn_h, vn_h):
        s_c = jnp.where(mult_c > 0.0, _dot(q_h, kt), NEG_INF)
        s_n = jnp.where(mult_n > 0.0, _dot_nt(q_h, kn_h), NEG_INF)
        m = jnp.maximum(jnp.max(s_c, axis=-1, keepdims=True), jnp.max(s_n, axis=-1, keepdims=True))
        p_c = jnp.exp(s_c - m) * mult_c
        p_n = jnp.exp(s_n - m) * mult_n
        l = jnp.sum(p_c, axis=-1, keepdims=True) + jnp.sum(p_n, axis=-1, keepdims=True)
        return (_dot_nt(p_c.astype(BF16), vt) + _dot(p_n.astype(BF16), vn_h)) / l

    group = N_HEADS // B_KV_HEADS
    lb = cbk_ref.shape[-1]
    t_b = lax.broadcasted_iota(jnp.int32, (group * NEW_ROWS, lb), 0) & (NEW_ROWS - 1)
    t_b = jnp.where(t_b < t_new, t_b, 0)
    valid_bc = lax.broadcasted_iota(jnp.int32, (group * NEW_ROWS, lb), 1) >= lb + t_b - BAND
    t_bn = lax.broadcasted_iota(jnp.int32, (group * NEW_ROWS, NEW_ROWS), 0) & (NEW_ROWS - 1)
    t_bn = jnp.where(t_bn < t_new, t_bn, 0)
    key_bn = lax.broadcasted_iota(jnp.int32, (group * NEW_ROWS, NEW_ROWS), 1)
    valid_bn = (key_bn <= t_bn) & (key_bn < t_new)

    def kv_head_attn_b(q_g, kt, vt, kn_g, vn_g, sink):
        s_c = jnp.where(valid_bc, _dot(q_g, kt), NEG_INF)
        s_n = jnp.where(valid_bn, _dot_nt(q_g, kn_g), NEG_INF)
        m = jnp.maximum(jnp.max(s_c, axis=-1, keepdims=True), jnp.max(s_n, axis=-1, keepdims=True))
        m = jnp.maximum(m, sink)
        p_c = jnp.exp(s_c - m[:, :lb])
        p_n = jnp.exp(s_n - m[:, :NEW_ROWS])
        l = (jnp.sum(p_c, axis=-1, keepdims=True) + jnp.sum(p_n, axis=-1, keepdims=True) + jnp.exp(sink - m))
        return (_dot_nt(p_c.astype(BF16), vt) + _dot(p_n.astype(BF16), vn_g)) / l[:, :HEAD_DIM]

    def one_sequence(n, carry):
        new0 = pl.multiple_of(n * NEW_ROWS, NEW_ROWS)
        qa = qa_ref[pl.ds(new0, NEW_ROWS), :].astype(BF16)
        kna = kna_ref[pl.ds(new0, NEW_ROWS), :].astype(BF16)
        vna = vna_ref[pl.ds(new0, NEW_ROWS), :].astype(BF16)
        heads_out = []
        for h in range(N_HEADS):
            hs = slice(h * HEAD_DIM, (h + 1) * HEAD_DIM)
            heads_out.append(head_attn_a(qa[:, hs], kt_ref[n, h].astype(BF16), vt_ref[n, h].astype(BF16),
                                         kna[:, hs], vna[:, hs]))
        oa_ref[pl.ds(new0, NEW_ROWS), :] = jnp.concatenate(heads_out, axis=1).astype(oa_ref.dtype)
        qb = qb_ref[pl.ds(new0, NEW_ROWS), :].astype(BF16)
        knb = knb_ref[pl.ds(new0, NEW_ROWS), :].astype(BF16)
        vnb = vnb_ref[pl.ds(new0, NEW_ROWS), :].astype(BF16)
        heads_b = []
        for g in range(B_KV_HEADS):
            heads = range(g * group, (g + 1) * group)
            q_g = jnp.concatenate([qb[:, h * HEAD_DIM:(h + 1) * HEAD_DIM] for h in heads], axis=0)
            sink = jnp.concatenate([jnp.broadcast_to(sink_ref[h:h + 1, :], (NEW_ROWS, LANES)) for h in heads], axis=0)
            gs = slice(g * HEAD_DIM, (g + 1) * HEAD_DIM)
            o_g = kv_head_attn_b(q_g, cbk_ref[n, g].astype(BF16), cbv_ref[n, g].astype(BF16), knb[:, gs], vnb[:, gs], sink)
            heads_b += [o_g[hh * NEW_ROWS:(hh + 1) * NEW_ROWS] for hh in range(group)]
        ob_ref[pl.ds(new0, NEW_ROWS), :] = jnp.concatenate(heads_b, axis=1).astype(ob_ref.dtype)
        return carry

    lax.fori_loop(0, nb, one_sequence, 0)


def _pad_new_rows(a, n_seq, t_new):
    w = a.shape[-1]
    a = a.reshape(n_seq, t_new, w)
    a = jnp.pad(a, ((0, 0), (0, NEW_ROWS - t_new), (0, 0)))
    return a.reshape(n_seq * NEW_ROWS, w)


def _sample_attn(qa, ka, va, qb, kb, vb, cache_a_k, cache_a_v, cache_b_k, cache_b_v, sink_rows, *, t_new):
    n_seq, la, nh, hd = cache_a_k.shape
    lb, nkv = cache_b_k.shape[1:3]
    w = nh * hd
    assert t_new <= NEW_ROWS
    nb = 2 if n_seq % 2 == 0 else 1
    padded = [_pad_new_rows(a, n_seq, t_new) for a in (qa, ka, va, qb, kb, vb)]
    dim_major = lambda c: jnp.transpose(c, (0, 2, 3, 1))
    new_a = pl.BlockSpec((nb * NEW_ROWS, w), lambda i: (i, 0))
    new_b = pl.BlockSpec((nb * NEW_ROWS, KVW), lambda i: (i, 0))
    spec_ca = pl.BlockSpec((nb, nh, hd, la), lambda i: (i, 0, 0, 0))
    spec_cb = pl.BlockSpec((nb, nkv, hd, lb), lambda i: (i, 0, 0, 0))
    sds = jax.ShapeDtypeStruct
    oa, ob = pl.pallas_call(
        functools.partial(_sample_attn_kernel, nb=nb, t_new=t_new),
        grid=(n_seq // nb,),
        in_specs=[new_a, new_a, new_a, new_a, new_b, new_b, spec_ca, spec_ca, spec_cb, spec_cb,
                  pl.BlockSpec((N_HEADS, LANES), lambda i: (0, 0))],
        out_specs=[new_a, new_a],
        out_shape=[sds((n_seq * NEW_ROWS, w), BF16), sds((n_seq * NEW_ROWS, w), BF16)],
        compiler_params=_cparams("arbitrary"),
        name="sample_attn",
    )(*padded, dim_major(cache_a_k), dim_major(cache_a_v), dim_major(cache_b_k), dim_major(cache_b_v), sink_rows)
    unpad = lambda o: o.reshape(n_seq, NEW_ROWS, w)[:, :t_new].reshape(n_seq * t_new, w)
    return unpad(oa), unpad(ob)


def _silu(a):
    return a * (1.0 / (1.0 + jnp.exp(-a)))


def _natural_order(o_ref, lse_ref, o_stage, lse_stage):
    r, sub, _ = o_ref.shape
    if r == 1:
        return o_ref[0].astype(F32), lse_ref[0]
    for c in range(r):
        lse_stage[pl.ds(c, sub, stride=r), :] = lse_ref[c]
        for cb in range(N_HEAD_BLOCKS):
            o_stage[cb, pl.ds(c, sub, stride=r), :] = o_ref[c, :, cb * LANES:(cb + 1) * LANES].astype(F32)
    return jnp.concatenate([o_stage[cb] for cb in range(N_HEAD_BLOCKS)], axis=1), lse_stage[...]


def _merge_branches(outs, lses):
    top = functools.reduce(jnp.maximum, lses)
    ws = [jnp.exp(l - top) for l in lses]
    inv_den = 1.0 / functools.reduce(lambda a, b: a + b, ws)
    head = lax.broadcasted_iota(jnp.int32, (LANES, AW), 0)
    lane_head = lax.broadcasted_iota(jnp.int32, (LANES, AW), 1) // HEAD_DIM
    expand = (head == lane_head).astype(BF16)
    oa = None
    for w, o in zip(ws, outs):
        w = w * inv_den
        w_hi = w.astype(BF16)
        w_lo = (w - w_hi.astype(F32)).astype(BF16)
        part = (_dot(w_hi, expand) + _dot(w_lo, expand)) * o
        oa = part if oa is None else oa + part
    return oa.astype(BF16)


def _mix_out_ffn_kernel(x_ref, *refs, n_branches):
    if n_branches:
        branch_refs, refs = refs[:2 * n_branches], refs[2 * n_branches:]
        refs, (o_stage, lse_stage) = refs[:-2], refs[-2:]
    else:
        oa_ref, refs = refs[0], refs[1:]
    ob_ref, wo_ref, g_ref, w1_ref, w3_ref, w2_ref, out_ref, x1_ref, h_ref, acc_ref = refs
    j = pl.program_id(1)

    @pl.when(j == 0)
    def _():
        if n_branches:
            outs, lses = zip(*[_natural_order(branch_refs[2 * i], branch_refs[2 * i + 1], o_stage, lse_stage)
                               for i in range(n_branches)])
            oa = _merge_branches(outs, lses)
        else:
            oa = oa_ref[...]
        x1 = x_ref[...] + _dot(oa, wo_ref[0:AW, :]) + _dot(ob_ref[...], wo_ref[AW:, :])
        x1_ref[...] = x1
        h_ref[...] = _rms_norm(x1, g_ref[...]).astype(BF16)
        acc_ref[...] = jnp.zeros_like(acc_ref)

    h = h_ref[...]
    gate = (_silu(_dot(h, w1_ref[...])) * _dot(h, w3_ref[...])).astype(BF16)
    acc_ref[...] += _dot(gate, w2_ref[...])

    @pl.when(j == pl.num_programs(1) - 1)
    def _():
        out_ref[...] = x1_ref[...] + acc_ref[...]


def _ff_tile(ff, target):
    best = LANES
    for k in range(1, ff // LANES + 1):
        if ff % (k * LANES) == 0 and k * LANES <= target:
            best = k * LANES
    return best


def _mix_out_ffn(x, oa, ob, wo, g, w1, w3, w2, *, tm):
    t, d = x.shape
    ff = w1.shape[1]
    tf = ff
    row = lambda i, j: (i, 0)
    fixed = lambda i, j: (0, 0)
    once = pl.Buffered(1)
    if isinstance(oa, (list, tuple)):
        n_branches = len(oa)
        tiles = t // oa[0][0].shape[0] // tm
        a_specs, a_args = [], []
        for o, lse in oa:
            r = o.shape[1]
            assert tm % (r * SUBLANES) == 0
            tile = lambda i, j: (i // tiles, 0, i % tiles, 0)
            a_specs += [pl.BlockSpec((None, r, tm // r, AW), tile), pl.BlockSpec((None, r, tm // r, LANES), tile)]
            a_args += [o, lse]
        stage = [pltpu.VMEM((N_HEAD_BLOCKS, tm, LANES), F32), pltpu.VMEM((tm, LANES), F32)]
    else:
        n_branches, a_specs, a_args, stage = 0, [pl.BlockSpec((tm, AW), row)], [oa], []
    return pl.pallas_call(
        functools.partial(_mix_out_ffn_kernel, n_branches=n_branches),
        grid=(t // tm, ff // tf),
        in_specs=[pl.BlockSpec((tm, d), row)] + a_specs
                 + [pl.BlockSpec((tm, AW), row),
                  pl.BlockSpec((2 * AW, d), fixed, pipeline_mode=once), pl.BlockSpec((1, d), fixed),
                  pl.BlockSpec((d, tf), lambda i, j: (0, j), pipeline_mode=once),
                  pl.BlockSpec((d, tf), lambda i, j: (0, j), pipeline_mode=once),
                  pl.BlockSpec((tf, d), lambda i, j: (j, 0), pipeline_mode=once)],
        out_specs=pl.BlockSpec((tm, d), row),
        out_shape=jax.ShapeDtypeStruct((t, d), F32),
        scratch_shapes=[pltpu.VMEM((tm, d), F32), pltpu.VMEM((tm, d), BF16), pltpu.VMEM((tm, d), F32)] + stage,
        compiler_params=_cparams("arbitrary", "arbitrary"),
        name="mix_out_ffn",
    )(x, *a_args, ob, wo, g, w1, w3, w2)


def _gelu_exact(z):
    return 0.5 * z * (1.0 + lax.erf(z * (2.0 ** -0.5)))


def _gmlp_kernel(x_ref, g_ref, win_ref, lng_ref, lnb_ref, mix_ref, bias_ref, wout_ref, gffn_ref, wr_ref,
                 *out_refs, write_v):
    if write_v:
        xo_ref, h2_ref, logit_ref, v_ref = out_refs[:4]
    else:
        xo_ref, h2_ref, logit_ref = out_refs[:3]
    gate_ref = out_refs[-1]
    x = x_ref[...]
    tm, cd = x.shape
    h = _rms_norm(x, g_ref[...]).astype(BF16)
    z = _gelu_exact(_dot(h, win_ref[...]))
    u = z[:, :cd]
    v = z[:, cd:]
    mu = jnp.mean(v, axis=-1, keepdims=True)
    vc = v - mu
    v = vc * lax.rsqrt(jnp.mean(vc * vc, axis=-1, keepdims=True) + NORM_EPS) * lng_ref[...] + lnb_ref[...]
    if write_v:
        v_ref[...] = v
    vb = v.astype(BF16)
    bias = bias_ref[...]
    gw = cd // C_GROUPS
    for c in range(tm // CHUNK):
        rs = slice(c * CHUNK, (c + 1) * CHUNK)
        for g in range(C_GROUPS):
            cs = slice(g * gw, (g + 1) * gw)
            f = _dot(mix_ref[g], vb[rs, cs]) + bias[:, cs]
            gate_ref[rs, cs] = (u[rs, cs] * f).astype(BF16)
    xo = x + _dot(gate_ref[...], wout_ref[...])
    xo_ref[...] = xo
    h2 = _rms_norm(xo, gffn_ref[...])
    h2_ref[...] = h2
    h_hi = h2.astype(BF16)
    h_lo = (h2 - h_hi.astype(F32)).astype(BF16)
    wr = wr_ref[...]
    both = _dot(h_hi, wr)
    logit_ref[...] = both[:, :LANES] + both[:, LANES:] + _dot(h_lo, wr[:, :LANES])


def _gmlp_block(x, g, w_in, ln_g, ln_b, mix, bias_full, w_out, g_ffn, wr_hi_lo, *, tm, write_v):
    t, d = x.shape
    cd = w_out.shape[0]
    row = lambda i: (i, 0)
    fixed = lambda i: (0, 0)
    sds = jax.ShapeDtypeStruct
    out_specs = [pl.BlockSpec((tm, d), row), pl.BlockSpec((tm, d), row), pl.BlockSpec((tm, LANES), row)]
    out_shape = [sds((t, d), F32), sds((t, d), F32), sds((t, LANES), F32)]
    if write_v:
        out_specs.append(pl.BlockSpec((tm, cd), row))
        out_shape.append(sds((t, cd), F32))
    return pl.pallas_call(
        functools.partial(_gmlp_kernel, write_v=write_v),
        grid=(t // tm,),
        in_specs=[pl.BlockSpec((tm, d), row), pl.BlockSpec((1, d), fixed), pl.BlockSpec((d, 2 * cd), fixed),
                  pl.BlockSpec((1, cd), fixed), pl.BlockSpec((1, cd), fixed),
                  pl.BlockSpec((C_GROUPS, CHUNK, CHUNK), lambda i: (0, 0, 0)),
                  pl.BlockSpec((CHUNK, cd), fixed), pl.BlockSpec((cd, d), fixed), pl.BlockSpec((1, d), fixed),
                  pl.BlockSpec((d, 2 * LANES), fixed)],
        out_specs=out_specs, out_shape=out_shape,
        scratch_shapes=[pltpu.VMEM((tm, cd), BF16)],
        compiler_params=_cparams("arbitrary"),
        name="gmlp_block",
    )(x, g, w_in, ln_g, ln_b, mix, bias_full, w_out, g_ffn, wr_hi_lo)


MOE_BLOCK = 1024
MOE_SUB = 256
DISPATCH_ROWS = 512


def _dispatch_kernel(dest_ref, h_ref, xs_in_ref, xs_ref, sem):
    del xs_in_ref
    rows = h_ref.shape[0]

    def issue(g, carry):
        for s in range(SUBLANES):
            r = g * SUBLANES + s
            for k in range(TOP_K):
                d = dest_ref[0, 0, r * TOP_K + k]
                pltpu.make_async_copy(h_ref.at[pl.ds(r, 1), :], xs_ref.at[pl.ds(d, 1), :], sem).start(priority=k)
        return carry

    lax.fori_loop(0, rows // SUBLANES, issue, 0)
    for _ in range(TOP_K):
        pltpu.make_async_copy(h_ref, xs_ref.at[pl.ds(0, rows), :], sem).wait()


def _dispatch(h, dest, xs):
    t, d = h.shape
    tg = DISPATCH_ROWS
    return pl.pallas_call(
        _dispatch_kernel,
        grid=(t // tg,),
        in_specs=[pl.BlockSpec((1, 1, tg * TOP_K), lambda i: (i, 0, 0), memory_space=pltpu.SMEM),
                  pl.BlockSpec((tg, d), lambda i: (i, 0)),
                  pl.BlockSpec(memory_space=pl.ANY)],
        out_specs=pl.BlockSpec(memory_space=pl.ANY),
        out_shape=jax.ShapeDtypeStruct(xs.shape, xs.dtype),
        scratch_shapes=[pltpu.SemaphoreType.DMA],
        input_output_aliases={2: 0},
        compiler_params=_cparams("arbitrary"),
        name="moe_dispatch",
    )(dest.reshape(t // tg, 1, tg * TOP_K), h, xs)


def _expert_kernel(be_ref, na_ref, ns_ref, xs_ref, w1_ref, w3_ref, w2_ref, y_ref, h_ref):
    del be_ref, na_ref
    b = pl.program_id(0)
    j = pl.program_id(1)
    n_sub = ns_ref[b]

    @pl.when(j == 0)
    def _():
        y_ref[...] = jnp.zeros_like(y_ref)

    @pl.when((n_sub > 0) & (j == 0))
    def _():
        h_ref[...] = xs_ref[...].astype(BF16)

    for k in range(1, MOE_BLOCK // MOE_SUB + 1):
        @pl.when(n_sub == k)
        def _(rows=k * MOE_SUB):
            h = h_ref[0:rows, :]
            w1 = w1_ref[...].astype(BF16)
            w3 = w3_ref[...].astype(BF16)
            gate = (_silu(_dot(h, w1)) * _dot(h, w3)).astype(BF16)
            y_ref[0:rows, :] += _dot(gate, w2_ref[...].astype(BF16))


def _experts(xs, block_expert, n_active, n_sub, w1, w3, w2):
    n_rows, d = xs.shape
    ff = w1.shape[2]
    tf = _ff_tile(ff, 512)
    n_blocks = n_rows // MOE_BLOCK
    nj = ff // tf
    def wcol(b, j, be, na, ns):
        live = b < na[0]
        return (be[b], 0, jnp.where(live, j, nj - 1))

    def wrow(b, j, be, na, ns):
        live = b < na[0]
        return (be[b], jnp.where(live, j, nj - 1), 0)

    def xrow(b, j, be, na, ns):
        return (jnp.minimum(b, na[0] - 1), 0)

    grid_spec = pltpu.PrefetchScalarGridSpec(
        num_scalar_prefetch=3,
        grid=(n_blocks, nj),
        in_specs=[pl.BlockSpec((MOE_BLOCK, d), xrow),
                  pl.BlockSpec((None, d, tf), wcol), pl.BlockSpec((None, d, tf), wcol),
                  pl.BlockSpec((None, tf, d), wrow)],
        out_specs=pl.BlockSpec((MOE_BLOCK, d), lambda b, j, be, na, ns: (b, 0)),
        scratch_shapes=[pltpu.VMEM((MOE_BLOCK, d), BF16)],
    )
    return pl.pallas_call(
        _expert_kernel,
        grid_spec=grid_spec,
        out_shape=jax.ShapeDtypeStruct((n_rows, d), F32),
        compiler_params=_cparams("arbitrary", "arbitrary"),
        name="moe_experts",
    )(block_expert, n_active, n_sub, xs, w1, w3, w2)


def _combine_kernel(dest_ref, dest_next_ref, x_ref, gate_ref, y_ref, out_ref, buf, sems, *, steps):
    rows = x_ref.shape[0]
    i = pl.program_id(0)

    def fetch(dests, slot):
        def issue(g, carry):
            for s in range(SUBLANES):
                r = g * SUBLANES + s
                for k in range(TOP_K):
                    d = dests[0, 0, r * TOP_K + k]
                    pltpu.make_async_copy(y_ref.at[pl.ds(d, 1), :], buf.at[slot, k, pl.ds(r, 1), :],
                                          sems.at[slot]).start(priority=k)
            return carry
        lax.fori_loop(0, rows // SUBLANES, issue, 0)

    @pl.when(i == 0)
    def _():
        fetch(dest_ref, 0)

    if steps > 1:
        @pl.when(i + 1 < steps)
        def _():
            fetch(dest_next_ref, (i + 1) % 2)

    slot = i % 2
    for k in range(TOP_K):
        pltpu.make_async_copy(y_ref.at[pl.ds(0, rows), :], buf.at[slot, k], sems.at[slot]).wait()
    gates = gate_ref[...]
    y = buf[slot, 0] * gates[:, 0:1]
    for k in range(1, TOP_K):
        y = y + buf[slot, k] * gates[:, k:k + 1]
    out_ref[...] = x_ref[...] + y


def _combine(x, gates, dest, y):
    t, d = x.shape
    tg = DISPATCH_ROWS
    steps = t // tg
    dests = dest.reshape(steps, 1, tg * TOP_K)
    return pl.pallas_call(
        functools.partial(_combine_kernel, steps=steps),
        grid=(steps,),
        in_specs=[pl.BlockSpec((1, 1, tg * TOP_K), lambda i: (i, 0, 0), memory_space=pltpu.SMEM),
                  pl.BlockSpec((1, 1, tg * TOP_K), lambda i: (jnp.minimum(i + 1, steps - 1), 0, 0),
                               memory_space=pltpu.SMEM),
                  pl.BlockSpec((tg, d), lambda i: (i, 0)),
                  pl.BlockSpec((tg, TOP_K), lambda i: (i, 0)),
                  pl.BlockSpec(memory_space=pl.ANY)],
        out_specs=pl.BlockSpec((tg, d), lambda i: (i, 0)),
        out_shape=jax.ShapeDtypeStruct((t, d), F32),
        scratch_shapes=[pltpu.VMEM((2, TOP_K, tg, d), F32), pltpu.SemaphoreType.DMA((2,))],
        compiler_params=_cparams("arbitrary"),
        name="moe_combine",
    )(dests, dests, x, gates, y)


def _route(logits):
    t = logits.shape[0]
    top_val, top_idx = lax.top_k(logits, TOP_K)
    gates = jax.nn.softmax(top_val, axis=-1)
    expert = top_idx.reshape(-1)
    one_hot = (expert[:, None] == jnp.arange(N_EXPERTS, dtype=expert.dtype)[None, :]).astype(jnp.int32)
    running = jnp.cumsum(one_hot, axis=0)
    rank = jnp.sum(running * one_hot, axis=1) - 1
    counts = running[-1]
    padded = (counts + MOE_BLOCK - 1) // MOE_BLOCK * MOE_BLOCK
    pad_ends = jnp.cumsum(padded)
    pad_starts = pad_ends - padded
    dest = (jnp.sum(pad_starts[None, :] * one_hot, axis=1) + rank).astype(jnp.int32)
    n_blocks = -(-(t * TOP_K) // MOE_BLOCK) + N_EXPERTS
    block_start = jnp.arange(n_blocks, dtype=jnp.int32) * MOE_BLOCK
    block_expert = jnp.sum((block_start[:, None] >= pad_ends[None, :]).astype(jnp.int32), axis=1)
    block_expert = jnp.minimum(block_expert, N_EXPERTS - 1).astype(jnp.int32)
    used_end = (pad_starts + counts)[block_expert]
    used_rows = jnp.clip(used_end - block_start, 0, MOE_BLOCK)
    used_rows = jnp.where(block_start < pad_ends[-1], used_rows, 0)
    n_sub = ((used_rows + MOE_SUB - 1) // MOE_SUB).astype(jnp.int32)
    n_active = (pad_ends[-1:] // MOE_BLOCK).astype(jnp.int32)
    return gates, dest.reshape(t, TOP_K), block_expert, n_sub, n_active, n_blocks * MOE_BLOCK


def _moe(groups, w1, w3, w2):
    sizes = [x.shape[0] for x, _, _ in groups]
    gates, dest, block_expert, n_sub, n_active, n_rows = _route(jnp.concatenate([l for _, _, l in groups]))
    xs = jnp.zeros((n_rows, w1.shape[1]), F32)
    start = 0
    for (_, h2, _), size in zip(groups, sizes):
        xs = _dispatch(h2, dest[start:start + size].reshape(-1), xs)
        start += size
    y = _experts(xs, block_expert, n_active, n_sub, w1, w3, w2)
    outs, start = [], 0
    for (x, _, _), size in zip(groups, sizes):
        outs.append(_combine(x, gates[start:start + size], dest[start:start + size].reshape(-1), y))
        start += size
    return outs


def _rope_tables(pos):
    inv_freq = ROPE_THETA ** (-jnp.arange(HALF_DIM, dtype=F32) / HALF_DIM)
    ang = pos.astype(F32)[:, None] * inv_freq[None, :]
    cos, sin = jnp.cos(ang), jnp.sin(ang)
    reps = LANES // HEAD_DIM
    return jnp.tile(cos, (1, 2 * reps)), jnp.tile(jnp.concatenate([-sin, sin], axis=1), (1, reps))


def _qk_gain_row(qk_gain):
    ones_a = jnp.ones((AW,), F32)
    ones_b = jnp.ones((KVW,), F32)
    return jnp.concatenate([jnp.tile(qk_gain[0], N_HEADS), jnp.tile(qk_gain[1], N_HEADS), ones_a,
                            jnp.tile(qk_gain[2], N_HEADS), jnp.tile(qk_gain[3], B_KV_HEADS), ones_b])[None, :]


def _row_block(t):
    return 512 if t % 512 == 0 else t


def kernel(x_prompt, x_sample, cache_a_k, cache_a_v, cache_b_k, cache_b_v, norm_mix, norm_ffn, ab_w_in,
           ab_qk_gain, ab_sinks, ab_w_out, ffn_w1, ffn_w3, ffn_w2, c_w_in, c_ln_g, c_ln_b, c_w_s, c_b_s,
           c_w_out, moe_router, moe_w1, moe_w3, moe_w2):
    n_p, s_p, d = x_prompt.shape
    n_s, t_new, _ = x_sample.shape
    depth = norm_mix.shape[0]
    past_len = 8192
    xp = x_prompt.reshape(n_p * s_p, d)
    xs = x_sample.reshape(n_s * t_new, d)
    pos_p = jnp.tile(jnp.arange(s_p), n_p)
    pos_s = jnp.tile(past_len + jnp.arange(t_new), n_s)
    cos_p, sin_p = _rope_tables(pos_p)
    cos_s, sin_s = _rope_tables(pos_s)
    row = lambda a: a[None, :]
    akp, avp, bkp, bvp, aks, avs, bks, bvs, cvs = [], [], [], [], [], [], [], [], []
    la_p, lb_p = min(BAND * A_DILATIONS[-1], s_p), min(BAND, s_p)
    for layer in range(depth):
        i = layer // 2
        if layer % 2 == 0:
            w_in = ab_w_in[i].astype(BF16)
            gain_row = _qk_gain_row(ab_qk_gain[i])
            w_out = ab_w_out[i].astype(BF16)
            w1, w3, w2 = ffn_w1[i].astype(BF16), ffn_w3[i].astype(BF16), ffn_w2[i].astype(BF16)
            sinks = ab_sinks[i].astype(F32)
            sink_rows = jnp.broadcast_to(sinks[:, None], (N_HEADS, LANES))
            qa, ka, va, qb, kb, vb, ka16, va16, kvb16, *dilated = _qkv_proj(
                xp, row(norm_mix[layer]), w_in, gain_row, cos_p, sin_p,
                tm=_row_block(s_p), q_dtype=BF16, seq_len=s_p)
            seq = lambda a: a.reshape(n_p, s_p, a.shape[-1])
            qkv_by_branch = [[seq(a)[:, None] for a in (qa, ka16, va16)]] + [dilated[3 * i:3 * i + 3]
                                                                            for i in range(len(A_DILATIONS) - 1)]
            branches = [_attn_a_branch(q, k, v) for q, k, v in qkv_by_branch]
            ob = _attn_b(seq(qb), seq(kvb16), sink_rows).reshape(n_p * s_p, AW)
            xp = _mix_out_ffn(xp, branches, ob, w_out, row(norm_ffn[layer]), w1, w3, w2, tm=_row_block(s_p))
            tail = lambda a, rows, nh: seq(a)[:, s_p - rows:].reshape(n_p, rows, nh, HEAD_DIM)
            akp.append(tail(ka, la_p, N_HEADS))
            avp.append(tail(va, la_p, N_HEADS))
            bkp.append(tail(kb, lb_p, B_KV_HEADS))
            bvp.append(tail(vb, lb_p, B_KV_HEADS))
            qa, ka, va, qb, kb, vb = _qkv_proj(xs, row(norm_mix[layer]), w_in, gain_row, cos_s, sin_s,
                                               tm=_row_block(xs.shape[0]), q_dtype=F32)
            oa, ob = _sample_attn(qa, ka, va, qb, kb, vb, cache_a_k[i], cache_a_v[i],
                                  cache_b_k[i], cache_b_v[i], sink_rows, t_new=t_new)
            xs = _mix_out_ffn(xs, oa, ob, w_out, row(norm_ffn[layer]), w1, w3, w2, tm=_row_block(xs.shape[0]))
            heads_s = lambda a, nh: a.reshape(n_s, t_new, nh, HEAD_DIM)
            aks.append(heads_s(ka, N_HEADS))
            avs.append(heads_s(va, N_HEADS))
            bks.append(heads_s(kb, B_KV_HEADS))
            bvs.append(heads_s(vb, B_KV_HEADS))
        else:
            w_in = c_w_in[i].astype(BF16)
            w_out = c_w_out[i].astype(BF16)
            tril = jnp.tril(jnp.ones((CHUNK, CHUNK), F32))
            mix_p = (c_w_s[i] * tril).astype(BF16)
            gw = w_out.shape[0] // C_GROUPS
            bias_p = jnp.repeat(c_b_s[i].T, gw, axis=1)
            per_tile = CHUNK // t_new
            mix_s = jnp.einsum("ab,gij->gaibj", jnp.eye(per_tile, dtype=F32), (c_w_s[i] * tril)[:, :t_new, :t_new])
            mix_s = mix_s.reshape(C_GROUPS, CHUNK, CHUNK).astype(BF16)
            bias_s = jnp.tile(bias_p[:t_new], (per_tile, 1))
            wr = moe_router[i]
            wr_hi = wr.astype(BF16)
            wr_lo = (wr - wr_hi.astype(F32)).astype(BF16)
            pad_e = lambda a: jnp.pad(a, ((0, 0), (0, LANES - N_EXPERTS)))
            wr_hi_lo = jnp.concatenate([pad_e(wr_hi), pad_e(wr_lo)], axis=1)
            common = (row(c_ln_g[i]), row(c_ln_b[i]))
            xp1, hp, lp = _gmlp_block(xp, row(norm_mix[layer]), w_in, *common, mix_p, bias_p, w_out,
                                      row(norm_ffn[layer]), wr_hi_lo, tm=_row_block(xp.shape[0]), write_v=False)
            xs1, hs, ls, v_new = _gmlp_block(xs, row(norm_mix[layer]), w_in, *common, mix_s, bias_s, w_out,
                                             row(norm_ffn[layer]), wr_hi_lo, tm=_row_block(xs.shape[0]),
                                             write_v=True)
            cvs.append(v_new.reshape(n_s, t_new, v_new.shape[-1]))
            xp, xs = _moe([(xp1, hp, lp[:, :N_EXPERTS]), (xs1, hs, ls[:, :N_EXPERTS])],
                          moe_w1[i], moe_w3[i], moe_w2[i])
    return (xp.reshape(n_p, s_p, d), xs.reshape(n_s, t_new, d),
            jnp.stack(akp), jnp.stack(avp), jnp.stack(bkp), jnp.stack(bvp),
            jnp.stack(aks), jnp.stack(avs), jnp.stack(bks), jnp.stack(bvs),
            jnp.stack(cvs))
```
